```python
import math, functools
import jax, jax.numpy as jnp
from jax import lax
import numpy as np

D_MODEL = 2048
BATCH = 4
SEQ = 4096
DEPTH = 1

GRID_W = 64
CTX_LEN = 256

DN_HEADS = 16
DN_DK = 128
DN_DV = 128
DN_QK_W = DN_HEADS * DN_DK
DN_V_W = DN_HEADS * DN_DV
DN_CONV = 5
DN_CHUNK = 64
HY_W = D_MODEL
HY_ORDER = 2
HY_CONV = 3
HY_BANDS = 16
HY_EMB = 1 + 2 * HY_BANDS
HY_HIDDEN = 64
HY_FILTER_GAIN = 0.05
HY_DECAY_TARGET = 1e-2
HY_FAST_DECAY = 0.3
HY_SLOW_DECAY = 1.5
OFF_Q = 0
OFF_K = OFF_Q + DN_QK_W
OFF_V = OFF_K + DN_QK_W
OFF_Z = OFF_V + DN_V_W
OFF_AB = OFF_Z + DN_V_W
OFF_HY = OFF_AB + 4 * DN_HEADS
OFF_GATE = OFF_HY + 3 * HY_W
PROJ_W = OFF_GATE + 2 * D_MODEL
PEER_NKEYS = 128
PEER_EXPERTS = PEER_NKEYS * PEER_NKEYS
PEER_HEADS = 8
PEER_TOPK = 16
PEER_DKEY = 256
PEER_HALF = PEER_DKEY // 2
PEER_BLOCK = 128
DEEPNORM_ALPHA = (2 * DEPTH) ** 0.25
DEEPNORM_BETA = (8 * DEPTH) ** -0.25
LN_EPS = 1e-5
RMS_EPS = 1e-6

kernel_name = 'hybrid_gdn_hyena_peer_diffusion_block'

F32 = jnp.float32


def layer_norm(x, g, b):
    xf = x.astype(F32)
    mu = jnp.mean(xf, -1, keepdims=True)
    var = jnp.mean(jnp.square(xf - mu), -1, keepdims=True)
    return ((xf - mu) * lax.rsqrt(var + LN_EPS) * g.astype(F32) + b.astype(F32)).astype(x.dtype)


def rms_norm(x, w):
    xf = x.astype(F32)
    return xf * lax.rsqrt(jnp.mean(xf * xf, -1, keepdims=True) + RMS_EPS) * w.astype(F32)


def l2_normalize(x):
    xf = x.astype(F32)
    return xf * lax.rsqrt(jnp.sum(xf * xf, -1, keepdims=True) + 1e-6)


def dwconv_centered(x, w):
    k = w.shape[0]
    pad = k // 2
    n = x.shape[-2]
    xp = jnp.pad(x, [(0, 0)] * (x.ndim - 2) + [(pad, pad), (0, 0)])
    out = xp[..., 0:n, :] * w[0]
    for j in range(1, k):
        out = out + xp[..., j:j + n, :] * w[j]
    return out


def grid_row_conv(x, w, rows):
    b, n, ch = x.shape
    return dwconv_centered(x.reshape(b, rows, GRID_W, ch), w).reshape(b, n, ch)


def delta_inputs(proj, conv, conv_w, a_log, dt_bias):
    b, n, _ = proj.shape
    qkv = jax.nn.silu(conv(proj[..., OFF_Q:OFF_Z], conv_w).astype(F32))
    q = qkv[..., :DN_QK_W].reshape(b, n, DN_HEADS, DN_DK)
    k = qkv[..., DN_QK_W:2 * DN_QK_W].reshape(b, n, DN_HEADS, DN_DK)
    v = qkv[..., 2 * DN_QK_W:].reshape(b, n, DN_HEADS, DN_DV)
    q = l2_normalize(q) * (DN_DK ** -0.5)
    k = l2_normalize(k)
    ab = proj[..., OFF_AB:OFF_HY].astype(F32).reshape(b, n, 2, 2, DN_HEADS)
    g = -jnp.exp(a_log.astype(F32)) * jax.nn.softplus(ab[..., 0, :] + dt_bias.astype(F32))
    beta = jax.nn.sigmoid(ab[..., 1, :])
    bhn = lambda t: t.transpose(0, 2, 1, 3)
    return bhn(q), bhn(k), bhn(v), g.transpose(2, 0, 3, 1), beta.transpose(2, 0, 3, 1)


def gated_delta_chunked(q, k, v, g, beta, s0):
    b, h, t, dk = q.shape
    dv = v.shape[-1]
    c = DN_CHUNK
    nc = t // c
    chunks = lambda a: a.reshape(b, h, nc, c, *a.shape[3:])
    q, k, v = chunks(q), chunks(k), chunks(v)
    g = lax.cumsum(chunks(g), axis=3)
    beta = chunks(beta)
    kb = k * beta[..., None]
    vb = v * beta[..., None]
    tril = jnp.tril(jnp.ones((c, c), dtype=bool))
    strict = jnp.tril(jnp.ones((c, c), dtype=bool), -1)
    decay = jnp.exp(jnp.where(tril, g[..., :, None] - g[..., None, :], -jnp.inf))
    m = jnp.where(strict, jnp.einsum('bhncd,bhnmd->bhncm', kb, k) * decay, 0.0)
    a_mat = m + jnp.eye(c, dtype=F32)
    rhs = jnp.concatenate([vb, kb * jnp.exp(g)[..., None]], axis=-1)
    sol = lax.linalg.triangular_solve(a_mat, rhs, left_side=True, lower=True)
    u, w = sol[..., :dv], sol[..., dv:]
    attn = jnp.einsum('bhncd,bhnmd->bhncm', q, k) * decay

    def step(s, inp):
        q_i, k_i, u_i, w_i, g_i, a_i = inp
        v_new = u_i - jnp.einsum('bhcd,bhdv->bhcv', w_i, s)
        o = jnp.einsum('bhcd,bhdv->bhcv', q_i * jnp.exp(g_i)[..., None], s) + jnp.einsum('bhcm,bhmv->bhcv', a_i, v_new)
        g_last = g_i[..., -1]
        s = s * jnp.exp(g_last)[..., None, None] + jnp.einsum('bhcd,bhcv->bhdv', k_i * jnp.exp(g_last[..., None] - g_i)[..., None], v_new)
        return s, o

    xs = tuple(jnp.moveaxis(a, 2, 0) for a in (q, k, u, w, g, attn))
    s_final, o = lax.scan(step, s0, xs)
    return jnp.moveaxis(o, 0, 2).reshape(b, h, t, dv), s_final


def bidir_delta(q, k, v, g, beta, s0_fwd, s0_bwd):
    o_f, s_f = gated_delta_chunked(q, k, v, g[0], beta[0], s0_fwd)
    flip = lambda a: jnp.flip(a, axis=2)
    o_b, s_b = gated_delta_chunked(flip(q), flip(k), flip(v), flip(g[1]), flip(beta[1]), s0_bwd)
    return o_f + flip(o_b), s_f, s_b


def hyena_filters(n, w1, b1, w2, b2, w3, b3, freq):
    t = jnp.linspace(0.0, 1.0, n, dtype=F32)
    pos = jnp.arange(n, dtype=F32)
    bands = jnp.linspace(1e-4, HY_BANDS - 1, HY_BANDS, dtype=F32)
    ang = (2.0 * math.pi / n) * pos[:, None] * bands[None, :]
    z = jnp.concatenate([t[:, None], jnp.cos(ang), -jnp.sin(ang)], axis=-1)
    fr = freq.astype(F32)
    hid = jnp.sin(fr * (z @ w1.astype(F32) + b1.astype(F32)))
    hid = jnp.sin(fr * (hid @ w2.astype(F32) + b2.astype(F32)))
    filt = (hid @ w3.astype(F32) + b3.astype(F32)).reshape(n, 2, HY_ORDER, HY_W)
    deltas = jnp.abs(jnp.linspace(math.log(HY_DECAY_TARGET) / HY_SLOW_DECAY, math.log(HY_DECAY_TARGET) / HY_FAST_DECAY, HY_W, dtype=F32))
    filt = filt * jnp.exp(-t[:, None] * deltas[None, :])[:, None, None, :]
    circ = jnp.concatenate([filt[:, 0], jnp.zeros((1, HY_ORDER, HY_W), F32), filt[:0:-1, 1]], axis=0)
    return jnp.fft.rfft(circ, axis=0)


def long_conv(u, filt_f, skip):
    n = u.shape[1]
    y = jnp.fft.irfft(jnp.fft.rfft(u, n=2 * n, axis=1) * filt_f, n=2 * n, axis=1)[:, :n]
    return y + u * skip.astype(F32)


def hyena_mix(p, conv, conv_w, filt_f, skip):
    u = conv(p, conv_w).astype(F32)
    v, x1, x2 = u[..., :HY_W], u[..., HY_W:2 * HY_W], u[..., 2 * HY_W:]
    y = x1 * long_conv(v, filt_f[:, 0], skip[0])
    return x2 * long_conv(y, filt_f[:, 1], skip[1])


def merge_branches(o_dn, proj, y_hy, norm_w, w_pa, w_ph, w_out):
    b, n, _ = proj.shape
    dt = proj.dtype
    z = proj[..., OFF_Z:OFF_AB].astype(F32).reshape(b, n, DN_HEADS, DN_DV)
    o = (rms_norm(o_dn.transpose(0, 2, 1, 3), norm_w) * jax.nn.silu(z)).reshape(b, n, DN_V_W).astype(dt)
    gates = jax.nn.sigmoid(proj[..., OFF_GATE:].astype(F32))
    merged = gates[..., :D_MODEL] * (o @ w_pa) + gates[..., D_MODEL:] * (y_hy.astype(dt) @ w_ph)
    return merged.astype(dt) @ w_out


def peer_ffn(h, wq, subkeys, u_tab, v_tab):
    b, n, d = h.shape
    t = b * n
    hf = h.reshape(t, d)
    q = (hf @ wq).astype(F32).reshape(t, PEER_HEADS, 2, PEER_HALF)
    s = jnp.einsum('thpd,hpkd->thpk', q, subkeys.astype(F32))
    top_s, top_i = lax.top_k(s, PEER_TOPK)
    cand_s = (top_s[:, :, 0, :, None] + top_s[:, :, 1, None, :]).reshape(t, PEER_HEADS, PEER_TOPK * PEER_TOPK)
    cand_i = (top_i[:, :, 0, :, None] * PEER_NKEYS + top_i[:, :, 1, None, :]).reshape(t, PEER_HEADS, PEER_TOPK * PEER_TOPK)
    best_s, best_j = lax.top_k(cand_s, PEER_TOPK)
    idx = jnp.take_along_axis(cand_i, best_j, axis=-1).reshape(t, PEER_HEADS * PEER_TOPK)
    gate = jax.nn.softmax(best_s, axis=-1).reshape(t, PEER_HEADS * PEER_TOPK)
    nb = t // PEER_BLOCK

    def block(args):
        xb, ib, gb = args
        act = jax.nn.gelu(jnp.einsum('tkd,td->tk', u_tab[ib], xb).astype(F32), approximate=False)
        return jnp.einsum('tk,tkd->td', (gb * act).astype(xb.dtype), v_tab[ib])

    y = lax.map(block, (hf.reshape(nb, PEER_BLOCK, d), idx.reshape(nb, PEER_BLOCK, -1), gate.reshape(nb, PEER_BLOCK, -1)))
    return y.reshape(b, n, d)


def setup_inputs(seed: int = 0) -> dict:
    key = jax.random.key(seed)
    ks = jax.random.split(key, 40)
    nrm = lambda k, shape, s: jax.random.normal(k, shape, F32) * s
    L = DEPTH
    dt = jnp.exp(jax.random.uniform(ks[8], (L, 2, DN_HEADS), F32, math.log(1e-3), math.log(1e-1)))
    return {
        'x': nrm(ks[0], (BATCH, SEQ, D_MODEL), 1.0),
        'c': nrm(ks[1], (BATCH, D_MODEL), 1.0),
        'ctx': nrm(ks[2], (BATCH, CTX_LEN, D_MODEL), 1.0),
        'c_ctx': nrm(ks[3], (D_MODEL,), 1.0),
        'w_ada': nrm(ks[4], (L, D_MODEL, 6 * D_MODEL), D_MODEL ** -0.5),
        'b_ada': nrm(ks[5], (L, 6 * D_MODEL), 0.02),
        'w_in': nrm(ks[6], (L, D_MODEL, PROJ_W), D_MODEL ** -0.5),
        'dn_conv_w': nrm(ks[7], (L, DN_CONV, 2 * DN_QK_W + DN_V_W), DN_CONV ** -0.5),
        'dn_a_log': jnp.log(jax.random.uniform(ks[9], (L, 2, DN_HEADS), F32, 1.0, 16.0)),
        'dn_dt_bias': dt + jnp.log(-jnp.expm1(-dt)),
        'dn_norm_w': 1.0 + nrm(ks[10], (L, DN_DV), 0.02),
        'hy_conv_w': nrm(ks[11], (L, HY_CONV, 3 * HY_W), HY_CONV ** -0.5),
        'hy_w1': nrm(ks[12], (L, HY_EMB, HY_HIDDEN), HY_EMB ** -0.5),
        'hy_b1': nrm(ks[13], (L, HY_HIDDEN), 0.02),
        'hy_w2': nrm(ks[14], (L, HY_HIDDEN, HY_HIDDEN), HY_HIDDEN ** -0.5),
        'hy_b2': nrm(ks[15], (L, HY_HIDDEN), 0.02),
        'hy_w3': nrm(ks[16], (L, HY_HIDDEN, 2 * HY_ORDER * HY_W), HY_FILTER_GAIN * HY_HIDDEN ** -0.5),
        'hy_b3': nrm(ks[17], (L, 2 * HY_ORDER * HY_W), 0.005),
        'hy_freq': 1.0 + nrm(ks[18], (L, HY_HIDDEN), 0.02),
        'hy_skip': nrm(ks[19], (L, HY_ORDER, HY_W), 0.5),
        'w_branch_dn': nrm(ks[20], (L, DN_V_W, D_MODEL), DN_V_W ** -0.5),
        'w_branch_hy': nrm(ks[21], (L, HY_W, D_MODEL), HY_W ** -0.5),
        'w_out': nrm(ks[22], (L, D_MODEL, D_MODEL), DEEPNORM_BETA * D_MODEL ** -0.5),
        'ln1_g': 1.0 + nrm(ks[23], (L, D_MODEL), 0.02),
        'ln1_b': nrm(ks[24], (L, D_MODEL), 0.02),
        'peer_wq': nrm(ks[25], (L, D_MODEL, PEER_HEADS * PEER_DKEY), D_MODEL ** -0.5),
        'peer_subkeys': nrm(ks[26], (L, PEER_HEADS, 2, PEER_NKEYS, PEER_HALF), PEER_HALF ** -0.5),
        'peer_u': nrm(ks[27], (L, PEER_EXPERTS, D_MODEL), D_MODEL ** -0.5),
        'peer_v': nrm(ks[28], (L, PEER_EXPERTS, D_MODEL), DEEPNORM_BETA),
        'ln2_g': 1.0 + nrm(ks[29], (L, D_MODEL), 0.02),
        'ln2_b': nrm(ks[30], (L, D_MODEL), 0.02),
    }


def reference(x, c, ctx, c_ctx, w_ada, b_ada, w_in, dn_conv_w, dn_a_log, dn_dt_bias, dn_norm_w, hy_conv_w, hy_w1, hy_b1, hy_w2, hy_b2, hy_w3, hy_b3, hy_freq, hy_skip, w_branch_dn, w_branch_hy, w_out, ln1_g, ln1_b, peer_wq, peer_subkeys, peer_u, peer_v, ln2_g, ln2_b):
    b, n, _ = x.shape
    rows = n // GRID_W
    n_ctx = ctx.shape[1]
    lat_conv = functools.partial(grid_row_conv, rows=rows)
    xc = ctx
    for l in range(DEPTH):
        mod = jax.nn.silu(c) @ w_ada[l] + b_ada[l]
        mod_c = jax.nn.silu(c_ctx) @ w_ada[l] + b_ada[l]
        sh1, sc1, g1, sh2, sc2, g2 = jnp.split(mod[:, None, :], 6, axis=-1)
        sh1c, sc1c, g1c, sh2c, sc2c, g2c = jnp.split(mod_c, 6)

        h = x * (1.0 + sc1) + sh1
        hc = xc * (1.0 + sc1c) + sh1c
        proj = h @ w_in[l]
        proj_c = hc @ w_in[l]

        qc, kc, vc, gc, bc = delta_inputs(proj_c, dwconv_centered, dn_conv_w[l], dn_a_log[l], dn_dt_bias[l])
        zero = jnp.zeros((b, DN_HEADS, DN_DK, DN_DV), F32)
        o_c, s_fwd, s_bwd = bidir_delta(qc, kc, vc, gc, bc, zero, zero)
        ql, kl, vl, gl, bl = delta_inputs(proj, lat_conv, dn_conv_w[l], dn_a_log[l], dn_dt_bias[l])
        o_l, _, _ = bidir_delta(ql, kl, vl, gl, bl, s_fwd, s_bwd)

        filt_l = hyena_filters(n, hy_w1[l], hy_b1[l], hy_w2[l], hy_b2[l], hy_w3[l], hy_b3[l], hy_freq[l])
        y_hy = hyena_mix(proj[..., OFF_HY:OFF_GATE], lat_conv, hy_conv_w[l], filt_l, hy_skip[l])

        mix = merge_branches(o_l, proj, y_hy, dn_norm_w[l], w_branch_dn[l], w_branch_hy[l], w_out[l])
        x_new = layer_norm(DEEPNORM_ALPHA * x + g1 * mix, ln1_g[l], ln1_b[l])

        h2 = x_new * (1.0 + sc2) + sh2
        y_peer = peer_ffn(h2, peer_wq[l], peer_subkeys[l], peer_u[l], peer_v[l])
        x_new = layer_norm(DEEPNORM_ALPHA * x_new + g2 * y_peer, ln2_g[l], ln2_b[l])

        if l < DEPTH - 1:
            filt_c = hyena_filters(n_ctx, hy_w1[l], hy_b1[l], hy_w2[l], hy_b2[l], hy_w3[l], hy_b3[l], hy_freq[l])
            y_hy_c = hyena_mix(proj_c[..., OFF_HY:OFF_GATE], dwconv_centered, hy_conv_w[l], filt_c, hy_skip[l])
            mix_c = merge_branches(o_c, proj_c, y_hy_c, dn_norm_w[l], w_branch_dn[l], w_branch_hy[l], w_out[l])
            xc = layer_norm(DEEPNORM_ALPHA * xc + g1c * mix_c, ln1_g[l], ln1_b[l])
            hc2 = xc * (1.0 + sc2c) + sh2c
            xc = layer_norm(DEEPNORM_ALPHA * xc + g2c * peer_ffn(hc2, peer_wq[l], peer_subkeys[l], peer_u[l], peer_v[l]), ln2_g[l], ln2_b[l])
        x = x_new
    return x
```

```python
import functools
import math

import jax
import jax.numpy as jnp
from jax import lax
from jax.experimental import pallas as pl
from jax.experimental.pallas import tpu as pltpu

F32 = jnp.float32
BF16 = jnp.bfloat16

GRID_W = 64
DN_CHUNK = 64
PEER_TOPK = 16
DEPTH = 1
DEEPNORM_ALPHA = (2 * DEPTH) ** 0.25
LN_EPS = 1e-5
RMS_EPS = 1e-6
L2_EPS = 1e-6
HY_DECAY_TARGET = 1e-2
HY_FAST_DECAY = 0.3
HY_SLOW_DECAY = 1.5

LANES = 128
SUBLANES = 8
VMEM_LIMIT_BYTES = 56 * 1024 * 1024

FFT_N2 = 64
FFT_ROW_PAD = 8


def _cparams(sem):
    return pltpu.CompilerParams(dimension_semantics=sem, vmem_limit_bytes=VMEM_LIMIT_BYTES)


def _silu(x):
    return x * jax.nn.sigmoid(x)


def _ada_kernel(c_ref, w_ref, b_ref, o_ref):
    s = _silu(c_ref[...])
    o_ref[...] = jnp.dot(s.astype(BF16), w_ref[...].astype(BF16), preferred_element_type=F32) + b_ref[...]


def _ada(cc, w, b):
    d, n6 = w.shape
    tn = min(n6, 1536)
    return pl.pallas_call(
        _ada_kernel,
        grid=(n6 // tn,),
        in_specs=[pl.BlockSpec((SUBLANES, d), lambda j: (0, 0)),
                  pl.BlockSpec((d, tn), lambda j: (0, j)),
                  pl.BlockSpec((1, tn), lambda j: (0, j))],
        out_specs=pl.BlockSpec((SUBLANES, tn), lambda j: (0, j)),
        out_shape=jax.ShapeDtypeStruct((SUBLANES, n6), F32),
        compiler_params=_cparams(("arbitrary",)),
    )(cc, w, b)


def _modmm_kernel(x_ref, sc_ref, sh_ref, w_ref, o_ref, h_scr):
    @pl.when(pl.program_id(1) == 0)
    def _():
        h_scr[...] = (x_ref[...] * (1.0 + sc_ref[0]) + sh_ref[0]).astype(BF16)

    o_ref[...] = jnp.dot(h_scr[...], w_ref[...], preferred_element_type=F32).astype(o_ref.dtype)


def _modmm(x2d, sc, sh, w, *, tm, tn, out_dtype=F32):
    m, d = x2d.shape
    ng = w.shape[1]
    r = sc.shape[0]
    tm = min(tm, m // r)
    tn = min(tn, ng)
    tiles_per_mod = (m // r) // tm
    return pl.pallas_call(
        _modmm_kernel,
        grid=(m // tm, ng // tn),
        in_specs=[pl.BlockSpec((tm, d), lambda i, j: (i, 0)),
                  pl.BlockSpec((1, 1, d), lambda i, j: (i // tiles_per_mod, 0, 0)),
                  pl.BlockSpec((1, 1, d), lambda i, j: (i // tiles_per_mod, 0, 0)),
                  pl.BlockSpec((d, tn), lambda i, j: (0, j))],
        out_specs=pl.BlockSpec((tm, tn), lambda i, j: (i, j)),
        out_shape=jax.ShapeDtypeStruct((m, ng), out_dtype),
        scratch_shapes=[pltpu.VMEM((tm, d), BF16)],
        compiler_params=_cparams(("parallel", "arbitrary")),
    )(x2d, sc, sh, w)


def _row_conv(x, cw, pos, row_w):
    rows = x.shape[0]
    k = cw.shape[0]
    pad = k // 2
    acc = None
    for j in range(k):
        off = j - pad
        if off == 0:
            term = x * cw[j:j + 1]
        else:
            shifted = pltpu.roll(x, (-off) % rows, axis=0)
            valid = (pos >= -off) if off < 0 else (pos < row_w - off)
            term = jnp.where(valid, shifted, 0.0) * cw[j:j + 1]
        acc = term if acc is None else acc + term
    return acc


SOLVE_BLOCK = 16


def _tri_solve(m_strict, rhs, lower):
    c = m_strict.shape[0]
    sb = SOLVE_BLOCK
    nb = c // sb
    ri = lax.broadcasted_iota(jnp.int32, (c, c), 0) // sb
    ci = lax.broadcasted_iota(jnp.int32, (c, c), 1) // sb
    m_off = jnp.where(ri != ci, m_strict, 0.0).astype(BF16)
    blocks = [rhs[i * sb:(i + 1) * sb] for i in range(nb)]
    order = list(range(nb)) if lower else list(range(nb - 1, -1, -1))
    for pos, bi in enumerate(order):
        blk = blocks[bi]
        if pos > 0:
            xcur = jnp.concatenate(blocks, axis=0).astype(BF16)
            blk = blk - jnp.dot(m_off[bi * sb:(bi + 1) * sb], xcur, preferred_element_type=F32)
        mrows = m_strict[bi * sb:(bi + 1) * sb]
        steps = range(sb - 1) if lower else range(sb - 1, 0, -1)
        for jj in steps:
            col = bi * sb + jj
            blk = blk - mrows[:, col:col + 1] * blk[jj:jj + 1, :]
        blocks[bi] = blk
    return jnp.concatenate(blocks, axis=0)


def _dn_prep_kernel(xq_ref, xk_ref, xv_ref, cq_ref, ck_ref, cv_ref, ab_ref, alog_ref, dt_ref,
                    u_ref, w_ref, qg_ref, kd_ref, attn_ref, eg_ref, *, row_w, n_heads, dk):
    h = pl.program_id(1)
    tt = xq_ref.shape[1]
    c = DN_CHUNK
    rows = lax.broadcasted_iota(jnp.int32, (tt, 1), 0)
    pos = rows % row_w
    cpos = rows % c

    def conv_silu(x_ref, cw_ref):
        return _silu(_row_conv(x_ref[0], cw_ref[...], pos, row_w))

    def l2n(y):
        return y * lax.rsqrt(jnp.sum(y * y, axis=-1, keepdims=True) + L2_EPS)

    q = l2n(conv_silu(xq_ref, cq_ref)) * (dk ** -0.5)
    k = l2n(conv_silu(xk_ref, ck_ref))
    v = conv_silu(xv_ref, cv_ref)

    ab = ab_ref[0]
    z = ab + dt_ref[...]
    softplus = jnp.maximum(z, 0.0) + jnp.log1p(jnp.exp(-jnp.abs(z)))
    g_all = -jnp.exp(alog_ref[...]) * softplus
    beta_all = jax.nn.sigmoid(ab)
    gf = g_all
    gb = g_all
    d = 1
    while d < c:
        gf = gf + jnp.where(cpos >= d, pltpu.roll(gf, d, axis=0), 0.0)
        gb = gb + jnp.where(cpos < c - d, pltpu.roll(gb, (tt - d) % tt, axis=0), 0.0)
        d *= 2
    lane = lax.broadcasted_iota(jnp.int32, (1, LANES), 1)

    def col(a, l):
        return jnp.sum(jnp.where(lane == l, a, 0.0), axis=1, keepdims=True)

    g_cols = (col(gf, h), col(gb, 2 * n_heads + h))
    b_cols = (col(beta_all, n_heads + h), col(beta_all, 3 * n_heads + h))

    ri = lax.broadcasted_iota(jnp.int32, (c, c), 0)
    ci = lax.broadcasted_iota(jnp.int32, (c, c), 1)
    eye = ri == ci
    eg_ref[...] = jnp.zeros(eg_ref.shape, F32)
    for cc in range(tt // c):
        sl = slice(cc * c, (cc + 1) * c)
        qc, kc, vc = q[sl], k[sl], v[sl]
        kcb = kc.astype(BF16)
        for dr in range(2):
            incl = (ri >= ci) if dr == 0 else (ri <= ci)
            strict = (ri > ci) if dr == 0 else (ri < ci)
            gcol = g_cols[dr][sl]
            bcol = b_cols[dr][sl]
            grow = jnp.sum(jnp.where(eye, gcol, 0.0), axis=0, keepdims=True)
            decay = jnp.where(incl, jnp.exp(jnp.where(incl, gcol - grow, 0.0)), 0.0)
            kb = kc * bcol
            vb = vc * bcol
            kq = lax.dot_general(jnp.concatenate([kb, qc], axis=0).astype(BF16), kcb,
                                 (((1,), (1,)), ((), ())), preferred_element_type=F32)
            m_strict = jnp.where(strict, kq[:c] * decay, 0.0)
            attn = kq[c:] * decay
            eg_col = jnp.exp(gcol)
            sol = _tri_solve(m_strict, jnp.concatenate([vb, kb * eg_col], axis=1), lower=(dr == 0))
            glast = gcol[c - 1:c] if dr == 0 else gcol[0:1]
            u_ref[dr, 0, sl, :] = sol[:, :dk]
            w_ref[dr, 0, sl, :] = sol[:, dk:].astype(BF16)
            qg_ref[dr, 0, sl, :] = (qc * eg_col).astype(BF16)
            kd_ref[dr, 0, sl, :] = (kc * jnp.exp(glast - gcol)).astype(BF16)
            attn_ref[dr, 0, 0, sl, :] = attn.astype(BF16)
            eg_ref[dr, 0, 0, 0, cc:cc + 1, :] = jnp.broadcast_to(jnp.exp(glast), (1, LANES))


def _dn_prep(qkv, conv_w, ab, alog_row, dt_row, *, n_heads, dk, row_w, tt):
    b, n, _ = qkv.shape
    hh = n_heads
    ng = n // tt
    c = DN_CHUNK
    assert tt % row_w == 0 and tt % c == 0 and tt // c <= SUBLANES
    kern = functools.partial(_dn_prep_kernel, row_w=row_w, n_heads=hh, dk=dk)
    xspec = lambda off: pl.BlockSpec((1, tt, dk), lambda bi, h, g: (bi, g, off + h))
    cspec = lambda off: pl.BlockSpec((conv_w.shape[0], dk), lambda bi, h, g: (0, off + h))
    ospec = pl.BlockSpec((2, 1, tt, dk), lambda bi, h, g: (0, bi, g, h))
    return pl.pallas_call(
        kern,
        grid=(b, hh, ng),
        in_specs=[xspec(0), xspec(hh), xspec(2 * hh), cspec(0), cspec(hh), cspec(2 * hh),
                  pl.BlockSpec((1, tt, LANES), lambda bi, h, g: (bi, g, 0)),
                  pl.BlockSpec((1, LANES), lambda bi, h, g: (0, 0)),
                  pl.BlockSpec((1, LANES), lambda bi, h, g: (0, 0))],
        out_specs=[ospec, ospec, ospec, ospec,
                   pl.BlockSpec((2, 1, 1, tt, c), lambda bi, h, g: (0, bi, h, g, 0)),
                   pl.BlockSpec((2, 1, 1, 1, SUBLANES, LANES), lambda bi, h, g: (0, bi, h, g, 0, 0))],
        out_shape=[jax.ShapeDtypeStruct((2, b, n, hh * dk), F32),
                   jax.ShapeDtypeStruct((2, b, n, hh * dk), BF16),
                   jax.ShapeDtypeStruct((2, b, n, hh * dk), BF16),
                   jax.ShapeDtypeStruct((2, b, n, hh * dk), BF16),
                   jax.ShapeDtypeStruct((2, b, hh, n, c), BF16),
                   jax.ShapeDtypeStruct((2, b, hh, ng, SUBLANES, LANES), F32)],
        compiler_params=_cparams(("parallel", "parallel", "arbitrary")),
    )(qkv, qkv, qkv, conv_w, conv_w, conv_w, ab, alog_row, dt_row)


def _dn_scan_kernel(uf_ref, wf_ref, qf_ref, kf_ref, af_ref, ef_ref,
                    ub_ref, wb_ref, qb_ref, kb_ref, ab_ref, eb_ref, s0_ref,
                    of_ref, ob_ref, sout_ref, s_scr, *, hp, dk):
    g = pl.program_id(2)
    ng = pl.num_programs(2)
    c = DN_CHUNK
    tt = uf_ref.shape[2]
    ncs = tt // c

    @pl.when(g == 0)
    def _():
        s_scr[...] = s0_ref[:, 0]

    refs = ((uf_ref, wf_ref, qf_ref, kf_ref, af_ref, ef_ref, of_ref),
            (ub_ref, wb_ref, qb_ref, kb_ref, ab_ref, eb_ref, ob_ref))
    for step in range(ncs):
        for dr in range(2):
            u_ref, w_ref, q_ref, k_ref, a_ref, e_ref, o_ref = refs[dr]
            cc = step if dr == 0 else ncs - 1 - step
            sl = slice(cc * c, (cc + 1) * c)
            for hh in range(hp):
                hs = slice(hh * dk, (hh + 1) * dk)
                s = s_scr[dr, hh]
                wq = jnp.concatenate([w_ref[0, 0, sl, hs], q_ref[0, 0, sl, hs]], axis=0)
                r = jnp.dot(wq, s.astype(BF16), preferred_element_type=F32)
                vnew = u_ref[0, 0, sl, hs] - r[:c]
                vnb = vnew.astype(BF16)
                o_ref[0, sl, hs] = r[c:] + jnp.dot(a_ref[0, 0, hh, sl, :], vnb, preferred_element_type=F32)
                upd = lax.dot_general(k_ref[0, 0, sl, hs], vnb, (((0,), (0,)), ((), ())),
                                      preferred_element_type=F32)
                s_scr[dr, hh] = s * e_ref[0, 0, hh, 0, cc:cc + 1, :] + upd

    @pl.when(g == ng - 1)
    def _():
        sout_ref[:, 0] = s_scr[...]


def _dn_scan(u, w, qg, kd, attn, eg, s0, *, dk, tt, hp):
    _, b, n, hd = u.shape
    hh = hd // dk
    ng = n // tt
    c = DN_CHUNK
    kern = functools.partial(_dn_scan_kernel, hp=hp, dk=dk)

    def specs(dr):
        gi = (lambda g: g) if dr == 0 else (lambda g: ng - 1 - g)
        tok = pl.BlockSpec((1, 1, tt, hp * dk), lambda bi, h, g: (dr, bi, gi(g), h))
        return [tok, tok, tok, tok,
                pl.BlockSpec((1, 1, hp, tt, c), lambda bi, h, g: (dr, bi, h, gi(g), 0)),
                pl.BlockSpec((1, 1, hp, 1, SUBLANES, LANES), lambda bi, h, g: (dr, bi, h, gi(g), 0, 0))]

    sspec = pl.BlockSpec((2, 1, hp, dk, dk), lambda bi, h, g: (0, bi, h, 0, 0))
    return pl.pallas_call(
        kern,
        grid=(b, hh // hp, ng),
        in_specs=specs(0) + specs(1) + [sspec],
        out_specs=[pl.BlockSpec((1, tt, hp * dk), lambda bi, h, g: (bi, g, h)),
                   pl.BlockSpec((1, tt, hp * dk), lambda bi, h, g: (bi, ng - 1 - g, h)),
                   sspec],
        out_shape=[jax.ShapeDtypeStruct((b, n, hd), F32),
                   jax.ShapeDtypeStruct((b, n, hd), F32),
                   jax.ShapeDtypeStruct((2, b, hh, dk, dk), F32)],
        scratch_shapes=[pltpu.VMEM((2, hp, dk, dk), F32)],
        compiler_params=_cparams(("parallel", "parallel", "arbitrary")),
    )(u, w, qg, kd, attn, eg, u, w, qg, kd, attn, eg, s0)


def _hy_hidden_kernel(z_ref, w1_ref, b1_ref, w2_ref, b2_ref, fr_ref, o_ref):
    fr = fr_ref[...]
    h1 = jnp.sin(fr * (jnp.dot(z_ref[...].astype(BF16), w1_ref[...].astype(BF16),
                               preferred_element_type=F32) + b1_ref[...]))
    o_ref[...] = jnp.sin(fr * (jnp.dot(h1.astype(BF16), w2_ref[...].astype(BF16),
                                       preferred_element_type=F32) + b2_ref[...]))


def _hy_filter_kernel(hid_ref, w3_ref, b3_ref, t_ref, delta_ref, o_ref):
    f = jnp.dot(hid_ref[...].astype(BF16), w3_ref[...].astype(BF16), preferred_element_type=F32) + b3_ref[...]
    o_ref[...] = f * jnp.exp(-t_ref[...] * delta_ref[...])


def _hy_filters(n, w1, b1, w2, b2, w3, b3, freq, hy_w):
    emb, hid = w1.shape
    bands = (emb - 1) // 2
    t = jnp.linspace(0.0, 1.0, n, dtype=F32)
    pos = jnp.arange(n, dtype=F32)
    bnd = jnp.linspace(1e-4, bands - 1, bands, dtype=F32)
    ang = (2.0 * math.pi / n) * pos[:, None] * bnd[None, :]
    z = jnp.concatenate([t[:, None], jnp.cos(ang), -jnp.sin(ang)], axis=-1)
    embp = -(-emb // SUBLANES) * SUBLANES
    z = jnp.pad(z, ((0, 0), (0, embp - emb)))
    w1p = jnp.pad(w1, ((0, embp - emb), (0, 0)))
    full = lambda a: pl.BlockSpec(a.shape, lambda: (0,) * a.ndim)
    args = (z, w1p, b1[None, :], w2, b2[None, :], freq[None, :])
    hid2 = pl.pallas_call(
        _hy_hidden_kernel,
        in_specs=[full(a) for a in args],
        out_specs=pl.BlockSpec((n, hid), lambda: (0, 0)),
        out_shape=jax.ShapeDtypeStruct((n, hid), F32),
        compiler_params=pltpu.CompilerParams(vmem_limit_bytes=VMEM_LIMIT_BYTES),
    )(*args)
    nout = w3.shape[1]
    deltas = jnp.abs(jnp.linspace(math.log(HY_DECAY_TARGET) / HY_SLOW_DECAY,
                                  math.log(HY_DECAY_TARGET) / HY_FAST_DECAY, hy_w, dtype=F32))
    delta_row = jnp.tile(deltas, nout // hy_w)[None, :]
    tc = min(nout, 1024)
    return pl.pallas_call(
        _hy_filter_kernel,
        grid=(nout // tc,),
        in_specs=[pl.BlockSpec((n, hid), lambda j: (0, 0)),
                  pl.BlockSpec((hid, tc), lambda j: (0, j)),
                  pl.BlockSpec((1, tc), lambda j: (0, j)),
                  pl.BlockSpec((n, 1), lambda j: (0, 0)),
                  pl.BlockSpec((1, tc), lambda j: (0, j))],
        out_specs=pl.BlockSpec((n, tc), lambda j: (0, j)),
        out_shape=jax.ShapeDtypeStruct((n, nout), F32),
        compiler_params=_cparams(("parallel",)),
    )(hid2, w3, b3[None, :], t[:, None], delta_row)


def _dft_tables(n):
    n2 = FFT_N2
    big = 2 * n
    n1 = big // n2
    k1 = n1 // 2
    two_pi = 2.0 * math.pi

    def cs(idx, period):
        a = (idx % period).astype(F32) * (two_pi / period)
        return jnp.cos(a), jnp.sin(a)

    f1 = jnp.arange(n1, dtype=jnp.int32)
    s1 = jnp.arange(k1, dtype=jnp.int32)
    s2 = jnp.arange(n2, dtype=jnp.int32)
    s_full = s1[None, None, :] * n2 + s2[:, None, None]
    c, s = cs(f1[None, :, None] * s_full, big)
    f1tw = jnp.concatenate([jnp.concatenate([c, s], axis=2), jnp.concatenate([-s, c], axis=2)], axis=1)
    ct, st = jnp.swapaxes(c, 1, 2), jnp.swapaxes(s, 1, 2)
    g1tw = jnp.concatenate([jnp.concatenate([ct, -st], axis=2), jnp.concatenate([st, ct], axis=2)], axis=1) / big
    c2, sn2 = cs(s2[:, None] * s2[None, :], n2)
    f2 = jnp.concatenate([jnp.concatenate([c2, sn2], axis=1), jnp.concatenate([-sn2, c2], axis=1)], axis=0)
    g2 = jnp.concatenate([jnp.concatenate([c2, -sn2], axis=1), jnp.concatenate([sn2, c2], axis=1)], axis=0)
    return f1tw.astype(BF16), f2.astype(BF16), g2.astype(BF16), g1tw.astype(BF16)


def _hy_conv_kernel(a_ref, m_ref, hf_ref, hb_ref, skip_ref, cwa_ref, cwm_ref,
                    f1_ref, f2_ref, g2_ref, g1_ref, o_ref,
                    h_scr, b_scr, zr_scr, zi_scr, *, conv_a, n1, k1):
    n2 = FFT_N2
    k1p = k1 + FFT_ROW_PAD
    pb = 2 * n2 + FFT_ROW_PAD
    pos = lax.broadcasted_iota(jnp.int32, (n2, 1), 0)

    def stage1(real_only):
        def body(s2, carry):
            base = pl.multiple_of(s2 * k1p, SUBLANES)
            if real_only:
                x = zr_scr[pl.ds(base, k1), :].astype(BF16)
                a = jnp.dot(f1_ref[s2, :, :k1], x, preferred_element_type=F32)
            else:
                x = jnp.concatenate([zr_scr[pl.ds(base, k1), :], zi_scr[pl.ds(base, k1), :]], axis=0)
                a = jnp.dot(f1_ref[s2], x.astype(BF16), preferred_element_type=F32)
            b_scr[pl.ds(s2, n1, stride=pb), :] = a[:n1]
            b_scr[pl.ds(n2 + s2, n1, stride=pb), :] = a[n1:]
            return carry
        lax.fori_loop(0, n2, body, 0)

    @pl.when(pl.program_id(1) == 0)
    def _():
        for which, ref in ((0, hf_ref), (1, hb_ref)):
            def load(s1, carry, ref=ref):
                r0 = pl.multiple_of(s1 * n2, n2)
                zr_scr[pl.ds(s1, n2, stride=k1p), :] = ref[pl.ds(r0, n2), :]
                return carry
            lax.fori_loop(0, k1, load, 0)
            if which == 1:
                zr_scr[0:1, :] = jnp.zeros((1, zr_scr.shape[1]), F32)
            stage1(True)

            def spec(f1, carry, which=which):
                bb = pl.multiple_of(f1 * pb, SUBLANES)
                hb_ = pl.multiple_of(f1 * 2 * n2, 2 * n2)
                zz = jnp.dot(f2_ref[...], b_scr[pl.ds(bb, 2 * n2), :].astype(BF16), preferred_element_type=F32)
                if which == 0:
                    h_scr[pl.ds(hb_, 2 * n2), :] = zz
                else:
                    h_scr[pl.ds(hb_, n2), :] = h_scr[pl.ds(hb_, n2), :] + zz[:n2]
                    h_scr[pl.ds(hb_ + n2, n2), :] = h_scr[pl.ds(hb_ + n2, n2), :] - zz[n2:]
                return carry
            lax.fori_loop(0, n1, spec, 0)

    def load_a(s1, carry):
        r0 = pl.multiple_of(s1 * n2, n2)
        for bi, scr in ((0, zr_scr), (1, zi_scr)):
            blk = a_ref[bi, pl.ds(r0, n2), :]
            if conv_a:
                blk = _row_conv(blk, cwa_ref[...], pos, n2)
            scr[pl.ds(s1, n2, stride=k1p), :] = blk
        return carry
    lax.fori_loop(0, k1, load_a, 0)

    stage1(False)

    def mid(f1, carry):
        bb = pl.multiple_of(f1 * pb, SUBLANES)
        hb_ = pl.multiple_of(f1 * 2 * n2, 2 * n2)
        zz = jnp.dot(f2_ref[...], b_scr[pl.ds(bb, 2 * n2), :].astype(BF16), preferred_element_type=F32)
        zr, zi = zz[:n2], zz[n2:]
        hr = h_scr[pl.ds(hb_, n2), :]
        hi = h_scr[pl.ds(hb_ + n2, n2), :]
        yy = jnp.concatenate([zr * hr - zi * hi, zr * hi + zi * hr], axis=0).astype(BF16)
        b_scr[pl.ds(bb, 2 * n2), :] = jnp.dot(g2_ref[...], yy, preferred_element_type=F32)
        return carry
    lax.fori_loop(0, n1, mid, 0)

    skip = skip_ref[...]

    def last(s2, carry):
        base = pl.multiple_of(s2 * k1p, SUBLANES)
        ww = jnp.concatenate([b_scr[pl.ds(s2, n1, stride=pb), :],
                              b_scr[pl.ds(n2 + s2, n1, stride=pb), :]], axis=0).astype(BF16)
        y = jnp.dot(g1_ref[s2], ww, preferred_element_type=F32)
        zr_scr[pl.ds(base, k1), :] = y[:k1] + zr_scr[pl.ds(base, k1), :] * skip
        zi_scr[pl.ds(base, k1), :] = y[k1:] + zi_scr[pl.ds(base, k1), :] * skip
        return carry
    lax.fori_loop(0, n2, last, 0)

    def fin(s1, carry):
        r0 = pl.multiple_of(s1 * n2, n2)
        for bi, scr in ((0, zr_scr), (1, zi_scr)):
            mm = _row_conv(m_ref[bi, pl.ds(r0, n2), :], cwm_ref[...], pos, n2)
            o_ref[bi, pl.ds(r0, n2), :] = mm * scr[pl.ds(s1, n2, stride=k1p), :]
        return carry
    lax.fori_loop(0, k1, fin, 0)


def _hy_conv(a, a_col0, m, m_col0, filt, f_col_fwd, f_col_bwd, skip, cwa, cwm, tables, *, conv_a, hy_w):
    b, n, _ = a.shape
    assert b % 2 == 0 and GRID_W == FFT_N2 and n % FFT_N2 == 0
    n2 = FFT_N2
    n1 = 2 * n // n2
    k1 = n1 // 2
    k1p = k1 + FFT_ROW_PAD
    pb = 2 * n2 + FFT_ROW_PAD
    cw = LANES
    f1tw, f2, g2, g1tw = tables
    kern = functools.partial(_hy_conv_kernel, conv_a=conv_a, n1=n1, k1=k1)
    once = pl.Buffered(1)
    return pl.pallas_call(
        kern,
        grid=(hy_w // cw, b // 2),
        in_specs=[pl.BlockSpec((2, n, cw), lambda c, p: (p, 0, a_col0 + c), pipeline_mode=once),
                  pl.BlockSpec((2, n, cw), lambda c, p: (p, 0, m_col0 + c), pipeline_mode=once),
                  pl.BlockSpec((n, cw), lambda c, p: (0, f_col_fwd + c), pipeline_mode=once),
                  pl.BlockSpec((n, cw), lambda c, p: (0, f_col_bwd + c), pipeline_mode=once),
                  pl.BlockSpec((1, cw), lambda c, p: (0, c)),
                  pl.BlockSpec((cwa.shape[0], cw), lambda c, p: (0, a_col0 + c if conv_a else 0)),
                  pl.BlockSpec((cwm.shape[0], cw), lambda c, p: (0, m_col0 + c)),
                  pl.BlockSpec(f1tw.shape, lambda c, p: (0, 0, 0), pipeline_mode=once),
                  pl.BlockSpec(f2.shape, lambda c, p: (0, 0)),
                  pl.BlockSpec(g2.shape, lambda c, p: (0, 0)),
                  pl.BlockSpec(g1tw.shape, lambda c, p: (0, 0, 0), pipeline_mode=once)],
        out_specs=pl.BlockSpec((2, n, cw), lambda c, p: (p, 0, c)),
        out_shape=jax.ShapeDtypeStruct((b, n, hy_w), F32),
        scratch_shapes=[pltpu.VMEM((n1 * 2 * n2, cw), F32),
                        pltpu.VMEM((n1 * pb, cw), F32),
                        pltpu.VMEM((n2 * k1p, cw), F32),
                        pltpu.VMEM((n2 * k1p, cw), F32)],
        compiler_params=_cparams(("parallel", "arbitrary")),
    )(a, m, filt, filt, skip, cwa, cwm, f1tw, f2, g2, g1tw)


def _merge_kernel(of_ref, ob_ref, z_ref, y_ref, ga_ref, gb_ref, nw_ref, wa_ref, wh_ref, o_ref,
                  on_scr, yb_scr, *, dv):
    @pl.when(pl.program_id(1) == 0)
    def _():
        nw = nw_ref[...]
        for hh in range(of_ref.shape[1] // dv):
            hs = slice(hh * dv, (hh + 1) * dv)
            o = of_ref[:, hs] + ob_ref[:, hs]
            o = o * lax.rsqrt(jnp.mean(o * o, axis=-1, keepdims=True) + RMS_EPS) * nw
            on_scr[:, hs] = (o * _silu(z_ref[:, hs])).astype(BF16)
        yb_scr[...] = y_ref[...].astype(BF16)

    pa = jnp.dot(on_scr[...], wa_ref[...], preferred_element_type=F32)
    ph = jnp.dot(yb_scr[...], wh_ref[...], preferred_element_type=F32)
    o_ref[...] = (jax.nn.sigmoid(ga_ref[...]) * pa + jax.nn.sigmoid(gb_ref[...]) * ph).astype(o_ref.dtype)


def _merge(o_f, o_b, z, z_blk, y_hy, gate, norm_w, w_pa, w_ph, *, dv, tm, tn):
    m, dvw = o_f.shape
    d = w_pa.shape[1]
    tm = min(tm, m)
    tn = min(tn, d)
    nj = d // tn
    kern = functools.partial(_merge_kernel, dv=dv)
    row = lambda w_: pl.BlockSpec((tm, w_), lambda i, j: (i, 0))
    return pl.pallas_call(
        kern,
        grid=(m // tm, nj),
        in_specs=[row(dvw), row(dvw), pl.BlockSpec((tm, dvw), lambda i, j: (i, z_blk)), row(y_hy.shape[1]),
                  pl.BlockSpec((tm, tn), lambda i, j: (i, j)),
                  pl.BlockSpec((tm, tn), lambda i, j: (i, nj + j)),
                  pl.BlockSpec((1, dv), lambda i, j: (0, 0)),
                  pl.BlockSpec((dvw, tn), lambda i, j: (0, j)),
                  pl.BlockSpec((y_hy.shape[1], tn), lambda i, j: (0, j))],
        out_specs=pl.BlockSpec((tm, tn), lambda i, j: (i, j)),
        out_shape=jax.ShapeDtypeStruct((m, d), BF16),
        scratch_shapes=[pltpu.VMEM((tm, dvw), BF16), pltpu.VMEM((tm, y_hy.shape[1]), BF16)],
        compiler_params=_cparams(("parallel", "arbitrary")),
    )(o_f, o_b, z, y_hy, gate, gate, norm_w, w_pa, w_ph)


def _layer_norm(x, g, b):
    mu = jnp.mean(x, axis=-1, keepdims=True)
    xc = x - mu
    var = jnp.mean(xc * xc, axis=-1, keepdims=True)
    return xc * lax.rsqrt(var + LN_EPS) * g + b


def _outproj_kernel(mg_ref, x_ref, wo_ref, g1_ref, lg_ref, lb_ref, sc_ref, sh_ref, wq_ref, sk_ref,
                    x1_ref, h2_ref, s_ref, *, half):
    mix = jnp.dot(mg_ref[...], wo_ref[...], preferred_element_type=F32)
    x1 = _layer_norm(DEEPNORM_ALPHA * x_ref[...] + g1_ref[0] * mix, lg_ref[...], lb_ref[...])
    x1_ref[...] = x1
    h2 = (x1 * (1.0 + sc_ref[0]) + sh_ref[0]).astype(BF16)
    h2_ref[...] = h2
    qp = jnp.dot(h2, wq_ref[...], preferred_element_type=F32).astype(BF16)
    for j in range(sk_ref.shape[0]):
        js = slice(j * half, (j + 1) * half)
        s_ref[:, j * LANES:(j + 1) * LANES] = lax.dot_general(
            qp[:, js], sk_ref[j], (((1,), (1,)), ((), ())), preferred_element_type=F32)


def _outproj(merged, x2d, w_out, g1, ln_g, ln_b, sc2, sh2, wq, subkeys, *, tm):
    m, d = x2d.shape
    r = g1.shape[0]
    tm = min(tm, m // r)
    tpm = (m // r) // tm
    nsk, nkeys, half = subkeys.shape
    assert nkeys == LANES
    kern = functools.partial(_outproj_kernel, half=half)
    row = lambda w_: pl.BlockSpec((tm, w_), lambda i: (i, 0))
    mod = pl.BlockSpec((1, 1, d), lambda i: (i // tpm, 0, 0))
    vec = pl.BlockSpec((1, d), lambda i: (0, 0))
    return pl.pallas_call(
        kern,
        grid=(m // tm,),
        in_specs=[row(d), row(d), pl.BlockSpec(w_out.shape, lambda i: (0, 0), pipeline_mode=pl.Buffered(1)),
                  mod, vec, vec, mod, mod,
                  pl.BlockSpec(wq.shape, lambda i: (0, 0), pipeline_mode=pl.Buffered(1)),
                  pl.BlockSpec(subkeys.shape, lambda i: (0, 0, 0))],
        out_specs=[row(d), row(d), row(nsk * LANES)],
        out_shape=[jax.ShapeDtypeStruct((m, d), F32),
                   jax.ShapeDtypeStruct((m, d), BF16),
                   jax.ShapeDtypeStruct((m, nsk * LANES), F32)],
        compiler_params=_cparams(("parallel",)),
    )(merged, x2d, w_out, g1, ln_g, ln_b, sc2, sh2, wq, subkeys)


def _cand_tables():
    pairs = [(a, b) for a in range(PEER_TOPK) for b in range(PEER_TOPK) if (a + 1) * (b + 1) <= PEER_TOPK]
    ia = jnp.array([p[0] for p in pairs], dtype=jnp.int32)
    ib = jnp.array([p[1] for p in pairs], dtype=jnp.int32)
    lanes = jnp.arange(LANES, dtype=jnp.int32)
    ncand = len(pairs)
    p1 = (lanes[:, None] == jnp.pad(ia, (0, LANES - ncand), constant_values=-1)[None, :]).astype(F32)
    p2 = (lanes[:, None] == jnp.pad(ib, (0, LANES - ncand), constant_values=-1)[None, :]).astype(F32)
    return p1, p2, ncand


def _peer_topk_kernel(s_ref, p1_ref, p2_ref, p1t_ref, c1_ref, r2_ref, e1_ref, e2_ref, *, n_heads, ncand):
    lane = lax.broadcasted_iota(jnp.int32, (1, LANES), 1).astype(F32)
    neg = -jnp.inf
    hi = lax.Precision.HIGHEST
    no_rank = float(LANES)

    def pop_max(x):
        mx = jnp.max(x, axis=-1, keepdims=True)
        first = jnp.min(jnp.where(x == mx, lane, no_rank), axis=-1, keepdims=True)
        hit = lane == first
        return mx, hit, jnp.where(hit, neg, x)

    def sorted_top(s):
        t = jnp.zeros(s.shape, F32)
        rank = jnp.full(s.shape, no_rank, F32)
        x = s
        for a in range(PEER_TOPK):
            mx, hit, x = pop_max(x)
            t = jnp.where(lane == a, mx, t)
            rank = jnp.where(hit, float(a), rank)
        return t, rank

    for h in range(n_heads):
        s1 = s_ref[:, (2 * h) * LANES:(2 * h + 1) * LANES]
        s2 = s_ref[:, (2 * h + 1) * LANES:(2 * h + 2) * LANES]
        t1, rank1 = sorted_top(s1)
        t2, rank2 = sorted_top(s2)
        cand = (jnp.dot(t1, p1_ref[...], precision=hi, preferred_element_type=F32)
                + jnp.dot(t2, p2_ref[...], precision=hi, preferred_element_type=F32))
        x = jnp.where(lane < ncand, cand, neg)
        sel = jnp.zeros(s1.shape, F32)
        zsum = None
        cmax = None
        for r in range(PEER_TOPK):
            mx, hit, x = pop_max(x)
            sel = jnp.where(hit, 1.0, sel)
            if r == 0:
                cmax = mx
                zsum = jnp.ones_like(mx)
            else:
                zsum = zsum + jnp.exp(mx - cmax)
        cnt = jnp.dot(sel.astype(BF16), p1t_ref[...].astype(BF16), preferred_element_type=F32)
        count1 = jnp.zeros(s1.shape, F32)
        for a in range(PEER_TOPK):
            cnt_a = jnp.sum(jnp.where(lane == a, cnt, 0.0), axis=-1, keepdims=True)
            count1 = jnp.where(rank1 == a, cnt_a, count1)
        m1 = jnp.sum(jnp.where(lane == 0, t1, 0.0), axis=-1, keepdims=True)
        m2 = jnp.sum(jnp.where(lane == 0, t2, 0.0), axis=-1, keepdims=True)
        hs = slice(h * LANES, (h + 1) * LANES)
        c1_ref[:, hs] = count1
        r2_ref[:, hs] = rank2
        e1_ref[:, hs] = jnp.exp(s1 - m1) / zsum
        e2_ref[:, hs] = jnp.exp(s2 - m2)


def _peer_topk(scores, n_heads, *, tm):
    m, w = scores.shape
    tm = min(tm, m)
    p1, p2, ncand = _cand_tables()
    kern = functools.partial(_peer_topk_kernel, n_heads=n_heads, ncand=ncand)
    sq = pl.BlockSpec((LANES, LANES), lambda i: (0, 0))
    out = pl.BlockSpec((tm, n_heads * LANES), lambda i: (i, 0))
    shp = jax.ShapeDtypeStruct((m, n_heads * LANES), F32)
    return pl.pallas_call(
        kern,
        grid=(m // tm,),
        in_specs=[pl.BlockSpec((tm, w), lambda i: (i, 0)), sq, sq, sq],
        out_specs=[out, out, out, out],
        out_shape=[shp, shp, shp, shp],
        compiler_params=_cparams(("parallel",)),
    )(scores, p1, p2, p1.T)


def _peer_dense_kernel(h2_ref, u_ref, v_ref, r2_ref, e2_ref, c1_ref, e1_ref, x1_ref, g2_ref, lg_ref, lb_ref,
                       o_ref, acc_scr, p_scr, *, n_heads, ti):
    e = pl.program_id(1)

    @pl.when(e == 0)
    def _():
        acc_scr[...] = jnp.zeros(acc_scr.shape, F32)

    a = lax.dot_general(h2_ref[...], u_ref[...], (((1,), (1,)), ((), ())), preferred_element_type=F32)
    act = 0.5 * a * (1.0 + lax.erf(a * (2.0 ** -0.5)))
    for ii in range(ti):
        gate = None
        for h in range(n_heads):
            l = ii * n_heads + h
            hs = slice(h * LANES, (h + 1) * LANES)
            term = jnp.where(r2_ref[:, hs] < c1_ref[0, :, l:l + 1], e2_ref[:, hs], 0.0) * e1_ref[0, :, l:l + 1]
            gate = term if gate is None else gate + term
        js = slice(ii * LANES, (ii + 1) * LANES)
        p_scr[:, js] = (gate * act[:, js]).astype(BF16)
    acc_scr[...] += jnp.dot(p_scr[...], v_ref[...], preferred_element_type=F32)

    @pl.when(e == pl.num_programs(1) - 1)
    def _():
        o_ref[...] = _layer_norm(DEEPNORM_ALPHA * x1_ref[...] + g2_ref[0] * acc_scr[...],
                                 lg_ref[...], lb_ref[...])


def _peer_dense(h2, u_tab, v_tab, rank2, e2, count1_t, e1_t, x1, g2, ln_g, ln_b, *, n_heads, ti, tm):
    m, d = h2.shape
    r = g2.shape[0]
    tm = min(tm, m // r)
    tpm = (m // r) // tm
    ne = u_tab.shape[0]
    te = ti * LANES
    kern = functools.partial(_peer_dense_kernel, n_heads=n_heads, ti=ti)
    row = lambda w_: pl.BlockSpec((tm, w_), lambda i, e: (i, 0), pipeline_mode=pl.Buffered(1))
    vec = pl.BlockSpec((1, d), lambda i, e: (0, 0))
    tbl = pl.BlockSpec((te, d), lambda i, e: (e, 0))
    per_e = pl.BlockSpec((1, tm, LANES), lambda i, e: (e, i, 0))
    return pl.pallas_call(
        kern,
        grid=(m // tm, ne // te),
        in_specs=[row(d), tbl, tbl, row(rank2.shape[1]), row(e2.shape[1]), per_e, per_e, row(d),
                  pl.BlockSpec((1, 1, d), lambda i, e: (i // tpm, 0, 0)), vec, vec],
        out_specs=pl.BlockSpec((tm, d), lambda i, e: (i, 0)),
        out_shape=jax.ShapeDtypeStruct((m, d), F32),
        scratch_shapes=[pltpu.VMEM((tm, d), F32), pltpu.VMEM((tm, te), BF16)],
        compiler_params=_cparams(("parallel", "arbitrary")),
    )(h2, u_tab, v_tab, rank2, e2, count1_t, e1_t, x1, g2, ln_g, ln_b)


def _block(x, c, ctx, c_ctx, w_ada, b_ada, w_in, dn_conv_w, dn_a_log, dn_dt_bias, dn_norm_w, hy_conv_w, hy_w1, hy_b1, hy_w2, hy_b2, hy_w3, hy_b3, hy_freq, hy_skip, w_branch_dn, w_branch_hy, w_out, ln1_g, ln1_b, peer_wq, peer_subkeys, peer_u, peer_v, ln2_g, ln2_b):
    b, n, d = x.shape
    n_ctx = ctx.shape[1]
    assert w_ada.shape[0] == DEPTH == 1, "single-layer block: the context stream update is never read"
    hh = dn_a_log.shape[2]
    dk = dn_conv_w.shape[2] // (3 * hh)
    qk_w = hh * dk
    hy_w = hy_skip.shape[2]
    p_heads, _, nkeys, half = peer_subkeys.shape[1:]
    assert dk == LANES and nkeys == LANES and 4 * hh <= LANES and n % GRID_W == 0
    off_z = 3 * qk_w
    off_ab = 4 * qk_w
    off_hy = off_ab + 4 * hh
    off_gate = off_hy + 3 * hy_w
    assert w_in.shape[2] == off_gate + 2 * d

    cc = jnp.zeros((SUBLANES, d), F32).at[:b].set(c).at[b].set(c_ctx)
    mods = _ada(cc, w_ada[0], b_ada[0][None, :])
    sh1, sc1, g1, sh2, sc2, g2 = [mods[:b, i * d:(i + 1) * d][:, None, :] for i in range(6)]
    sh1c, sc1c = [mods[b:b + 1, i * d:(i + 1) * d][:, None, :] for i in range(2)]

    w = w_in[0]
    w_qkvz = w[:, :off_ab].astype(BF16)
    w_ab = jnp.pad(w[:, off_ab:off_hy], ((0, 0), (0, LANES - 4 * hh))).astype(BF16)
    w_hy = w[:, off_hy:off_gate].astype(BF16)
    w_gate = w[:, off_gate:].astype(BF16)

    x2d = x.reshape(b * n, d)
    c2d = ctx.reshape(b * n_ctx, d)
    proj_qkvz = _modmm(x2d, sc1, sh1, w_qkvz, tm=512, tn=2048)
    proj_ab = _modmm(x2d, sc1, sh1, w_ab, tm=512, tn=LANES)
    proj_hy = _modmm(x2d, sc1, sh1, w_hy, tm=512, tn=2048)
    proj_gate = _modmm(x2d, sc1, sh1, w_gate, tm=512, tn=2048)
    projc_qkvz = _modmm(c2d, sc1c, sh1c, w_qkvz, tm=256, tn=2048)
    projc_ab = _modmm(c2d, sc1c, sh1c, w_ab, tm=256, tn=LANES)

    def decay_row(a):
        row = jnp.concatenate([a, jnp.zeros_like(a)], axis=1).reshape(-1)
        return jnp.pad(row, (0, LANES - 4 * hh))[None, :]

    alog_row = decay_row(dn_a_log[0])
    dt_row = decay_row(dn_dt_bias[0])

    conv_w = dn_conv_w[0]
    prep_c = _dn_prep(projc_qkvz.reshape(b, n_ctx, -1), conv_w, projc_ab.reshape(b, n_ctx, LANES),
                      alog_row, dt_row, n_heads=hh, dk=dk, row_w=n_ctx, tt=n_ctx)
    prep_l = _dn_prep(proj_qkvz.reshape(b, n, -1), conv_w, proj_ab.reshape(b, n, LANES),
                      alog_row, dt_row, n_heads=hh, dk=dk, row_w=GRID_W, tt=4 * GRID_W)
    hp = 2 if hh % 2 == 0 else 1
    s_zero = jnp.zeros((2, b, hh, dk, dk), F32)
    _, _, s_ctx = _dn_scan(*prep_c, s_zero, dk=dk, tt=n_ctx, hp=hp)
    o_f, o_b, _ = _dn_scan(*prep_l, s_ctx, dk=dk, tt=4 * GRID_W, hp=hp)

    filt = _hy_filters(n, hy_w1[0], hy_b1[0], hy_w2[0], hy_b2[0], hy_w3[0], hy_b3[0], hy_freq[0], hy_w)
    tables = _dft_tables(n)
    hy3 = proj_hy.reshape(b, n, 3 * hy_w)
    cwh = hy_conv_w[0]
    cb = hy_w // LANES
    y1 = _hy_conv(hy3, 0, hy3, cb, filt, 0, 2 * cb, hy_skip[0, 0:1], cwh, cwh, tables, conv_a=True, hy_w=hy_w)
    y_hy = _hy_conv(y1, 0, hy3, 2 * cb, filt, cb, 3 * cb, hy_skip[0, 1:2], cwh, cwh, tables, conv_a=False, hy_w=hy_w)

    merged = _merge(o_f.reshape(b * n, qk_w), o_b.reshape(b * n, qk_w), proj_qkvz, off_z // qk_w,
                    y_hy.reshape(b * n, hy_w), proj_gate, dn_norm_w[0][None, :],
                    w_branch_dn[0].astype(BF16), w_branch_hy[0].astype(BF16), dv=dk, tm=256, tn=1024)
    sk = peer_subkeys[0].reshape(p_heads * 2, nkeys, half).astype(BF16)
    x1, h2, scores = _outproj(merged, x2d, w_out[0].astype(BF16), g1, ln1_g[0][None, :], ln1_b[0][None, :],
                              sc2, sh2, peer_wq[0].astype(BF16), sk, tm=256)

    count1, rank2, e1, e2 = _peer_topk(scores, p_heads, tm=256)
    ti = 8
    assert ti * p_heads <= LANES

    def rearr(a):
        a = a.reshape(b * n, p_heads, nkeys // ti, ti).transpose(2, 0, 3, 1).reshape(nkeys // ti, b * n, ti * p_heads)
        return jnp.pad(a, ((0, 0), (0, 0), (0, LANES - ti * p_heads)))

    out = _peer_dense(h2, peer_u[0].astype(BF16), peer_v[0].astype(BF16), rank2, e2, rearr(count1), rearr(e1),
                      x1, g2, ln2_g[0][None, :], ln2_b[0][None, :], n_heads=p_heads, ti=ti, tm=512)
    stages = dict(o_f=o_f, o_b=o_b, s_ctx=s_ctx, filt=filt, y1=y1, y_hy=y_hy, merged=merged, x1=x1, scores=scores,
                  count1=count1, rank2=rank2, e1=e1, e2=e2)
    return out.reshape(b, n, d), stages


def kernel(x, c, ctx, c_ctx, w_ada, b_ada, w_in, dn_conv_w, dn_a_log, dn_dt_bias, dn_norm_w, hy_conv_w, hy_w1, hy_b1, hy_w2, hy_b2, hy_w3, hy_b3, hy_freq, hy_skip, w_branch_dn, w_branch_hy, w_out, ln1_g, ln1_b, peer_wq, peer_subkeys, peer_u, peer_v, ln2_g, ln2_b):
    out, _ = _block(x, c, ctx, c_ctx, w_ada, b_ada, w_in, dn_conv_w, dn_a_log, dn_dt_bias, dn_norm_w, hy_conv_w, hy_w1, hy_b1, hy_w2, hy_b2, hy_w3, hy_b3, hy_freq, hy_skip, w_branch_dn, w_branch_hy, w_out, ln1_g, ln1_b, peer_wq, peer_subkeys, peer_u, peer_v, ln2_g, ln2_b)
    return out
```

```python
import functools
import math

import jax
import jax.numpy as jnp
from jax import lax
from jax.experimental import pallas as pl
from jax.experimental.pallas import tpu as pltpu

F32 = jnp.float32
BF16 = jnp.bfloat16

GRID_W = 64
DN_CHUNK = 64
PEER_TOPK = 16
DEPTH = 1
DEEPNORM_ALPHA = (2 * DEPTH) ** 0.25
LN_EPS = 1e-5
RMS_EPS = 1e-6
L2_EPS = 1e-6
HY_DECAY_TARGET = 1e-2
HY_FAST_DECAY = 0.3
HY_SLOW_DECAY = 1.5

LANES = 128
SUBLANES = 8
VMEM_LIMIT_BYTES = 56 * 1024 * 1024

FFT_N2 = 64
FFT_ROW_PAD = 8
FFT_UNROLL = 4


def _cparams(sem):
    return pltpu.CompilerParams(dimension_semantics=sem, vmem_limit_bytes=VMEM_LIMIT_BYTES)


def _silu(x):
    return x * jax.nn.sigmoid(x)


def _ada_kernel(c_ref, w_ref, b_ref, o_ref):
    s = _silu(c_ref[...])
    o_ref[...] = jnp.dot(s.astype(BF16), w_ref[...].astype(BF16), preferred_element_type=F32) + b_ref[...]


def _ada(cc, w, b):
    d, n6 = w.shape
    tn = min(n6, 1536)
    return pl.pallas_call(
        _ada_kernel,
        grid=(n6 // tn,),
        in_specs=[pl.BlockSpec((SUBLANES, d), lambda j: (0, 0)),
                  pl.BlockSpec((d, tn), lambda j: (0, j)),
                  pl.BlockSpec((1, tn), lambda j: (0, j))],
        out_specs=pl.BlockSpec((SUBLANES, tn), lambda j: (0, j)),
        out_shape=jax.ShapeDtypeStruct((SUBLANES, n6), F32),
        compiler_params=_cparams(("arbitrary",)),
    )(cc, w, b)


def _modmm_kernel(x_ref, sc_ref, sh_ref, w_ref, o_ref, h_scr):
    @pl.when(pl.program_id(1) == 0)
    def _():
        h_scr[...] = (x_ref[...] * (1.0 + sc_ref[0]) + sh_ref[0]).astype(BF16)

    o_ref[...] = jnp.dot(h_scr[...], w_ref[...], preferred_element_type=F32).astype(o_ref.dtype)


def _modmm(x2d, sc, sh, w, *, tm, tn, out_dtype=F32):
    m, d = x2d.shape
    ng = w.shape[1]
    r = sc.shape[0]
    tm = min(tm, m // r)
    tn = min(tn, ng)
    tiles_per_mod = (m // r) // tm
    return pl.pallas_call(
        _modmm_kernel,
        grid=(m // tm, ng // tn),
        in_specs=[pl.BlockSpec((tm, d), lambda i, j: (i, 0)),
                  pl.BlockSpec((1, 1, d), lambda i, j: (i // tiles_per_mod, 0, 0)),
                  pl.BlockSpec((1, 1, d), lambda i, j: (i // tiles_per_mod, 0, 0)),
                  pl.BlockSpec((d, tn), lambda i, j: (0, j))],
        out_specs=pl.BlockSpec((tm, tn), lambda i, j: (i, j)),
        out_shape=jax.ShapeDtypeStruct((m, ng), out_dtype),
        scratch_shapes=[pltpu.VMEM((tm, d), BF16)],
        compiler_params=_cparams(("parallel", "arbitrary")),
    )(x2d, sc, sh, w)


def _row_conv(x, cw, pos, row_w):
    rows = x.shape[0]
    k = cw.shape[0]
    pad = k // 2
    acc = None
    for j in range(k):
        off = j - pad
        if off == 0:
            term = x * cw[j:j + 1]
        else:
            shifted = pltpu.roll(x, (-off) % rows, axis=0)
            valid = (pos >= -off) if off < 0 else (pos < row_w - off)
            term = jnp.where(valid, shifted, 0.0) * cw[j:j + 1]
        acc = term if acc is None else acc + term
    return acc


SOLVE_BLOCK = 16


def _tri_solve(m_strict, rhs, lower):
    c = m_strict.shape[0]
    sb = SOLVE_BLOCK
    nb = c // sb
    ri = lax.broadcasted_iota(jnp.int32, (c, c), 0) // sb
    ci = lax.broadcasted_iota(jnp.int32, (c, c), 1) // sb
    m_off = jnp.where(ri != ci, m_strict, 0.0).astype(BF16)
    blocks = [rhs[i * sb:(i + 1) * sb] for i in range(nb)]
    order = list(range(nb)) if lower else list(range(nb - 1, -1, -1))
    for pos, bi in enumerate(order):
        blk = blocks[bi]
        if pos > 0:
            xcur = jnp.concatenate(blocks, axis=0).astype(BF16)
            blk = blk - jnp.dot(m_off[bi * sb:(bi + 1) * sb], xcur, preferred_element_type=F32)
        mrows = m_strict[bi * sb:(bi + 1) * sb]
        steps = range(sb - 1) if lower else range(sb - 1, 0, -1)
        for jj in steps:
            col = bi * sb + jj
            blk = blk - mrows[:, col:col + 1] * blk[jj:jj + 1, :]
        blocks[bi] = blk
    return jnp.concatenate(blocks, axis=0)


def _dn_prep_kernel(xq_ref, xk_ref, xv_ref, cq_ref, ck_ref, cv_ref, ab_ref, alog_ref, dt_ref,
                    u_ref, w_ref, qg_ref, kd_ref, attn_ref, eg_ref, *, row_w, n_heads, dk):
    h = pl.program_id(1)
    tt = xq_ref.shape[1]
    c = DN_CHUNK
    rows = lax.broadcasted_iota(jnp.int32, (tt, 1), 0)
    pos = rows % row_w
    cpos = rows % c

    def conv_silu(x_ref, cw_ref):
        return _silu(_row_conv(x_ref[0], cw_ref[...], pos, row_w))

    def l2n(y):
        return y * lax.rsqrt(jnp.sum(y * y, axis=-1, keepdims=True) + L2_EPS)

    q = l2n(conv_silu(xq_ref, cq_ref)) * (dk ** -0.5)
    k = l2n(conv_silu(xk_ref, ck_ref))
    v = conv_silu(xv_ref, cv_ref)

    ab = ab_ref[0]
    z = ab + dt_ref[...]
    softplus = jnp.maximum(z, 0.0) + jnp.log1p(jnp.exp(-jnp.abs(z)))
    g_all = -jnp.exp(alog_ref[...]) * softplus
    beta_all = jax.nn.sigmoid(ab)
    gf = g_all
    gb = g_all
    d = 1
    while d < c:
        gf = gf + jnp.where(cpos >= d, pltpu.roll(gf, d, axis=0), 0.0)
        gb = gb + jnp.where(cpos < c - d, pltpu.roll(gb, (tt - d) % tt, axis=0), 0.0)
        d *= 2
    lane = lax.broadcasted_iota(jnp.int32, (1, LANES), 1)

    def col(a, l):
        return jnp.sum(jnp.where(lane == l, a, 0.0), axis=1, keepdims=True)

    g_cols = (col(gf, h), col(gb, 2 * n_heads + h))
    b_cols = (col(beta_all, n_heads + h), col(beta_all, 3 * n_heads + h))

    ri = lax.broadcasted_iota(jnp.int32, (c, c), 0)
    ci = lax.broadcasted_iota(jnp.int32, (c, c), 1)
    eye = ri == ci
    eg_ref[...] = jnp.zeros(eg_ref.shape, F32)
    for cc in range(tt // c):
        sl = slice(cc * c, (cc + 1) * c)
        qc, kc, vc = q[sl], k[sl], v[sl]
        kcb = kc.astype(BF16)
        for dr in range(2):
            incl = (ri >= ci) if dr == 0 else (ri <= ci)
            strict = (ri > ci) if dr == 0 else (ri < ci)
            gcol = g_cols[dr][sl]
            bcol = b_cols[dr][sl]
            grow = jnp.sum(jnp.where(eye, gcol, 0.0), axis=0, keepdims=True)
            decay = jnp.where(incl, jnp.exp(jnp.where(incl, gcol - grow, 0.0)), 0.0)
            kb = kc * bcol
            vb = vc * bcol
            kq = lax.dot_general(jnp.concatenate([kb, qc], axis=0).astype(BF16), kcb,
                                 (((1,), (1,)), ((), ())), preferred_element_type=F32)
            m_strict = jnp.where(strict, kq[:c] * decay, 0.0)
            attn = kq[c:] * decay
            eg_col = jnp.exp(gcol)
            sol = _tri_solve(m_strict, jnp.concatenate([vb, kb * eg_col], axis=1), lower=(dr == 0))
            glast = gcol[c - 1:c] if dr == 0 else gcol[0:1]
            u_ref[dr, 0, sl, :] = sol[:, :dk]
            w_ref[dr, 0, sl, :] = sol[:, dk:].astype(BF16)
            qg_ref[dr, 0, sl, :] = (qc * eg_col).astype(BF16)
            kd_ref[dr, 0, sl, :] = (kc * jnp.exp(glast - gcol)).astype(BF16)
            attn_ref[dr, 0, 0, sl, :] = attn.astype(BF16)
            eg_ref[dr, 0, 0, 0, cc:cc + 1, :] = jnp.broadcast_to(jnp.exp(glast), (1, LANES))


def _dn_prep(qkv, conv_w, ab, alog_row, dt_row, *, n_heads, dk, row_w, tt):
    b, n, _ = qkv.shape
    hh = n_heads
    ng = n // tt
    c = DN_CHUNK
    assert tt % row_w == 0 and tt % c == 0 and tt // c <= SUBLANES
    kern = functools.partial(_dn_prep_kernel, row_w=row_w, n_heads=hh, dk=dk)
    xspec = lambda off: pl.BlockSpec((1, tt, dk), lambda bi, h, g: (bi, g, off + h))
    cspec = lambda off: pl.BlockSpec((conv_w.shape[0], dk), lambda bi, h, g: (0, off + h))
    ospec = pl.BlockSpec((2, 1, tt, dk), lambda bi, h, g: (0, bi, g, h))
    return pl.pallas_call(
        kern,
        grid=(b, hh, ng),
        in_specs=[xspec(0), xspec(hh), xspec(2 * hh), cspec(0), cspec(hh), cspec(2 * hh),
                  pl.BlockSpec((1, tt, LANES), lambda bi, h, g: (bi, g, 0)),
                  pl.BlockSpec((1, LANES), lambda bi, h, g: (0, 0)),
                  pl.BlockSpec((1, LANES), lambda bi, h, g: (0, 0))],
        out_specs=[ospec, ospec, ospec, ospec,
                   pl.BlockSpec((2, 1, 1, tt, c), lambda bi, h, g: (0, bi, h, g, 0)),
                   pl.BlockSpec((2, 1, 1, 1, SUBLANES, LANES), lambda bi, h, g: (0, bi, h, g, 0, 0))],
        out_shape=[jax.ShapeDtypeStruct((2, b, n, hh * dk), F32),
                   jax.ShapeDtypeStruct((2, b, n, hh * dk), BF16),
                   jax.ShapeDtypeStruct((2, b, n, hh * dk), BF16),
                   jax.ShapeDtypeStruct((2, b, n, hh * dk), BF16),
                   jax.ShapeDtypeStruct((2, b, hh, n, c), BF16),
                   jax.ShapeDtypeStruct((2, b, hh, ng, SUBLANES, LANES), F32)],
        compiler_params=_cparams(("parallel", "parallel", "arbitrary")),
    )(qkv, qkv, qkv, conv_w, conv_w, conv_w, ab, alog_row, dt_row)


def _dn_scan_kernel(uf_ref, wf_ref, qf_ref, kf_ref, af_ref, ef_ref,
                    ub_ref, wb_ref, qb_ref, kb_ref, ab_ref, eb_ref, s0_ref,
                    of_ref, ob_ref, sout_ref, s_scr, *, hp, dk):
    g = pl.program_id(2)
    ng = pl.num_programs(2)
    c = DN_CHUNK
    tt = uf_ref.shape[2]
    ncs = tt // c

    @pl.when(g == 0)
    def _():
        s_scr[...] = s0_ref[:, 0]

    refs = ((uf_ref, wf_ref, qf_ref, kf_ref, af_ref, ef_ref, of_ref),
            (ub_ref, wb_ref, qb_ref, kb_ref, ab_ref, eb_ref, ob_ref))
    for step in range(ncs):
        for dr in range(2):
            u_ref, w_ref, q_ref, k_ref, a_ref, e_ref, o_ref = refs[dr]
            cc = step if dr == 0 else ncs - 1 - step
            sl = slice(cc * c, (cc + 1) * c)
            for hh in range(hp):
                hs = slice(hh * dk, (hh + 1) * dk)
                s = s_scr[dr, hh]
                wq = jnp.concatenate([w_ref[0, 0, sl, hs], q_ref[0, 0, sl, hs]], axis=0)
                r = jnp.dot(wq, s.astype(BF16), preferred_element_type=F32)
                vnew = u_ref[0, 0, sl, hs] - r[:c]
                vnb = vnew.astype(BF16)
                o_ref[0, sl, hs] = r[c:] + jnp.dot(a_ref[0, 0, hh, sl, :], vnb, preferred_element_type=F32)
                upd = lax.dot_general(k_ref[0, 0, sl, hs], vnb, (((0,), (0,)), ((), ())),
                                      preferred_element_type=F32)
                s_scr[dr, hh] = s * e_ref[0, 0, hh, 0, cc:cc + 1, :] + upd

    @pl.when(g == ng - 1)
    def _():
        sout_ref[:, 0] = s_scr[...]


def _dn_scan(u, w, qg, kd, attn, eg, s0, *, dk, tt, hp):
    _, b, n, hd = u.shape
    hh = hd // dk
    ng = n // tt
    c = DN_CHUNK
    kern = functools.partial(_dn_scan_kernel, hp=hp, dk=dk)

    def specs(dr):
        gi = (lambda g: g) if dr == 0 else (lambda g: ng - 1 - g)
        tok = pl.BlockSpec((1, 1, tt, hp * dk), lambda bi, h, g: (dr, bi, gi(g), h))
        return [tok, tok, tok, tok,
                pl.BlockSpec((1, 1, hp, tt, c), lambda bi, h, g: (dr, bi, h, gi(g), 0)),
                pl.BlockSpec((1, 1, hp, 1, SUBLANES, LANES), lambda bi, h, g: (dr, bi, h, gi(g), 0, 0))]

    sspec = pl.BlockSpec((2, 1, hp, dk, dk), lambda bi, h, g: (0, bi, h, 0, 0))
    return pl.pallas_call(
        kern,
        grid=(b, hh // hp, ng),
        in_specs=specs(0) + specs(1) + [sspec],
        out_specs=[pl.BlockSpec((1, tt, hp * dk), lambda bi, h, g: (bi, g, h)),
                   pl.BlockSpec((1, tt, hp * dk), lambda bi, h, g: (bi, ng - 1 - g, h)),
                   sspec],
        out_shape=[jax.ShapeDtypeStruct((b, n, hd), F32),
                   jax.ShapeDtypeStruct((b, n, hd), F32),
                   jax.ShapeDtypeStruct((2, b, hh, dk, dk), F32)],
        scratch_shapes=[pltpu.VMEM((2, hp, dk, dk), F32)],
        compiler_params=_cparams(("parallel", "parallel", "arbitrary")),
    )(u, w, qg, kd, attn, eg, u, w, qg, kd, attn, eg, s0)


def _hy_hidden_kernel(z_ref, w1_ref, b1_ref, w2_ref, b2_ref, fr_ref, o_ref):
    fr = fr_ref[...]
    h1 = jnp.sin(fr * (jnp.dot(z_ref[...].astype(BF16), w1_ref[...].astype(BF16),
                               preferred_element_type=F32) + b1_ref[...]))
    o_ref[...] = jnp.sin(fr * (jnp.dot(h1.astype(BF16), w2_ref[...].astype(BF16),
                                       preferred_element_type=F32) + b2_ref[...]))


def _hy_filter_kernel(hid_ref, w3_ref, b3_ref, t_ref, delta_ref, o_ref):
    f = jnp.dot(hid_ref[...].astype(BF16), w3_ref[...].astype(BF16), preferred_element_type=F32) + b3_ref[...]
    o_ref[...] = f * jnp.exp(-t_ref[...] * delta_ref[...])


def _hy_filters(n, w1, b1, w2, b2, w3, b3, freq, hy_w):
    emb, hid = w1.shape
    bands = (emb - 1) // 2
    t = jnp.linspace(0.0, 1.0, n, dtype=F32)
    pos = jnp.arange(n, dtype=F32)
    bnd = jnp.linspace(1e-4, bands - 1, bands, dtype=F32)
    ang = (2.0 * math.pi / n) * pos[:, None] * bnd[None, :]
    z = jnp.concatenate([t[:, None], jnp.cos(ang), -jnp.sin(ang)], axis=-1)
    embp = -(-emb // SUBLANES) * SUBLANES
    z = jnp.pad(z, ((0, 0), (0, embp - emb)))
    w1p = jnp.pad(w1, ((0, embp - emb), (0, 0)))
    full = lambda a: pl.BlockSpec(a.shape, lambda: (0,) * a.ndim)
    args = (z, w1p, b1[None, :], w2, b2[None, :], freq[None, :])
    hid2 = pl.pallas_call(
        _hy_hidden_kernel,
        in_specs=[full(a) for a in args],
        out_specs=pl.BlockSpec((n, hid), lambda: (0, 0)),
        out_shape=jax.ShapeDtypeStruct((n, hid), F32),
        compiler_params=pltpu.CompilerParams(vmem_limit_bytes=VMEM_LIMIT_BYTES),
    )(*args)
    nout = w3.shape[1]
    deltas = jnp.abs(jnp.linspace(math.log(HY_DECAY_TARGET) / HY_SLOW_DECAY,
                                  math.log(HY_DECAY_TARGET) / HY_FAST_DECAY, hy_w, dtype=F32))
    delta_row = jnp.tile(deltas, nout // hy_w)[None, :]
    tc = min(nout, 1024)
    return pl.pallas_call(
        _hy_filter_kernel,
        grid=(nout // tc,),
        in_specs=[pl.BlockSpec((n, hid), lambda j: (0, 0)),
                  pl.BlockSpec((hid, tc), lambda j: (0, j)),
                  pl.BlockSpec((1, tc), lambda j: (0, j)),
                  pl.BlockSpec((n, 1), lambda j: (0, 0)),
                  pl.BlockSpec((1, tc), lambda j: (0, j))],
        out_specs=pl.BlockSpec((n, tc), lambda j: (0, j)),
        out_shape=jax.ShapeDtypeStruct((n, nout), F32),
        compiler_params=_cparams(("parallel",)),
    )(hid2, w3, b3[None, :], t[:, None], delta_row)


def _dft_tables(n):
    n2 = FFT_N2
    big = 2 * n
    n1 = big // n2
    k1 = n1 // 2
    two_pi = 2.0 * math.pi

    def cs(idx, period):
        a = (idx % period).astype(F32) * (two_pi / period)
        return jnp.cos(a), jnp.sin(a)

    f1 = jnp.arange(n1, dtype=jnp.int32)
    s1 = jnp.arange(k1, dtype=jnp.int32)
    s2 = jnp.arange(n2, dtype=jnp.int32)
    s_full = s1[None, None, :] * n2 + s2[:, None, None]
    c, s = cs(f1[None, :, None] * s_full, big)
    f1tw = jnp.concatenate([jnp.concatenate([c, s], axis=2), jnp.concatenate([-s, c], axis=2)], axis=1)
    ct, st = jnp.swapaxes(c, 1, 2), jnp.swapaxes(s, 1, 2)
    g1tw = jnp.concatenate([jnp.concatenate([ct, -st], axis=2), jnp.concatenate([st, ct], axis=2)], axis=1) / big
    c2, sn2 = cs(s2[:, None] * s2[None, :], n2)
    f2 = jnp.concatenate([jnp.concatenate([c2, sn2], axis=1), jnp.concatenate([-sn2, c2], axis=1)], axis=0)
    g2 = jnp.concatenate([jnp.concatenate([c2, -sn2], axis=1), jnp.concatenate([sn2, c2], axis=1)], axis=0)
    return f1tw.astype(BF16), f2.astype(BF16), g2.astype(BF16), g1tw.astype(BF16)


def _hy_conv_kernel(a_ref, m_ref, hf_ref, hb_ref, skip_ref, cwa_ref, cwm_ref,
                    f1_ref, f2_ref, g2_ref, g1_ref, o_ref,
                    h_scr, b_scr, zr_scr, zi_scr, *, conv_a, n1, k1):
    n2 = FFT_N2
    k1p = k1 + FFT_ROW_PAD
    pb = 2 * n2 + FFT_ROW_PAD
    pos = lax.broadcasted_iota(jnp.int32, (n2, 1), 0)

    def stage1(real_only):
        def body(s2, carry):
            base = pl.multiple_of(s2 * k1p, SUBLANES)
            if real_only:
                x = zr_scr[pl.ds(base, k1), :].astype(BF16)
                a = jnp.dot(f1_ref[s2, :, :k1], x, preferred_element_type=F32)
            else:
                x = jnp.concatenate([zr_scr[pl.ds(base, k1), :], zi_scr[pl.ds(base, k1), :]], axis=0)
                a = jnp.dot(f1_ref[s2], x.astype(BF16), preferred_element_type=F32)
            b_scr[pl.ds(s2, n1, stride=pb), :] = a[:n1]
            b_scr[pl.ds(n2 + s2, n1, stride=pb), :] = a[n1:]
            return carry
        lax.fori_loop(0, n2, body, 0, unroll=FFT_UNROLL)

    @pl.when(pl.program_id(1) == 0)
    def _():
        for which, ref in ((0, hf_ref), (1, hb_ref)):
            def load(s1, carry, ref=ref):
                r0 = pl.multiple_of(s1 * n2, n2)
                zr_scr[pl.ds(s1, n2, stride=k1p), :] = ref[pl.ds(r0, n2), :]
                return carry
            lax.fori_loop(0, k1, load, 0, unroll=FFT_UNROLL)
            if which == 1:
                zr_scr[0:1, :] = jnp.zeros((1, zr_scr.shape[1]), F32)
            stage1(True)

            def spec(f1, carry, which=which):
                bb = pl.multiple_of(f1 * pb, SUBLANES)
                hb_ = pl.multiple_of(f1 * 2 * n2, 2 * n2)
                zz = jnp.dot(f2_ref[...], b_scr[pl.ds(bb, 2 * n2), :].astype(BF16), preferred_element_type=F32)
                if which == 0:
                    h_scr[pl.ds(hb_, 2 * n2), :] = zz
                else:
                    h_scr[pl.ds(hb_, n2), :] = h_scr[pl.ds(hb_, n2), :] + zz[:n2]
                    h_scr[pl.ds(hb_ + n2, n2), :] = h_scr[pl.ds(hb_ + n2, n2), :] - zz[n2:]
                return carry
            lax.fori_loop(0, n1, spec, 0, unroll=FFT_UNROLL)

    def load_a(s1, carry):
        r0 = pl.multiple_of(s1 * n2, n2)
        for bi, scr in ((0, zr_scr), (1, zi_scr)):
            blk = a_ref[bi, pl.ds(r0, n2), :]
            if conv_a:
                blk = _row_conv(blk, cwa_ref[...], pos, n2)
            scr[pl.ds(s1, n2, stride=k1p), :] = blk
        return carry
    lax.fori_loop(0, k1, load_a, 0, unroll=FFT_UNROLL)

    stage1(False)

    def mid(f1, carry):
        bb = pl.multiple_of(f1 * pb, SUBLANES)
        hb_ = pl.multiple_of(f1 * 2 * n2, 2 * n2)
        zz = jnp.dot(f2_ref[...], b_scr[pl.ds(bb, 2 * n2), :].astype(BF16), preferred_element_type=F32)
        zr, zi = zz[:n2], zz[n2:]
        hr = h_scr[pl.ds(hb_, n2), :]
        hi = h_scr[pl.ds(hb_ + n2, n2), :]
        yy = jnp.concatenate([zr * hr - zi * hi, zr * hi + zi * hr], axis=0).astype(BF16)
        b_scr[pl.ds(bb, 2 * n2), :] = jnp.dot(g2_ref[...], yy, preferred_element_type=F32)
        return carry
    lax.fori_loop(0, n1, mid, 0, unroll=FFT_UNROLL)

    skip = skip_ref[...]

    def last(s2, carry):
        base = pl.multiple_of(s2 * k1p, SUBLANES)
        ww = jnp.concatenate([b_scr[pl.ds(s2, n1, stride=pb), :],
                              b_scr[pl.ds(n2 + s2, n1, stride=pb), :]], axis=0).astype(BF16)
        y = jnp.dot(g1_ref[s2], ww, preferred_element_type=F32)
        zr_scr[pl.ds(base, k1), :] = y[:k1] + zr_scr[pl.ds(base, k1), :] * skip
        zi_scr[pl.ds(base, k1), :] = y[k1:] + zi_scr[pl.ds(base, k1), :] * skip
        return carry
    lax.fori_loop(0, n2, last, 0, unroll=FFT_UNROLL)

    def fin(s1, carry):
        r0 = pl.multiple_of(s1 * n2, n2)
        for bi, scr in ((0, zr_scr), (1, zi_scr)):
            mm = _row_conv(m_ref[bi, pl.ds(r0, n2), :], cwm_ref[...], pos, n2)
            o_ref[bi, pl.ds(r0, n2), :] = mm * scr[pl.ds(s1, n2, stride=k1p), :]
        return carry
    lax.fori_loop(0, k1, fin, 0, unroll=FFT_UNROLL)


def _hy_conv(a, a_col0, m, m_col0, filt, f_col_fwd, f_col_bwd, skip, cwa, cwm, tables, *, conv_a, hy_w):
    b, n, _ = a.shape
    assert b % 2 == 0 and GRID_W == FFT_N2 and n % FFT_N2 == 0
    n2 = FFT_N2
    n1 = 2 * n // n2
    k1 = n1 // 2
    k1p = k1 + FFT_ROW_PAD
    pb = 2 * n2 + FFT_ROW_PAD
    cw = LANES
    f1tw, f2, g2, g1tw = tables
    kern = functools.partial(_hy_conv_kernel, conv_a=conv_a, n1=n1, k1=k1)
    once = pl.Buffered(1)
    return pl.pallas_call(
        kern,
        grid=(hy_w // cw, b // 2),
        in_specs=[pl.BlockSpec((2, n, cw), lambda c, p: (p, 0, a_col0 + c), pipeline_mode=once),
                  pl.BlockSpec((2, n, cw), lambda c, p: (p, 0, m_col0 + c), pipeline_mode=once),
                  pl.BlockSpec((n, cw), lambda c, p: (0, f_col_fwd + c), pipeline_mode=once),
                  pl.BlockSpec((n, cw), lambda c, p: (0, f_col_bwd + c), pipeline_mode=once),
                  pl.BlockSpec((1, cw), lambda c, p: (0, c)),
                  pl.BlockSpec((cwa.shape[0], cw), lambda c, p: (0, a_col0 + c if conv_a else 0)),
                  pl.BlockSpec((cwm.shape[0], cw), lambda c, p: (0, m_col0 + c)),
                  pl.BlockSpec(f1tw.shape, lambda c, p: (0, 0, 0), pipeline_mode=once),
                  pl.BlockSpec(f2.shape, lambda c, p: (0, 0)),
                  pl.BlockSpec(g2.shape, lambda c, p: (0, 0)),
                  pl.BlockSpec(g1tw.shape, lambda c, p: (0, 0, 0), pipeline_mode=once)],
        out_specs=pl.BlockSpec((2, n, cw), lambda c, p: (p, 0, c)),
        out_shape=jax.ShapeDtypeStruct((b, n, hy_w), F32),
        scratch_shapes=[pltpu.VMEM((n1 * 2 * n2, cw), F32),
                        pltpu.VMEM((n1 * pb, cw), F32),
                        pltpu.VMEM((n2 * k1p, cw), F32),
                        pltpu.VMEM((n2 * k1p, cw), F32)],
        compiler_params=_cparams(("parallel", "arbitrary")),
    )(a, m, filt, filt, skip, cwa, cwm, f1tw, f2, g2, g1tw)


def _merge_kernel(of_ref, ob_ref, z_ref, y_ref, ga_ref, gb_ref, nw_ref, wa_ref, wh_ref, o_ref,
                  on_scr, yb_scr, *, dv):
    @pl.when(pl.program_id(1) == 0)
    def _():
        nw = nw_ref[...]
        for hh in range(of_ref.shape[1] // dv):
            hs = slice(hh * dv, (hh + 1) * dv)
            o = of_ref[:, hs] + ob_ref[:, hs]
            o = o * lax.rsqrt(jnp.mean(o * o, axis=-1, keepdims=True) + RMS_EPS) * nw
            on_scr[:, hs] = (o * _silu(z_ref[:, hs])).astype(BF16)
        yb_scr[...] = y_ref[...].astype(BF16)

    pa = jnp.dot(on_scr[...], wa_ref[...], preferred_element_type=F32)
    ph = jnp.dot(yb_scr[...], wh_ref[...], preferred_element_type=F32)
    o_ref[...] = (jax.nn.sigmoid(ga_ref[...]) * pa + jax.nn.sigmoid(gb_ref[...]) * ph).astype(o_ref.dtype)


def _merge(o_f, o_b, z, z_blk, y_hy, gate, norm_w, w_pa, w_ph, *, dv, tm, tn):
    m, dvw = o_f.shape
    d = w_pa.shape[1]
    tm = min(tm, m)
    tn = min(tn, d)
    nj = d // tn
    kern = functools.partial(_merge_kernel, dv=dv)
    row = lambda w_: pl.BlockSpec((tm, w_), lambda i, j: (i, 0))
    return pl.pallas_call(
        kern,
        grid=(m // tm, nj),
        in_specs=[row(dvw), row(dvw), pl.BlockSpec((tm, dvw), lambda i, j: (i, z_blk)), row(y_hy.shape[1]),
                  pl.BlockSpec((tm, tn), lambda i, j: (i, j)),
                  pl.BlockSpec((tm, tn), lambda i, j: (i, nj + j)),
                  pl.BlockSpec((1, dv), lambda i, j: (0, 0)),
                  pl.BlockSpec((dvw, tn), lambda i, j: (0, j)),
                  pl.BlockSpec((y_hy.shape[1], tn), lambda i, j: (0, j))],
        out_specs=pl.BlockSpec((tm, tn), lambda i, j: (i, j)),
        out_shape=jax.ShapeDtypeStruct((m, d), BF16),
        scratch_shapes=[pltpu.VMEM((tm, dvw), BF16), pltpu.VMEM((tm, y_hy.shape[1]), BF16)],
        compiler_params=_cparams(("parallel", "arbitrary")),
    )(o_f, o_b, z, y_hy, gate, gate, norm_w, w_pa, w_ph)


def _layer_norm(x, g, b):
    mu = jnp.mean(x, axis=-1, keepdims=True)
    xc = x - mu
    var = jnp.mean(xc * xc, axis=-1, keepdims=True)
    return xc * lax.rsqrt(var + LN_EPS) * g + b


def _outproj_kernel(mg_ref, x_ref, wo_ref, g1_ref, lg_ref, lb_ref, sc_ref, sh_ref, wq_ref, sk_ref,
                    x1_ref, h2_ref, s_ref, *, half):
    mix = jnp.dot(mg_ref[...], wo_ref[...], preferred_element_type=F32)
    x1 = _layer_norm(DEEPNORM_ALPHA * x_ref[...] + g1_ref[0] * mix, lg_ref[...], lb_ref[...])
    x1_ref[...] = x1
    h2 = (x1 * (1.0 + sc_ref[0]) + sh_ref[0]).astype(BF16)
    h2_ref[...] = h2
    qp = jnp.dot(h2, wq_ref[...], preferred_element_type=F32).astype(BF16)
    for j in range(sk_ref.shape[0]):
        js = slice(j * half, (j + 1) * half)
        s_ref[j] = lax.dot_general(sk_ref[j], qp[:, js], (((1,), (1,)), ((), ())), preferred_element_type=F32)


def _outproj(merged, x2d, w_out, g1, ln_g, ln_b, sc2, sh2, wq, subkeys, *, tm):
    m, d = x2d.shape
    r = g1.shape[0]
    tm = min(tm, m // r)
    tpm = (m // r) // tm
    nsk, nkeys, half = subkeys.shape
    assert nkeys == LANES
    kern = functools.partial(_outproj_kernel, half=half)
    row = lambda w_: pl.BlockSpec((tm, w_), lambda i: (i, 0))
    mod = pl.BlockSpec((1, 1, d), lambda i: (i // tpm, 0, 0))
    vec = pl.BlockSpec((1, d), lambda i: (0, 0))
    return pl.pallas_call(
        kern,
        grid=(m // tm,),
        in_specs=[row(d), row(d), pl.BlockSpec(w_out.shape, lambda i: (0, 0), pipeline_mode=pl.Buffered(1)),
                  mod, vec, vec, mod, mod,
                  pl.BlockSpec(wq.shape, lambda i: (0, 0), pipeline_mode=pl.Buffered(1)),
                  pl.BlockSpec(subkeys.shape, lambda i: (0, 0, 0))],
        out_specs=[row(d), row(d), pl.BlockSpec((nsk, nkeys, tm), lambda i: (0, 0, i))],
        out_shape=[jax.ShapeDtypeStruct((m, d), F32),
                   jax.ShapeDtypeStruct((m, d), BF16),
                   jax.ShapeDtypeStruct((nsk, nkeys, m), F32)],
        compiler_params=_cparams(("parallel",)),
    )(merged, x2d, w_out, g1, ln_g, ln_b, sc2, sh2, wq, subkeys)


def _cand_tables():
    pairs = [(a, b) for a in range(PEER_TOPK) for b in range(PEER_TOPK) if (a + 1) * (b + 1) <= PEER_TOPK]
    ia = jnp.array([p[0] for p in pairs], dtype=jnp.int32)
    ib = jnp.array([p[1] for p in pairs], dtype=jnp.int32)
    ncand = len(pairs)
    ranks = jnp.arange(LANES, dtype=jnp.int32)
    p1 = (jnp.pad(ia, (0, LANES - ncand), constant_values=-1)[:, None] == ranks[None, :]).astype(F32)
    p2 = (jnp.pad(ib, (0, LANES - ncand), constant_values=-1)[:, None] == ranks[None, :]).astype(F32)
    return p1, p2, ncand


def _peer_topk_kernel(s_ref, p1_ref, p2_ref, p1t_ref, c1_ref, r2_ref, e1_ref, e2_ref, *, n_heads, ncand):
    tm = s_ref.shape[2]
    neg = -jnp.inf
    hi = lax.Precision.HIGHEST
    no_rank = float(LANES)

    def pop_max(x, row_id):
        mx = jnp.max(x, axis=0, keepdims=True)
        first = jnp.min(jnp.where(x == mx, row_id, no_rank), axis=0, keepdims=True)
        hit = row_id == first
        return mx, hit, jnp.where(hit, neg, x)

    key_id = lax.broadcasted_iota(jnp.int32, (LANES, tm), 0).astype(F32)
    cand_id = key_id
    pad_rows = jnp.zeros((LANES - PEER_TOPK, tm), F32)

    def sorted_top(s):
        tops = []
        rank = jnp.full(s.shape, no_rank, F32)
        x = s
        for a in range(PEER_TOPK):
            mx, hit, x = pop_max(x, key_id)
            tops.append(mx)
            rank = jnp.where(hit, float(a), rank)
        return jnp.concatenate(tops + [pad_rows], axis=0), rank

    for h in range(n_heads):
        s1 = s_ref[2 * h]
        s2 = s_ref[2 * h + 1]
        t1, rank1 = sorted_top(s1)
        t2, rank2 = sorted_top(s2)
        cand = (jnp.dot(p1_ref[...], t1, precision=hi, preferred_element_type=F32)
                + jnp.dot(p2_ref[...], t2, precision=hi, preferred_element_type=F32))
        x = jnp.where(cand_id < ncand, cand, neg)
        sel = jnp.zeros(x.shape, F32)
        zsum = None
        cmax = None
        for r in range(PEER_TOPK):
            mx, hit, x = pop_max(x, cand_id)
            sel = jnp.where(hit, 1.0, sel)
            if r == 0:
                cmax = mx
                zsum = jnp.ones_like(mx)
            else:
                zsum = zsum + jnp.exp(mx - cmax)
        cnt = jnp.dot(p1t_ref[...].astype(BF16), sel.astype(BF16), preferred_element_type=F32)
        count1 = jnp.zeros(s1.shape, F32)
        for a in range(PEER_TOPK):
            count1 = jnp.where(rank1 == a, cnt[a:a + 1], count1)
        c1_ref[h] = count1
        r2_ref[h] = rank2
        e1_ref[h] = jnp.exp(s1 - t1[0:1]) / zsum
        e2_ref[h] = jnp.exp(s2 - t2[0:1])


def _peer_topk(scores, n_heads, *, tm):
    nsk, nkeys, m = scores.shape
    tm = min(tm, m)
    p1, p2, ncand = _cand_tables()
    kern = functools.partial(_peer_topk_kernel, n_heads=n_heads, ncand=ncand)
    full = lambda a: pl.BlockSpec(a.shape, lambda i: (0, 0))
    out = pl.BlockSpec((n_heads, nkeys, tm), lambda i: (0, 0, i))
    shp = jax.ShapeDtypeStruct((n_heads, nkeys, m), F32)
    p1t = p1.T
    return pl.pallas_call(
        kern,
        grid=(m // tm,),
        in_specs=[pl.BlockSpec((nsk, nkeys, tm), lambda i: (0, 0, i)), full(p1), full(p2), full(p1t)],
        out_specs=[out, out, out, out],
        out_shape=[shp, shp, shp, shp],
        compiler_params=_cparams(("parallel",)),
    )(scores, p1, p2, p1t)


def _peer_dense_kernel(h2_ref, u_ref, v_ref, r2_ref, e2_ref, c1_ref, e1_ref, x1_ref, g2_ref, lg_ref, lb_ref,
                       o_ref, acc_scr, a_scr, p_scr, *, n_heads, ti):
    e = pl.program_id(1)

    @pl.when(e == 0)
    def _():
        acc_scr[...] = jnp.zeros(acc_scr.shape, F32)

    a_scr[...] = lax.dot_general(u_ref[...], h2_ref[...], (((1,), (1,)), ((), ())), preferred_element_type=F32)
    tm = h2_ref.shape[0]
    for ii in range(ti):
        rs = slice(ii * LANES, (ii + 1) * LANES)
        for cb in range(tm // LANES):
            cs = slice(cb * LANES, (cb + 1) * LANES)
            gate = None
            for h in range(n_heads):
                keep = r2_ref[h, :, cs] < c1_ref[h, ii:ii + 1, cs]
                term = jnp.where(keep, e2_ref[h, :, cs], 0.0) * e1_ref[h, ii:ii + 1, cs]
                gate = term if gate is None else gate + term
            a = a_scr[rs, cs]
            act = 0.5 * a * (1.0 + lax.erf(a * (2.0 ** -0.5)))
            p_scr[rs, cs] = (gate * act).astype(BF16)
    acc_scr[...] += lax.dot_general(p_scr[...], v_ref[...], (((0,), (0,)), ((), ())), preferred_element_type=F32)

    @pl.when(e == pl.num_programs(1) - 1)
    def _():
        o_ref[...] = _layer_norm(DEEPNORM_ALPHA * x1_ref[...] + g2_ref[0] * acc_scr[...],
                                 lg_ref[...], lb_ref[...])


def _peer_dense(h2, u_tab, v_tab, rank2, e2, count1, e1, x1, g2, ln_g, ln_b, *, n_heads, ti, tm):
    m, d = h2.shape
    r = g2.shape[0]
    tm = min(tm, m // r)
    tpm = (m // r) // tm
    ne = u_tab.shape[0]
    nkeys = rank2.shape[1]
    te = ti * nkeys
    kern = functools.partial(_peer_dense_kernel, n_heads=n_heads, ti=ti)
    row = lambda w_: pl.BlockSpec((tm, w_), lambda i, e: (i, 0), pipeline_mode=pl.Buffered(1))
    vec = pl.BlockSpec((1, d), lambda i, e: (0, 0))
    tbl = pl.BlockSpec((te, d), lambda i, e: (e, 0))
    second = pl.BlockSpec((n_heads, nkeys, tm), lambda i, e: (0, 0, i), pipeline_mode=pl.Buffered(1))
    first = pl.BlockSpec((n_heads, ti, tm), lambda i, e: (0, e, i))
    return pl.pallas_call(
        kern,
        grid=(m // tm, ne // te),
        in_specs=[row(d), tbl, tbl, second, second, first, first, row(d),
                  pl.BlockSpec((1, 1, d), lambda i, e: (i // tpm, 0, 0)), vec, vec],
        out_specs=pl.BlockSpec((tm, d), lambda i, e: (i, 0)),
        out_shape=jax.ShapeDtypeStruct((m, d), F32),
        scratch_shapes=[pltpu.VMEM((tm, d), F32), pltpu.VMEM((te, tm), F32), pltpu.VMEM((te, tm), BF16)],
        compiler_params=_cparams(("parallel", "arbitrary")),
    )(h2, u_tab, v_tab, rank2, e2, count1, e1, x1, g2, ln_g, ln_b)


def _block(x, c, ctx, c_ctx, w_ada, b_ada, w_in, dn_conv_w, dn_a_log, dn_dt_bias, dn_norm_w, hy_conv_w, hy_w1, hy_b1, hy_w2, hy_b2, hy_w3, hy_b3, hy_freq, hy_skip, w_branch_dn, w_branch_hy, w_out, ln1_g, ln1_b, peer_wq, peer_subkeys, peer_u, peer_v, ln2_g, ln2_b):
    b, n, d = x.shape
    n_ctx = ctx.shape[1]
    assert w_ada.shape[0] == DEPTH == 1, "single-layer block: the context stream update is never read"
    hh = dn_a_log.shape[2]
    dk = dn_conv_w.shape[2] // (3 * hh)
    qk_w = hh * dk
    hy_w = hy_skip.shape[2]
    p_heads, _, nkeys, half = peer_subkeys.shape[1:]
    assert dk == LANES and nkeys == LANES and 4 * hh <= LANES and n % GRID_W == 0
    off_z = 3 * qk_w
    off_ab = 4 * qk_w
    off_hy = off_ab + 4 * hh
    off_gate = off_hy + 3 * hy_w
    assert w_in.shape[2] == off_gate + 2 * d

    cc = jnp.zeros((SUBLANES, d), F32).at[:b].set(c).at[b].set(c_ctx)
    mods = _ada(cc, w_ada[0], b_ada[0][None, :])
    sh1, sc1, g1, sh2, sc2, g2 = [mods[:b, i * d:(i + 1) * d][:, None, :] for i in range(6)]
    sh1c, sc1c = [mods[b:b + 1, i * d:(i + 1) * d][:, None, :] for i in range(2)]

    w = w_in[0]
    w_qkvz = w[:, :off_ab].astype(BF16)
    w_ab = jnp.pad(w[:, off_ab:off_hy], ((0, 0), (0, LANES - 4 * hh))).astype(BF16)
    w_hy = w[:, off_hy:off_gate].astype(BF16)
    w_gate = w[:, off_gate:].astype(BF16)

    x2d = x.reshape(b * n, d)
    c2d = ctx.reshape(b * n_ctx, d)
    proj_qkvz = _modmm(x2d, sc1, sh1, w_qkvz, tm=512, tn=2048)
    proj_ab = _modmm(x2d, sc1, sh1, w_ab, tm=512, tn=LANES)
    proj_hy = _modmm(x2d, sc1, sh1, w_hy, tm=512, tn=2048)
    proj_gate = _modmm(x2d, sc1, sh1, w_gate, tm=512, tn=2048)
    projc_qkvz = _modmm(c2d, sc1c, sh1c, w_qkvz, tm=256, tn=2048)
    projc_ab = _modmm(c2d, sc1c, sh1c, w_ab, tm=256, tn=LANES)

    def decay_row(a):
        row = jnp.concatenate([a, jnp.zeros_like(a)], axis=1).reshape(-1)
        return jnp.pad(row, (0, LANES - 4 * hh))[None, :]

    alog_row = decay_row(dn_a_log[0])
    dt_row = decay_row(dn_dt_bias[0])

    conv_w = dn_conv_w[0]
    prep_c = _dn_prep(projc_qkvz.reshape(b, n_ctx, -1), conv_w, projc_ab.reshape(b, n_ctx, LANES),
                      alog_row, dt_row, n_heads=hh, dk=dk, row_w=n_ctx, tt=n_ctx)
    prep_l = _dn_prep(proj_qkvz.reshape(b, n, -1), conv_w, proj_ab.reshape(b, n, LANES),
                      alog_row, dt_row, n_heads=hh, dk=dk, row_w=GRID_W, tt=4 * GRID_W)
    hp = 2 if hh % 2 == 0 else 1
    s_zero = jnp.zeros((2, b, hh, dk, dk), F32)
    _, _, s_ctx = _dn_scan(*prep_c, s_zero, dk=dk, tt=n_ctx, hp=hp)
    o_f, o_b, _ = _dn_scan(*prep_l, s_ctx, dk=dk, tt=4 * GRID_W, hp=hp)

    filt = _hy_filters(n, hy_w1[0], hy_b1[0], hy_w2[0], hy_b2[0], hy_w3[0], hy_b3[0], hy_freq[0], hy_w)
    tables = _dft_tables(n)
    hy3 = proj_hy.reshape(b, n, 3 * hy_w)
    cwh = hy_conv_w[0]
    cb = hy_w // LANES
    y1 = _hy_conv(hy3, 0, hy3, cb, filt, 0, 2 * cb, hy_skip[0, 0:1], cwh, cwh, tables, conv_a=True, hy_w=hy_w)
    y_hy = _hy_conv(y1, 0, hy3, 2 * cb, filt, cb, 3 * cb, hy_skip[0, 1:2], cwh, cwh, tables, conv_a=False, hy_w=hy_w)

    merged = _merge(o_f.reshape(b * n, qk_w), o_b.reshape(b * n, qk_w), proj_qkvz, off_z // qk_w,
                    y_hy.reshape(b * n, hy_w), proj_gate, dn_norm_w[0][None, :],
                    w_branch_dn[0].astype(BF16), w_branch_hy[0].astype(BF16), dv=dk, tm=256, tn=1024)
    sk = peer_subkeys[0].reshape(p_heads * 2, nkeys, half).astype(BF16)
    x1, h2, scores = _outproj(merged, x2d, w_out[0].astype(BF16), g1, ln1_g[0][None, :], ln1_b[0][None, :],
                              sc2, sh2, peer_wq[0].astype(BF16), sk, tm=256)

    count1, rank2, e1, e2 = _peer_topk(scores, p_heads, tm=256)
    out = _peer_dense(h2, peer_u[0].astype(BF16), peer_v[0].astype(BF16), rank2, e2, count1, e1,
                      x1, g2, ln2_g[0][None, :], ln2_b[0][None, :], n_heads=p_heads, ti=SUBLANES, tm=512)
    stages = dict(o_f=o_f, o_b=o_b, s_ctx=s_ctx, filt=filt, y1=y1, y_hy=y_hy, merged=merged, x1=x1, scores=scores,
                  count1=count1, rank2=rank2, e1=e1, e2=e2)
    return out.reshape(b, n, d), stages


def kernel(x, c, ctx, c_ctx, w_ada, b_ada, w_in, dn_conv_w, dn_a_log, dn_dt_bias, dn_norm_w, hy_conv_w, hy_w1, hy_b1, hy_w2, hy_b2, hy_w3, hy_b3, hy_freq, hy_skip, w_branch_dn, w_branch_hy, w_out, ln1_g, ln1_b, peer_wq, peer_subkeys, peer_u, peer_v, ln2_g, ln2_b):
    out, _ = _block(x, c, ctx, c_ctx, w_ada, b_ada, w_in, dn_conv_w, dn_a_log, dn_dt_bias, dn_norm_w, hy_conv_w, hy_w1, hy_b1, hy_w2, hy_b2, hy_w3, hy_b3, hy_freq, hy_skip, w_branch_dn, w_branch_hy, w_out, ln1_g, ln1_b, peer_wq, peer_subkeys, peer_u, peer_v, ln2_g, ln2_b)
    return out
```

```python
import functools
import math

import jax
import jax.numpy as jnp
from jax import lax
from jax.experimental import pallas as pl
from jax.experimental.pallas import tpu as pltpu

F32 = jnp.float32
BF16 = jnp.bfloat16

GRID_W = 64
DN_CHUNK = 64
PEER_TOPK = 16
DEPTH = 1
DEEPNORM_ALPHA = (2 * DEPTH) ** 0.25
LN_EPS = 1e-5
RMS_EPS = 1e-6
L2_EPS = 1e-6
HY_DECAY_TARGET = 1e-2
HY_FAST_DECAY = 0.3
HY_SLOW_DECAY = 1.5

LANES = 128
SUBLANES = 8
VMEM_LIMIT_BYTES = 56 * 1024 * 1024

FFT_N2 = 64
FFT_ROW_PAD = 8
FFT_UNROLL = 4


def _cparams(sem):
    return pltpu.CompilerParams(dimension_semantics=sem, vmem_limit_bytes=VMEM_LIMIT_BYTES)


def _silu(x):
    return x * jax.nn.sigmoid(x)


def _ada_kernel(c_ref, w_ref, b_ref, o_ref):
    s = _silu(c_ref[...])
    o_ref[...] = jnp.dot(s.astype(BF16), w_ref[...].astype(BF16), preferred_element_type=F32) + b_ref[...]


def _ada(cc, w, b):
    d, n6 = w.shape
    tn = min(n6, 1536)
    return pl.pallas_call(
        _ada_kernel,
        grid=(n6 // tn,),
        in_specs=[pl.BlockSpec((SUBLANES, d), lambda j: (0, 0)),
                  pl.BlockSpec((d, tn), lambda j: (0, j)),
                  pl.BlockSpec((1, tn), lambda j: (0, j))],
        out_specs=pl.BlockSpec((SUBLANES, tn), lambda j: (0, j)),
        out_shape=jax.ShapeDtypeStruct((SUBLANES, n6), F32),
        compiler_params=_cparams(("arbitrary",)),
    )(cc, w, b)


def _modmm_kernel(x_ref, sc_ref, sh_ref, w_ref, o_ref, h_scr):
    @pl.when(pl.program_id(1) == 0)
    def _():
        h_scr[...] = (x_ref[...] * (1.0 + sc_ref[0]) + sh_ref[0]).astype(BF16)

    o_ref[...] = jnp.dot(h_scr[...], w_ref[...], preferred_element_type=F32).astype(o_ref.dtype)


def _modmm(x2d, sc, sh, w, *, tm, tn, out_dtype=F32):
    m, d = x2d.shape
    ng = w.shape[1]
    r = sc.shape[0]
    tm = min(tm, m // r)
    tn = min(tn, ng)
    tiles_per_mod = (m // r) // tm
    return pl.pallas_call(
        _modmm_kernel,
        grid=(m // tm, ng // tn),
        in_specs=[pl.BlockSpec((tm, d), lambda i, j: (i, 0)),
                  pl.BlockSpec((1, 1, d), lambda i, j: (i // tiles_per_mod, 0, 0)),
                  pl.BlockSpec((1, 1, d), lambda i, j: (i // tiles_per_mod, 0, 0)),
                  pl.BlockSpec((d, tn), lambda i, j: (0, j))],
        out_specs=pl.BlockSpec((tm, tn), lambda i, j: (i, j)),
        out_shape=jax.ShapeDtypeStruct((m, ng), out_dtype),
        scratch_shapes=[pltpu.VMEM((tm, d), BF16)],
        compiler_params=_cparams(("parallel", "arbitrary")),
    )(x2d, sc, sh, w)


def _row_conv(x, cw, pos, row_w):
    rows = x.shape[0]
    k = cw.shape[0]
    pad = k // 2
    acc = None
    for j in range(k):
        off = j - pad
        if off == 0:
            term = x * cw[j:j + 1]
        else:
            shifted = pltpu.roll(x, (-off) % rows, axis=0)
            valid = (pos >= -off) if off < 0 else (pos < row_w - off)
            term = jnp.where(valid, shifted, 0.0) * cw[j:j + 1]
        acc = term if acc is None else acc + term
    return acc


SOLVE_BLOCK = 16


def _tri_inverse_pairs(pairs):
    c = pairs[0][0].shape[0]
    sb = SOLVE_BLOCK
    nb = c // sb
    ri = lax.broadcasted_iota(jnp.int32, (c, 2 * c), 0)
    ci = lax.broadcasted_iota(jnp.int32, (c, 2 * c), 1)
    ci = jnp.where(ci >= c, ci - c, ci)
    lane_lo = lax.broadcasted_iota(jnp.int32, (1, 2 * c), 1) < c
    off_diag = ri // sb != ci // sb
    eye2 = (ri == ci).astype(F32)
    mcats = [jnp.concatenate([m_a, m_b], axis=1) for m_a, m_b in pairs]
    m_offs = [jnp.where(off_diag, m, 0.0).astype(BF16) for m in mcats]
    xs = [[eye2[i * sb:(i + 1) * sb] for i in range(nb)] for _ in pairs]
    for bi in range(nb):
        rs = slice(bi * sb, (bi + 1) * sb)
        blks = [x[bi] for x in xs]
        if bi > 0:
            for s, x in enumerate(xs):
                xcur = jnp.concatenate(x, axis=0)
                xbd = jnp.concatenate([jnp.where(lane_lo, xcur, 0.0), jnp.where(lane_lo, 0.0, xcur)], axis=0)
                blks[s] = blks[s] - jnp.dot(m_offs[s][rs], xbd.astype(BF16), preferred_element_type=F32)
        mrs = [m[rs] for m in mcats]
        for jj in range(sb - 1):
            col = bi * sb + jj
            src = jnp.broadcast_to(jnp.where(lane_lo, col, c + col), (sb, 2 * c))
            for s in range(len(pairs)):
                mult = jnp.take_along_axis(mrs[s], src, axis=1)
                blks[s] = blks[s] - mult * blks[s][jj:jj + 1, :]
        for s, x in enumerate(xs):
            x[bi] = blks[s]
    return [jnp.concatenate(x, axis=0) for x in xs]


def _dn_gates_kernel(ab_ref, alog_ref, dt_ref, gc_ref, beta_ref, *, n_heads):
    tt = ab_ref.shape[1]
    c = DN_CHUNK
    cpos = lax.broadcasted_iota(jnp.int32, (tt, 1), 0) % c
    ab = ab_ref[0]
    z = ab + dt_ref[...]
    softplus = jnp.maximum(z, 0.0) + jnp.log1p(jnp.exp(-jnp.abs(z)))
    g_all = -jnp.exp(alog_ref[...]) * softplus
    gf = g_all
    gb = g_all
    d = 1
    while d < c:
        gf = gf + jnp.where(cpos >= d, pltpu.roll(gf, d, axis=0), 0.0)
        gb = gb + jnp.where(cpos < c - d, pltpu.roll(gb, (tt - d) % tt, axis=0), 0.0)
        d *= 2
    lane = lax.broadcasted_iota(jnp.int32, (1, LANES), 1)
    gc_ref[0] = jnp.where(lane < 2 * n_heads, gf, gb)
    beta_ref[0] = jax.nn.sigmoid(ab)


def _dn_gates(ab, alog_row, dt_row, *, n_heads, tt):
    b, n, _ = ab.shape
    blk = pl.BlockSpec((1, tt, LANES), lambda bi, g: (bi, g, 0))
    row = pl.BlockSpec((1, LANES), lambda bi, g: (0, 0))
    shp = jax.ShapeDtypeStruct((b, n, LANES), F32)
    return pl.pallas_call(
        functools.partial(_dn_gates_kernel, n_heads=n_heads),
        grid=(b, n // tt),
        in_specs=[blk, row, row],
        out_specs=[blk, blk],
        out_shape=[shp, shp],
        compiler_params=_cparams(("parallel", "parallel")),
    )(ab, alog_row, dt_row)


def _dn_prep_kernel(xq_ref, xk_ref, xv_ref, cq_ref, ck_ref, cv_ref, gc_ref, beta_ref,
                    u_ref, w_ref, qg_ref, kd_ref, attn_ref, eg_ref, *, row_w, n_heads, dk, hp):
    tt = xq_ref.shape[1]
    c = DN_CHUNK
    rows = lax.broadcasted_iota(jnp.int32, (tt, 1), 0)
    pos = rows % row_w
    lane = lax.broadcasted_iota(jnp.int32, (1, LANES), 1)
    gc = gc_ref[0]
    beta_all = beta_ref[0]
    eg_ref[...] = jnp.zeros(eg_ref.shape, F32)
    systems = []
    for hh in range(hp):
        systems += _dn_prep_head(pl.program_id(1) * hp + hh, hh, slice(hh * dk, (hh + 1) * dk), pos, lane, gc,
                                 beta_all, xq_ref, xk_ref, xv_ref, cq_ref, ck_ref, cv_ref, qg_ref, kd_ref, attn_ref,
                                 eg_ref, row_w=row_w, n_heads=n_heads, dk=dk, tt=tt)
    tinvs = _tri_inverse_pairs([(m_f, m_bt) for m_f, m_bt, _, _, _, _ in systems])
    zero_rows = jnp.zeros((c, 2 * dk), BF16)
    for tinv, (_, _, rhs_f, rhs_b, sl, hs) in zip(tinvs, systems):
        tinv = tinv.astype(BF16)
        sol_f = jnp.dot(tinv, jnp.concatenate([rhs_f, zero_rows], axis=0), preferred_element_type=F32)
        sol_b = lax.dot_general(tinv, rhs_b, (((0,), (0,)), ((), ())), preferred_element_type=F32)[c:]
        for dr, sol in enumerate((sol_f, sol_b)):
            u_ref[dr, 0, sl, hs] = sol[:, :dk]
            w_ref[dr, 0, sl, hs] = sol[:, dk:].astype(BF16)


def _dn_prep_head(h, hh, hs, pos, lane, gc, beta_all, xq_ref, xk_ref, xv_ref, cq_ref, ck_ref, cv_ref,
                  qg_ref, kd_ref, attn_ref, eg_ref, *, row_w, n_heads, dk, tt):
    c = DN_CHUNK
    systems = []

    def conv_silu(x_ref, cw_ref):
        return _silu(_row_conv(x_ref[0, :, hs], cw_ref[:, hs], pos, row_w))

    def l2n(y):
        return y * lax.rsqrt(jnp.sum(y * y, axis=-1, keepdims=True) + L2_EPS)

    q = l2n(conv_silu(xq_ref, cq_ref)) * (dk ** -0.5)
    k = l2n(conv_silu(xk_ref, ck_ref))
    v = conv_silu(xv_ref, cv_ref)

    def col(a, l):
        return jnp.sum(jnp.where(lane == l, a, 0.0), axis=1, keepdims=True)

    g_cols = (col(gc, h), col(gc, 2 * n_heads + h))
    b_cols = (col(beta_all, n_heads + h), col(beta_all, 3 * n_heads + h))

    ri = lax.broadcasted_iota(jnp.int32, (c, c), 0)
    ci = lax.broadcasted_iota(jnp.int32, (c, c), 1)
    eye = ri == ci
    lower = ri >= ci
    upper = ri <= ci
    strict_lower = ri > ci
    nt = (((1,), (1,)), ((), ()))
    for cc in range(tt // c):
        sl = slice(cc * c, (cc + 1) * c)
        qc, kc, vc = q[sl], k[sl], v[sl]
        kcb = kc.astype(BF16)
        gcols = [g_cols[dr][sl] for dr in range(2)]
        grows = [jnp.sum(jnp.where(eye, g, 0.0), axis=0, keepdims=True) for g in gcols]
        kbs = [kc * b_cols[dr][sl] for dr in range(2)]
        vbs = [vc * b_cols[dr][sl] for dr in range(2)]
        egs = [jnp.exp(g) for g in gcols]
        dif_f = gcols[0] - grows[0]
        dif_b = gcols[1] - grows[1]
        decay_f = jnp.where(lower, jnp.exp(jnp.where(lower, dif_f, 0.0)), 0.0)
        decay_b = jnp.where(upper, jnp.exp(jnp.where(upper, dif_b, 0.0)), 0.0)
        decay_bt = jnp.where(lower, jnp.exp(jnp.where(lower, -dif_b, 0.0)), 0.0)
        kq = lax.dot_general(jnp.concatenate([kbs[0], qc], axis=0).astype(BF16), kcb, nt,
                             preferred_element_type=F32)
        kkt_b = lax.dot_general(kcb, kbs[1].astype(BF16), nt, preferred_element_type=F32)
        m_f = jnp.where(strict_lower, kq[:c] * decay_f, 0.0)
        m_bt = jnp.where(strict_lower, kkt_b * decay_bt, 0.0)
        rhs_f = jnp.concatenate([vbs[0], kbs[0] * egs[0]], axis=1).astype(BF16)
        rhs_b = jnp.concatenate([vbs[1], kbs[1] * egs[1]], axis=1).astype(BF16)
        systems.append((m_f, m_bt, rhs_f, rhs_b, sl, hs))
        attns = (kq[c:] * decay_f, kq[c:] * decay_b)
        for dr in range(2):
            gcol = gcols[dr]
            glast = gcol[c - 1:c] if dr == 0 else gcol[0:1]
            qg_ref[dr, 0, sl, hs] = (qc * egs[dr]).astype(BF16)
            kd_ref[dr, 0, sl, hs] = (kc * jnp.exp(glast - gcol)).astype(BF16)
            attn_ref[dr, 0, hh, sl, :] = attns[dr].astype(BF16)
            eg_ref[dr, 0, hh, 0, cc:cc + 1, :] = jnp.broadcast_to(jnp.exp(glast), (1, LANES))
    return systems


def _dn_prep(qkv, conv_w, ab, alog_row, dt_row, *, n_heads, dk, row_w, tt):
    gc, beta = _dn_gates(ab, alog_row, dt_row, n_heads=n_heads, tt=tt)
    b, n, _ = qkv.shape
    hh = n_heads
    ng = n // tt
    c = DN_CHUNK
    assert tt % row_w == 0 and tt % c == 0 and tt // c <= SUBLANES
    hp = 2 if hh % 2 == 0 else 1
    nhb = hh // hp
    kern = functools.partial(_dn_prep_kernel, row_w=row_w, n_heads=hh, dk=dk, hp=hp)
    xspec = lambda off: pl.BlockSpec((1, tt, hp * dk), lambda bi, h, g: (bi, g, off + h))
    cspec = lambda off: pl.BlockSpec((conv_w.shape[0], hp * dk), lambda bi, h, g: (0, off + h))
    ospec = pl.BlockSpec((2, 1, tt, hp * dk), lambda bi, h, g: (0, bi, g, h))
    return pl.pallas_call(
        kern,
        grid=(b, nhb, ng),
        in_specs=[xspec(0), xspec(nhb), xspec(2 * nhb), cspec(0), cspec(nhb), cspec(2 * nhb),
                  pl.BlockSpec((1, tt, LANES), lambda bi, h, g: (bi, g, 0)),
                  pl.BlockSpec((1, tt, LANES), lambda bi, h, g: (bi, g, 0))],
        out_specs=[ospec, ospec, ospec, ospec,
                   pl.BlockSpec((2, 1, hp, tt, c), lambda bi, h, g: (0, bi, h, g, 0)),
                   pl.BlockSpec((2, 1, hp, 1, SUBLANES, LANES), lambda bi, h, g: (0, bi, h, g, 0, 0))],
        out_shape=[jax.ShapeDtypeStruct((2, b, n, hh * dk), F32),
                   jax.ShapeDtypeStruct((2, b, n, hh * dk), BF16),
                   jax.ShapeDtypeStruct((2, b, n, hh * dk), BF16),
                   jax.ShapeDtypeStruct((2, b, n, hh * dk), BF16),
                   jax.ShapeDtypeStruct((2, b, hh, n, c), BF16),
                   jax.ShapeDtypeStruct((2, b, hh, ng, SUBLANES, LANES), F32)],
        compiler_params=_cparams(("parallel", "parallel", "arbitrary")),
    )(qkv, qkv, qkv, conv_w, conv_w, conv_w, gc, beta)


def _dn_scan_kernel(uf_ref, wf_ref, qf_ref, kf_ref, af_ref, ef_ref,
                    ub_ref, wb_ref, qb_ref, kb_ref, ab_ref, eb_ref, s0_ref,
                    of_ref, ob_ref, sout_ref, s_scr, *, hp, dk):
    g = pl.program_id(2)
    ng = pl.num_programs(2)
    c = DN_CHUNK
    tt = uf_ref.shape[2]
    ncs = tt // c

    @pl.when(g == 0)
    def _():
        s_scr[...] = s0_ref[:, 0]

    refs = ((uf_ref, wf_ref, qf_ref, kf_ref, af_ref, ef_ref, of_ref),
            (ub_ref, wb_ref, qb_ref, kb_ref, ab_ref, eb_ref, ob_ref))
    for step in range(ncs):
        for dr in range(2):
            u_ref, w_ref, q_ref, k_ref, a_ref, e_ref, o_ref = refs[dr]
            cc = step if dr == 0 else ncs - 1 - step
            sl = slice(cc * c, (cc + 1) * c)
            for hh in range(hp):
                hs = slice(hh * dk, (hh + 1) * dk)
                s = s_scr[dr, hh]
                wq = jnp.concatenate([w_ref[0, 0, sl, hs], q_ref[0, 0, sl, hs]], axis=0)
                r = jnp.dot(wq, s.astype(BF16), preferred_element_type=F32)
                vnew = u_ref[0, 0, sl, hs] - r[:c]
                vnb = vnew.astype(BF16)
                o_ref[0, sl, hs] = r[c:] + jnp.dot(a_ref[0, 0, hh, sl, :], vnb, preferred_element_type=F32)
                upd = lax.dot_general(k_ref[0, 0, sl, hs], vnb, (((0,), (0,)), ((), ())),
                                      preferred_element_type=F32)
                s_scr[dr, hh] = s * e_ref[0, 0, hh, 0, cc:cc + 1, :] + upd

    @pl.when(g == ng - 1)
    def _():
        sout_ref[:, 0] = s_scr[...]


def _dn_scan(u, w, qg, kd, attn, eg, s0, *, dk, tt, hp):
    _, b, n, hd = u.shape
    hh = hd // dk
    ng = n // tt
    c = DN_CHUNK
    kern = functools.partial(_dn_scan_kernel, hp=hp, dk=dk)

    def specs(dr):
        gi = (lambda g: g) if dr == 0 else (lambda g: ng - 1 - g)
        tok = pl.BlockSpec((1, 1, tt, hp * dk), lambda bi, h, g: (dr, bi, gi(g), h))
        return [tok, tok, tok, tok,
                pl.BlockSpec((1, 1, hp, tt, c), lambda bi, h, g: (dr, bi, h, gi(g), 0)),
                pl.BlockSpec((1, 1, hp, 1, SUBLANES, LANES), lambda bi, h, g: (dr, bi, h, gi(g), 0, 0))]

    sspec = pl.BlockSpec((2, 1, hp, dk, dk), lambda bi, h, g: (0, bi, h, 0, 0))
    return pl.pallas_call(
        kern,
        grid=(b, hh // hp, ng),
        in_specs=specs(0) + specs(1) + [sspec],
        out_specs=[pl.BlockSpec((1, tt, hp * dk), lambda bi, h, g: (bi, g, h)),
                   pl.BlockSpec((1, tt, hp * dk), lambda bi, h, g: (bi, ng - 1 - g, h)),
                   sspec],
        out_shape=[jax.ShapeDtypeStruct((b, n, hd), F32),
                   jax.ShapeDtypeStruct((b, n, hd), F32),
                   jax.ShapeDtypeStruct((2, b, hh, dk, dk), F32)],
        scratch_shapes=[pltpu.VMEM((2, hp, dk, dk), F32)],
        compiler_params=_cparams(("parallel", "parallel", "arbitrary")),
    )(u, w, qg, kd, attn, eg, u, w, qg, kd, attn, eg, s0)


def _hy_hidden_kernel(z_ref, w1_ref, b1_ref, w2_ref, b2_ref, fr_ref, o_ref):
    fr = fr_ref[...]
    h1 = jnp.sin(fr * (jnp.dot(z_ref[...].astype(BF16), w1_ref[...].astype(BF16),
                               preferred_element_type=F32) + b1_ref[...]))
    o_ref[...] = jnp.sin(fr * (jnp.dot(h1.astype(BF16), w2_ref[...].astype(BF16),
                                       preferred_element_type=F32) + b2_ref[...]))


def _hy_filter_kernel(hid_ref, w3_ref, b3_ref, t_ref, delta_ref, o_ref):
    f = jnp.dot(hid_ref[...].astype(BF16), w3_ref[...].astype(BF16), preferred_element_type=F32) + b3_ref[...]
    o_ref[...] = f * jnp.exp(-t_ref[...] * delta_ref[...])


def _hy_filters(n, w1, b1, w2, b2, w3, b3, freq, hy_w):
    emb, hid = w1.shape
    bands = (emb - 1) // 2
    t = jnp.linspace(0.0, 1.0, n, dtype=F32)
    pos = jnp.arange(n, dtype=F32)
    bnd = jnp.linspace(1e-4, bands - 1, bands, dtype=F32)
    ang = (2.0 * math.pi / n) * pos[:, None] * bnd[None, :]
    z = jnp.concatenate([t[:, None], jnp.cos(ang), -jnp.sin(ang)], axis=-1)
    embp = -(-emb // SUBLANES) * SUBLANES
    z = jnp.pad(z, ((0, 0), (0, embp - emb)))
    w1p = jnp.pad(w1, ((0, embp - emb), (0, 0)))
    full = lambda a: pl.BlockSpec(a.shape, lambda: (0,) * a.ndim)
    args = (z, w1p, b1[None, :], w2, b2[None, :], freq[None, :])
    hid2 = pl.pallas_call(
        _hy_hidden_kernel,
        in_specs=[full(a) for a in args],
        out_specs=pl.BlockSpec((n, hid), lambda: (0, 0)),
        out_shape=jax.ShapeDtypeStruct((n, hid), F32),
        compiler_params=pltpu.CompilerParams(vmem_limit_bytes=VMEM_LIMIT_BYTES),
    )(*args)
    nout = w3.shape[1]
    deltas = jnp.abs(jnp.linspace(math.log(HY_DECAY_TARGET) / HY_SLOW_DECAY,
                                  math.log(HY_DECAY_TARGET) / HY_FAST_DECAY, hy_w, dtype=F32))
    delta_row = jnp.tile(deltas, nout // hy_w)[None, :]
    tc = min(nout, 1024)
    return pl.pallas_call(
        _hy_filter_kernel,
        grid=(nout // tc,),
        in_specs=[pl.BlockSpec((n, hid), lambda j: (0, 0)),
                  pl.BlockSpec((hid, tc), lambda j: (0, j)),
                  pl.BlockSpec((1, tc), lambda j: (0, j)),
                  pl.BlockSpec((n, 1), lambda j: (0, 0)),
                  pl.BlockSpec((1, tc), lambda j: (0, j))],
        out_specs=pl.BlockSpec((n, tc), lambda j: (0, j)),
        out_shape=jax.ShapeDtypeStruct((n, nout), F32),
        compiler_params=_cparams(("parallel",)),
    )(hid2, w3, b3[None, :], t[:, None], delta_row)


def _dft_tables(n):
    n2 = FFT_N2
    big = 2 * n
    n1 = big // n2
    k1 = n1 // 2
    two_pi = 2.0 * math.pi

    def cs(idx, period):
        a = (idx % period).astype(F32) * (two_pi / period)
        return jnp.cos(a), jnp.sin(a)

    f1 = jnp.arange(n1, dtype=jnp.int32)
    s1 = jnp.arange(k1, dtype=jnp.int32)
    s2 = jnp.arange(n2, dtype=jnp.int32)
    s_full = s1[None, None, :] * n2 + s2[:, None, None]
    c, s = cs(f1[None, :, None] * s_full, big)
    f1tw = jnp.concatenate([jnp.concatenate([c, s], axis=2), jnp.concatenate([-s, c], axis=2)], axis=1)
    ct, st = jnp.swapaxes(c, 1, 2), jnp.swapaxes(s, 1, 2)
    g1tw = jnp.concatenate([jnp.concatenate([ct, -st], axis=2), jnp.concatenate([st, ct], axis=2)], axis=1) / big
    c2, sn2 = cs(s2[:, None] * s2[None, :], n2)
    f2 = jnp.concatenate([jnp.concatenate([c2, sn2], axis=1), jnp.concatenate([-sn2, c2], axis=1)], axis=0)
    g2 = jnp.concatenate([jnp.concatenate([c2, -sn2], axis=1), jnp.concatenate([sn2, c2], axis=1)], axis=0)
    return f1tw.astype(BF16), f2.astype(BF16), g2.astype(BF16), g1tw.astype(BF16)


def _hy_conv_kernel(a_ref, m_ref, hf_ref, hb_ref, skip_ref, cwa_ref, cwm_ref,
                    f1_ref, f2_ref, g2_ref, g1_ref, o_ref,
                    h_scr, b_scr, zr_scr, zi_scr, *, conv_a, n1, k1):
    n2 = FFT_N2
    k1p = k1 + FFT_ROW_PAD
    pb = 2 * n2 + FFT_ROW_PAD
    pos = lax.broadcasted_iota(jnp.int32, (n2, 1), 0)

    def stage1(real_only):
        def body(s2, carry):
            base = pl.multiple_of(s2 * k1p, SUBLANES)
            if real_only:
                x = zr_scr[pl.ds(base, k1), :].astype(BF16)
                a = jnp.dot(f1_ref[s2, :, :k1], x, preferred_element_type=F32)
            else:
                x = jnp.concatenate([zr_scr[pl.ds(base, k1), :], zi_scr[pl.ds(base, k1), :]], axis=0)
                a = jnp.dot(f1_ref[s2], x.astype(BF16), preferred_element_type=F32)
            b_scr[pl.ds(s2, n1, stride=pb), :] = a[:n1]
            b_scr[pl.ds(n2 + s2, n1, stride=pb), :] = a[n1:]
            return carry
        lax.fori_loop(0, n2, body, 0, unroll=FFT_UNROLL)

    @pl.when(pl.program_id(1) == 0)
    def _():
        for which, ref in ((0, hf_ref), (1, hb_ref)):
            def load(s1, carry, ref=ref):
                r0 = pl.multiple_of(s1 * n2, n2)
                zr_scr[pl.ds(s1, n2, stride=k1p), :] = ref[pl.ds(r0, n2), :]
                return carry
            lax.fori_loop(0, k1, load, 0, unroll=FFT_UNROLL)
            if which == 1:
                zr_scr[0:1, :] = jnp.zeros((1, zr_scr.shape[1]), F32)
            stage1(True)

            def spec(f1, carry, which=which):
                bb = pl.multiple_of(f1 * pb, SUBLANES)
                hb_ = pl.multiple_of(f1 * 2 * n2, 2 * n2)
                zz = jnp.dot(f2_ref[...], b_scr[pl.ds(bb, 2 * n2), :].astype(BF16), preferred_element_type=F32)
                if which == 0:
                    h_scr[pl.ds(hb_, 2 * n2), :] = zz
                else:
                    h_scr[pl.ds(hb_, n2), :] = h_scr[pl.ds(hb_, n2), :] + zz[:n2]
                    h_scr[pl.ds(hb_ + n2, n2), :] = h_scr[pl.ds(hb_ + n2, n2), :] - zz[n2:]
                return carry
            lax.fori_loop(0, n1, spec, 0, unroll=FFT_UNROLL)

    def load_a(s1, carry):
        r0 = pl.multiple_of(s1 * n2, n2)
        for bi, scr in ((0, zr_scr), (1, zi_scr)):
            blk = a_ref[bi, pl.ds(r0, n2), :]
            if conv_a:
                blk = _row_conv(blk, cwa_ref[...], pos, n2)
            scr[pl.ds(s1, n2, stride=k1p), :] = blk
        return carry
    lax.fori_loop(0, k1, load_a, 0, unroll=FFT_UNROLL)

    stage1(False)

    def mid(f1, carry):
        bb = pl.multiple_of(f1 * pb, SUBLANES)
        hb_ = pl.multiple_of(f1 * 2 * n2, 2 * n2)
        zz = jnp.dot(f2_ref[...], b_scr[pl.ds(bb, 2 * n2), :].astype(BF16), preferred_element_type=F32)
        zr, zi = zz[:n2], zz[n2:]
        hr = h_scr[pl.ds(hb_, n2), :]
        hi = h_scr[pl.ds(hb_ + n2, n2), :]
        yy = jnp.concatenate([zr * hr - zi * hi, zr * hi + zi * hr], axis=0).astype(BF16)
        b_scr[pl.ds(bb, 2 * n2), :] = jnp.dot(g2_ref[...], yy, preferred_element_type=F32)
        return carry
    lax.fori_loop(0, n1, mid, 0, unroll=FFT_UNROLL)

    skip = skip_ref[...]

    def last(s2, carry):
        base = pl.multiple_of(s2 * k1p, SUBLANES)
        ww = jnp.concatenate([b_scr[pl.ds(s2, n1, stride=pb), :],
                              b_scr[pl.ds(n2 + s2, n1, stride=pb), :]], axis=0).astype(BF16)
        y = jnp.dot(g1_ref[s2], ww, preferred_element_type=F32)
        zr_scr[pl.ds(base, k1), :] = y[:k1] + zr_scr[pl.ds(base, k1), :] * skip
        zi_scr[pl.ds(base, k1), :] = y[k1:] + zi_scr[pl.ds(base, k1), :] * skip
        return carry
    lax.fori_loop(0, n2, last, 0, unroll=FFT_UNROLL)

    def fin(s1, carry):
        r0 = pl.multiple_of(s1 * n2, n2)
        for bi, scr in ((0, zr_scr), (1, zi_scr)):
            mm = _row_conv(m_ref[bi, pl.ds(r0, n2), :], cwm_ref[...], pos, n2)
            o_ref[bi, pl.ds(r0, n2), :] = mm * scr[pl.ds(s1, n2, stride=k1p), :]
        return carry
    lax.fori_loop(0, k1, fin, 0, unroll=FFT_UNROLL)


def _hy_conv(a, a_col0, m, m_col0, filt, f_col_fwd, f_col_bwd, skip, cwa, cwm, tables, *, conv_a, hy_w):
    b, n, _ = a.shape
    assert b % 2 == 0 and GRID_W == FFT_N2 and n % FFT_N2 == 0
    n2 = FFT_N2
    n1 = 2 * n // n2
    k1 = n1 // 2
    k1p = k1 + FFT_ROW_PAD
    pb = 2 * n2 + FFT_ROW_PAD
    cw = LANES
    f1tw, f2, g2, g1tw = tables
    kern = functools.partial(_hy_conv_kernel, conv_a=conv_a, n1=n1, k1=k1)
    once = pl.Buffered(1)
    return pl.pallas_call(
        kern,
        grid=(hy_w // cw, b // 2),
        in_specs=[pl.BlockSpec((2, n, cw), lambda c, p: (p, 0, a_col0 + c), pipeline_mode=once),
                  pl.BlockSpec((2, n, cw), lambda c, p: (p, 0, m_col0 + c), pipeline_mode=once),
                  pl.BlockSpec((n, cw), lambda c, p: (0, f_col_fwd + c), pipeline_mode=once),
                  pl.BlockSpec((n, cw), lambda c, p: (0, f_col_bwd + c), pipeline_mode=once),
                  pl.BlockSpec((1, cw), lambda c, p: (0, c)),
                  pl.BlockSpec((cwa.shape[0], cw), lambda c, p: (0, a_col0 + c if conv_a else 0)),
                  pl.BlockSpec((cwm.shape[0], cw), lambda c, p: (0, m_col0 + c)),
                  pl.BlockSpec(f1tw.shape, lambda c, p: (0, 0, 0), pipeline_mode=once),
                  pl.BlockSpec(f2.shape, lambda c, p: (0, 0)),
                  pl.BlockSpec(g2.shape, lambda c, p: (0, 0)),
                  pl.BlockSpec(g1tw.shape, lambda c, p: (0, 0, 0), pipeline_mode=once)],
        out_specs=pl.BlockSpec((2, n, cw), lambda c, p: (p, 0, c)),
        out_shape=jax.ShapeDtypeStruct((b, n, hy_w), F32),
        scratch_shapes=[pltpu.VMEM((n1 * 2 * n2, cw), F32),
                        pltpu.VMEM((n1 * pb, cw), F32),
                        pltpu.VMEM((n2 * k1p, cw), F32),
                        pltpu.VMEM((n2 * k1p, cw), F32)],
        compiler_params=_cparams(("parallel", "arbitrary")),
    )(a, m, filt, filt, skip, cwa, cwm, f1tw, f2, g2, g1tw)


def _merge_kernel(of_ref, ob_ref, z_ref, y_ref, ga_ref, gb_ref, nw_ref, wa_ref, wh_ref, o_ref,
                  on_scr, yb_scr, *, dv):
    @pl.when(pl.program_id(1) == 0)
    def _():
        nw = nw_ref[...]
        for hh in range(of_ref.shape[1] // dv):
            hs = slice(hh * dv, (hh + 1) * dv)
            o = of_ref[:, hs] + ob_ref[:, hs]
            o = o * lax.rsqrt(jnp.mean(o * o, axis=-1, keepdims=True) + RMS_EPS) * nw
            on_scr[:, hs] = (o * _silu(z_ref[:, hs])).astype(BF16)
        yb_scr[...] = y_ref[...].astype(BF16)

    pa = jnp.dot(on_scr[...], wa_ref[...], preferred_element_type=F32)
    ph = jnp.dot(yb_scr[...], wh_ref[...], preferred_element_type=F32)
    o_ref[...] = (jax.nn.sigmoid(ga_ref[...]) * pa + jax.nn.sigmoid(gb_ref[...]) * ph).astype(o_ref.dtype)


def _merge(o_f, o_b, z, z_blk, y_hy, gate, norm_w, w_pa, w_ph, *, dv, tm, tn):
    m, dvw = o_f.shape
    d = w_pa.shape[1]
    tm = min(tm, m)
    tn = min(tn, d)
    nj = d // tn
    kern = functools.partial(_merge_kernel, dv=dv)
    row = lambda w_: pl.BlockSpec((tm, w_), lambda i, j: (i, 0))
    return pl.pallas_call(
        kern,
        grid=(m // tm, nj),
        in_specs=[row(dvw), row(dvw), pl.BlockSpec((tm, dvw), lambda i, j: (i, z_blk)), row(y_hy.shape[1]),
                  pl.BlockSpec((tm, tn), lambda i, j: (i, j)),
                  pl.BlockSpec((tm, tn), lambda i, j: (i, nj + j)),
                  pl.BlockSpec((1, dv), lambda i, j: (0, 0)),
                  pl.BlockSpec((dvw, tn), lambda i, j: (0, j)),
                  pl.BlockSpec((y_hy.shape[1], tn), lambda i, j: (0, j))],
        out_specs=pl.BlockSpec((tm, tn), lambda i, j: (i, j)),
        out_shape=jax.ShapeDtypeStruct((m, d), BF16),
        scratch_shapes=[pltpu.VMEM((tm, dvw), BF16), pltpu.VMEM((tm, y_hy.shape[1]), BF16)],
        compiler_params=_cparams(("parallel", "arbitrary")),
    )(o_f, o_b, z, y_hy, gate, gate, norm_w, w_pa, w_ph)


def _layer_norm(x, g, b):
    mu = jnp.mean(x, axis=-1, keepdims=True)
    xc = x - mu
    var = jnp.mean(xc * xc, axis=-1, keepdims=True)
    return xc * lax.rsqrt(var + LN_EPS) * g + b


def _outproj_kernel(mg_ref, x_ref, wo_ref, g1_ref, lg_ref, lb_ref, sc_ref, sh_ref, wq_ref, sk_ref,
                    x1_ref, h2_ref, s_ref, *, half):
    mix = jnp.dot(mg_ref[...], wo_ref[...], preferred_element_type=F32)
    x1 = _layer_norm(DEEPNORM_ALPHA * x_ref[...] + g1_ref[0] * mix, lg_ref[...], lb_ref[...])
    x1_ref[...] = x1
    h2 = (x1 * (1.0 + sc_ref[0]) + sh_ref[0]).astype(BF16)
    h2_ref[...] = h2
    qp = jnp.dot(h2, wq_ref[...], preferred_element_type=F32).astype(BF16)
    for j in range(sk_ref.shape[0]):
        js = slice(j * half, (j + 1) * half)
        s_ref[j] = lax.dot_general(sk_ref[j], qp[:, js], (((1,), (1,)), ((), ())), preferred_element_type=F32)


def _outproj(merged, x2d, w_out, g1, ln_g, ln_b, sc2, sh2, wq, subkeys, *, tm):
    m, d = x2d.shape
    r = g1.shape[0]
    tm = min(tm, m // r)
    tpm = (m // r) // tm
    nsk, nkeys, half = subkeys.shape
    assert nkeys == LANES
    kern = functools.partial(_outproj_kernel, half=half)
    row = lambda w_: pl.BlockSpec((tm, w_), lambda i: (i, 0))
    mod = pl.BlockSpec((1, 1, d), lambda i: (i // tpm, 0, 0))
    vec = pl.BlockSpec((1, d), lambda i: (0, 0))
    return pl.pallas_call(
        kern,
        grid=(m // tm,),
        in_specs=[row(d), row(d), pl.BlockSpec(w_out.shape, lambda i: (0, 0), pipeline_mode=pl.Buffered(1)),
                  mod, vec, vec, mod, mod,
                  pl.BlockSpec(wq.shape, lambda i: (0, 0), pipeline_mode=pl.Buffered(1)),
                  pl.BlockSpec(subkeys.shape, lambda i: (0, 0, 0))],
        out_specs=[row(d), row(d), pl.BlockSpec((nsk, nkeys, tm), lambda i: (0, 0, i))],
        out_shape=[jax.ShapeDtypeStruct((m, d), F32),
                   jax.ShapeDtypeStruct((m, d), BF16),
                   jax.ShapeDtypeStruct((nsk, nkeys, m), F32)],
        compiler_params=_cparams(("parallel",)),
    )(merged, x2d, w_out, g1, ln_g, ln_b, sc2, sh2, wq, subkeys)


def _cand_tables():
    pairs = [(a, b) for a in range(PEER_TOPK) for b in range(PEER_TOPK) if (a + 1) * (b + 1) <= PEER_TOPK]
    ia = jnp.array([p[0] for p in pairs], dtype=jnp.int32)
    ib = jnp.array([p[1] for p in pairs], dtype=jnp.int32)
    ncand = len(pairs)
    ranks = jnp.arange(LANES, dtype=jnp.int32)
    p1 = (jnp.pad(ia, (0, LANES - ncand), constant_values=-1)[:, None] == ranks[None, :]).astype(F32)
    p2 = (jnp.pad(ib, (0, LANES - ncand), constant_values=-1)[:, None] == ranks[None, :]).astype(F32)
    return p1, p2, ncand


def _peer_topk_kernel(s_ref, p1_ref, p2_ref, p1t_ref, c1_ref, r2_ref, e1_ref, e2_ref, *, n_heads, ncand):
    tm = s_ref.shape[2]
    neg = -jnp.inf
    hi = lax.Precision.HIGHEST
    no_rank = float(LANES)

    def pop_max(x, row_id):
        mx = jnp.max(x, axis=0, keepdims=True)
        first = jnp.min(jnp.where(x == mx, row_id, no_rank), axis=0, keepdims=True)
        hit = row_id == first
        return mx, hit, jnp.where(hit, neg, x)

    key_id = lax.broadcasted_iota(jnp.int32, (LANES, tm), 0).astype(F32)
    cand_id = key_id
    pad_rows = jnp.zeros((LANES - PEER_TOPK, tm), F32)

    def sorted_top(s):
        tops = []
        rank = jnp.full(s.shape, no_rank, F32)
        x = s
        for a in range(PEER_TOPK):
            mx, hit, x = pop_max(x, key_id)
            tops.append(mx)
            rank = jnp.where(hit, float(a), rank)
        return jnp.concatenate(tops + [pad_rows], axis=0), rank

    for h in range(n_heads):
        s1 = s_ref[2 * h]
        s2 = s_ref[2 * h + 1]
        t1, rank1 = sorted_top(s1)
        t2, rank2 = sorted_top(s2)
        cand = (jnp.dot(p1_ref[...], t1, precision=hi, preferred_element_type=F32)
                + jnp.dot(p2_ref[...], t2, precision=hi, preferred_element_type=F32))
        x = jnp.where(cand_id < ncand, cand, neg)
        sel = jnp.zeros(x.shape, F32)
        zsum = None
        cmax = None
        for r in range(PEER_TOPK):
            mx, hit, x = pop_max(x, cand_id)
            sel = jnp.where(hit, 1.0, sel)
            if r == 0:
                cmax = mx
                zsum = jnp.ones_like(mx)
            else:
                zsum = zsum + jnp.exp(mx - cmax)
        cnt = jnp.dot(p1t_ref[...].astype(BF16), sel.astype(BF16), preferred_element_type=F32)
        count1 = jnp.zeros(s1.shape, F32)
        for a in range(PEER_TOPK):
            count1 = jnp.where(rank1 == a, cnt[a:a + 1], count1)
        c1_ref[h] = count1
        r2_ref[h] = rank2.astype(r2_ref.dtype)
        e1_ref[h] = jnp.exp(s1 - t1[0:1]) / zsum
        e2_ref[h] = jnp.exp(s2 - t2[0:1]).astype(e2_ref.dtype)


def _peer_topk(scores, n_heads, *, tm):
    nsk, nkeys, m = scores.shape
    tm = min(tm, m)
    p1, p2, ncand = _cand_tables()
    kern = functools.partial(_peer_topk_kernel, n_heads=n_heads, ncand=ncand)
    full = lambda a: pl.BlockSpec(a.shape, lambda i: (0, 0))
    out = pl.BlockSpec((n_heads, nkeys, tm), lambda i: (0, 0, i))
    shp = lambda dt: jax.ShapeDtypeStruct((n_heads, nkeys, m), dt)
    p1t = p1.T
    return pl.pallas_call(
        kern,
        grid=(m // tm,),
        in_specs=[pl.BlockSpec((nsk, nkeys, tm), lambda i: (0, 0, i)), full(p1), full(p2), full(p1t)],
        out_specs=[out, out, out, out],
        out_shape=[shp(F32), shp(BF16), shp(F32), shp(BF16)],
        compiler_params=_cparams(("parallel",)),
    )(scores, p1, p2, p1t)


def _peer_dense_kernel(h2_ref, u_ref, v_ref, r2_ref, e2_ref, c1_ref, e1_ref, x1_ref, g2_ref, lg_ref, lb_ref,
                       o_ref, acc_scr, a_scr, p_scr, *, n_heads, ti):
    e = pl.program_id(1)

    @pl.when(e == 0)
    def _():
        acc_scr[...] = jnp.zeros(acc_scr.shape, F32)

    a_scr[...] = lax.dot_general(u_ref[...], h2_ref[...], (((1,), (1,)), ((), ())), preferred_element_type=F32)
    tm = h2_ref.shape[0]
    for ii in range(ti):
        rs = slice(ii * LANES, (ii + 1) * LANES)
        for cb in range(tm // LANES):
            cs = slice(cb * LANES, (cb + 1) * LANES)
            gate = None
            for h in range(n_heads):
                keep = r2_ref[h, :, cs] < c1_ref[h, ii:ii + 1, cs].astype(BF16)
                term = jnp.where(keep, e2_ref[h, :, cs], jnp.zeros((), BF16)) * e1_ref[h, ii:ii + 1, cs].astype(BF16)
                gate = term if gate is None else gate + term
            a = a_scr[rs, cs]
            act = 0.5 * a * (1.0 + lax.erf(a * (2.0 ** -0.5)))
            p_scr[rs, cs] = gate * act.astype(BF16)
    acc_scr[...] += lax.dot_general(p_scr[...], v_ref[...], (((0,), (0,)), ((), ())), preferred_element_type=F32)

    @pl.when(e == pl.num_programs(1) - 1)
    def _():
        o_ref[...] = _layer_norm(DEEPNORM_ALPHA * x1_ref[...] + g2_ref[0] * acc_scr[...],
                                 lg_ref[...], lb_ref[...])


def _peer_dense(h2, u_tab, v_tab, rank2, e2, count1, e1, x1, g2, ln_g, ln_b, *, n_heads, ti, tm):
    m, d = h2.shape
    r = g2.shape[0]
    tm = min(tm, m // r)
    tpm = (m // r) // tm
    ne = u_tab.shape[0]
    nkeys = rank2.shape[1]
    te = ti * nkeys
    kern = functools.partial(_peer_dense_kernel, n_heads=n_heads, ti=ti)
    row = lambda w_: pl.BlockSpec((tm, w_), lambda i, e: (i, 0), pipeline_mode=pl.Buffered(1))
    vec = pl.BlockSpec((1, d), lambda i, e: (0, 0))
    tbl = pl.BlockSpec((te, d), lambda i, e: (e, 0))
    second = pl.BlockSpec((n_heads, nkeys, tm), lambda i, e: (0, 0, i), pipeline_mode=pl.Buffered(1))
    first = pl.BlockSpec((n_heads, ti, tm), lambda i, e: (0, e, i))
    return pl.pallas_call(
        kern,
        grid=(m // tm, ne // te),
        in_specs=[row(d), tbl, tbl, second, second, first, first, row(d),
                  pl.BlockSpec((1, 1, d), lambda i, e: (i // tpm, 0, 0)), vec, vec],
        out_specs=pl.BlockSpec((tm, d), lambda i, e: (i, 0)),
        out_shape=jax.ShapeDtypeStruct((m, d), F32),
        scratch_shapes=[pltpu.VMEM((tm, d), F32), pltpu.VMEM((te, tm), F32), pltpu.VMEM((te, tm), BF16)],
        compiler_params=_cparams(("parallel", "arbitrary")),
    )(h2, u_tab, v_tab, rank2, e2, count1, e1, x1, g2, ln_g, ln_b)


def _block(x, c, ctx, c_ctx, w_ada, b_ada, w_in, dn_conv_w, dn_a_log, dn_dt_bias, dn_norm_w, hy_conv_w, hy_w1, hy_b1, hy_w2, hy_b2, hy_w3, hy_b3, hy_freq, hy_skip, w_branch_dn, w_branch_hy, w_out, ln1_g, ln1_b, peer_wq, peer_subkeys, peer_u, peer_v, ln2_g, ln2_b):
    b, n, d = x.shape
    n_ctx = ctx.shape[1]
    assert w_ada.shape[0] == DEPTH == 1, "single-layer block: the context stream update is never read"
    hh = dn_a_log.shape[2]
    dk = dn_conv_w.shape[2] // (3 * hh)
    qk_w = hh * dk
    hy_w = hy_skip.shape[2]
    p_heads, _, nkeys, half = peer_subkeys.shape[1:]
    assert dk == LANES and nkeys == LANES and 4 * hh <= LANES and n % GRID_W == 0
    off_z = 3 * qk_w
    off_ab = 4 * qk_w
    off_hy = off_ab + 4 * hh
    off_gate = off_hy + 3 * hy_w
    assert w_in.shape[2] == off_gate + 2 * d

    cc = jnp.zeros((SUBLANES, d), F32).at[:b].set(c).at[b].set(c_ctx)
    mods = _ada(cc, w_ada[0], b_ada[0][None, :])
    sh1, sc1, g1, sh2, sc2, g2 = [mods[:b, i * d:(i + 1) * d][:, None, :] for i in range(6)]
    sh1c, sc1c = [mods[b:b + 1, i * d:(i + 1) * d][:, None, :] for i in range(2)]

    w = w_in[0]
    w_qkvz = w[:, :off_ab].astype(BF16)
    w_ab = jnp.pad(w[:, off_ab:off_hy], ((0, 0), (0, LANES - 4 * hh))).astype(BF16)
    w_hy = w[:, off_hy:off_gate].astype(BF16)
    w_gate = w[:, off_gate:].astype(BF16)

    x2d = x.reshape(b * n, d)
    c2d = ctx.reshape(b * n_ctx, d)
    proj_qkvz = _modmm(x2d, sc1, sh1, w_qkvz, tm=512, tn=2048)
    proj_ab = _modmm(x2d, sc1, sh1, w_ab, tm=512, tn=LANES)
    proj_hy = _modmm(x2d, sc1, sh1, w_hy, tm=512, tn=2048)
    proj_gate = _modmm(x2d, sc1, sh1, w_gate, tm=512, tn=2048)
    projc_qkvz = _modmm(c2d, sc1c, sh1c, w_qkvz, tm=256, tn=2048)
    projc_ab = _modmm(c2d, sc1c, sh1c, w_ab, tm=256, tn=LANES)

    def decay_row(a):
        row = jnp.concatenate([a, jnp.zeros_like(a)], axis=1).reshape(-1)
        return jnp.pad(row, (0, LANES - 4 * hh))[None, :]

    alog_row = decay_row(dn_a_log[0])
    dt_row = decay_row(dn_dt_bias[0])

    conv_w = dn_conv_w[0]
    prep_c = _dn_prep(projc_qkvz.reshape(b, n_ctx, -1), conv_w, projc_ab.reshape(b, n_ctx, LANES),
                      alog_row, dt_row, n_heads=hh, dk=dk, row_w=n_ctx, tt=n_ctx)
    prep_l = _dn_prep(proj_qkvz.reshape(b, n, -1), conv_w, proj_ab.reshape(b, n, LANES),
                      alog_row, dt_row, n_heads=hh, dk=dk, row_w=GRID_W, tt=4 * GRID_W)
    hp = 2 if hh % 2 == 0 else 1
    s_zero = jnp.zeros((2, b, hh, dk, dk), F32)
    _, _, s_ctx = _dn_scan(*prep_c, s_zero, dk=dk, tt=n_ctx, hp=hp)
    o_f, o_b, _ = _dn_scan(*prep_l, s_ctx, dk=dk, tt=4 * GRID_W, hp=hp)

    filt = _hy_filters(n, hy_w1[0], hy_b1[0], hy_w2[0], hy_b2[0], hy_w3[0], hy_b3[0], hy_freq[0], hy_w)
    tables = _dft_tables(n)
    hy3 = proj_hy.reshape(b, n, 3 * hy_w)
    cwh = hy_conv_w[0]
    cb = hy_w // LANES
    y1 = _hy_conv(hy3, 0, hy3, cb, filt, 0, 2 * cb, hy_skip[0, 0:1], cwh, cwh, tables, conv_a=True, hy_w=hy_w)
    y_hy = _hy_conv(y1, 0, hy3, 2 * cb, filt, cb, 3 * cb, hy_skip[0, 1:2], cwh, cwh, tables, conv_a=False, hy_w=hy_w)

    merged = _merge(o_f.reshape(b * n, qk_w), o_b.reshape(b * n, qk_w), proj_qkvz, off_z // qk_w,
                    y_hy.reshape(b * n, hy_w), proj_gate, dn_norm_w[0][None, :],
                    w_branch_dn[0].astype(BF16), w_branch_hy[0].astype(BF16), dv=dk, tm=256, tn=1024)
    sk = peer_subkeys[0].reshape(p_heads * 2, nkeys, half).astype(BF16)
    x1, h2, scores = _outproj(merged, x2d, w_out[0].astype(BF16), g1, ln1_g[0][None, :], ln1_b[0][None, :],
                              sc2, sh2, peer_wq[0].astype(BF16), sk, tm=256)

    count1, rank2, e1, e2 = _peer_topk(scores, p_heads, tm=256)
    out = _peer_dense(h2, peer_u[0].astype(BF16), peer_v[0].astype(BF16), rank2, e2, count1, e1,
                      x1, g2, ln2_g[0][None, :], ln2_b[0][None, :], n_heads=p_heads, ti=SUBLANES, tm=512)
    stages = dict(o_f=o_f, o_b=o_b, s_ctx=s_ctx, filt=filt, y1=y1, y_hy=y_hy, merged=merged, x1=x1, scores=scores,
                  count1=count1, rank2=rank2, e1=e1, e2=e2)
    return out.reshape(b, n, d), stages


def kernel(x, c, ctx, c_ctx, w_ada, b_ada, w_in, dn_conv_w, dn_a_log, dn_dt_bias, dn_norm_w, hy_conv_w, hy_w1, hy_b1, hy_w2, hy_b2, hy_w3, hy_b3, hy_freq, hy_skip, w_branch_dn, w_branch_hy, w_out, ln1_g, ln1_b, peer_wq, peer_subkeys, peer_u, peer_v, ln2_g, ln2_b):
    out, _ = _block(x, c, ctx, c_ctx, w_ada, b_ada, w_in, dn_conv_w, dn_a_log, dn_dt_bias, dn_norm_w, hy_conv_w, hy_w1, hy_b1, hy_w2, hy_b2, hy_w3, hy_b3, hy_freq, hy_skip, w_branch_dn, w_branch_hy, w_out, ln1_g, ln1_b, peer_wq, peer_subkeys, peer_u, peer_v, ln2_g, ln2_b)
    return out
```

```python
import functools
import math

import jax
import jax.numpy as jnp
from jax import lax
from jax.experimental import pallas as pl
from jax.experimental.pallas import tpu as pltpu

F32 = jnp.float32
BF16 = jnp.bfloat16

GRID_W = 64
DN_CHUNK = 64
PEER_TOPK = 16
DEPTH = 1
DEEPNORM_ALPHA = (2 * DEPTH) ** 0.25
LN_EPS = 1e-5
RMS_EPS = 1e-6
L2_EPS = 1e-6
HY_DECAY_TARGET = 1e-2
HY_FAST_DECAY = 0.3
HY_SLOW_DECAY = 1.5

LANES = 128
SUBLANES = 8
VMEM_LIMIT_BYTES = 56 * 1024 * 1024

FFT_N2 = 64
FFT_ROW_PAD = 8
FFT_UNROLL = 4


def _cparams(sem):
    return pltpu.CompilerParams(dimension_semantics=sem, vmem_limit_bytes=VMEM_LIMIT_BYTES)


def _silu(x):
    return x * jax.nn.sigmoid(x)


def _ada_kernel(c_ref, w_ref, b_ref, o_ref):
    s = _silu(c_ref[...])
    o_ref[...] = jnp.dot(s.astype(BF16), w_ref[...].astype(BF16), preferred_element_type=F32) + b_ref[...]


def _ada(cc, w, b):
    d, n6 = w.shape
    tn = min(n6, 1536)
    return pl.pallas_call(
        _ada_kernel,
        grid=(n6 // tn,),
        in_specs=[pl.BlockSpec((SUBLANES, d), lambda j: (0, 0)),
                  pl.BlockSpec((d, tn), lambda j: (0, j)),
                  pl.BlockSpec((1, tn), lambda j: (0, j))],
        out_specs=pl.BlockSpec((SUBLANES, tn), lambda j: (0, j)),
        out_shape=jax.ShapeDtypeStruct((SUBLANES, n6), F32),
        compiler_params=_cparams(("arbitrary",)),
    )(cc, w, b)


def _modmm_kernel(x_ref, sc_ref, sh_ref, w_ref, o_ref, h_scr):
    @pl.when(pl.program_id(1) == 0)
    def _():
        h_scr[...] = (x_ref[...] * (1.0 + sc_ref[0]) + sh_ref[0]).astype(BF16)

    o_ref[...] = jnp.dot(h_scr[...], w_ref[...], preferred_element_type=F32).astype(o_ref.dtype)


def _modmm(x2d, sc, sh, w, *, tm, tn, out_dtype=F32):
    m, d = x2d.shape
    ng = w.shape[1]
    r = sc.shape[0]
    tm = min(tm, m // r)
    tn = min(tn, ng)
    tiles_per_mod = (m // r) // tm
    return pl.pallas_call(
        _modmm_kernel,
        grid=(m // tm, ng // tn),
        in_specs=[pl.BlockSpec((tm, d), lambda i, j: (i, 0)),
                  pl.BlockSpec((1, 1, d), lambda i, j: (i // tiles_per_mod, 0, 0)),
                  pl.BlockSpec((1, 1, d), lambda i, j: (i // tiles_per_mod, 0, 0)),
                  pl.BlockSpec((d, tn), lambda i, j: (0, j))],
        out_specs=pl.BlockSpec((tm, tn), lambda i, j: (i, j)),
        out_shape=jax.ShapeDtypeStruct((m, ng), out_dtype),
        scratch_shapes=[pltpu.VMEM((tm, d), BF16)],
        compiler_params=_cparams(("parallel", "arbitrary")),
    )(x2d, sc, sh, w)


def _row_conv(x, cw, pos, row_w):
    rows = x.shape[0]
    k = cw.shape[0]
    pad = k // 2
    acc = None
    for j in range(k):
        off = j - pad
        if off == 0:
            term = x * cw[j:j + 1]
        else:
            shifted = pltpu.roll(x, (-off) % rows, axis=0)
            valid = (pos >= -off) if off < 0 else (pos < row_w - off)
            term = jnp.where(valid, shifted, 0.0) * cw[j:j + 1]
        acc = term if acc is None else acc + term
    return acc


SOLVE_BLOCK = 16


def _tri_inverse_pairs(pairs):
    c = pairs[0][0].shape[0]
    sb = SOLVE_BLOCK
    nb = c // sb
    ri = lax.broadcasted_iota(jnp.int32, (c, 2 * c), 0)
    ci = lax.broadcasted_iota(jnp.int32, (c, 2 * c), 1)
    ci = jnp.where(ci >= c, ci - c, ci)
    lane_lo = lax.broadcasted_iota(jnp.int32, (1, 2 * c), 1) < c
    off_diag = ri // sb != ci // sb
    eye2 = (ri == ci).astype(F32)
    mcats = [jnp.concatenate([m_a, m_b], axis=1) for m_a, m_b in pairs]
    m_offs = [jnp.where(off_diag, m, 0.0).astype(BF16) for m in mcats]
    xs = [[eye2[i * sb:(i + 1) * sb] for i in range(nb)] for _ in pairs]
    for bi in range(nb):
        rs = slice(bi * sb, (bi + 1) * sb)
        blks = [x[bi] for x in xs]
        if bi > 0:
            for s, x in enumerate(xs):
                xcur = jnp.concatenate(x, axis=0)
                xbd = jnp.concatenate([jnp.where(lane_lo, xcur, 0.0), jnp.where(lane_lo, 0.0, xcur)], axis=0)
                blks[s] = blks[s] - jnp.dot(m_offs[s][rs], xbd.astype(BF16), preferred_element_type=F32)
        mrs = [m[rs] for m in mcats]
        for jj in range(sb - 1):
            col = bi * sb + jj
            src = jnp.broadcast_to(jnp.where(lane_lo, col, c + col), (sb, 2 * c))
            for s in range(len(pairs)):
                mult = jnp.take_along_axis(mrs[s], src, axis=1)
                blks[s] = blks[s] - mult * blks[s][jj:jj + 1, :]
        for s, x in enumerate(xs):
            x[bi] = blks[s]
    return [jnp.concatenate(x, axis=0) for x in xs]


def _dn_gates_kernel(ab_ref, alog_ref, dt_ref, gc_ref, beta_ref, *, n_heads):
    tt = ab_ref.shape[1]
    c = DN_CHUNK
    cpos = lax.broadcasted_iota(jnp.int32, (tt, 1), 0) % c
    ab = ab_ref[0]
    z = ab + dt_ref[...]
    softplus = jnp.maximum(z, 0.0) + jnp.log1p(jnp.exp(-jnp.abs(z)))
    g_all = -jnp.exp(alog_ref[...]) * softplus
    gf = g_all
    gb = g_all
    d = 1
    while d < c:
        gf = gf + jnp.where(cpos >= d, pltpu.roll(gf, d, axis=0), 0.0)
        gb = gb + jnp.where(cpos < c - d, pltpu.roll(gb, (tt - d) % tt, axis=0), 0.0)
        d *= 2
    lane = lax.broadcasted_iota(jnp.int32, (1, LANES), 1)
    gc_ref[0] = jnp.where(lane < 2 * n_heads, gf, gb)
    beta_ref[0] = jax.nn.sigmoid(ab)


def _dn_gates(ab, alog_row, dt_row, *, n_heads, tt):
    b, n, _ = ab.shape
    blk = pl.BlockSpec((1, tt, LANES), lambda bi, g: (bi, g, 0))
    row = pl.BlockSpec((1, LANES), lambda bi, g: (0, 0))
    shp = jax.ShapeDtypeStruct((b, n, LANES), F32)
    return pl.pallas_call(
        functools.partial(_dn_gates_kernel, n_heads=n_heads),
        grid=(b, n // tt),
        in_specs=[blk, row, row],
        out_specs=[blk, blk],
        out_shape=[shp, shp],
        compiler_params=_cparams(("parallel", "parallel")),
    )(ab, alog_row, dt_row)


def _dn_prep_kernel(xq_ref, xk_ref, xv_ref, cq_ref, ck_ref, cv_ref, gc_ref, beta_ref,
                    u_ref, w_ref, qg_ref, kd_ref, attn_ref, eg_ref, *, row_w, n_heads, dk, hp):
    tt = xq_ref.shape[1]
    c = DN_CHUNK
    rows = lax.broadcasted_iota(jnp.int32, (tt, 1), 0)
    pos = rows % row_w
    lane = lax.broadcasted_iota(jnp.int32, (1, LANES), 1)
    gc = gc_ref[0]
    beta_all = beta_ref[0]
    eg_ref[...] = jnp.zeros(eg_ref.shape, F32)
    systems = []
    for hh in range(hp):
        systems += _dn_prep_head(pl.program_id(1) * hp + hh, hh, slice(hh * dk, (hh + 1) * dk), pos, lane, gc,
                                 beta_all, xq_ref, xk_ref, xv_ref, cq_ref, ck_ref, cv_ref, qg_ref, kd_ref, attn_ref,
                                 eg_ref, row_w=row_w, n_heads=n_heads, dk=dk, tt=tt)
    tinvs = _tri_inverse_pairs([(m_f, m_bt) for m_f, m_bt, _, _, _, _ in systems])
    zero_rows = jnp.zeros((c, 2 * dk), BF16)
    for tinv, (_, _, rhs_f, rhs_b, sl, hs) in zip(tinvs, systems):
        tinv = tinv.astype(BF16)
        sol_f = jnp.dot(tinv, jnp.concatenate([rhs_f, zero_rows], axis=0), preferred_element_type=F32)
        sol_b = lax.dot_general(tinv, rhs_b, (((0,), (0,)), ((), ())), preferred_element_type=F32)[c:]
        for dr, sol in enumerate((sol_f, sol_b)):
            u_ref[dr, 0, sl, hs] = sol[:, :dk]
            w_ref[dr, 0, sl, hs] = sol[:, dk:].astype(BF16)


def _dn_prep_head(h, hh, hs, pos, lane, gc, beta_all, xq_ref, xk_ref, xv_ref, cq_ref, ck_ref, cv_ref,
                  qg_ref, kd_ref, attn_ref, eg_ref, *, row_w, n_heads, dk, tt):
    c = DN_CHUNK
    systems = []

    def conv_silu(x_ref, cw_ref):
        return _silu(_row_conv(x_ref[0, :, hs], cw_ref[:, hs], pos, row_w))

    def l2n(y):
        return y * lax.rsqrt(jnp.sum(y * y, axis=-1, keepdims=True) + L2_EPS)

    q = l2n(conv_silu(xq_ref, cq_ref)) * (dk ** -0.5)
    k = l2n(conv_silu(xk_ref, ck_ref))
    v = conv_silu(xv_ref, cv_ref)

    def col(a, l):
        return jnp.sum(jnp.where(lane == l, a, 0.0), axis=1, keepdims=True)

    g_cols = (col(gc, h), col(gc, 2 * n_heads + h))
    b_cols = (col(beta_all, n_heads + h), col(beta_all, 3 * n_heads + h))

    ri = lax.broadcasted_iota(jnp.int32, (c, c), 0)
    ci = lax.broadcasted_iota(jnp.int32, (c, c), 1)
    eye = ri == ci
    lower = ri >= ci
    upper = ri <= ci
    strict_lower = ri > ci
    nt = (((1,), (1,)), ((), ()))
    for cc in range(tt // c):
        sl = slice(cc * c, (cc + 1) * c)
        qc, kc, vc = q[sl], k[sl], v[sl]
        kcb = kc.astype(BF16)
        gcols = [g_cols[dr][sl] for dr in range(2)]
        grows = [jnp.sum(jnp.where(eye, g, 0.0), axis=0, keepdims=True) for g in gcols]
        kbs = [kc * b_cols[dr][sl] for dr in range(2)]
        vbs = [vc * b_cols[dr][sl] for dr in range(2)]
        egs = [jnp.exp(g) for g in gcols]
        dif_f = gcols[0] - grows[0]
        dif_b = gcols[1] - grows[1]
        decay_f = jnp.where(lower, jnp.exp(jnp.where(lower, dif_f, 0.0)), 0.0)
        decay_b = jnp.where(upper, jnp.exp(jnp.where(upper, dif_b, 0.0)), 0.0)
        decay_bt = jnp.where(lower, jnp.exp(jnp.where(lower, -dif_b, 0.0)), 0.0)
        kq = lax.dot_general(jnp.concatenate([kbs[0], qc], axis=0).astype(BF16), kcb, nt,
                             preferred_element_type=F32)
        kkt_b = lax.dot_general(kcb, kbs[1].astype(BF16), nt, preferred_element_type=F32)
        m_f = jnp.where(strict_lower, kq[:c] * decay_f, 0.0)
        m_bt = jnp.where(strict_lower, kkt_b * decay_bt, 0.0)
        rhs_f = jnp.concatenate([vbs[0], kbs[0] * egs[0]], axis=1).astype(BF16)
        rhs_b = jnp.concatenate([vbs[1], kbs[1] * egs[1]], axis=1).astype(BF16)
        systems.append((m_f, m_bt, rhs_f, rhs_b, sl, hs))
        attns = (kq[c:] * decay_f, kq[c:] * decay_b)
        for dr in range(2):
            gcol = gcols[dr]
            glast = gcol[c - 1:c] if dr == 0 else gcol[0:1]
            qg_ref[dr, 0, sl, hs] = (qc * egs[dr]).astype(BF16)
            kd_ref[dr, 0, sl, hs] = (kc * jnp.exp(glast - gcol)).astype(BF16)
            attn_ref[dr, 0, hh, sl, :] = attns[dr].astype(BF16)
            eg_ref[dr, 0, hh, 0, cc:cc + 1, :] = jnp.broadcast_to(jnp.exp(glast), (1, LANES))
    return systems


def _dn_prep(qkv, conv_w, ab, alog_row, dt_row, *, n_heads, dk, row_w, tt):
    gc, beta = _dn_gates(ab, alog_row, dt_row, n_heads=n_heads, tt=tt)
    b, n, _ = qkv.shape
    hh = n_heads
    ng = n // tt
    c = DN_CHUNK
    assert tt % row_w == 0 and tt % c == 0 and tt // c <= SUBLANES
    hp = 2 if hh % 2 == 0 else 1
    nhb = hh // hp
    kern = functools.partial(_dn_prep_kernel, row_w=row_w, n_heads=hh, dk=dk, hp=hp)
    xspec = lambda off: pl.BlockSpec((1, tt, hp * dk), lambda bi, h, g: (bi, g, off + h))
    cspec = lambda off: pl.BlockSpec((conv_w.shape[0], hp * dk), lambda bi, h, g: (0, off + h))
    ospec = pl.BlockSpec((2, 1, tt, hp * dk), lambda bi, h, g: (0, bi, g, h))
    return pl.pallas_call(
        kern,
        grid=(b, nhb, ng),
        in_specs=[xspec(0), xspec(nhb), xspec(2 * nhb), cspec(0), cspec(nhb), cspec(2 * nhb),
                  pl.BlockSpec((1, tt, LANES), lambda bi, h, g: (bi, g, 0)),
                  pl.BlockSpec((1, tt, LANES), lambda bi, h, g: (bi, g, 0))],
        out_specs=[ospec, ospec, ospec, ospec,
                   pl.BlockSpec((2, 1, hp, tt, c), lambda bi, h, g: (0, bi, h, g, 0)),
                   pl.BlockSpec((2, 1, hp, 1, SUBLANES, LANES), lambda bi, h, g: (0, bi, h, g, 0, 0))],
        out_shape=[jax.ShapeDtypeStruct((2, b, n, hh * dk), F32),
                   jax.ShapeDtypeStruct((2, b, n, hh * dk), BF16),
                   jax.ShapeDtypeStruct((2, b, n, hh * dk), BF16),
                   jax.ShapeDtypeStruct((2, b, n, hh * dk), BF16),
                   jax.ShapeDtypeStruct((2, b, hh, n, c), BF16),
                   jax.ShapeDtypeStruct((2, b, hh, ng, SUBLANES, LANES), F32)],
        compiler_params=_cparams(("parallel", "parallel", "arbitrary")),
    )(qkv, qkv, qkv, conv_w, conv_w, conv_w, gc, beta)


def _dn_scan_kernel(uf_ref, wf_ref, qf_ref, kf_ref, af_ref, ef_ref,
                    ub_ref, wb_ref, qb_ref, kb_ref, ab_ref, eb_ref, s0_ref,
                    of_ref, ob_ref, sout_ref, s_scr, *, hp, dk):
    g = pl.program_id(2)
    ng = pl.num_programs(2)
    c = DN_CHUNK
    tt = uf_ref.shape[2]
    ncs = tt // c

    @pl.when(g == 0)
    def _():
        s_scr[...] = s0_ref[:, 0]

    refs = ((uf_ref, wf_ref, qf_ref, kf_ref, af_ref, ef_ref, of_ref),
            (ub_ref, wb_ref, qb_ref, kb_ref, ab_ref, eb_ref, ob_ref))
    for step in range(ncs):
        for dr in range(2):
            u_ref, w_ref, q_ref, k_ref, a_ref, e_ref, o_ref = refs[dr]
            cc = step if dr == 0 else ncs - 1 - step
            sl = slice(cc * c, (cc + 1) * c)
            for hh in range(hp):
                hs = slice(hh * dk, (hh + 1) * dk)
                s = s_scr[dr, hh]
                wq = jnp.concatenate([w_ref[0, 0, sl, hs], q_ref[0, 0, sl, hs]], axis=0)
                r = jnp.dot(wq, s.astype(BF16), preferred_element_type=F32)
                vnew = u_ref[0, 0, sl, hs] - r[:c]
                vnb = vnew.astype(BF16)
                o_ref[0, sl, hs] = r[c:] + jnp.dot(a_ref[0, 0, hh, sl, :], vnb, preferred_element_type=F32)
                upd = lax.dot_general(k_ref[0, 0, sl, hs], vnb, (((0,), (0,)), ((), ())),
                                      preferred_element_type=F32)
                s_scr[dr, hh] = s * e_ref[0, 0, hh, 0, cc:cc + 1, :] + upd

    @pl.when(g == ng - 1)
    def _():
        sout_ref[:, 0] = s_scr[...]


def _dn_scan(u, w, qg, kd, attn, eg, s0, *, dk, tt, hp):
    _, b, n, hd = u.shape
    hh = hd // dk
    ng = n // tt
    c = DN_CHUNK
    kern = functools.partial(_dn_scan_kernel, hp=hp, dk=dk)

    def specs(dr):
        gi = (lambda g: g) if dr == 0 else (lambda g: ng - 1 - g)
        tok = pl.BlockSpec((1, 1, tt, hp * dk), lambda bi, h, g: (dr, bi, gi(g), h))
        return [tok, tok, tok, tok,
                pl.BlockSpec((1, 1, hp, tt, c), lambda bi, h, g: (dr, bi, h, gi(g), 0)),
                pl.BlockSpec((1, 1, hp, 1, SUBLANES, LANES), lambda bi, h, g: (dr, bi, h, gi(g), 0, 0))]

    sspec = pl.BlockSpec((2, 1, hp, dk, dk), lambda bi, h, g: (0, bi, h, 0, 0))
    return pl.pallas_call(
        kern,
        grid=(b, hh // hp, ng),
        in_specs=specs(0) + specs(1) + [sspec],
        out_specs=[pl.BlockSpec((1, tt, hp * dk), lambda bi, h, g: (bi, g, h)),
                   pl.BlockSpec((1, tt, hp * dk), lambda bi, h, g: (bi, ng - 1 - g, h)),
                   sspec],
        out_shape=[jax.ShapeDtypeStruct((b, n, hd), F32),
                   jax.ShapeDtypeStruct((b, n, hd), F32),
                   jax.ShapeDtypeStruct((2, b, hh, dk, dk), F32)],
        scratch_shapes=[pltpu.VMEM((2, hp, dk, dk), F32)],
        compiler_params=_cparams(("parallel", "parallel", "arbitrary")),
    )(u, w, qg, kd, attn, eg, u, w, qg, kd, attn, eg, s0)


def _hy_hidden_kernel(z_ref, w1_ref, b1_ref, w2_ref, b2_ref, fr_ref, o_ref):
    fr = fr_ref[...]
    h1 = jnp.sin(fr * (jnp.dot(z_ref[...].astype(BF16), w1_ref[...].astype(BF16),
                               preferred_element_type=F32) + b1_ref[...]))
    o_ref[...] = jnp.sin(fr * (jnp.dot(h1.astype(BF16), w2_ref[...].astype(BF16),
                                       preferred_element_type=F32) + b2_ref[...]))


def _hy_filter_kernel(hid_ref, w3_ref, b3_ref, t_ref, delta_ref, o_ref):
    f = jnp.dot(hid_ref[...].astype(BF16), w3_ref[...].astype(BF16), preferred_element_type=F32) + b3_ref[...]
    o_ref[...] = f * jnp.exp(-t_ref[...] * delta_ref[...])


def _hy_filters(n, w1, b1, w2, b2, w3, b3, freq, hy_w):
    emb, hid = w1.shape
    bands = (emb - 1) // 2
    t = jnp.linspace(0.0, 1.0, n, dtype=F32)
    pos = jnp.arange(n, dtype=F32)
    bnd = jnp.linspace(1e-4, bands - 1, bands, dtype=F32)
    ang = (2.0 * math.pi / n) * pos[:, None] * bnd[None, :]
    z = jnp.concatenate([t[:, None], jnp.cos(ang), -jnp.sin(ang)], axis=-1)
    embp = -(-emb // SUBLANES) * SUBLANES
    z = jnp.pad(z, ((0, 0), (0, embp - emb)))
    w1p = jnp.pad(w1, ((0, embp - emb), (0, 0)))
    full = lambda a: pl.BlockSpec(a.shape, lambda: (0,) * a.ndim)
    args = (z, w1p, b1[None, :], w2, b2[None, :], freq[None, :])
    hid2 = pl.pallas_call(
        _hy_hidden_kernel,
        in_specs=[full(a) for a in args],
        out_specs=pl.BlockSpec((n, hid), lambda: (0, 0)),
        out_shape=jax.ShapeDtypeStruct((n, hid), F32),
        compiler_params=pltpu.CompilerParams(vmem_limit_bytes=VMEM_LIMIT_BYTES),
    )(*args)
    nout = w3.shape[1]
    deltas = jnp.abs(jnp.linspace(math.log(HY_DECAY_TARGET) / HY_SLOW_DECAY,
                                  math.log(HY_DECAY_TARGET) / HY_FAST_DECAY, hy_w, dtype=F32))
    delta_row = jnp.tile(deltas, nout // hy_w)[None, :]
    tc = min(nout, 1024)
    return pl.pallas_call(
        _hy_filter_kernel,
        grid=(nout // tc,),
        in_specs=[pl.BlockSpec((n, hid), lambda j: (0, 0)),
                  pl.BlockSpec((hid, tc), lambda j: (0, j)),
                  pl.BlockSpec((1, tc), lambda j: (0, j)),
                  pl.BlockSpec((n, 1), lambda j: (0, 0)),
                  pl.BlockSpec((1, tc), lambda j: (0, j))],
        out_specs=pl.BlockSpec((n, tc), lambda j: (0, j)),
        out_shape=jax.ShapeDtypeStruct((n, nout), F32),
        compiler_params=_cparams(("parallel",)),
    )(hid2, w3, b3[None, :], t[:, None], delta_row)


def _dft_tables(n):
    n2 = FFT_N2
    big = 2 * n
    n1 = big // n2
    k1 = n1 // 2
    two_pi = 2.0 * math.pi

    def cs(idx, period):
        a = (idx % period).astype(F32) * (two_pi / period)
        return jnp.cos(a), jnp.sin(a)

    f1 = jnp.arange(n1, dtype=jnp.int32)
    s1 = jnp.arange(k1, dtype=jnp.int32)
    s2 = jnp.arange(n2, dtype=jnp.int32)
    s_full = s1[None, None, :] * n2 + s2[:, None, None]
    c, s = cs(f1[None, :, None] * s_full, big)
    f1tw = jnp.concatenate([jnp.concatenate([c, s], axis=2), jnp.concatenate([-s, c], axis=2)], axis=1)
    ct, st = jnp.swapaxes(c, 1, 2), jnp.swapaxes(s, 1, 2)
    g1tw = jnp.concatenate([jnp.concatenate([ct, -st], axis=2), jnp.concatenate([st, ct], axis=2)], axis=1) / big
    c2, sn2 = cs(s2[:, None] * s2[None, :], n2)
    f2 = jnp.concatenate([jnp.concatenate([c2, sn2], axis=1), jnp.concatenate([-sn2, c2], axis=1)], axis=0)
    g2 = jnp.concatenate([jnp.concatenate([c2, -sn2], axis=1), jnp.concatenate([sn2, c2], axis=1)], axis=0)
    return f1tw.astype(BF16), f2.astype(BF16), g2.astype(BF16), g1tw.astype(BF16)


def _hy_conv_kernel(a_ref, m_ref, hf_ref, hb_ref, skip_ref, cwa_ref, cwm_ref,
                    f1_ref, f2_ref, g2_ref, g1_ref, o_ref,
                    h_scr, b_scr, zr_scr, zi_scr, *, conv_a, n1, k1):
    n2 = FFT_N2
    k1p = k1 + FFT_ROW_PAD
    pb = 2 * n2 + FFT_ROW_PAD
    pos = lax.broadcasted_iota(jnp.int32, (n2, 1), 0)

    def stage1(real_only):
        def body(s2, carry):
            base = pl.multiple_of(s2 * k1p, SUBLANES)
            if real_only:
                x = zr_scr[pl.ds(base, k1), :].astype(BF16)
                a = jnp.dot(f1_ref[s2, :, :k1], x, preferred_element_type=F32)
            else:
                x = jnp.concatenate([zr_scr[pl.ds(base, k1), :], zi_scr[pl.ds(base, k1), :]], axis=0)
                a = jnp.dot(f1_ref[s2], x.astype(BF16), preferred_element_type=F32)
            b_scr[pl.ds(s2, n1, stride=pb), :] = a[:n1]
            b_scr[pl.ds(n2 + s2, n1, stride=pb), :] = a[n1:]
            return carry
        lax.fori_loop(0, n2, body, 0, unroll=FFT_UNROLL)

    @pl.when(pl.program_id(1) == 0)
    def _():
        for which, ref in ((0, hf_ref), (1, hb_ref)):
            def load(s1, carry, ref=ref):
                r0 = pl.multiple_of(s1 * n2, n2)
                zr_scr[pl.ds(s1, n2, stride=k1p), :] = ref[pl.ds(r0, n2), :]
                return carry
            lax.fori_loop(0, k1, load, 0, unroll=FFT_UNROLL)
            if which == 1:
                zr_scr[0:1, :] = jnp.zeros((1, zr_scr.shape[1]), F32)
            stage1(True)

            def spec(fp, carry, which=which):
                lanes = b_scr.shape[1]
                bbs = [pl.multiple_of((2 * fp + t) * pb, SUBLANES) for t in range(2)]
                bpair = jnp.concatenate([b_scr[pl.ds(bb, 2 * n2), :] for bb in bbs], axis=1).astype(BF16)
                zpair = jnp.dot(f2_ref[...], bpair, preferred_element_type=F32)
                for t in range(2):
                    hb_ = pl.multiple_of((2 * fp + t) * 2 * n2, 2 * n2)
                    zz = zpair[:, t * lanes:(t + 1) * lanes]
                    if which == 0:
                        h_scr[pl.ds(hb_, 2 * n2), :] = zz
                    else:
                        h_scr[pl.ds(hb_, n2), :] = h_scr[pl.ds(hb_, n2), :] + zz[:n2]
                        h_scr[pl.ds(hb_ + n2, n2), :] = h_scr[pl.ds(hb_ + n2, n2), :] - zz[n2:]
                return carry
            lax.fori_loop(0, n1 // 2, spec, 0, unroll=FFT_UNROLL)

    def load_a(s1, carry):
        r0 = pl.multiple_of(s1 * n2, n2)
        for bi, scr in ((0, zr_scr), (1, zi_scr)):
            blk = a_ref[bi, pl.ds(r0, n2), :]
            if conv_a:
                blk = _row_conv(blk, cwa_ref[...], pos, n2)
            scr[pl.ds(s1, n2, stride=k1p), :] = blk
        return carry
    lax.fori_loop(0, k1, load_a, 0, unroll=FFT_UNROLL)

    stage1(False)

    cw = b_scr.shape[1]

    def mid(fp, carry):
        bbs = [pl.multiple_of((2 * fp + t) * pb, SUBLANES) for t in range(2)]
        hbs = [pl.multiple_of((2 * fp + t) * 2 * n2, 2 * n2) for t in range(2)]
        bpair = jnp.concatenate([b_scr[pl.ds(bb, 2 * n2), :] for bb in bbs], axis=1).astype(BF16)
        zz = jnp.dot(f2_ref[...], bpair, preferred_element_type=F32)
        zr, zi = zz[:n2], zz[n2:]
        hr = jnp.concatenate([h_scr[pl.ds(hb_, n2), :] for hb_ in hbs], axis=1)
        hi = jnp.concatenate([h_scr[pl.ds(hb_ + n2, n2), :] for hb_ in hbs], axis=1)
        yy = jnp.concatenate([zr * hr - zi * hi, zr * hi + zi * hr], axis=0).astype(BF16)
        ww = jnp.dot(g2_ref[...], yy, preferred_element_type=F32)
        for t in range(2):
            b_scr[pl.ds(bbs[t], 2 * n2), :] = ww[:, t * cw:(t + 1) * cw]
        return carry
    lax.fori_loop(0, n1 // 2, mid, 0, unroll=FFT_UNROLL)

    skip = skip_ref[...]

    def last(s2, carry):
        base = pl.multiple_of(s2 * k1p, SUBLANES)
        ww = jnp.concatenate([b_scr[pl.ds(s2, n1, stride=pb), :],
                              b_scr[pl.ds(n2 + s2, n1, stride=pb), :]], axis=0).astype(BF16)
        y = jnp.dot(g1_ref[s2], ww, preferred_element_type=F32)
        zr_scr[pl.ds(base, k1), :] = y[:k1] + zr_scr[pl.ds(base, k1), :] * skip
        zi_scr[pl.ds(base, k1), :] = y[k1:] + zi_scr[pl.ds(base, k1), :] * skip
        return carry
    lax.fori_loop(0, n2, last, 0, unroll=FFT_UNROLL)

    def fin(s1, carry):
        r0 = pl.multiple_of(s1 * n2, n2)
        for bi, scr in ((0, zr_scr), (1, zi_scr)):
            mm = _row_conv(m_ref[bi, pl.ds(r0, n2), :], cwm_ref[...], pos, n2)
            o_ref[bi, pl.ds(r0, n2), :] = mm * scr[pl.ds(s1, n2, stride=k1p), :]
        return carry
    lax.fori_loop(0, k1, fin, 0, unroll=FFT_UNROLL)


def _hy_conv(a, a_col0, m, m_col0, filt, f_col_fwd, f_col_bwd, skip, cwa, cwm, tables, *, conv_a, hy_w):
    b, n, _ = a.shape
    assert b % 2 == 0 and GRID_W == FFT_N2 and n % FFT_N2 == 0
    n2 = FFT_N2
    n1 = 2 * n // n2
    k1 = n1 // 2
    k1p = k1 + FFT_ROW_PAD
    pb = 2 * n2 + FFT_ROW_PAD
    cw = LANES
    f1tw, f2, g2, g1tw = tables
    kern = functools.partial(_hy_conv_kernel, conv_a=conv_a, n1=n1, k1=k1)
    once = pl.Buffered(1)
    return pl.pallas_call(
        kern,
        grid=(hy_w // cw, b // 2),
        in_specs=[pl.BlockSpec((2, n, cw), lambda c, p: (p, 0, a_col0 + c), pipeline_mode=once),
                  pl.BlockSpec((2, n, cw), lambda c, p: (p, 0, m_col0 + c), pipeline_mode=once),
                  pl.BlockSpec((n, cw), lambda c, p: (0, f_col_fwd + c), pipeline_mode=once),
                  pl.BlockSpec((n, cw), lambda c, p: (0, f_col_bwd + c), pipeline_mode=once),
                  pl.BlockSpec((1, cw), lambda c, p: (0, c)),
                  pl.BlockSpec((cwa.shape[0], cw), lambda c, p: (0, a_col0 + c if conv_a else 0)),
                  pl.BlockSpec((cwm.shape[0], cw), lambda c, p: (0, m_col0 + c)),
                  pl.BlockSpec(f1tw.shape, lambda c, p: (0, 0, 0), pipeline_mode=once),
                  pl.BlockSpec(f2.shape, lambda c, p: (0, 0)),
                  pl.BlockSpec(g2.shape, lambda c, p: (0, 0)),
                  pl.BlockSpec(g1tw.shape, lambda c, p: (0, 0, 0), pipeline_mode=once)],
        out_specs=pl.BlockSpec((2, n, cw), lambda c, p: (p, 0, c)),
        out_shape=jax.ShapeDtypeStruct((b, n, hy_w), F32),
        scratch_shapes=[pltpu.VMEM((n1 * 2 * n2, cw), F32),
                        pltpu.VMEM((n1 * pb, cw), F32),
                        pltpu.VMEM((n2 * k1p, cw), F32),
                        pltpu.VMEM((n2 * k1p, cw), F32)],
        compiler_params=_cparams(("parallel", "arbitrary")),
    )(a, m, filt, filt, skip, cwa, cwm, f1tw, f2, g2, g1tw)


def _merge_kernel(of_ref, ob_ref, z_ref, y_ref, ga_ref, gb_ref, nw_ref, wa_ref, wh_ref, o_ref, *, dv):
    nw = nw_ref[...]
    heads = []
    for hh in range(of_ref.shape[1] // dv):
        hs = slice(hh * dv, (hh + 1) * dv)
        o = of_ref[:, hs] + ob_ref[:, hs]
        o = o * lax.rsqrt(jnp.mean(o * o, axis=-1, keepdims=True) + RMS_EPS) * nw
        heads.append((o * _silu(z_ref[:, hs])).astype(BF16))
    pa = jnp.dot(jnp.concatenate(heads, axis=1), wa_ref[...], preferred_element_type=F32)
    ph = jnp.dot(y_ref[...].astype(BF16), wh_ref[...], preferred_element_type=F32)
    o_ref[...] = (jax.nn.sigmoid(ga_ref[...]) * pa + jax.nn.sigmoid(gb_ref[...]) * ph).astype(o_ref.dtype)


def _merge(o_f, o_b, z, z_blk, y_hy, gate, norm_w, w_pa, w_ph, *, dv, tm):
    m, dvw = o_f.shape
    d = w_pa.shape[1]
    tm = min(tm, m)
    kern = functools.partial(_merge_kernel, dv=dv)
    row = lambda w_, blk=0: pl.BlockSpec((tm, w_), lambda i: (i, blk))
    const = lambda a: pl.BlockSpec(a.shape, lambda i: (0, 0), pipeline_mode=pl.Buffered(1))
    return pl.pallas_call(
        kern,
        grid=(m // tm,),
        in_specs=[row(dvw), row(dvw), row(dvw, z_blk), row(y_hy.shape[1]), row(d, 0), row(d, 1),
                  pl.BlockSpec((1, dv), lambda i: (0, 0)), const(w_pa), const(w_ph)],
        out_specs=row(d),
        out_shape=jax.ShapeDtypeStruct((m, d), BF16),
        compiler_params=_cparams(("parallel",)),
    )(o_f, o_b, z, y_hy, gate, gate, norm_w, w_pa, w_ph)


def _layer_norm(x, g, b):
    mu = jnp.mean(x, axis=-1, keepdims=True)
    xc = x - mu
    var = jnp.mean(xc * xc, axis=-1, keepdims=True)
    return xc * lax.rsqrt(var + LN_EPS) * g + b


def _outproj_kernel(mg_ref, x_ref, wo_ref, g1_ref, lg_ref, lb_ref, sc_ref, sh_ref, wq_ref, sk_ref,
                    x1_ref, h2_ref, s_ref, *, half):
    mix = jnp.dot(mg_ref[...], wo_ref[...], preferred_element_type=F32)
    x1 = _layer_norm(DEEPNORM_ALPHA * x_ref[...] + g1_ref[0] * mix, lg_ref[...], lb_ref[...])
    x1_ref[...] = x1
    h2 = (x1 * (1.0 + sc_ref[0]) + sh_ref[0]).astype(BF16)
    h2_ref[...] = h2
    qp = jnp.dot(h2, wq_ref[...], preferred_element_type=F32).astype(BF16)
    for j in range(sk_ref.shape[0]):
        js = slice(j * half, (j + 1) * half)
        s_ref[j] = lax.dot_general(sk_ref[j], qp[:, js], (((1,), (1,)), ((), ())), preferred_element_type=F32)


def _outproj(merged, x2d, w_out, g1, ln_g, ln_b, sc2, sh2, wq, subkeys, *, tm):
    m, d = x2d.shape
    r = g1.shape[0]
    tm = min(tm, m // r)
    tpm = (m // r) // tm
    nsk, nkeys, half = subkeys.shape
    assert nkeys == LANES
    kern = functools.partial(_outproj_kernel, half=half)
    row = lambda w_: pl.BlockSpec((tm, w_), lambda i: (i, 0))
    mod = pl.BlockSpec((1, 1, d), lambda i: (i // tpm, 0, 0))
    vec = pl.BlockSpec((1, d), lambda i: (0, 0))
    return pl.pallas_call(
        kern,
        grid=(m // tm,),
        in_specs=[row(d), row(d), pl.BlockSpec(w_out.shape, lambda i: (0, 0), pipeline_mode=pl.Buffered(1)),
                  mod, vec, vec, mod, mod,
                  pl.BlockSpec(wq.shape, lambda i: (0, 0), pipeline_mode=pl.Buffered(1)),
                  pl.BlockSpec(subkeys.shape, lambda i: (0, 0, 0))],
        out_specs=[row(d), row(d), pl.BlockSpec((nsk, nkeys, tm), lambda i: (0, 0, i))],
        out_shape=[jax.ShapeDtypeStruct((m, d), F32),
                   jax.ShapeDtypeStruct((m, d), BF16),
                   jax.ShapeDtypeStruct((nsk, nkeys, m), F32)],
        compiler_params=_cparams(("parallel",)),
    )(merged, x2d, w_out, g1, ln_g, ln_b, sc2, sh2, wq, subkeys)


def _cand_tables():
    pairs = [(a, b) for a in range(PEER_TOPK) for b in range(PEER_TOPK) if (a + 1) * (b + 1) <= PEER_TOPK]
    ia = jnp.array([p[0] for p in pairs], dtype=jnp.int32)
    ib = jnp.array([p[1] for p in pairs], dtype=jnp.int32)
    ncand = len(pairs)
    rows = -(-ncand // (2 * SUBLANES)) * (2 * SUBLANES)
    ranks = jnp.arange(LANES, dtype=jnp.int32)
    p1 = (jnp.pad(ia, (0, rows - ncand), constant_values=-1)[:, None] == ranks[None, :]).astype(F32)
    p2 = (jnp.pad(ib, (0, rows - ncand), constant_values=-1)[:, None] == ranks[None, :]).astype(F32)
    return p1, p2, ncand


def _peer_topk_kernel(s_ref, p1_ref, p2_ref, p1t_ref, c1_ref, r2_ref, e1_ref, e2_ref, *, n_heads, ncand):
    tm = s_ref.shape[2]
    neg = -jnp.inf
    hi = lax.Precision.HIGHEST
    no_rank = float(LANES)

    def pop_max(x, row_id):
        mx = jnp.max(x, axis=0, keepdims=True)
        first = jnp.min(jnp.where(x == mx, row_id, no_rank), axis=0, keepdims=True)
        hit = row_id == first
        return mx, hit, jnp.where(hit, neg, x)

    key_id = lax.broadcasted_iota(jnp.int32, (LANES, tm), 0).astype(F32)
    cand_id = lax.broadcasted_iota(jnp.int32, (p1_ref.shape[0], tm), 0).astype(F32)
    pad_rows = jnp.zeros((LANES - PEER_TOPK, tm), F32)

    def sorted_top(s):
        tops = []
        rank = jnp.full(s.shape, no_rank, F32)
        x = s
        for a in range(PEER_TOPK):
            mx, hit, x = pop_max(x, key_id)
            tops.append(mx)
            rank = jnp.where(hit, float(a), rank)
        return jnp.concatenate(tops + [pad_rows], axis=0), rank

    for h in range(n_heads):
        s1 = s_ref[2 * h]
        s2 = s_ref[2 * h + 1]
        t1, rank1 = sorted_top(s1)
        t2, rank2 = sorted_top(s2)
        cand = (jnp.dot(p1_ref[...], t1, precision=hi, preferred_element_type=F32)
                + jnp.dot(p2_ref[...], t2, precision=hi, preferred_element_type=F32))
        x = jnp.where(cand_id < ncand, cand, neg)
        sel = jnp.zeros(x.shape, F32)
        zsum = None
        cmax = None
        for r in range(PEER_TOPK):
            mx, hit, x = pop_max(x, cand_id)
            sel = jnp.where(hit, 1.0, sel)
            if r == 0:
                cmax = mx
                zsum = jnp.ones_like(mx)
            else:
                zsum = zsum + jnp.exp(mx - cmax)
        cnt = jnp.dot(p1t_ref[...].astype(BF16), sel.astype(BF16), preferred_element_type=F32)
        count1 = jnp.zeros(s1.shape, F32)
        for a in range(PEER_TOPK):
            count1 = jnp.where(rank1 == a, cnt[a:a + 1], count1)
        c1_ref[h] = count1
        r2_ref[h] = rank2.astype(r2_ref.dtype)
        e1_ref[h] = jnp.exp(s1 - t1[0:1]) / zsum
        e2_ref[h] = jnp.exp(s2 - t2[0:1]).astype(e2_ref.dtype)


def _peer_topk(scores, n_heads, *, tm):
    nsk, nkeys, m = scores.shape
    tm = min(tm, m)
    p1, p2, ncand = _cand_tables()
    kern = functools.partial(_peer_topk_kernel, n_heads=n_heads, ncand=ncand)
    full = lambda a: pl.BlockSpec(a.shape, lambda i: (0, 0))
    out = pl.BlockSpec((n_heads, nkeys, tm), lambda i: (0, 0, i))
    shp = lambda dt: jax.ShapeDtypeStruct((n_heads, nkeys, m), dt)
    p1t = p1.T
    return pl.pallas_call(
        kern,
        grid=(m // tm,),
        in_specs=[pl.BlockSpec((nsk, nkeys, tm), lambda i: (0, 0, i)), full(p1), full(p2), full(p1t)],
        out_specs=[out, out, out, out],
        out_shape=[shp(F32), shp(BF16), shp(F32), shp(BF16)],
        compiler_params=_cparams(("parallel",)),
    )(scores, p1, p2, p1t)


def _peer_dense_kernel(h2_ref, u_ref, v_ref, r2_ref, e2_ref, c1_ref, e1_ref, x1_ref, g2_ref, lg_ref, lb_ref,
                       o_ref, acc_scr, a_scr, p_scr, *, n_heads, ti):
    e = pl.program_id(1)

    @pl.when(e == 0)
    def _():
        acc_scr[...] = jnp.zeros(acc_scr.shape, F32)

    a_scr[...] = lax.dot_general(u_ref[...], h2_ref[...], (((1,), (1,)), ((), ())), preferred_element_type=F32)
    tm = h2_ref.shape[0]
    nkeys = r2_ref.shape[1]
    for ii in range(ti):
        rs = slice(ii * nkeys, (ii + 1) * nkeys)
        for cb in range(tm // LANES):
            cs = slice(cb * LANES, (cb + 1) * LANES)
            gate = None
            for h in range(n_heads):
                keep = r2_ref[h, :, cs] < c1_ref[h, ii:ii + 1, cs].astype(BF16)
                term = jnp.where(keep, e2_ref[h, :, cs], jnp.zeros((), BF16)) * e1_ref[h, ii:ii + 1, cs].astype(BF16)
                gate = term if gate is None else gate + term
            a = a_scr[rs, cs]
            act = 0.5 * a * (1.0 + lax.erf(a * (2.0 ** -0.5)))
            p_scr[rs, cs] = gate * act.astype(BF16)
    acc_scr[...] += lax.dot_general(p_scr[...], v_ref[...], (((0,), (0,)), ((), ())), preferred_element_type=F32)

    @pl.when(e == pl.num_programs(1) - 1)
    def _():
        o_ref[...] = _layer_norm(DEEPNORM_ALPHA * x1_ref[...] + g2_ref[0] * acc_scr[...],
                                 lg_ref[...], lb_ref[...])


def _peer_dense(h2, u_tab, v_tab, rank2, e2, count1, e1, x1, g2, ln_g, ln_b, *, n_heads, ti, tm):
    m, d = h2.shape
    r = g2.shape[0]
    tm = min(tm, m // r)
    tpm = (m // r) // tm
    ne = u_tab.shape[0]
    nkeys = rank2.shape[1]
    te = ti * nkeys
    kern = functools.partial(_peer_dense_kernel, n_heads=n_heads, ti=ti)
    row = lambda w_: pl.BlockSpec((tm, w_), lambda i, e: (i, 0), pipeline_mode=pl.Buffered(1))
    vec = pl.BlockSpec((1, d), lambda i, e: (0, 0))
    tbl = pl.BlockSpec((te, d), lambda i, e: (e, 0))
    second = pl.BlockSpec((n_heads, nkeys, tm), lambda i, e: (0, 0, i), pipeline_mode=pl.Buffered(1))
    first = pl.BlockSpec((n_heads, ti, tm), lambda i, e: (0, e, i))
    return pl.pallas_call(
        kern,
        grid=(m // tm, ne // te),
        in_specs=[row(d), tbl, tbl, second, second, first, first, row(d),
                  pl.BlockSpec((1, 1, d), lambda i, e: (i // tpm, 0, 0)), vec, vec],
        out_specs=pl.BlockSpec((tm, d), lambda i, e: (i, 0)),
        out_shape=jax.ShapeDtypeStruct((m, d), F32),
        scratch_shapes=[pltpu.VMEM((tm, d), F32), pltpu.VMEM((te, tm), F32), pltpu.VMEM((te, tm), BF16)],
        compiler_params=_cparams(("parallel", "arbitrary")),
    )(h2, u_tab, v_tab, rank2, e2, count1, e1, x1, g2, ln_g, ln_b)


def _block(x, c, ctx, c_ctx, w_ada, b_ada, w_in, dn_conv_w, dn_a_log, dn_dt_bias, dn_norm_w, hy_conv_w, hy_w1, hy_b1, hy_w2, hy_b2, hy_w3, hy_b3, hy_freq, hy_skip, w_branch_dn, w_branch_hy, w_out, ln1_g, ln1_b, peer_wq, peer_subkeys, peer_u, peer_v, ln2_g, ln2_b):
    b, n, d = x.shape
    n_ctx = ctx.shape[1]
    assert w_ada.shape[0] == DEPTH == 1, "single-layer block: the context stream update is never read"
    hh = dn_a_log.shape[2]
    dk = dn_conv_w.shape[2] // (3 * hh)
    qk_w = hh * dk
    hy_w = hy_skip.shape[2]
    p_heads, _, nkeys, half = peer_subkeys.shape[1:]
    assert dk == LANES and nkeys == LANES and 4 * hh <= LANES and n % GRID_W == 0
    off_z = 3 * qk_w
    off_ab = 4 * qk_w
    off_hy = off_ab + 4 * hh
    off_gate = off_hy + 3 * hy_w
    assert w_in.shape[2] == off_gate + 2 * d

    cc = jnp.zeros((SUBLANES, d), F32).at[:b].set(c).at[b].set(c_ctx)
    mods = _ada(cc, w_ada[0], b_ada[0][None, :])
    sh1, sc1, g1, sh2, sc2, g2 = [mods[:b, i * d:(i + 1) * d][:, None, :] for i in range(6)]
    sh1c, sc1c = [mods[b:b + 1, i * d:(i + 1) * d][:, None, :] for i in range(2)]

    w = w_in[0]
    w_qkvz = w[:, :off_ab].astype(BF16)
    w_ab = jnp.pad(w[:, off_ab:off_hy], ((0, 0), (0, LANES - 4 * hh))).astype(BF16)
    w_hy = w[:, off_hy:off_gate].astype(BF16)
    w_gate = w[:, off_gate:].astype(BF16)

    x2d = x.reshape(b * n, d)
    c2d = ctx.reshape(b * n_ctx, d)
    proj_qkvz = _modmm(x2d, sc1, sh1, w_qkvz, tm=512, tn=2048)
    proj_ab = _modmm(x2d, sc1, sh1, w_ab, tm=512, tn=LANES)
    proj_hy = _modmm(x2d, sc1, sh1, w_hy, tm=512, tn=2048)
    proj_gate = _modmm(x2d, sc1, sh1, w_gate, tm=512, tn=2048)
    projc_qkvz = _modmm(c2d, sc1c, sh1c, w_qkvz, tm=256, tn=2048)
    projc_ab = _modmm(c2d, sc1c, sh1c, w_ab, tm=256, tn=LANES)

    def decay_row(a):
        row = jnp.concatenate([a, jnp.zeros_like(a)], axis=1).reshape(-1)
        return jnp.pad(row, (0, LANES - 4 * hh))[None, :]

    alog_row = decay_row(dn_a_log[0])
    dt_row = decay_row(dn_dt_bias[0])

    conv_w = dn_conv_w[0]
    prep_c = _dn_prep(projc_qkvz.reshape(b, n_ctx, -1), conv_w, projc_ab.reshape(b, n_ctx, LANES),
                      alog_row, dt_row, n_heads=hh, dk=dk, row_w=n_ctx, tt=n_ctx)
    prep_l = _dn_prep(proj_qkvz.reshape(b, n, -1), conv_w, proj_ab.reshape(b, n, LANES),
                      alog_row, dt_row, n_heads=hh, dk=dk, row_w=GRID_W, tt=4 * GRID_W)
    hp = 4 if hh % 4 == 0 else (2 if hh % 2 == 0 else 1)
    s_zero = jnp.zeros((2, b, hh, dk, dk), F32)
    _, _, s_ctx = _dn_scan(*prep_c, s_zero, dk=dk, tt=n_ctx, hp=hp)
    o_f, o_b, _ = _dn_scan(*prep_l, s_ctx, dk=dk, tt=4 * GRID_W, hp=hp)

    filt = _hy_filters(n, hy_w1[0], hy_b1[0], hy_w2[0], hy_b2[0], hy_w3[0], hy_b3[0], hy_freq[0], hy_w)
    tables = _dft_tables(n)
    hy3 = proj_hy.reshape(b, n, 3 * hy_w)
    cwh = hy_conv_w[0]
    cb = hy_w // LANES
    y1 = _hy_conv(hy3, 0, hy3, cb, filt, 0, 2 * cb, hy_skip[0, 0:1], cwh, cwh, tables, conv_a=True, hy_w=hy_w)
    y_hy = _hy_conv(y1, 0, hy3, 2 * cb, filt, cb, 3 * cb, hy_skip[0, 1:2], cwh, cwh, tables, conv_a=False, hy_w=hy_w)

    merged = _merge(o_f.reshape(b * n, qk_w), o_b.reshape(b * n, qk_w), proj_qkvz, off_z // qk_w,
                    y_hy.reshape(b * n, hy_w), proj_gate, dn_norm_w[0][None, :],
                    w_branch_dn[0].astype(BF16), w_branch_hy[0].astype(BF16), dv=dk, tm=256)
    sk = peer_subkeys[0].reshape(p_heads * 2, nkeys, half).astype(BF16)
    x1, h2, scores = _outproj(merged, x2d, w_out[0].astype(BF16), g1, ln1_g[0][None, :], ln1_b[0][None, :],
                              sc2, sh2, peer_wq[0].astype(BF16), sk, tm=256)

    count1, rank2, e1, e2 = _peer_topk(scores, p_heads, tm=256)
    out = _peer_dense(h2, peer_u[0].astype(BF16), peer_v[0].astype(BF16), rank2, e2, count1, e1,
                      x1, g2, ln2_g[0][None, :], ln2_b[0][None, :], n_heads=p_heads, ti=SUBLANES, tm=512)
    stages = dict(o_f=o_f, o_b=o_b, s_ctx=s_ctx, filt=filt, y1=y1, y_hy=y_hy, merged=merged, x1=x1, scores=scores,
                  count1=count1, rank2=rank2, e1=e1, e2=e2)
    return out.reshape(b, n, d), stages


def kernel(x, c, ctx, c_ctx, w_ada, b_ada, w_in, dn_conv_w, dn_a_log, dn_dt_bias, dn_norm_w, hy_conv_w, hy_w1, hy_b1, hy_w2, hy_b2, hy_w3, hy_b3, hy_freq, hy_skip, w_branch_dn, w_branch_hy, w_out, ln1_g, ln1_b, peer_wq, peer_subkeys, peer_u, peer_v, ln2_g, ln2_b):
    out, _ = _block(x, c, ctx, c_ctx, w_ada, b_ada, w_in, dn_conv_w, dn_a_log, dn_dt_bias, dn_norm_w, hy_conv_w, hy_w1, hy_b1, hy_w2, hy_b2, hy_w3, hy_b3, hy_freq, hy_skip, w_branch_dn, w_branch_hy, w_out, ln1_g, ln1_b, peer_wq, peer_subkeys, peer_u, peer_v, ln2_g, ln2_b)
    return out
```

```python
import functools
import math

import jax
import jax.numpy as jnp
from jax import lax
from jax.experimental import pallas as pl
from jax.experimental.pallas import tpu as pltpu

F32 = jnp.float32
BF16 = jnp.bfloat16

GRID_W = 64
DN_CHUNK = 64
PEER_TOPK = 16
DEPTH = 1
DEEPNORM_ALPHA = (2 * DEPTH) ** 0.25
LN_EPS = 1e-5
RMS_EPS = 1e-6
L2_EPS = 1e-6
HY_DECAY_TARGET = 1e-2
HY_FAST_DECAY = 0.3
HY_SLOW_DECAY = 1.5

LANES = 128
SUBLANES = 8
VMEM_LIMIT_BYTES = 56 * 1024 * 1024

FFT_N2 = 64
FFT_ROW_PAD = 8
FFT_UNROLL = 4


def _cparams(sem):
    return pltpu.CompilerParams(dimension_semantics=sem, vmem_limit_bytes=VMEM_LIMIT_BYTES)


def _silu(x):
    return x * jax.nn.sigmoid(x)


def _ada_kernel(c_ref, w_ref, b_ref, o_ref):
    s = _silu(c_ref[...])
    o_ref[...] = jnp.dot(s.astype(BF16), w_ref[...].astype(BF16), preferred_element_type=F32) + b_ref[...]


def _ada(cc, w, b):
    d, n6 = w.shape
    tn = min(n6, 1536)
    return pl.pallas_call(
        _ada_kernel,
        grid=(n6 // tn,),
        in_specs=[pl.BlockSpec((SUBLANES, d), lambda j: (0, 0)),
                  pl.BlockSpec((d, tn), lambda j: (0, j)),
                  pl.BlockSpec((1, tn), lambda j: (0, j))],
        out_specs=pl.BlockSpec((SUBLANES, tn), lambda j: (0, j)),
        out_shape=jax.ShapeDtypeStruct((SUBLANES, n6), F32),
        compiler_params=_cparams(("arbitrary",)),
    )(cc, w, b)


def _modmm_kernel(x_ref, sc_ref, sh_ref, w_ref, o_ref, h_scr):
    @pl.when(pl.program_id(1) == 0)
    def _():
        h_scr[...] = (x_ref[...] * (1.0 + sc_ref[0]) + sh_ref[0]).astype(BF16)

    o_ref[...] = jnp.dot(h_scr[...], w_ref[...], preferred_element_type=F32).astype(o_ref.dtype)


def _modmm(x2d, sc, sh, w, *, tm, tn, out_dtype=F32):
    m, d = x2d.shape
    ng = w.shape[1]
    r = sc.shape[0]
    tm = min(tm, m // r)
    tn = min(tn, ng)
    tiles_per_mod = (m // r) // tm
    return pl.pallas_call(
        _modmm_kernel,
        grid=(m // tm, ng // tn),
        in_specs=[pl.BlockSpec((tm, d), lambda i, j: (i, 0)),
                  pl.BlockSpec((1, 1, d), lambda i, j: (i // tiles_per_mod, 0, 0)),
                  pl.BlockSpec((1, 1, d), lambda i, j: (i // tiles_per_mod, 0, 0)),
                  pl.BlockSpec((d, tn), lambda i, j: (0, j))],
        out_specs=pl.BlockSpec((tm, tn), lambda i, j: (i, j)),
        out_shape=jax.ShapeDtypeStruct((m, ng), out_dtype),
        scratch_shapes=[pltpu.VMEM((tm, d), BF16)],
        compiler_params=_cparams(("parallel", "arbitrary")),
    )(x2d, sc, sh, w)


def _row_conv(x, cw, pos, row_w):
    rows = x.shape[0]
    k = cw.shape[0]
    pad = k // 2
    acc = None
    for j in range(k):
        off = j - pad
        if off == 0:
            term = x * cw[j:j + 1]
        else:
            shifted = pltpu.roll(x, (-off) % rows, axis=0)
            valid = (pos >= -off) if off < 0 else (pos < row_w - off)
            term = jnp.where(valid, shifted, 0.0) * cw[j:j + 1]
        acc = term if acc is None else acc + term
    return acc


SOLVE_BLOCK = 16


def _tri_inverse_pairs(pairs):
    c = pairs[0][0].shape[0]
    sb = SOLVE_BLOCK
    nb = c // sb
    ri = lax.broadcasted_iota(jnp.int32, (c, 2 * c), 0)
    ci = lax.broadcasted_iota(jnp.int32, (c, 2 * c), 1)
    ci = jnp.where(ci >= c, ci - c, ci)
    lane_lo = lax.broadcasted_iota(jnp.int32, (1, 2 * c), 1) < c
    off_diag = ri // sb != ci // sb
    eye2 = (ri == ci).astype(F32)
    mcats = [jnp.concatenate([m_a, m_b], axis=1) for m_a, m_b in pairs]
    m_offs = [jnp.where(off_diag, m, 0.0).astype(BF16) for m in mcats]
    xs = [[eye2[i * sb:(i + 1) * sb] for i in range(nb)] for _ in pairs]
    for bi in range(nb):
        rs = slice(bi * sb, (bi + 1) * sb)
        blks = [x[bi] for x in xs]
        if bi > 0:
            for s, x in enumerate(xs):
                xcur = jnp.concatenate(x, axis=0)
                xbd = jnp.concatenate([jnp.where(lane_lo, xcur, 0.0), jnp.where(lane_lo, 0.0, xcur)], axis=0)
                blks[s] = blks[s] - jnp.dot(m_offs[s][rs], xbd.astype(BF16), preferred_element_type=F32)
        mrs = [m[rs] for m in mcats]
        for jj in range(sb - 1):
            col = bi * sb + jj
            src = jnp.broadcast_to(jnp.where(lane_lo, col, c + col), (sb, 2 * c))
            for s in range(len(pairs)):
                mult = jnp.take_along_axis(mrs[s], src, axis=1)
                blks[s] = blks[s] - mult * blks[s][jj:jj + 1, :]
        for s, x in enumerate(xs):
            x[bi] = blks[s]
    return [jnp.concatenate(x, axis=0) for x in xs]


def _dn_gates_kernel(ab_ref, alog_ref, dt_ref, gc_ref, beta_ref, *, n_heads):
    tt = ab_ref.shape[1]
    c = DN_CHUNK
    cpos = lax.broadcasted_iota(jnp.int32, (tt, 1), 0) % c
    ab = ab_ref[0]
    z = ab + dt_ref[...]
    softplus = jnp.maximum(z, 0.0) + jnp.log1p(jnp.exp(-jnp.abs(z)))
    g_all = -jnp.exp(alog_ref[...]) * softplus
    gf = g_all
    gb = g_all
    d = 1
    while d < c:
        gf = gf + jnp.where(cpos >= d, pltpu.roll(gf, d, axis=0), 0.0)
        gb = gb + jnp.where(cpos < c - d, pltpu.roll(gb, (tt - d) % tt, axis=0), 0.0)
        d *= 2
    lane = lax.broadcasted_iota(jnp.int32, (1, LANES), 1)
    gc_ref[0] = jnp.where(lane < 2 * n_heads, gf, gb)
    beta_ref[0] = jax.nn.sigmoid(ab)


def _dn_gates(ab, alog_row, dt_row, *, n_heads, tt):
    b, n, _ = ab.shape
    blk = pl.BlockSpec((1, tt, LANES), lambda bi, g: (bi, g, 0))
    row = pl.BlockSpec((1, LANES), lambda bi, g: (0, 0))
    shp = jax.ShapeDtypeStruct((b, n, LANES), F32)
    return pl.pallas_call(
        functools.partial(_dn_gates_kernel, n_heads=n_heads),
        grid=(b, n // tt),
        in_specs=[blk, row, row],
        out_specs=[blk, blk],
        out_shape=[shp, shp],
        compiler_params=_cparams(("parallel", "parallel")),
    )(ab, alog_row, dt_row)


def _dn_prep_kernel(xq_ref, xk_ref, xv_ref, cq_ref, ck_ref, cv_ref, gc_ref, beta_ref,
                    u_ref, w_ref, qg_ref, kd_ref, attn_ref, eg_ref, *, row_w, n_heads, dk, hp):
    tt = xq_ref.shape[1]
    c = DN_CHUNK
    rows = lax.broadcasted_iota(jnp.int32, (tt, 1), 0)
    pos = rows % row_w
    lane = lax.broadcasted_iota(jnp.int32, (1, LANES), 1)
    gc = gc_ref[0]
    beta_all = beta_ref[0]
    eg_ref[...] = jnp.zeros(eg_ref.shape, F32)
    systems = []
    for hh in range(hp):
        systems += _dn_prep_head(pl.program_id(1) * hp + hh, hh, slice(hh * dk, (hh + 1) * dk), pos, lane, gc,
                                 beta_all, xq_ref, xk_ref, xv_ref, cq_ref, ck_ref, cv_ref, qg_ref, kd_ref, attn_ref,
                                 eg_ref, row_w=row_w, n_heads=n_heads, dk=dk, tt=tt)
    tinvs = _tri_inverse_pairs([(m_f, m_bt) for m_f, m_bt, _, _, _, _ in systems])
    zero_rows = jnp.zeros((c, 2 * dk), BF16)
    for tinv, (_, _, rhs_f, rhs_b, sl, hs) in zip(tinvs, systems):
        tinv = tinv.astype(BF16)
        sol_f = jnp.dot(tinv, jnp.concatenate([rhs_f, zero_rows], axis=0), preferred_element_type=F32)
        sol_b = lax.dot_general(tinv, rhs_b, (((0,), (0,)), ((), ())), preferred_element_type=F32)[c:]
        for dr, sol in enumerate((sol_f, sol_b)):
            u_ref[dr, 0, sl, hs] = sol[:, :dk]
            w_ref[dr, 0, sl, hs] = sol[:, dk:].astype(BF16)


def _dn_prep_head(h, hh, hs, pos, lane, gc, beta_all, xq_ref, xk_ref, xv_ref, cq_ref, ck_ref, cv_ref,
                  qg_ref, kd_ref, attn_ref, eg_ref, *, row_w, n_heads, dk, tt):
    c = DN_CHUNK
    systems = []

    def conv_silu(x_ref, cw_ref):
        return _silu(_row_conv(x_ref[0, :, hs], cw_ref[:, hs], pos, row_w))

    def l2n(y):
        return y * lax.rsqrt(jnp.sum(y * y, axis=-1, keepdims=True) + L2_EPS)

    q = l2n(conv_silu(xq_ref, cq_ref)) * (dk ** -0.5)
    k = l2n(conv_silu(xk_ref, ck_ref))
    v = conv_silu(xv_ref, cv_ref)

    def col(a, l):
        return jnp.sum(jnp.where(lane == l, a, 0.0), axis=1, keepdims=True)

    g_cols = (col(gc, h), col(gc, 2 * n_heads + h))
    b_cols = (col(beta_all, n_heads + h), col(beta_all, 3 * n_heads + h))

    ri = lax.broadcasted_iota(jnp.int32, (c, c), 0)
    ci = lax.broadcasted_iota(jnp.int32, (c, c), 1)
    eye = ri == ci
    lower = ri >= ci
    upper = ri <= ci
    strict_lower = ri > ci
    nt = (((1,), (1,)), ((), ()))
    for cc in range(tt // c):
        sl = slice(cc * c, (cc + 1) * c)
        qc, kc, vc = q[sl], k[sl], v[sl]
        kcb = kc.astype(BF16)
        gcols = [g_cols[dr][sl] for dr in range(2)]
        grows = [jnp.sum(jnp.where(eye, g, 0.0), axis=0, keepdims=True) for g in gcols]
        kbs = [kc * b_cols[dr][sl] for dr in range(2)]
        vbs = [vc * b_cols[dr][sl] for dr in range(2)]
        egs = [jnp.exp(g) for g in gcols]
        dif_f = gcols[0] - grows[0]
        dif_b = gcols[1] - grows[1]
        decay_f = jnp.where(lower, jnp.exp(jnp.where(lower, dif_f, 0.0)), 0.0)
        decay_b = jnp.where(upper, jnp.exp(jnp.where(upper, dif_b, 0.0)), 0.0)
        decay_bt = jnp.where(lower, jnp.exp(jnp.where(lower, -dif_b, 0.0)), 0.0)
        kq = lax.dot_general(jnp.concatenate([kbs[0], qc], axis=0).astype(BF16), kcb, nt,
                             preferred_element_type=F32)
        kkt_b = lax.dot_general(kcb, kbs[1].astype(BF16), nt, preferred_element_type=F32)
        m_f = jnp.where(strict_lower, kq[:c] * decay_f, 0.0)
        m_bt = jnp.where(strict_lower, kkt_b * decay_bt, 0.0)
        rhs_f = jnp.concatenate([vbs[0], kbs[0] * egs[0]], axis=1).astype(BF16)
        rhs_b = jnp.concatenate([vbs[1], kbs[1] * egs[1]], axis=1).astype(BF16)
        systems.append((m_f, m_bt, rhs_f, rhs_b, sl, hs))
        attns = (kq[c:] * decay_f, kq[c:] * decay_b)
        for dr in range(2):
            gcol = gcols[dr]
            glast = gcol[c - 1:c] if dr == 0 else gcol[0:1]
            qg_ref[dr, 0, sl, hs] = (qc * egs[dr]).astype(BF16)
            kd_ref[dr, 0, sl, hs] = (kc * jnp.exp(glast - gcol)).astype(BF16)
            attn_ref[dr, 0, hh, sl, :] = attns[dr].astype(BF16)
            eg_ref[dr, 0, hh, 0, cc:cc + 1, :] = jnp.broadcast_to(jnp.exp(glast), (1, LANES))
    return systems


def _dn_prep(qkv, conv_w, ab, alog_row, dt_row, *, n_heads, dk, row_w, tt):
    gc, beta = _dn_gates(ab, alog_row, dt_row, n_heads=n_heads, tt=tt)
    b, n, _ = qkv.shape
    hh = n_heads
    ng = n // tt
    c = DN_CHUNK
    assert tt % row_w == 0 and tt % c == 0 and tt // c <= SUBLANES
    hp = 2 if hh % 2 == 0 else 1
    nhb = hh // hp
    kern = functools.partial(_dn_prep_kernel, row_w=row_w, n_heads=hh, dk=dk, hp=hp)
    xspec = lambda off: pl.BlockSpec((1, tt, hp * dk), lambda bi, h, g: (bi, g, off + h))
    cspec = lambda off: pl.BlockSpec((conv_w.shape[0], hp * dk), lambda bi, h, g: (0, off + h))
    ospec = pl.BlockSpec((2, 1, tt, hp * dk), lambda bi, h, g: (0, bi, g, h))
    return pl.pallas_call(
        kern,
        grid=(b, nhb, ng),
        in_specs=[xspec(0), xspec(nhb), xspec(2 * nhb), cspec(0), cspec(nhb), cspec(2 * nhb),
                  pl.BlockSpec((1, tt, LANES), lambda bi, h, g: (bi, g, 0)),
                  pl.BlockSpec((1, tt, LANES), lambda bi, h, g: (bi, g, 0))],
        out_specs=[ospec, ospec, ospec, ospec,
                   pl.BlockSpec((2, 1, hp, tt, c), lambda bi, h, g: (0, bi, h, g, 0)),
                   pl.BlockSpec((2, 1, hp, 1, SUBLANES, LANES), lambda bi, h, g: (0, bi, h, g, 0, 0))],
        out_shape=[jax.ShapeDtypeStruct((2, b, n, hh * dk), F32),
                   jax.ShapeDtypeStruct((2, b, n, hh * dk), BF16),
                   jax.ShapeDtypeStruct((2, b, n, hh * dk), BF16),
                   jax.ShapeDtypeStruct((2, b, n, hh * dk), BF16),
                   jax.ShapeDtypeStruct((2, b, hh, n, c), BF16),
                   jax.ShapeDtypeStruct((2, b, hh, ng, SUBLANES, LANES), F32)],
        compiler_params=_cparams(("parallel", "parallel", "arbitrary")),
    )(qkv, qkv, qkv, conv_w, conv_w, conv_w, gc, beta)


def _dn_scan_kernel(uf_ref, wf_ref, qf_ref, kf_ref, af_ref, ef_ref,
                    ub_ref, wb_ref, qb_ref, kb_ref, ab_ref, eb_ref, s0_ref,
                    of_ref, ob_ref, sout_ref, s_scr, *, hp, dk):
    g = pl.program_id(2)
    ng = pl.num_programs(2)
    c = DN_CHUNK
    tt = uf_ref.shape[2]
    ncs = tt // c

    @pl.when(g == 0)
    def _():
        s_scr[...] = s0_ref[:, 0]

    refs = ((uf_ref, wf_ref, qf_ref, kf_ref, af_ref, ef_ref, of_ref),
            (ub_ref, wb_ref, qb_ref, kb_ref, ab_ref, eb_ref, ob_ref))
    chains = [(dr, hh) for dr in range(2) for hh in range(hp)]
    state = {ch: s_scr[ch[0], ch[1]] for ch in chains}
    for step in range(ncs):
        rows = {}
        for dr, hh in chains:
            cc = step if dr == 0 else ncs - 1 - step
            rows[dr, hh] = (slice(cc * c, (cc + 1) * c), slice(hh * dk, (hh + 1) * dk), cc)
        r = {}
        for ch in chains:
            w_ref, q_ref = refs[ch[0]][1], refs[ch[0]][2]
            sl, hs, _ = rows[ch]
            wq = jnp.concatenate([w_ref[0, 0, sl, hs], q_ref[0, 0, sl, hs]], axis=0)
            r[ch] = jnp.dot(wq, state[ch].astype(BF16), preferred_element_type=F32)
        vnb = {}
        for ch in chains:
            sl, hs, _ = rows[ch]
            vnb[ch] = (refs[ch[0]][0][0, 0, sl, hs] - r[ch][:c]).astype(BF16)
        for ch in chains:
            _, _, _, k_ref, a_ref, e_ref, o_ref = refs[ch[0]]
            sl, hs, cc = rows[ch]
            o_ref[0, sl, hs] = r[ch][c:] + jnp.dot(a_ref[0, 0, ch[1], sl, :], vnb[ch], preferred_element_type=F32)
            upd = lax.dot_general(k_ref[0, 0, sl, hs], vnb[ch], (((0,), (0,)), ((), ())),
                                  preferred_element_type=F32)
            state[ch] = state[ch] * e_ref[0, 0, ch[1], 0, cc:cc + 1, :] + upd
    for ch in chains:
        s_scr[ch[0], ch[1]] = state[ch]

    @pl.when(g == ng - 1)
    def _():
        sout_ref[:, 0] = s_scr[...]


def _dn_scan(u, w, qg, kd, attn, eg, s0, *, dk, tt, hp):
    _, b, n, hd = u.shape
    hh = hd // dk
    ng = n // tt
    c = DN_CHUNK
    kern = functools.partial(_dn_scan_kernel, hp=hp, dk=dk)

    def specs(dr):
        gi = (lambda g: g) if dr == 0 else (lambda g: ng - 1 - g)
        tok = pl.BlockSpec((1, 1, tt, hp * dk), lambda bi, h, g: (dr, bi, gi(g), h))
        return [tok, tok, tok, tok,
                pl.BlockSpec((1, 1, hp, tt, c), lambda bi, h, g: (dr, bi, h, gi(g), 0)),
                pl.BlockSpec((1, 1, hp, 1, SUBLANES, LANES), lambda bi, h, g: (dr, bi, h, gi(g), 0, 0))]

    sspec = pl.BlockSpec((2, 1, hp, dk, dk), lambda bi, h, g: (0, bi, h, 0, 0))
    return pl.pallas_call(
        kern,
        grid=(b, hh // hp, ng),
        in_specs=specs(0) + specs(1) + [sspec],
        out_specs=[pl.BlockSpec((1, tt, hp * dk), lambda bi, h, g: (bi, g, h)),
                   pl.BlockSpec((1, tt, hp * dk), lambda bi, h, g: (bi, ng - 1 - g, h)),
                   sspec],
        out_shape=[jax.ShapeDtypeStruct((b, n, hd), F32),
                   jax.ShapeDtypeStruct((b, n, hd), F32),
                   jax.ShapeDtypeStruct((2, b, hh, dk, dk), F32)],
        scratch_shapes=[pltpu.VMEM((2, hp, dk, dk), F32)],
        compiler_params=_cparams(("parallel", "parallel", "arbitrary")),
    )(u, w, qg, kd, attn, eg, u, w, qg, kd, attn, eg, s0)


def _hy_hidden_kernel(z_ref, w1_ref, b1_ref, w2_ref, b2_ref, fr_ref, o_ref):
    fr = fr_ref[...]
    h1 = jnp.sin(fr * (jnp.dot(z_ref[...].astype(BF16), w1_ref[...].astype(BF16),
                               preferred_element_type=F32) + b1_ref[...]))
    o_ref[...] = jnp.sin(fr * (jnp.dot(h1.astype(BF16), w2_ref[...].astype(BF16),
                                       preferred_element_type=F32) + b2_ref[...]))


def _hy_filter_kernel(hid_ref, w3_ref, b3_ref, t_ref, delta_ref, o_ref):
    f = jnp.dot(hid_ref[...].astype(BF16), w3_ref[...].astype(BF16), preferred_element_type=F32) + b3_ref[...]
    o_ref[...] = f * jnp.exp(-t_ref[...] * delta_ref[...])


def _hy_filters(n, w1, b1, w2, b2, w3, b3, freq, hy_w):
    emb, hid = w1.shape
    bands = (emb - 1) // 2
    t = jnp.linspace(0.0, 1.0, n, dtype=F32)
    pos = jnp.arange(n, dtype=F32)
    bnd = jnp.linspace(1e-4, bands - 1, bands, dtype=F32)
    ang = (2.0 * math.pi / n) * pos[:, None] * bnd[None, :]
    z = jnp.concatenate([t[:, None], jnp.cos(ang), -jnp.sin(ang)], axis=-1)
    embp = -(-emb // SUBLANES) * SUBLANES
    z = jnp.pad(z, ((0, 0), (0, embp - emb)))
    w1p = jnp.pad(w1, ((0, embp - emb), (0, 0)))
    full = lambda a: pl.BlockSpec(a.shape, lambda: (0,) * a.ndim)
    args = (z, w1p, b1[None, :], w2, b2[None, :], freq[None, :])
    hid2 = pl.pallas_call(
        _hy_hidden_kernel,
        in_specs=[full(a) for a in args],
        out_specs=pl.BlockSpec((n, hid), lambda: (0, 0)),
        out_shape=jax.ShapeDtypeStruct((n, hid), F32),
        compiler_params=pltpu.CompilerParams(vmem_limit_bytes=VMEM_LIMIT_BYTES),
    )(*args)
    nout = w3.shape[1]
    deltas = jnp.abs(jnp.linspace(math.log(HY_DECAY_TARGET) / HY_SLOW_DECAY,
                                  math.log(HY_DECAY_TARGET) / HY_FAST_DECAY, hy_w, dtype=F32))
    delta_row = jnp.tile(deltas, nout // hy_w)[None, :]
    tc = min(nout, 1024)
    return pl.pallas_call(
        _hy_filter_kernel,
        grid=(nout // tc,),
        in_specs=[pl.BlockSpec((n, hid), lambda j: (0, 0)),
                  pl.BlockSpec((hid, tc), lambda j: (0, j)),
                  pl.BlockSpec((1, tc), lambda j: (0, j)),
                  pl.BlockSpec((n, 1), lambda j: (0, 0)),
                  pl.BlockSpec((1, tc), lambda j: (0, j))],
        out_specs=pl.BlockSpec((n, tc), lambda j: (0, j)),
        out_shape=jax.ShapeDtypeStruct((n, nout), F32),
        compiler_params=_cparams(("parallel",)),
    )(hid2, w3, b3[None, :], t[:, None], delta_row)


def _dft_tables(n):
    n2 = FFT_N2
    big = 2 * n
    n1 = big // n2
    k1 = n1 // 2
    two_pi = 2.0 * math.pi

    def cs(idx, period):
        a = (idx % period).astype(F32) * (two_pi / period)
        return jnp.cos(a), jnp.sin(a)

    f1 = jnp.arange(n1, dtype=jnp.int32)
    s1 = jnp.arange(k1, dtype=jnp.int32)
    s2 = jnp.arange(n2, dtype=jnp.int32)
    s_full = s1[None, None, :] * n2 + s2[:, None, None]
    c, s = cs(f1[None, :, None] * s_full, big)
    f1tw = jnp.concatenate([jnp.concatenate([c, s], axis=2), jnp.concatenate([-s, c], axis=2)], axis=1)
    ct, st = jnp.swapaxes(c, 1, 2), jnp.swapaxes(s, 1, 2)
    g1tw = jnp.concatenate([jnp.concatenate([ct, -st], axis=2), jnp.concatenate([st, ct], axis=2)], axis=1) / big
    c2, sn2 = cs(s2[:, None] * s2[None, :], n2)
    f2 = jnp.concatenate([jnp.concatenate([c2, sn2], axis=1), jnp.concatenate([-sn2, c2], axis=1)], axis=0)
    g2 = jnp.concatenate([jnp.concatenate([c2, -sn2], axis=1), jnp.concatenate([sn2, c2], axis=1)], axis=0)
    return f1tw.astype(BF16), f2.astype(BF16), g2.astype(BF16), g1tw.astype(BF16)


def _hy_conv_kernel(a_ref, m_ref, hf_ref, hb_ref, skip_ref, cwa_ref, cwm_ref,
                    f1_ref, f2_ref, g2_ref, g1_ref, o_ref,
                    h_scr, b_scr, zr_scr, zi_scr, *, conv_a, n1, k1):
    n2 = FFT_N2
    k1p = k1 + FFT_ROW_PAD
    pb = 2 * n2 + FFT_ROW_PAD
    pos = lax.broadcasted_iota(jnp.int32, (n2, 1), 0)

    grp = FFT_UNROLL
    cw = b_scr.shape[1]

    def stage1(real_only):
        def body(g, carry):
            s2s = [g * grp + t for t in range(grp)]
            bases = [pl.multiple_of(s2 * k1p, SUBLANES) for s2 in s2s]
            if real_only:
                xs = [zr_scr[pl.ds(base, k1), :].astype(BF16) for base in bases]
                outs = [jnp.dot(f1_ref[s2, :, :k1], x, preferred_element_type=F32) for s2, x in zip(s2s, xs)]
            else:
                xs = [jnp.concatenate([zr_scr[pl.ds(base, k1), :], zi_scr[pl.ds(base, k1), :]], axis=0).astype(BF16)
                      for base in bases]
                outs = [jnp.dot(f1_ref[s2], x, preferred_element_type=F32) for s2, x in zip(s2s, xs)]
            for s2, a in zip(s2s, outs):
                b_scr[pl.ds(s2, n1, stride=pb), :] = a[:n1]
                b_scr[pl.ds(n2 + s2, n1, stride=pb), :] = a[n1:]
            return carry
        lax.fori_loop(0, n2 // grp, body, 0)

    def slab_pairs(g):
        f1s = [[(g * grp + t) * 2 + u for u in range(2)] for t in range(grp)]
        bbs = [[pl.multiple_of(f1 * pb, SUBLANES) for f1 in pair] for pair in f1s]
        hbs = [[pl.multiple_of(f1 * 2 * n2, 2 * n2) for f1 in pair] for pair in f1s]
        return bbs, hbs

    def forward2(bbs):
        bpairs = [jnp.concatenate([b_scr[pl.ds(bb, 2 * n2), :] for bb in pair], axis=1).astype(BF16) for pair in bbs]
        return [jnp.dot(f2_ref[...], bp, preferred_element_type=F32) for bp in bpairs]

    @pl.when(pl.program_id(1) == 0)
    def _():
        for which, ref in ((0, hf_ref), (1, hb_ref)):
            def load(s1, carry, ref=ref):
                r0 = pl.multiple_of(s1 * n2, n2)
                zr_scr[pl.ds(s1, n2, stride=k1p), :] = ref[pl.ds(r0, n2), :]
                return carry
            lax.fori_loop(0, k1, load, 0, unroll=FFT_UNROLL)
            if which == 1:
                zr_scr[0:1, :] = jnp.zeros((1, zr_scr.shape[1]), F32)
            stage1(True)

            def spec(g, carry, which=which):
                bbs, hbs = slab_pairs(g)
                zpairs = forward2(bbs)
                for zpair, hpair in zip(zpairs, hbs):
                    for t, hb_ in enumerate(hpair):
                        zz = zpair[:, t * cw:(t + 1) * cw]
                        if which == 0:
                            h_scr[pl.ds(hb_, 2 * n2), :] = zz
                        else:
                            h_scr[pl.ds(hb_, n2), :] = h_scr[pl.ds(hb_, n2), :] + zz[:n2]
                            h_scr[pl.ds(hb_ + n2, n2), :] = h_scr[pl.ds(hb_ + n2, n2), :] - zz[n2:]
                return carry
            lax.fori_loop(0, n1 // (2 * grp), spec, 0)

    def load_a(s1, carry):
        r0 = pl.multiple_of(s1 * n2, n2)
        for bi, scr in ((0, zr_scr), (1, zi_scr)):
            blk = a_ref[bi, pl.ds(r0, n2), :]
            if conv_a:
                blk = _row_conv(blk, cwa_ref[...], pos, n2)
            scr[pl.ds(s1, n2, stride=k1p), :] = blk
        return carry
    lax.fori_loop(0, k1, load_a, 0, unroll=FFT_UNROLL)

    stage1(False)

    def mid(g, carry):
        bbs, hbs = slab_pairs(g)
        zzs = forward2(bbs)
        yys = []
        for zz, hpair in zip(zzs, hbs):
            zr, zi = zz[:n2], zz[n2:]
            hr = jnp.concatenate([h_scr[pl.ds(hb_, n2), :] for hb_ in hpair], axis=1)
            hi = jnp.concatenate([h_scr[pl.ds(hb_ + n2, n2), :] for hb_ in hpair], axis=1)
            yys.append(jnp.concatenate([zr * hr - zi * hi, zr * hi + zi * hr], axis=0).astype(BF16))
        wws = [jnp.dot(g2_ref[...], yy, preferred_element_type=F32) for yy in yys]
        for ww, pair in zip(wws, bbs):
            for t, bb in enumerate(pair):
                b_scr[pl.ds(bb, 2 * n2), :] = ww[:, t * cw:(t + 1) * cw]
        return carry
    lax.fori_loop(0, n1 // (2 * grp), mid, 0)

    skip = skip_ref[...]

    def last(g, carry):
        s2s = [g * grp + t for t in range(grp)]
        wws = [jnp.concatenate([b_scr[pl.ds(s2, n1, stride=pb), :],
                                b_scr[pl.ds(n2 + s2, n1, stride=pb), :]], axis=0).astype(BF16) for s2 in s2s]
        ys = [jnp.dot(g1_ref[s2], ww, preferred_element_type=F32) for s2, ww in zip(s2s, wws)]
        for s2, y in zip(s2s, ys):
            base = pl.multiple_of(s2 * k1p, SUBLANES)
            zr_scr[pl.ds(base, k1), :] = y[:k1] + zr_scr[pl.ds(base, k1), :] * skip
            zi_scr[pl.ds(base, k1), :] = y[k1:] + zi_scr[pl.ds(base, k1), :] * skip
        return carry
    lax.fori_loop(0, n2 // grp, last, 0)

    def fin(s1, carry):
        r0 = pl.multiple_of(s1 * n2, n2)
        for bi, scr in ((0, zr_scr), (1, zi_scr)):
            mm = _row_conv(m_ref[bi, pl.ds(r0, n2), :], cwm_ref[...], pos, n2)
            o_ref[bi, pl.ds(r0, n2), :] = mm * scr[pl.ds(s1, n2, stride=k1p), :]
        return carry
    lax.fori_loop(0, k1, fin, 0, unroll=FFT_UNROLL)


def _hy_conv(a, a_col0, m, m_col0, filt, f_col_fwd, f_col_bwd, skip, cwa, cwm, tables, *, conv_a, hy_w):
    b, n, _ = a.shape
    assert b % 2 == 0 and GRID_W == FFT_N2 and n % FFT_N2 == 0
    n2 = FFT_N2
    n1 = 2 * n // n2
    k1 = n1 // 2
    k1p = k1 + FFT_ROW_PAD
    pb = 2 * n2 + FFT_ROW_PAD
    cw = LANES
    f1tw, f2, g2, g1tw = tables
    kern = functools.partial(_hy_conv_kernel, conv_a=conv_a, n1=n1, k1=k1)
    once = pl.Buffered(1)
    return pl.pallas_call(
        kern,
        grid=(hy_w // cw, b // 2),
        in_specs=[pl.BlockSpec((2, n, cw), lambda c, p: (p, 0, a_col0 + c), pipeline_mode=once),
                  pl.BlockSpec((2, n, cw), lambda c, p: (p, 0, m_col0 + c), pipeline_mode=once),
                  pl.BlockSpec((n, cw), lambda c, p: (0, f_col_fwd + c), pipeline_mode=once),
                  pl.BlockSpec((n, cw), lambda c, p: (0, f_col_bwd + c), pipeline_mode=once),
                  pl.BlockSpec((1, cw), lambda c, p: (0, c)),
                  pl.BlockSpec((cwa.shape[0], cw), lambda c, p: (0, a_col0 + c if conv_a else 0)),
                  pl.BlockSpec((cwm.shape[0], cw), lambda c, p: (0, m_col0 + c)),
                  pl.BlockSpec(f1tw.shape, lambda c, p: (0, 0, 0), pipeline_mode=once),
                  pl.BlockSpec(f2.shape, lambda c, p: (0, 0)),
                  pl.BlockSpec(g2.shape, lambda c, p: (0, 0)),
                  pl.BlockSpec(g1tw.shape, lambda c, p: (0, 0, 0), pipeline_mode=once)],
        out_specs=pl.BlockSpec((2, n, cw), lambda c, p: (p, 0, c)),
        out_shape=jax.ShapeDtypeStruct((b, n, hy_w), F32),
        scratch_shapes=[pltpu.VMEM((n1 * 2 * n2, cw), F32),
                        pltpu.VMEM((n1 * pb, cw), F32),
                        pltpu.VMEM((n2 * k1p, cw), F32),
                        pltpu.VMEM((n2 * k1p, cw), F32)],
        compiler_params=_cparams(("parallel", "arbitrary")),
    )(a, m, filt, filt, skip, cwa, cwm, f1tw, f2, g2, g1tw)


def _merge_kernel(of_ref, ob_ref, z_ref, y_ref, ga_ref, gb_ref, nw_ref, wa_ref, wh_ref, o_ref, *, dv):
    nw = nw_ref[...]
    heads = []
    for hh in range(of_ref.shape[1] // dv):
        hs = slice(hh * dv, (hh + 1) * dv)
        o = of_ref[:, hs] + ob_ref[:, hs]
        o = o * lax.rsqrt(jnp.mean(o * o, axis=-1, keepdims=True) + RMS_EPS) * nw
        heads.append((o * _silu(z_ref[:, hs])).astype(BF16))
    pa = jnp.dot(jnp.concatenate(heads, axis=1), wa_ref[...], preferred_element_type=F32)
    ph = jnp.dot(y_ref[...].astype(BF16), wh_ref[...], preferred_element_type=F32)
    o_ref[...] = (jax.nn.sigmoid(ga_ref[...]) * pa + jax.nn.sigmoid(gb_ref[...]) * ph).astype(o_ref.dtype)


def _merge(o_f, o_b, z, z_blk, y_hy, gate, norm_w, w_pa, w_ph, *, dv, tm):
    m, dvw = o_f.shape
    d = w_pa.shape[1]
    tm = min(tm, m)
    kern = functools.partial(_merge_kernel, dv=dv)
    row = lambda w_, blk=0: pl.BlockSpec((tm, w_), lambda i: (i, blk))
    const = lambda a: pl.BlockSpec(a.shape, lambda i: (0, 0), pipeline_mode=pl.Buffered(1))
    return pl.pallas_call(
        kern,
        grid=(m // tm,),
        in_specs=[row(dvw), row(dvw), row(dvw, z_blk), row(y_hy.shape[1]), row(d, 0), row(d, 1),
                  pl.BlockSpec((1, dv), lambda i: (0, 0)), const(w_pa), const(w_ph)],
        out_specs=row(d),
        out_shape=jax.ShapeDtypeStruct((m, d), BF16),
        compiler_params=_cparams(("parallel",)),
    )(o_f, o_b, z, y_hy, gate, gate, norm_w, w_pa, w_ph)


def _layer_norm(x, g, b):
    mu = jnp.mean(x, axis=-1, keepdims=True)
    xc = x - mu
    var = jnp.mean(xc * xc, axis=-1, keepdims=True)
    return xc * lax.rsqrt(var + LN_EPS) * g + b


def _outproj_kernel(mg_ref, x_ref, wo_ref, g1_ref, lg_ref, lb_ref, sc_ref, sh_ref, wq_ref, sk_ref,
                    x1_ref, h2_ref, s_ref, *, half):
    mix = jnp.dot(mg_ref[...], wo_ref[...], preferred_element_type=F32)
    x1 = _layer_norm(DEEPNORM_ALPHA * x_ref[...] + g1_ref[0] * mix, lg_ref[...], lb_ref[...])
    x1_ref[...] = x1
    h2 = (x1 * (1.0 + sc_ref[0]) + sh_ref[0]).astype(BF16)
    h2_ref[...] = h2
    qp = jnp.dot(h2, wq_ref[...], preferred_element_type=F32).astype(BF16)
    for j in range(sk_ref.shape[0]):
        js = slice(j * half, (j + 1) * half)
        s_ref[j] = lax.dot_general(sk_ref[j], qp[:, js], (((1,), (1,)), ((), ())), preferred_element_type=F32)


def _outproj(merged, x2d, w_out, g1, ln_g, ln_b, sc2, sh2, wq, subkeys, *, tm):
    m, d = x2d.shape
    r = g1.shape[0]
    tm = min(tm, m // r)
    tpm = (m // r) // tm
    nsk, nkeys, half = subkeys.shape
    assert nkeys == LANES
    kern = functools.partial(_outproj_kernel, half=half)
    row = lambda w_: pl.BlockSpec((tm, w_), lambda i: (i, 0))
    mod = pl.BlockSpec((1, 1, d), lambda i: (i // tpm, 0, 0))
    vec = pl.BlockSpec((1, d), lambda i: (0, 0))
    return pl.pallas_call(
        kern,
        grid=(m // tm,),
        in_specs=[row(d), row(d), pl.BlockSpec(w_out.shape, lambda i: (0, 0), pipeline_mode=pl.Buffered(1)),
                  mod, vec, vec, mod, mod,
                  pl.BlockSpec(wq.shape, lambda i: (0, 0), pipeline_mode=pl.Buffered(1)),
                  pl.BlockSpec(subkeys.shape, lambda i: (0, 0, 0))],
        out_specs=[row(d), row(d), pl.BlockSpec((nsk, nkeys, tm), lambda i: (0, 0, i))],
        out_shape=[jax.ShapeDtypeStruct((m, d), F32),
                   jax.ShapeDtypeStruct((m, d), BF16),
                   jax.ShapeDtypeStruct((nsk, nkeys, m), F32)],
        compiler_params=_cparams(("parallel",)),
    )(merged, x2d, w_out, g1, ln_g, ln_b, sc2, sh2, wq, subkeys)


def _cand_tables():
    pairs = [(a, b) for a in range(PEER_TOPK) for b in range(PEER_TOPK) if (a + 1) * (b + 1) <= PEER_TOPK]
    ia = jnp.array([p[0] for p in pairs], dtype=jnp.int32)
    ib = jnp.array([p[1] for p in pairs], dtype=jnp.int32)
    ncand = len(pairs)
    rows = -(-ncand // (2 * SUBLANES)) * (2 * SUBLANES)
    ranks = jnp.arange(LANES, dtype=jnp.int32)
    p1 = (jnp.pad(ia, (0, rows - ncand), constant_values=-1)[:, None] == ranks[None, :]).astype(F32)
    p2 = (jnp.pad(ib, (0, rows - ncand), constant_values=-1)[:, None] == ranks[None, :]).astype(F32)
    return p1, p2, ncand


def _peer_topk_kernel(s_ref, p1_ref, p2_ref, p1t_ref, c1_ref, r2_ref, e1_ref, e2_ref, *, n_heads, ncand):
    tm = s_ref.shape[2]
    neg = -jnp.inf
    hi = lax.Precision.HIGHEST
    no_rank = float(LANES)

    def pop_max(x, row_id):
        mx = jnp.max(x, axis=0, keepdims=True)
        first = jnp.min(jnp.where(x == mx, row_id, no_rank), axis=0, keepdims=True)
        hit = row_id == first
        return mx, hit, jnp.where(hit, neg, x)

    key_id = lax.broadcasted_iota(jnp.int32, (LANES, tm), 0).astype(F32)
    cand_id = lax.broadcasted_iota(jnp.int32, (p1_ref.shape[0], tm), 0).astype(F32)
    pad_rows = jnp.zeros((LANES - PEER_TOPK, tm), F32)

    def sorted_top(s):
        tops = []
        rank = jnp.full(s.shape, no_rank, F32)
        x = s
        for a in range(PEER_TOPK):
            mx, hit, x = pop_max(x, key_id)
            tops.append(mx)
            rank = jnp.where(hit, float(a), rank)
        return jnp.concatenate(tops + [pad_rows], axis=0), rank

    for h in range(n_heads):
        s1 = s_ref[2 * h]
        s2 = s_ref[2 * h + 1]
        t1, rank1 = sorted_top(s1)
        t2, rank2 = sorted_top(s2)
        cand = (jnp.dot(p1_ref[...], t1, precision=hi, preferred_element_type=F32)
                + jnp.dot(p2_ref[...], t2, precision=hi, preferred_element_type=F32))
        x = jnp.where(cand_id < ncand, cand, neg)
        sel = jnp.zeros(x.shape, F32)
        zsum = None
        cmax = None
        for r in range(PEER_TOPK):
            mx, hit, x = pop_max(x, cand_id)
            sel = jnp.where(hit, 1.0, sel)
            if r == 0:
                cmax = mx
                zsum = jnp.ones_like(mx)
            else:
                zsum = zsum + jnp.exp(mx - cmax)
        cnt = jnp.dot(p1t_ref[...].astype(BF16), sel.astype(BF16), preferred_element_type=F32)
        count1 = jnp.zeros(s1.shape, F32)
        for a in range(PEER_TOPK):
            count1 = jnp.where(rank1 == a, cnt[a:a + 1], count1)
        c1_ref[h] = count1
        r2_ref[h] = rank2.astype(r2_ref.dtype)
        e1_ref[h] = jnp.exp(s1 - t1[0:1]) / zsum
        e2_ref[h] = jnp.exp(s2 - t2[0:1]).astype(e2_ref.dtype)


def _peer_topk(scores, n_heads, *, tm):
    nsk, nkeys, m = scores.shape
    tm = min(tm, m)
    p1, p2, ncand = _cand_tables()
    kern = functools.partial(_peer_topk_kernel, n_heads=n_heads, ncand=ncand)
    full = lambda a: pl.BlockSpec(a.shape, lambda i: (0, 0))
    out = pl.BlockSpec((n_heads, nkeys, tm), lambda i: (0, 0, i))
    shp = lambda dt: jax.ShapeDtypeStruct((n_heads, nkeys, m), dt)
    p1t = p1.T
    return pl.pallas_call(
        kern,
        grid=(m // tm,),
        in_specs=[pl.BlockSpec((nsk, nkeys, tm), lambda i: (0, 0, i)), full(p1), full(p2), full(p1t)],
        out_specs=[out, out, out, out],
        out_shape=[shp(F32), shp(BF16), shp(F32), shp(BF16)],
        compiler_params=_cparams(("parallel",)),
    )(scores, p1, p2, p1t)


def _peer_dense_kernel(h2_ref, u_ref, vt_ref, r2_ref, e2_ref, c1_ref, e1_ref, x1_ref, g2_ref, lg_ref, lb_ref,
                       o_ref, acc_scr, a_scr, p_scr, *, n_heads, ti):
    e = pl.program_id(1)

    @pl.when(e == 0)
    def _():
        acc_scr[...] = jnp.zeros(acc_scr.shape, F32)

    a_scr[...] = lax.dot_general(u_ref[...], h2_ref[...], (((1,), (1,)), ((), ())), preferred_element_type=F32)
    tm = h2_ref.shape[0]
    nkeys = r2_ref.shape[1]
    for ii in range(ti):
        rs = slice(ii * nkeys, (ii + 1) * nkeys)
        for cb in range(tm // LANES):
            cs = slice(cb * LANES, (cb + 1) * LANES)
            gate = None
            for h in range(n_heads):
                keep = r2_ref[h, :, cs] < c1_ref[h, ii:ii + 1, cs].astype(BF16)
                term = jnp.where(keep, e2_ref[h, :, cs], jnp.zeros((), BF16)) * e1_ref[h, ii:ii + 1, cs].astype(BF16)
                gate = term if gate is None else gate + term
            a = a_scr[rs, cs]
            act = 0.5 * a * (1.0 + lax.erf(a * (2.0 ** -0.5)))
            p_scr[rs, cs] = gate * act.astype(BF16)
    acc_scr[...] += jnp.dot(vt_ref[...], p_scr[...], preferred_element_type=F32)

    @pl.when(e == pl.num_programs(1) - 1)
    def _():
        o_ref[...] = _layer_norm(DEEPNORM_ALPHA * x1_ref[...] + g2_ref[0] * acc_scr[...].T,
                                 lg_ref[...], lb_ref[...])


def _peer_dense(h2, u_tab, v_tab, rank2, e2, count1, e1, x1, g2, ln_g, ln_b, *, n_heads, ti, tm):
    m, d = h2.shape
    r = g2.shape[0]
    tm = min(tm, m // r)
    tpm = (m // r) // tm
    ne = u_tab.shape[0]
    nkeys = rank2.shape[1]
    te = ti * nkeys
    kern = functools.partial(_peer_dense_kernel, n_heads=n_heads, ti=ti)
    row = lambda w_: pl.BlockSpec((tm, w_), lambda i, e: (i, 0), pipeline_mode=pl.Buffered(1))
    vec = pl.BlockSpec((1, d), lambda i, e: (0, 0))
    up_tbl = pl.BlockSpec((te, d), lambda i, e: (e, 0))
    down_tbl = pl.BlockSpec((d, te), lambda i, e: (0, e))
    second = pl.BlockSpec((n_heads, nkeys, tm), lambda i, e: (0, 0, i), pipeline_mode=pl.Buffered(1))
    first = pl.BlockSpec((n_heads, ti, tm), lambda i, e: (0, e, i))
    return pl.pallas_call(
        kern,
        grid=(m // tm, ne // te),
        in_specs=[row(d), up_tbl, down_tbl, second, second, first, first, row(d),
                  pl.BlockSpec((1, 1, d), lambda i, e: (i // tpm, 0, 0)), vec, vec],
        out_specs=pl.BlockSpec((tm, d), lambda i, e: (i, 0)),
        out_shape=jax.ShapeDtypeStruct((m, d), F32),
        scratch_shapes=[pltpu.VMEM((d, tm), F32), pltpu.VMEM((te, tm), F32), pltpu.VMEM((te, tm), BF16)],
        compiler_params=_cparams(("parallel", "arbitrary")),
    )(h2, u_tab, v_tab, rank2, e2, count1, e1, x1, g2, ln_g, ln_b)


def _block(x, c, ctx, c_ctx, w_ada, b_ada, w_in, dn_conv_w, dn_a_log, dn_dt_bias, dn_norm_w, hy_conv_w, hy_w1, hy_b1, hy_w2, hy_b2, hy_w3, hy_b3, hy_freq, hy_skip, w_branch_dn, w_branch_hy, w_out, ln1_g, ln1_b, peer_wq, peer_subkeys, peer_u, peer_v, ln2_g, ln2_b):
    b, n, d = x.shape
    n_ctx = ctx.shape[1]
    assert w_ada.shape[0] == DEPTH == 1, "single-layer block: the context stream update is never read"
    hh = dn_a_log.shape[2]
    dk = dn_conv_w.shape[2] // (3 * hh)
    qk_w = hh * dk
    hy_w = hy_skip.shape[2]
    p_heads, _, nkeys, half = peer_subkeys.shape[1:]
    assert dk == LANES and nkeys == LANES and 4 * hh <= LANES and n % GRID_W == 0
    off_z = 3 * qk_w
    off_ab = 4 * qk_w
    off_hy = off_ab + 4 * hh
    off_gate = off_hy + 3 * hy_w
    assert w_in.shape[2] == off_gate + 2 * d

    cc = jnp.zeros((SUBLANES, d), F32).at[:b].set(c).at[b].set(c_ctx)
    mods = _ada(cc, w_ada[0], b_ada[0][None, :])
    sh1, sc1, g1, sh2, sc2, g2 = [mods[:b, i * d:(i + 1) * d][:, None, :] for i in range(6)]
    sh1c, sc1c = [mods[b:b + 1, i * d:(i + 1) * d][:, None, :] for i in range(2)]

    w = w_in[0]
    w_qkvz = w[:, :off_ab].astype(BF16)
    w_ab = jnp.pad(w[:, off_ab:off_hy], ((0, 0), (0, LANES - 4 * hh))).astype(BF16)
    w_hy = w[:, off_hy:off_gate].astype(BF16)
    w_gate = w[:, off_gate:].astype(BF16)

    x2d = x.reshape(b * n, d)
    c2d = ctx.reshape(b * n_ctx, d)
    proj_qkvz = _modmm(x2d, sc1, sh1, w_qkvz, tm=512, tn=2048)
    proj_ab = _modmm(x2d, sc1, sh1, w_ab, tm=512, tn=LANES)
    proj_hy = _modmm(x2d, sc1, sh1, w_hy, tm=512, tn=2048)
    proj_gate = _modmm(x2d, sc1, sh1, w_gate, tm=512, tn=2048)
    projc_qkvz = _modmm(c2d, sc1c, sh1c, w_qkvz, tm=256, tn=2048)
    projc_ab = _modmm(c2d, sc1c, sh1c, w_ab, tm=256, tn=LANES)

    def decay_row(a):
        row = jnp.concatenate([a, jnp.zeros_like(a)], axis=1).reshape(-1)
        return jnp.pad(row, (0, LANES - 4 * hh))[None, :]

    alog_row = decay_row(dn_a_log[0])
    dt_row = decay_row(dn_dt_bias[0])

    conv_w = dn_conv_w[0]
    prep_c = _dn_prep(projc_qkvz.reshape(b, n_ctx, -1), conv_w, projc_ab.reshape(b, n_ctx, LANES),
                      alog_row, dt_row, n_heads=hh, dk=dk, row_w=n_ctx, tt=n_ctx)
    prep_l = _dn_prep(proj_qkvz.reshape(b, n, -1), conv_w, proj_ab.reshape(b, n, LANES),
                      alog_row, dt_row, n_heads=hh, dk=dk, row_w=GRID_W, tt=4 * GRID_W)
    hp = 4 if hh % 4 == 0 else (2 if hh % 2 == 0 else 1)
    s_zero = jnp.zeros((2, b, hh, dk, dk), F32)
    _, _, s_ctx = _dn_scan(*prep_c, s_zero, dk=dk, tt=n_ctx, hp=hp)
    o_f, o_b, _ = _dn_scan(*prep_l, s_ctx, dk=dk, tt=4 * GRID_W, hp=hp)

    filt = _hy_filters(n, hy_w1[0], hy_b1[0], hy_w2[0], hy_b2[0], hy_w3[0], hy_b3[0], hy_freq[0], hy_w)
    tables = _dft_tables(n)
    hy3 = proj_hy.reshape(b, n, 3 * hy_w)
    cwh = hy_conv_w[0]
    cb = hy_w // LANES
    y1 = _hy_conv(hy3, 0, hy3, cb, filt, 0, 2 * cb, hy_skip[0, 0:1], cwh, cwh, tables, conv_a=True, hy_w=hy_w)
    y_hy = _hy_conv(y1, 0, hy3, 2 * cb, filt, cb, 3 * cb, hy_skip[0, 1:2], cwh, cwh, tables, conv_a=False, hy_w=hy_w)

    merged = _merge(o_f.reshape(b * n, qk_w), o_b.reshape(b * n, qk_w), proj_qkvz, off_z // qk_w,
                    y_hy.reshape(b * n, hy_w), proj_gate, dn_norm_w[0][None, :],
                    w_branch_dn[0].astype(BF16), w_branch_hy[0].astype(BF16), dv=dk, tm=256)
    sk = peer_subkeys[0].reshape(p_heads * 2, nkeys, half).astype(BF16)
    x1, h2, scores = _outproj(merged, x2d, w_out[0].astype(BF16), g1, ln1_g[0][None, :], ln1_b[0][None, :],
                              sc2, sh2, peer_wq[0].astype(BF16), sk, tm=256)

    count1, rank2, e1, e2 = _peer_topk(scores, p_heads, tm=256)
    out = _peer_dense(h2, peer_u[0].astype(BF16), peer_v[0].T.astype(BF16), rank2, e2, count1, e1,
                      x1, g2, ln2_g[0][None, :], ln2_b[0][None, :], n_heads=p_heads, ti=SUBLANES, tm=512)
    stages = dict(o_f=o_f, o_b=o_b, s_ctx=s_ctx, filt=filt, y1=y1, y_hy=y_hy, merged=merged, x1=x1, scores=scores,
                  count1=count1, rank2=rank2, e1=e1, e2=e2)
    return out.reshape(b, n, d), stages


def kernel(x, c, ctx, c_ctx, w_ada, b_ada, w_in, dn_conv_w, dn_a_log, dn_dt_bias, dn_norm_w, hy_conv_w, hy_w1, hy_b1, hy_w2, hy_b2, hy_w3, hy_b3, hy_freq, hy_skip, w_branch_dn, w_branch_hy, w_out, ln1_g, ln1_b, peer_wq, peer_subkeys, peer_u, peer_v, ln2_g, ln2_b):
    out, _ = _block(x, c, ctx, c_ctx, w_ada, b_ada, w_in, dn_conv_w, dn_a_log, dn_dt_bias, dn_norm_w, hy_conv_w, hy_w1, hy_b1, hy_w2, hy_b2, hy_w3, hy_b3, hy_freq, hy_skip, w_branch_dn, w_branch_hy, w_out, ln1_g, ln1_b, peer_wq, peer_subkeys, peer_u, peer_v, ln2_g, ln2_b)
    return out
```

```python
import functools
import math

import jax
import jax.numpy as jnp
from jax import lax
from jax.experimental import pallas as pl
from jax.experimental.pallas import tpu as pltpu

F32 = jnp.float32
BF16 = jnp.bfloat16

GRID_W = 64
DN_CHUNK = 64
PEER_TOPK = 16
DEPTH = 1
DEEPNORM_ALPHA = (2 * DEPTH) ** 0.25
LN_EPS = 1e-5
RMS_EPS = 1e-6
L2_EPS = 1e-6
HY_DECAY_TARGET = 1e-2
HY_FAST_DECAY = 0.3
HY_SLOW_DECAY = 1.5

LANES = 128
SUBLANES = 8
VMEM_LIMIT_BYTES = 56 * 1024 * 1024

FFT_N2 = 64
FFT_ROW_PAD = 8
FFT_UNROLL = 4


def _cparams(sem):
    return pltpu.CompilerParams(dimension_semantics=sem, vmem_limit_bytes=VMEM_LIMIT_BYTES)


def _silu(x):
    return x * jax.nn.sigmoid(x)


def _ada_kernel(c_ref, w_ref, b_ref, o_ref):
    s = _silu(c_ref[...])
    o_ref[...] = jnp.dot(s.astype(BF16), w_ref[...].astype(BF16), preferred_element_type=F32) + b_ref[...]


def _ada(cc, w, b):
    d, n6 = w.shape
    tn = min(n6, 1536)
    return pl.pallas_call(
        _ada_kernel,
        grid=(n6 // tn,),
        in_specs=[pl.BlockSpec((SUBLANES, d), lambda j: (0, 0)),
                  pl.BlockSpec((d, tn), lambda j: (0, j)),
                  pl.BlockSpec((1, tn), lambda j: (0, j))],
        out_specs=pl.BlockSpec((SUBLANES, tn), lambda j: (0, j)),
        out_shape=jax.ShapeDtypeStruct((SUBLANES, n6), F32),
        compiler_params=_cparams(("arbitrary",)),
    )(cc, w, b)


def _modmm_kernel(x_ref, sc_ref, sh_ref, w_ref, o_ref, h_scr):
    @pl.when(pl.program_id(1) == 0)
    def _():
        h_scr[...] = (x_ref[...] * (1.0 + sc_ref[0]) + sh_ref[0]).astype(BF16)

    o_ref[...] = jnp.dot(h_scr[...], w_ref[...], preferred_element_type=F32).astype(o_ref.dtype)


def _modmm(x2d, sc, sh, w, *, tm, tn, out_dtype=F32):
    m, d = x2d.shape
    ng = w.shape[1]
    r = sc.shape[0]
    tm = min(tm, m // r)
    tn = min(tn, ng)
    tiles_per_mod = (m // r) // tm
    return pl.pallas_call(
        _modmm_kernel,
        grid=(m // tm, ng // tn),
        in_specs=[pl.BlockSpec((tm, d), lambda i, j: (i, 0)),
                  pl.BlockSpec((1, 1, d), lambda i, j: (i // tiles_per_mod, 0, 0)),
                  pl.BlockSpec((1, 1, d), lambda i, j: (i // tiles_per_mod, 0, 0)),
                  pl.BlockSpec((d, tn), lambda i, j: (0, j))],
        out_specs=pl.BlockSpec((tm, tn), lambda i, j: (i, j)),
        out_shape=jax.ShapeDtypeStruct((m, ng), out_dtype),
        scratch_shapes=[pltpu.VMEM((tm, d), BF16)],
        compiler_params=_cparams(("parallel", "arbitrary")),
    )(x2d, sc, sh, w)


def _row_conv(x, cw, pos, row_w):
    rows = x.shape[0]
    k = cw.shape[0]
    pad = k // 2
    acc = None
    for j in range(k):
        off = j - pad
        if off == 0:
            term = x * cw[j:j + 1]
        else:
            shifted = pltpu.roll(x, (-off) % rows, axis=0)
            valid = (pos >= -off) if off < 0 else (pos < row_w - off)
            term = jnp.where(valid, shifted, 0.0) * cw[j:j + 1]
        acc = term if acc is None else acc + term
    return acc


SOLVE_BLOCK = 16


def _tri_inverse_pairs(pairs):
    c = pairs[0][0].shape[0]
    sb = SOLVE_BLOCK
    nb = c // sb
    ri = lax.broadcasted_iota(jnp.int32, (c, 2 * c), 0)
    ci = lax.broadcasted_iota(jnp.int32, (c, 2 * c), 1)
    ci = jnp.where(ci >= c, ci - c, ci)
    lane_lo = lax.broadcasted_iota(jnp.int32, (1, 2 * c), 1) < c
    off_diag = ri // sb != ci // sb
    eye2 = (ri == ci).astype(F32)
    mcats = [jnp.concatenate([m_a, m_b], axis=1) for m_a, m_b in pairs]
    m_offs = [jnp.where(off_diag, m, 0.0).astype(BF16) for m in mcats]
    xs = [[eye2[i * sb:(i + 1) * sb] for i in range(nb)] for _ in pairs]
    for bi in range(nb):
        rs = slice(bi * sb, (bi + 1) * sb)
        blks = [x[bi] for x in xs]
        if bi > 0:
            for s, x in enumerate(xs):
                xcur = jnp.concatenate(x, axis=0)
                xbd = jnp.concatenate([jnp.where(lane_lo, xcur, 0.0), jnp.where(lane_lo, 0.0, xcur)], axis=0)
                blks[s] = blks[s] - jnp.dot(m_offs[s][rs], xbd.astype(BF16), preferred_element_type=F32)
        mrs = [m[rs] for m in mcats]
        for jj in range(sb - 1):
            col = bi * sb + jj
            src = jnp.broadcast_to(jnp.where(lane_lo, col, c + col), (sb, 2 * c))
            for s in range(len(pairs)):
                mult = jnp.take_along_axis(mrs[s], src, axis=1)
                blks[s] = blks[s] - mult * blks[s][jj:jj + 1, :]
        for s, x in enumerate(xs):
            x[bi] = blks[s]
    return [jnp.concatenate(x, axis=0) for x in xs]


def _dn_gates_kernel(ab_ref, alog_ref, dt_ref, gc_ref, beta_ref, *, n_heads):
    tt = ab_ref.shape[1]
    c = DN_CHUNK
    cpos = lax.broadcasted_iota(jnp.int32, (tt, 1), 0) % c
    ab = ab_ref[0]
    z = ab + dt_ref[...]
    softplus = jnp.maximum(z, 0.0) + jnp.log1p(jnp.exp(-jnp.abs(z)))
    g_all = -jnp.exp(alog_ref[...]) * softplus
    gf = g_all
    gb = g_all
    d = 1
    while d < c:
        gf = gf + jnp.where(cpos >= d, pltpu.roll(gf, d, axis=0), 0.0)
        gb = gb + jnp.where(cpos < c - d, pltpu.roll(gb, (tt - d) % tt, axis=0), 0.0)
        d *= 2
    lane = lax.broadcasted_iota(jnp.int32, (1, LANES), 1)
    gc_ref[0] = jnp.where(lane < 2 * n_heads, gf, gb)
    beta_ref[0] = jax.nn.sigmoid(ab)


def _dn_gates(ab, alog_row, dt_row, *, n_heads, tt):
    b, n, _ = ab.shape
    blk = pl.BlockSpec((1, tt, LANES), lambda bi, g: (bi, g, 0))
    row = pl.BlockSpec((1, LANES), lambda bi, g: (0, 0))
    shp = jax.ShapeDtypeStruct((b, n, LANES), F32)
    return pl.pallas_call(
        functools.partial(_dn_gates_kernel, n_heads=n_heads),
        grid=(b, n // tt),
        in_specs=[blk, row, row],
        out_specs=[blk, blk],
        out_shape=[shp, shp],
        compiler_params=_cparams(("parallel", "parallel")),
    )(ab, alog_row, dt_row)


def _dn_prep_kernel(xq_ref, xk_ref, xv_ref, cq_ref, ck_ref, cv_ref, gc_ref, beta_ref,
                    u_ref, w_ref, qg_ref, kd_ref, attn_ref, eg_ref, *, row_w, n_heads, dk, hp):
    tt = xq_ref.shape[1]
    c = DN_CHUNK
    rows = lax.broadcasted_iota(jnp.int32, (tt, 1), 0)
    pos = rows % row_w
    lane = lax.broadcasted_iota(jnp.int32, (1, LANES), 1)
    gc = gc_ref[0]
    beta_all = beta_ref[0]
    eg_ref[...] = jnp.zeros(eg_ref.shape, F32)
    systems = []
    for hh in range(hp):
        systems += _dn_prep_head(pl.program_id(1) * hp + hh, hh, slice(hh * dk, (hh + 1) * dk), pos, lane, gc,
                                 beta_all, xq_ref, xk_ref, xv_ref, cq_ref, ck_ref, cv_ref, qg_ref, kd_ref, attn_ref,
                                 eg_ref, row_w=row_w, n_heads=n_heads, dk=dk, tt=tt)
    tinvs = _tri_inverse_pairs([(m_f, m_bt) for m_f, m_bt, _, _, _, _ in systems])
    zero_rows = jnp.zeros((c, 2 * dk), BF16)
    for tinv, (_, _, rhs_f, rhs_b, sl, hs) in zip(tinvs, systems):
        tinv = tinv.astype(BF16)
        sol_f = jnp.dot(tinv, jnp.concatenate([rhs_f, zero_rows], axis=0), preferred_element_type=F32)
        sol_b = lax.dot_general(tinv, rhs_b, (((0,), (0,)), ((), ())), preferred_element_type=F32)[c:]
        for dr, sol in enumerate((sol_f, sol_b)):
            u_ref[dr, 0, sl, hs] = sol[:, :dk]
            w_ref[dr, 0, sl, hs] = sol[:, dk:].astype(BF16)


def _dn_prep_head(h, hh, hs, pos, lane, gc, beta_all, xq_ref, xk_ref, xv_ref, cq_ref, ck_ref, cv_ref,
                  qg_ref, kd_ref, attn_ref, eg_ref, *, row_w, n_heads, dk, tt):
    c = DN_CHUNK
    systems = []

    def conv_silu(x_ref, cw_ref):
        return _silu(_row_conv(x_ref[0, :, hs], cw_ref[:, hs], pos, row_w))

    def l2n(y):
        return y * lax.rsqrt(jnp.sum(y * y, axis=-1, keepdims=True) + L2_EPS)

    q = l2n(conv_silu(xq_ref, cq_ref)) * (dk ** -0.5)
    k = l2n(conv_silu(xk_ref, ck_ref))
    v = conv_silu(xv_ref, cv_ref)

    def col(a, l):
        return jnp.sum(jnp.where(lane == l, a, 0.0), axis=1, keepdims=True)

    g_cols = (col(gc, h), col(gc, 2 * n_heads + h))
    b_cols = (col(beta_all, n_heads + h), col(beta_all, 3 * n_heads + h))

    ri = lax.broadcasted_iota(jnp.int32, (c, c), 0)
    ci = lax.broadcasted_iota(jnp.int32, (c, c), 1)
    eye = ri == ci
    lower = ri >= ci
    upper = ri <= ci
    strict_lower = ri > ci
    nt = (((1,), (1,)), ((), ()))
    for cc in range(tt // c):
        sl = slice(cc * c, (cc + 1) * c)
        qc, kc, vc = q[sl], k[sl], v[sl]
        kcb = kc.astype(BF16)
        gcols = [g_cols[dr][sl] for dr in range(2)]
        grows = [jnp.sum(jnp.where(eye, g, 0.0), axis=0, keepdims=True) for g in gcols]
        kbs = [kc * b_cols[dr][sl] for dr in range(2)]
        vbs = [vc * b_cols[dr][sl] for dr in range(2)]
        egs = [jnp.exp(g) for g in gcols]
        dif_f = gcols[0] - grows[0]
        dif_b = gcols[1] - grows[1]
        decay_f = jnp.where(lower, jnp.exp(jnp.where(lower, dif_f, 0.0)), 0.0)
        decay_b = jnp.where(upper, jnp.exp(jnp.where(upper, dif_b, 0.0)), 0.0)
        decay_bt = jnp.where(lower, jnp.exp(jnp.where(lower, -dif_b, 0.0)), 0.0)
        kq = lax.dot_general(jnp.concatenate([kbs[0], qc], axis=0).astype(BF16), kcb, nt,
                             preferred_element_type=F32)
        kkt_b = lax.dot_general(kcb, kbs[1].astype(BF16), nt, preferred_element_type=F32)
        m_f = jnp.where(strict_lower, kq[:c] * decay_f, 0.0)
        m_bt = jnp.where(strict_lower, kkt_b * decay_bt, 0.0)
        rhs_f = jnp.concatenate([vbs[0], kbs[0] * egs[0]], axis=1).astype(BF16)
        rhs_b = jnp.concatenate([vbs[1], kbs[1] * egs[1]], axis=1).astype(BF16)
        systems.append((m_f, m_bt, rhs_f, rhs_b, sl, hs))
        attns = (kq[c:] * decay_f, kq[c:] * decay_b)
        for dr in range(2):
            gcol = gcols[dr]
            glast = gcol[c - 1:c] if dr == 0 else gcol[0:1]
            qg_ref[dr, 0, sl, hs] = (qc * egs[dr]).astype(BF16)
            kd_ref[dr, 0, sl, hs] = (kc * jnp.exp(glast - gcol)).astype(BF16)
            attn_ref[dr, 0, hh, sl, :] = attns[dr].astype(BF16)
            eg_ref[dr, 0, hh, 0, cc:cc + 1, :] = jnp.broadcast_to(jnp.exp(glast), (1, LANES))
    return systems


def _dn_prep(qkv, conv_w, ab, alog_row, dt_row, *, n_heads, dk, row_w, tt):
    gc, beta = _dn_gates(ab, alog_row, dt_row, n_heads=n_heads, tt=tt)
    b, n, _ = qkv.shape
    hh = n_heads
    ng = n // tt
    c = DN_CHUNK
    assert tt % row_w == 0 and tt % c == 0 and tt // c <= SUBLANES
    hp = 2 if hh % 2 == 0 else 1
    nhb = hh // hp
    kern = functools.partial(_dn_prep_kernel, row_w=row_w, n_heads=hh, dk=dk, hp=hp)
    xspec = lambda off: pl.BlockSpec((1, tt, hp * dk), lambda bi, h, g: (bi, g, off + h))
    cspec = lambda off: pl.BlockSpec((conv_w.shape[0], hp * dk), lambda bi, h, g: (0, off + h))
    ospec = pl.BlockSpec((2, 1, tt, hp * dk), lambda bi, h, g: (0, bi, g, h))
    return pl.pallas_call(
        kern,
        grid=(b, nhb, ng),
        in_specs=[xspec(0), xspec(nhb), xspec(2 * nhb), cspec(0), cspec(nhb), cspec(2 * nhb),
                  pl.BlockSpec((1, tt, LANES), lambda bi, h, g: (bi, g, 0)),
                  pl.BlockSpec((1, tt, LANES), lambda bi, h, g: (bi, g, 0))],
        out_specs=[ospec, ospec, ospec, ospec,
                   pl.BlockSpec((2, 1, hp, tt, c), lambda bi, h, g: (0, bi, h, g, 0)),
                   pl.BlockSpec((2, 1, hp, 1, SUBLANES, LANES), lambda bi, h, g: (0, bi, h, g, 0, 0))],
        out_shape=[jax.ShapeDtypeStruct((2, b, n, hh * dk), F32),
                   jax.ShapeDtypeStruct((2, b, n, hh * dk), BF16),
                   jax.ShapeDtypeStruct((2, b, n, hh * dk), BF16),
                   jax.ShapeDtypeStruct((2, b, n, hh * dk), BF16),
                   jax.ShapeDtypeStruct((2, b, hh, n, c), BF16),
                   jax.ShapeDtypeStruct((2, b, hh, ng, SUBLANES, LANES), F32)],
        compiler_params=_cparams(("parallel", "parallel", "arbitrary")),
    )(qkv, qkv, qkv, conv_w, conv_w, conv_w, gc, beta)


def _dn_scan_kernel(uf_ref, wf_ref, qf_ref, kf_ref, af_ref, ef_ref,
                    ub_ref, wb_ref, qb_ref, kb_ref, ab_ref, eb_ref, s0_ref,
                    of_ref, ob_ref, sout_ref, s_scr, *, hp, dk):
    g = pl.program_id(2)
    ng = pl.num_programs(2)
    c = DN_CHUNK
    tt = uf_ref.shape[2]
    ncs = tt // c

    @pl.when(g == 0)
    def _():
        s_scr[...] = s0_ref[:, 0]

    refs = ((uf_ref, wf_ref, qf_ref, kf_ref, af_ref, ef_ref, of_ref),
            (ub_ref, wb_ref, qb_ref, kb_ref, ab_ref, eb_ref, ob_ref))
    chains = [(dr, hh) for dr in range(2) for hh in range(hp)]
    state = {ch: s_scr[ch[0], ch[1]] for ch in chains}
    for step in range(ncs):
        rows = {}
        for dr, hh in chains:
            cc = step if dr == 0 else ncs - 1 - step
            rows[dr, hh] = (slice(cc * c, (cc + 1) * c), slice(hh * dk, (hh + 1) * dk), cc)
        r = {}
        for ch in chains:
            w_ref, q_ref = refs[ch[0]][1], refs[ch[0]][2]
            sl, hs, _ = rows[ch]
            wq = jnp.concatenate([w_ref[0, 0, sl, hs], q_ref[0, 0, sl, hs]], axis=0)
            r[ch] = jnp.dot(wq, state[ch].astype(BF16), preferred_element_type=F32)
        vnb = {}
        for ch in chains:
            sl, hs, _ = rows[ch]
            vnb[ch] = (refs[ch[0]][0][0, 0, sl, hs] - r[ch][:c]).astype(BF16)
        for ch in chains:
            _, _, _, k_ref, a_ref, e_ref, o_ref = refs[ch[0]]
            sl, hs, cc = rows[ch]
            o_ref[0, sl, hs] = r[ch][c:] + jnp.dot(a_ref[0, 0, ch[1], sl, :], vnb[ch], preferred_element_type=F32)
            upd = lax.dot_general(k_ref[0, 0, sl, hs], vnb[ch], (((0,), (0,)), ((), ())),
                                  preferred_element_type=F32)
            state[ch] = state[ch] * e_ref[0, 0, ch[1], 0, cc:cc + 1, :] + upd
    for ch in chains:
        s_scr[ch[0], ch[1]] = state[ch]

    @pl.when(g == ng - 1)
    def _():
        sout_ref[:, 0] = s_scr[...]


def _dn_scan(u, w, qg, kd, attn, eg, s0, *, dk, tt, hp):
    _, b, n, hd = u.shape
    hh = hd // dk
    ng = n // tt
    c = DN_CHUNK
    kern = functools.partial(_dn_scan_kernel, hp=hp, dk=dk)

    def specs(dr):
        gi = (lambda g: g) if dr == 0 else (lambda g: ng - 1 - g)
        tok = pl.BlockSpec((1, 1, tt, hp * dk), lambda bi, h, g: (dr, bi, gi(g), h))
        return [tok, tok, tok, tok,
                pl.BlockSpec((1, 1, hp, tt, c), lambda bi, h, g: (dr, bi, h, gi(g), 0)),
                pl.BlockSpec((1, 1, hp, 1, SUBLANES, LANES), lambda bi, h, g: (dr, bi, h, gi(g), 0, 0))]

    sspec = pl.BlockSpec((2, 1, hp, dk, dk), lambda bi, h, g: (0, bi, h, 0, 0))
    return pl.pallas_call(
        kern,
        grid=(b, hh // hp, ng),
        in_specs=specs(0) + specs(1) + [sspec],
        out_specs=[pl.BlockSpec((1, tt, hp * dk), lambda bi, h, g: (bi, g, h)),
                   pl.BlockSpec((1, tt, hp * dk), lambda bi, h, g: (bi, ng - 1 - g, h)),
                   sspec],
        out_shape=[jax.ShapeDtypeStruct((b, n, hd), F32),
                   jax.ShapeDtypeStruct((b, n, hd), F32),
                   jax.ShapeDtypeStruct((2, b, hh, dk, dk), F32)],
        scratch_shapes=[pltpu.VMEM((2, hp, dk, dk), F32)],
        compiler_params=_cparams(("parallel", "parallel", "arbitrary")),
    )(u, w, qg, kd, attn, eg, u, w, qg, kd, attn, eg, s0)


def _hy_hidden_kernel(z_ref, w1_ref, b1_ref, w2_ref, b2_ref, fr_ref, o_ref):
    fr = fr_ref[...]
    h1 = jnp.sin(fr * (jnp.dot(z_ref[...].astype(BF16), w1_ref[...].astype(BF16),
                               preferred_element_type=F32) + b1_ref[...]))
    o_ref[...] = jnp.sin(fr * (jnp.dot(h1.astype(BF16), w2_ref[...].astype(BF16),
                                       preferred_element_type=F32) + b2_ref[...]))


def _hy_filter_kernel(hid_ref, w3_ref, b3_ref, t_ref, delta_ref, o_ref):
    f = jnp.dot(hid_ref[...].astype(BF16), w3_ref[...].astype(BF16), preferred_element_type=F32) + b3_ref[...]
    o_ref[...] = f * jnp.exp(-t_ref[...] * delta_ref[...])


def _hy_filters(n, w1, b1, w2, b2, w3, b3, freq, hy_w):
    emb, hid = w1.shape
    bands = (emb - 1) // 2
    t = jnp.linspace(0.0, 1.0, n, dtype=F32)
    pos = jnp.arange(n, dtype=F32)
    bnd = jnp.linspace(1e-4, bands - 1, bands, dtype=F32)
    ang = (2.0 * math.pi / n) * pos[:, None] * bnd[None, :]
    z = jnp.concatenate([t[:, None], jnp.cos(ang), -jnp.sin(ang)], axis=-1)
    embp = -(-emb // SUBLANES) * SUBLANES
    z = jnp.pad(z, ((0, 0), (0, embp - emb)))
    w1p = jnp.pad(w1, ((0, embp - emb), (0, 0)))
    full = lambda a: pl.BlockSpec(a.shape, lambda: (0,) * a.ndim)
    args = (z, w1p, b1[None, :], w2, b2[None, :], freq[None, :])
    hid2 = pl.pallas_call(
        _hy_hidden_kernel,
        in_specs=[full(a) for a in args],
        out_specs=pl.BlockSpec((n, hid), lambda: (0, 0)),
        out_shape=jax.ShapeDtypeStruct((n, hid), F32),
        compiler_params=pltpu.CompilerParams(vmem_limit_bytes=VMEM_LIMIT_BYTES),
    )(*args)
    nout = w3.shape[1]
    deltas = jnp.abs(jnp.linspace(math.log(HY_DECAY_TARGET) / HY_SLOW_DECAY,
                                  math.log(HY_DECAY_TARGET) / HY_FAST_DECAY, hy_w, dtype=F32))
    delta_row = jnp.tile(deltas, nout // hy_w)[None, :]
    tc = min(nout, 1024)
    return pl.pallas_call(
        _hy_filter_kernel,
        grid=(nout // tc,),
        in_specs=[pl.BlockSpec((n, hid), lambda j: (0, 0)),
                  pl.BlockSpec((hid, tc), lambda j: (0, j)),
                  pl.BlockSpec((1, tc), lambda j: (0, j)),
                  pl.BlockSpec((n, 1), lambda j: (0, 0)),
                  pl.BlockSpec((1, tc), lambda j: (0, j))],
        out_specs=pl.BlockSpec((n, tc), lambda j: (0, j)),
        out_shape=jax.ShapeDtypeStruct((n, nout), F32),
        compiler_params=_cparams(("parallel",)),
    )(hid2, w3, b3[None, :], t[:, None], delta_row)


def _dft_tables(n):
    n2 = FFT_N2
    big = 2 * n
    n1 = big // n2
    k1 = n1 // 2
    two_pi = 2.0 * math.pi

    def cs(idx, period):
        a = (idx % period).astype(F32) * (two_pi / period)
        return jnp.cos(a), jnp.sin(a)

    f1 = jnp.arange(n1, dtype=jnp.int32)
    s1 = jnp.arange(k1, dtype=jnp.int32)
    s2 = jnp.arange(n2, dtype=jnp.int32)
    s_full = s1[None, None, :] * n2 + s2[:, None, None]
    c, s = cs(f1[None, :, None] * s_full, big)
    f1tw = jnp.concatenate([jnp.concatenate([c, s], axis=2), jnp.concatenate([-s, c], axis=2)], axis=1)
    ct, st = jnp.swapaxes(c, 1, 2), jnp.swapaxes(s, 1, 2)
    g1tw = jnp.concatenate([jnp.concatenate([ct, -st], axis=2), jnp.concatenate([st, ct], axis=2)], axis=1) / big
    c2, sn2 = cs(s2[:, None] * s2[None, :], n2)
    f2 = jnp.concatenate([jnp.concatenate([c2, sn2], axis=1), jnp.concatenate([-sn2, c2], axis=1)], axis=0)
    g2 = jnp.concatenate([jnp.concatenate([c2, -sn2], axis=1), jnp.concatenate([sn2, c2], axis=1)], axis=0)
    return f1tw.astype(BF16), f2.astype(BF16), g2.astype(BF16), g1tw.astype(BF16)


def _hy_conv_kernel(a_ref, m_ref, hf_ref, hb_ref, skip_ref, cwa_ref, cwm_ref,
                    f1_ref, f2_ref, g2_ref, g1_ref, o_ref,
                    h_scr, b_scr, zr_scr, zi_scr, *, conv_a, n1, k1):
    n2 = FFT_N2
    k1p = k1 + FFT_ROW_PAD
    pb = 2 * n2 + FFT_ROW_PAD
    pos = lax.broadcasted_iota(jnp.int32, (n2, 1), 0)

    grp = FFT_UNROLL
    cw = b_scr.shape[1]

    def stage1(real_only):
        def body(g, carry):
            s2s = [g * grp + t for t in range(grp)]
            bases = [pl.multiple_of(s2 * k1p, SUBLANES) for s2 in s2s]
            if real_only:
                xs = [zr_scr[pl.ds(base, k1), :].astype(BF16) for base in bases]
                outs = [jnp.dot(f1_ref[s2, :, :k1], x, preferred_element_type=F32) for s2, x in zip(s2s, xs)]
            else:
                xs = [jnp.concatenate([zr_scr[pl.ds(base, k1), :], zi_scr[pl.ds(base, k1), :]], axis=0).astype(BF16)
                      for base in bases]
                outs = [jnp.dot(f1_ref[s2], x, preferred_element_type=F32) for s2, x in zip(s2s, xs)]
            for s2, a in zip(s2s, outs):
                b_scr[pl.ds(s2, n1, stride=pb), :] = a[:n1]
                b_scr[pl.ds(n2 + s2, n1, stride=pb), :] = a[n1:]
            return carry
        lax.fori_loop(0, n2 // grp, body, 0)

    def slab_pairs(g):
        f1s = [[(g * grp + t) * 2 + u for u in range(2)] for t in range(grp)]
        bbs = [[pl.multiple_of(f1 * pb, SUBLANES) for f1 in pair] for pair in f1s]
        hbs = [[pl.multiple_of(f1 * 2 * n2, 2 * n2) for f1 in pair] for pair in f1s]
        return bbs, hbs

    def forward2(bbs):
        bpairs = [jnp.concatenate([b_scr[pl.ds(bb, 2 * n2), :] for bb in pair], axis=1).astype(BF16) for pair in bbs]
        return [jnp.dot(f2_ref[...], bp, preferred_element_type=F32) for bp in bpairs]

    @pl.when(pl.program_id(1) == 0)
    def _():
        for which, ref in ((0, hf_ref), (1, hb_ref)):
            def load(s1, carry, ref=ref):
                r0 = pl.multiple_of(s1 * n2, n2)
                zr_scr[pl.ds(s1, n2, stride=k1p), :] = ref[pl.ds(r0, n2), :]
                return carry
            lax.fori_loop(0, k1, load, 0, unroll=FFT_UNROLL)
            if which == 1:
                zr_scr[0:1, :] = jnp.zeros((1, zr_scr.shape[1]), F32)
            stage1(True)

            def spec(g, carry, which=which):
                bbs, hbs = slab_pairs(g)
                zpairs = forward2(bbs)
                for zpair, hpair in zip(zpairs, hbs):
                    for t, hb_ in enumerate(hpair):
                        zz = zpair[:, t * cw:(t + 1) * cw]
                        if which == 0:
                            h_scr[pl.ds(hb_, 2 * n2), :] = zz
                        else:
                            h_scr[pl.ds(hb_, n2), :] = h_scr[pl.ds(hb_, n2), :] + zz[:n2]
                            h_scr[pl.ds(hb_ + n2, n2), :] = h_scr[pl.ds(hb_ + n2, n2), :] - zz[n2:]
                return carry
            lax.fori_loop(0, n1 // (2 * grp), spec, 0)

    def load_a(s1, carry):
        r0 = pl.multiple_of(s1 * n2, n2)
        for bi, scr in ((0, zr_scr), (1, zi_scr)):
            blk = a_ref[bi, pl.ds(r0, n2), :]
            if conv_a:
                blk = _row_conv(blk, cwa_ref[...], pos, n2)
            scr[pl.ds(s1, n2, stride=k1p), :] = blk
        return carry
    lax.fori_loop(0, k1, load_a, 0, unroll=FFT_UNROLL)

    stage1(False)

    def mid(g, carry):
        bbs, hbs = slab_pairs(g)
        zzs = forward2(bbs)
        yys = []
        for zz, hpair in zip(zzs, hbs):
            zr, zi = zz[:n2], zz[n2:]
            hr = jnp.concatenate([h_scr[pl.ds(hb_, n2), :] for hb_ in hpair], axis=1)
            hi = jnp.concatenate([h_scr[pl.ds(hb_ + n2, n2), :] for hb_ in hpair], axis=1)
            yys.append(jnp.concatenate([zr * hr - zi * hi, zr * hi + zi * hr], axis=0).astype(BF16))
        wws = [jnp.dot(g2_ref[...], yy, preferred_element_type=F32) for yy in yys]
        for ww, pair in zip(wws, bbs):
            for t, bb in enumerate(pair):
                b_scr[pl.ds(bb, 2 * n2), :] = ww[:, t * cw:(t + 1) * cw]
        return carry
    lax.fori_loop(0, n1 // (2 * grp), mid, 0)

    skip = skip_ref[...]

    def last(g, carry):
        s2s = [g * grp + t for t in range(grp)]
        wws = [jnp.concatenate([b_scr[pl.ds(s2, n1, stride=pb), :],
                                b_scr[pl.ds(n2 + s2, n1, stride=pb), :]], axis=0).astype(BF16) for s2 in s2s]
        ys = [jnp.dot(g1_ref[s2], ww, preferred_element_type=F32) for s2, ww in zip(s2s, wws)]
        for s2, y in zip(s2s, ys):
            base = pl.multiple_of(s2 * k1p, SUBLANES)
            zr_scr[pl.ds(base, k1), :] = y[:k1] + zr_scr[pl.ds(base, k1), :] * skip
            zi_scr[pl.ds(base, k1), :] = y[k1:] + zi_scr[pl.ds(base, k1), :] * skip
        return carry
    lax.fori_loop(0, n2 // grp, last, 0)

    def fin(s1, carry):
        r0 = pl.multiple_of(s1 * n2, n2)
        for bi, scr in ((0, zr_scr), (1, zi_scr)):
            mm = _row_conv(m_ref[bi, pl.ds(r0, n2), :], cwm_ref[...], pos, n2)
            o_ref[bi, pl.ds(r0, n2), :] = mm * scr[pl.ds(s1, n2, stride=k1p), :]
        return carry
    lax.fori_loop(0, k1, fin, 0, unroll=FFT_UNROLL)


def _hy_conv(a, a_col0, m, m_col0, filt, f_col_fwd, f_col_bwd, skip, cwa, cwm, tables, *, conv_a, hy_w):
    b, n, _ = a.shape
    assert b % 2 == 0 and GRID_W == FFT_N2 and n % FFT_N2 == 0
    n2 = FFT_N2
    n1 = 2 * n // n2
    k1 = n1 // 2
    k1p = k1 + FFT_ROW_PAD
    pb = 2 * n2 + FFT_ROW_PAD
    cw = LANES
    f1tw, f2, g2, g1tw = tables
    kern = functools.partial(_hy_conv_kernel, conv_a=conv_a, n1=n1, k1=k1)
    once = pl.Buffered(1)
    return pl.pallas_call(
        kern,
        grid=(hy_w // cw, b // 2),
        in_specs=[pl.BlockSpec((2, n, cw), lambda c, p: (p, 0, a_col0 + c), pipeline_mode=once),
                  pl.BlockSpec((2, n, cw), lambda c, p: (p, 0, m_col0 + c), pipeline_mode=once),
                  pl.BlockSpec((n, cw), lambda c, p: (0, f_col_fwd + c), pipeline_mode=once),
                  pl.BlockSpec((n, cw), lambda c, p: (0, f_col_bwd + c), pipeline_mode=once),
                  pl.BlockSpec((1, cw), lambda c, p: (0, c)),
                  pl.BlockSpec((cwa.shape[0], cw), lambda c, p: (0, a_col0 + c if conv_a else 0)),
                  pl.BlockSpec((cwm.shape[0], cw), lambda c, p: (0, m_col0 + c)),
                  pl.BlockSpec(f1tw.shape, lambda c, p: (0, 0, 0), pipeline_mode=once),
                  pl.BlockSpec(f2.shape, lambda c, p: (0, 0)),
                  pl.BlockSpec(g2.shape, lambda c, p: (0, 0)),
                  pl.BlockSpec(g1tw.shape, lambda c, p: (0, 0, 0), pipeline_mode=once)],
        out_specs=pl.BlockSpec((2, n, cw), lambda c, p: (p, 0, c)),
        out_shape=jax.ShapeDtypeStruct((b, n, hy_w), F32),
        scratch_shapes=[pltpu.VMEM((n1 * 2 * n2, cw), F32),
                        pltpu.VMEM((n1 * pb, cw), F32),
                        pltpu.VMEM((n2 * k1p, cw), F32),
                        pltpu.VMEM((n2 * k1p, cw), F32)],
        compiler_params=_cparams(("parallel", "arbitrary")),
    )(a, m, filt, filt, skip, cwa, cwm, f1tw, f2, g2, g1tw)


def _merge_kernel(of_ref, ob_ref, z_ref, y_ref, ga_ref, gb_ref, nw_ref, wa_ref, wh_ref, o_ref, *, dv):
    nw = nw_ref[...]
    heads = []
    for hh in range(of_ref.shape[1] // dv):
        hs = slice(hh * dv, (hh + 1) * dv)
        o = of_ref[:, hs] + ob_ref[:, hs]
        o = o * lax.rsqrt(jnp.mean(o * o, axis=-1, keepdims=True) + RMS_EPS) * nw
        heads.append((o * _silu(z_ref[:, hs])).astype(BF16))
    pa = jnp.dot(jnp.concatenate(heads, axis=1), wa_ref[...], preferred_element_type=F32)
    ph = jnp.dot(y_ref[...].astype(BF16), wh_ref[...], preferred_element_type=F32)
    o_ref[...] = (jax.nn.sigmoid(ga_ref[...]) * pa + jax.nn.sigmoid(gb_ref[...]) * ph).astype(o_ref.dtype)


def _merge(o_f, o_b, z, z_blk, y_hy, gate, norm_w, w_pa, w_ph, *, dv, tm):
    m, dvw = o_f.shape
    d = w_pa.shape[1]
    tm = min(tm, m)
    kern = functools.partial(_merge_kernel, dv=dv)
    row = lambda w_, blk=0: pl.BlockSpec((tm, w_), lambda i: (i, blk))
    const = lambda a: pl.BlockSpec(a.shape, lambda i: (0, 0), pipeline_mode=pl.Buffered(1))
    return pl.pallas_call(
        kern,
        grid=(m // tm,),
        in_specs=[row(dvw), row(dvw), row(dvw, z_blk), row(y_hy.shape[1]), row(d, 0), row(d, 1),
                  pl.BlockSpec((1, dv), lambda i: (0, 0)), const(w_pa), const(w_ph)],
        out_specs=row(d),
        out_shape=jax.ShapeDtypeStruct((m, d), BF16),
        compiler_params=_cparams(("parallel",)),
    )(o_f, o_b, z, y_hy, gate, gate, norm_w, w_pa, w_ph)


def _layer_norm(x, g, b):
    mu = jnp.mean(x, axis=-1, keepdims=True)
    xc = x - mu
    var = jnp.mean(xc * xc, axis=-1, keepdims=True)
    return xc * lax.rsqrt(var + LN_EPS) * g + b


def _outproj_kernel(mg_ref, x_ref, wo_ref, g1_ref, lg_ref, lb_ref, sc_ref, sh_ref, wq_ref, sk_ref,
                    x1_ref, h2_ref, s_ref, *, half):
    mix = jnp.dot(mg_ref[...], wo_ref[...], preferred_element_type=F32)
    x1 = _layer_norm(DEEPNORM_ALPHA * x_ref[...] + g1_ref[0] * mix, lg_ref[...], lb_ref[...])
    x1_ref[...] = x1
    h2 = (x1 * (1.0 + sc_ref[0]) + sh_ref[0]).astype(BF16)
    h2_ref[...] = h2
    qp = jnp.dot(h2, wq_ref[...], preferred_element_type=F32).astype(BF16)
    for j in range(sk_ref.shape[0]):
        js = slice(j * half, (j + 1) * half)
        s_ref[j] = lax.dot_general(sk_ref[j], qp[:, js], (((1,), (1,)), ((), ())), preferred_element_type=F32)


def _outproj(merged, x2d, w_out, g1, ln_g, ln_b, sc2, sh2, wq, subkeys, *, tm):
    m, d = x2d.shape
    r = g1.shape[0]
    tm = min(tm, m // r)
    tpm = (m // r) // tm
    nsk, nkeys, half = subkeys.shape
    assert nkeys == LANES
    kern = functools.partial(_outproj_kernel, half=half)
    row = lambda w_: pl.BlockSpec((tm, w_), lambda i: (i, 0))
    mod = pl.BlockSpec((1, 1, d), lambda i: (i // tpm, 0, 0))
    vec = pl.BlockSpec((1, d), lambda i: (0, 0))
    return pl.pallas_call(
        kern,
        grid=(m // tm,),
        in_specs=[row(d), row(d), pl.BlockSpec(w_out.shape, lambda i: (0, 0), pipeline_mode=pl.Buffered(1)),
                  mod, vec, vec, mod, mod,
                  pl.BlockSpec(wq.shape, lambda i: (0, 0), pipeline_mode=pl.Buffered(1)),
                  pl.BlockSpec(subkeys.shape, lambda i: (0, 0, 0))],
        out_specs=[row(d), row(d), pl.BlockSpec((nsk, nkeys, tm), lambda i: (0, 0, i))],
        out_shape=[jax.ShapeDtypeStruct((m, d), F32),
                   jax.ShapeDtypeStruct((m, d), BF16),
                   jax.ShapeDtypeStruct((nsk, nkeys, m), F32)],
        compiler_params=_cparams(("parallel",)),
    )(merged, x2d, w_out, g1, ln_g, ln_b, sc2, sh2, wq, subkeys)


def _cand_tables():
    pairs = [(a, b) for a in range(PEER_TOPK) for b in range(PEER_TOPK) if (a + 1) * (b + 1) <= PEER_TOPK]
    ia = jnp.array([p[0] for p in pairs], dtype=jnp.int32)
    ib = jnp.array([p[1] for p in pairs], dtype=jnp.int32)
    ncand = len(pairs)
    rows = -(-ncand // (2 * SUBLANES)) * (2 * SUBLANES)
    ranks = jnp.arange(LANES, dtype=jnp.int32)
    p1 = (jnp.pad(ia, (0, rows - ncand), constant_values=-1)[:, None] == ranks[None, :]).astype(F32)
    p2 = (jnp.pad(ib, (0, rows - ncand), constant_values=-1)[:, None] == ranks[None, :]).astype(F32)
    return p1, p2, ncand


def _peer_topk_kernel(s_ref, p1_ref, p2_ref, p1t_ref, c1_ref, r2_ref, e1_ref, e2_ref, *, n_heads, ncand):
    tm = s_ref.shape[2]
    neg = -jnp.inf
    hi = lax.Precision.HIGHEST
    no_rank = float(LANES)

    def pop_max(x, row_id):
        mx = jnp.max(x, axis=0, keepdims=True)
        first = jnp.min(jnp.where(x == mx, row_id, no_rank), axis=0, keepdims=True)
        hit = row_id == first
        return mx, hit, jnp.where(hit, neg, x)

    key_id = lax.broadcasted_iota(jnp.int32, (LANES, tm), 0).astype(F32)
    cand_id = lax.broadcasted_iota(jnp.int32, (p1_ref.shape[0], tm), 0).astype(F32)
    pad_rows = jnp.zeros((LANES - PEER_TOPK, tm), F32)

    def sorted_top(s):
        tops = []
        rank = jnp.full(s.shape, no_rank, F32)
        x = s
        for a in range(PEER_TOPK):
            mx, hit, x = pop_max(x, key_id)
            tops.append(mx)
            rank = jnp.where(hit, float(a), rank)
        return jnp.concatenate(tops + [pad_rows], axis=0), rank

    for h in range(n_heads):
        s1 = s_ref[2 * h]
        s2 = s_ref[2 * h + 1]
        t1, rank1 = sorted_top(s1)
        t2, rank2 = sorted_top(s2)
        cand = (jnp.dot(p1_ref[...], t1, precision=hi, preferred_element_type=F32)
                + jnp.dot(p2_ref[...], t2, precision=hi, preferred_element_type=F32))
        x = jnp.where(cand_id < ncand, cand, neg)
        sel = jnp.zeros(x.shape, F32)
        zsum = None
        cmax = None
        for r in range(PEER_TOPK):
            mx, hit, x = pop_max(x, cand_id)
            sel = jnp.where(hit, 1.0, sel)
            if r == 0:
                cmax = mx
                zsum = jnp.ones_like(mx)
            else:
                zsum = zsum + jnp.exp(mx - cmax)
        cnt = jnp.dot(p1t_ref[...].astype(BF16), sel.astype(BF16), preferred_element_type=F32)
        count1 = jnp.zeros(s1.shape, F32)
        for a in range(PEER_TOPK):
            count1 = jnp.where(rank1 == a, cnt[a:a + 1], count1)
        c1_ref[h] = count1
        r2_ref[h] = rank2.astype(r2_ref.dtype)
        e1_ref[h] = jnp.exp(s1 - t1[0:1]) / zsum
        e2_ref[h] = jnp.exp(s2 - t2[0:1]).astype(e2_ref.dtype)


def _peer_topk(scores, n_heads, *, tm):
    nsk, nkeys, m = scores.shape
    tm = min(tm, m)
    p1, p2, ncand = _cand_tables()
    kern = functools.partial(_peer_topk_kernel, n_heads=n_heads, ncand=ncand)
    full = lambda a: pl.BlockSpec(a.shape, lambda i: (0, 0))
    out = pl.BlockSpec((n_heads, nkeys, tm), lambda i: (0, 0, i))
    shp = lambda dt: jax.ShapeDtypeStruct((n_heads, nkeys, m), dt)
    p1t = p1.T
    return pl.pallas_call(
        kern,
        grid=(m // tm,),
        in_specs=[pl.BlockSpec((nsk, nkeys, tm), lambda i: (0, 0, i)), full(p1), full(p2), full(p1t)],
        out_specs=[out, out, out, out],
        out_shape=[shp(F32), shp(F32), shp(F32), shp(F32)],
        compiler_params=_cparams(("parallel",)),
    )(scores, p1, p2, p1t)


def _peer_dense_kernel(h2_ref, u_ref, v_ref, r2_ref, e2_ref, c1_ref, e1_ref, x1_ref, g2_ref, lg_ref, lb_ref,
                       o_ref, acc_scr, a_scr, p_scr, r2_scr, e2_scr, *, n_heads, ti):
    e = pl.program_id(1)

    @pl.when(e == 0)
    def _():
        acc_scr[...] = jnp.zeros(acc_scr.shape, F32)
        r2_scr[...] = r2_ref[...].astype(BF16)
        e2_scr[...] = e2_ref[...].astype(BF16)

    a_scr[...] = lax.dot_general(u_ref[...], h2_ref[...], (((1,), (1,)), ((), ())), preferred_element_type=F32)
    tm = h2_ref.shape[0]
    nkeys = r2_ref.shape[1]
    for ii in range(ti):
        rs = slice(ii * nkeys, (ii + 1) * nkeys)
        for cb in range(tm // LANES):
            cs = slice(cb * LANES, (cb + 1) * LANES)
            gate = None
            for h in range(n_heads):
                keep = r2_scr[h, :, cs] < c1_ref[h, ii:ii + 1, cs].astype(BF16)
                term = jnp.where(keep, e2_scr[h, :, cs], jnp.zeros((), BF16)) * e1_ref[h, ii:ii + 1, cs].astype(BF16)
                gate = term if gate is None else gate + term
            a = a_scr[rs, cs]
            act = 0.5 * a * (1.0 + lax.erf(a * (2.0 ** -0.5)))
            p_scr[rs, cs] = gate * act.astype(BF16)
    acc_scr[...] += lax.dot_general(p_scr[...], v_ref[...], (((0,), (0,)), ((), ())), preferred_element_type=F32)

    @pl.when(e == pl.num_programs(1) - 1)
    def _():
        o_ref[...] = _layer_norm(DEEPNORM_ALPHA * x1_ref[...] + g2_ref[0] * acc_scr[...],
                                 lg_ref[...], lb_ref[...])


def _peer_dense(h2, u_tab, v_tab, rank2, e2, count1, e1, x1, g2, ln_g, ln_b, *, n_heads, ti, tm):
    m, d = h2.shape
    r = g2.shape[0]
    tm = min(tm, m // r)
    tpm = (m // r) // tm
    ne = u_tab.shape[0]
    nkeys = rank2.shape[1]
    te = ti * nkeys
    kern = functools.partial(_peer_dense_kernel, n_heads=n_heads, ti=ti)
    row = lambda w_: pl.BlockSpec((tm, w_), lambda i, e: (i, 0), pipeline_mode=pl.Buffered(1))
    vec = pl.BlockSpec((1, d), lambda i, e: (0, 0))
    tbl = pl.BlockSpec((te, d), lambda i, e: (e, 0))
    second = pl.BlockSpec((n_heads, nkeys, tm), lambda i, e: (0, 0, i), pipeline_mode=pl.Buffered(1))
    first = pl.BlockSpec((n_heads, ti, tm), lambda i, e: (0, e, i))
    return pl.pallas_call(
        kern,
        grid=(m // tm, ne // te),
        in_specs=[row(d), tbl, tbl, second, second, first, first, row(d),
                  pl.BlockSpec((1, 1, d), lambda i, e: (i // tpm, 0, 0)), vec, vec],
        out_specs=pl.BlockSpec((tm, d), lambda i, e: (i, 0)),
        out_shape=jax.ShapeDtypeStruct((m, d), F32),
        scratch_shapes=[pltpu.VMEM((tm, d), F32), pltpu.VMEM((te, tm), F32), pltpu.VMEM((te, tm), BF16),
                        pltpu.VMEM((n_heads, nkeys, tm), BF16), pltpu.VMEM((n_heads, nkeys, tm), BF16)],
        compiler_params=_cparams(("parallel", "arbitrary")),
    )(h2, u_tab, v_tab, rank2, e2, count1, e1, x1, g2, ln_g, ln_b)


def _block(x, c, ctx, c_ctx, w_ada, b_ada, w_in, dn_conv_w, dn_a_log, dn_dt_bias, dn_norm_w, hy_conv_w, hy_w1, hy_b1, hy_w2, hy_b2, hy_w3, hy_b3, hy_freq, hy_skip, w_branch_dn, w_branch_hy, w_out, ln1_g, ln1_b, peer_wq, peer_subkeys, peer_u, peer_v, ln2_g, ln2_b):
    b, n, d = x.shape
    n_ctx = ctx.shape[1]
    assert w_ada.shape[0] == DEPTH == 1, "single-layer block: the context stream update is never read"
    hh = dn_a_log.shape[2]
    dk = dn_conv_w.shape[2] // (3 * hh)
    qk_w = hh * dk
    hy_w = hy_skip.shape[2]
    p_heads, _, nkeys, half = peer_subkeys.shape[1:]
    assert dk == LANES and nkeys == LANES and 4 * hh <= LANES and n % GRID_W == 0
    off_z = 3 * qk_w
    off_ab = 4 * qk_w
    off_hy = off_ab + 4 * hh
    off_gate = off_hy + 3 * hy_w
    assert w_in.shape[2] == off_gate + 2 * d

    cc = jnp.zeros((SUBLANES, d), F32).at[:b].set(c).at[b].set(c_ctx)
    mods = _ada(cc, w_ada[0], b_ada[0][None, :])
    sh1, sc1, g1, sh2, sc2, g2 = [mods[:b, i * d:(i + 1) * d][:, None, :] for i in range(6)]
    sh1c, sc1c = [mods[b:b + 1, i * d:(i + 1) * d][:, None, :] for i in range(2)]

    w = w_in[0]
    w_qkvz = w[:, :off_ab].astype(BF16)
    w_ab = jnp.pad(w[:, off_ab:off_hy], ((0, 0), (0, LANES - 4 * hh))).astype(BF16)
    w_hy = w[:, off_hy:off_gate].astype(BF16)
    w_gate = w[:, off_gate:].astype(BF16)

    x2d = x.reshape(b * n, d)
    c2d = ctx.reshape(b * n_ctx, d)
    proj_qkvz = _modmm(x2d, sc1, sh1, w_qkvz, tm=512, tn=2048)
    proj_ab = _modmm(x2d, sc1, sh1, w_ab, tm=512, tn=LANES)
    proj_hy = _modmm(x2d, sc1, sh1, w_hy, tm=512, tn=2048)
    proj_gate = _modmm(x2d, sc1, sh1, w_gate, tm=512, tn=2048)
    projc_qkvz = _modmm(c2d, sc1c, sh1c, w_qkvz, tm=256, tn=2048)
    projc_ab = _modmm(c2d, sc1c, sh1c, w_ab, tm=256, tn=LANES)

    def decay_row(a):
        row = jnp.concatenate([a, jnp.zeros_like(a)], axis=1).reshape(-1)
        return jnp.pad(row, (0, LANES - 4 * hh))[None, :]

    alog_row = decay_row(dn_a_log[0])
    dt_row = decay_row(dn_dt_bias[0])

    conv_w = dn_conv_w[0]
    prep_c = _dn_prep(projc_qkvz.reshape(b, n_ctx, -1), conv_w, projc_ab.reshape(b, n_ctx, LANES),
                      alog_row, dt_row, n_heads=hh, dk=dk, row_w=n_ctx, tt=n_ctx)
    prep_l = _dn_prep(proj_qkvz.reshape(b, n, -1), conv_w, proj_ab.reshape(b, n, LANES),
                      alog_row, dt_row, n_heads=hh, dk=dk, row_w=GRID_W, tt=4 * GRID_W)
    hp = 4 if hh % 4 == 0 else (2 if hh % 2 == 0 else 1)
    s_zero = jnp.zeros((2, b, hh, dk, dk), F32)
    _, _, s_ctx = _dn_scan(*prep_c, s_zero, dk=dk, tt=n_ctx, hp=hp)
    o_f, o_b, _ = _dn_scan(*prep_l, s_ctx, dk=dk, tt=4 * GRID_W, hp=hp)

    filt = _hy_filters(n, hy_w1[0], hy_b1[0], hy_w2[0], hy_b2[0], hy_w3[0], hy_b3[0], hy_freq[0], hy_w)
    tables = _dft_tables(n)
    hy3 = proj_hy.reshape(b, n, 3 * hy_w)
    cwh = hy_conv_w[0]
    cb = hy_w // LANES
    y1 = _hy_conv(hy3, 0, hy3, cb, filt, 0, 2 * cb, hy_skip[0, 0:1], cwh, cwh, tables, conv_a=True, hy_w=hy_w)
    y_hy = _hy_conv(y1, 0, hy3, 2 * cb, filt, cb, 3 * cb, hy_skip[0, 1:2], cwh, cwh, tables, conv_a=False, hy_w=hy_w)

    merged = _merge(o_f.reshape(b * n, qk_w), o_b.reshape(b * n, qk_w), proj_qkvz, off_z // qk_w,
                    y_hy.reshape(b * n, hy_w), proj_gate, dn_norm_w[0][None, :],
                    w_branch_dn[0].astype(BF16), w_branch_hy[0].astype(BF16), dv=dk, tm=256)
    sk = peer_subkeys[0].reshape(p_heads * 2, nkeys, half).astype(BF16)
    x1, h2, scores = _outproj(merged, x2d, w_out[0].astype(BF16), g1, ln1_g[0][None, :], ln1_b[0][None, :],
                              sc2, sh2, peer_wq[0].astype(BF16), sk, tm=256)

    count1, rank2, e1, e2 = _peer_topk(scores, p_heads, tm=256)
    out = _peer_dense(h2, peer_u[0].astype(BF16), peer_v[0].astype(BF16), rank2, e2, count1, e1,
                      x1, g2, ln2_g[0][None, :], ln2_b[0][None, :], n_heads=p_heads, ti=SUBLANES, tm=512)
    stages = dict(o_f=o_f, o_b=o_b, s_ctx=s_ctx, filt=filt, y1=y1, y_hy=y_hy, merged=merged, x1=x1, scores=scores,
                  count1=count1, rank2=rank2, e1=e1, e2=e2)
    return out.reshape(b, n, d), stages


def kernel(x, c, ctx, c_ctx, w_ada, b_ada, w_in, dn_conv_w, dn_a_log, dn_dt_bias, dn_norm_w, hy_conv_w, hy_w1, hy_b1, hy_w2, hy_b2, hy_w3, hy_b3, hy_freq, hy_skip, w_branch_dn, w_branch_hy, w_out, ln1_g, ln1_b, peer_wq, peer_subkeys, peer_u, peer_v, ln2_g, ln2_b):
    out, _ = _block(x, c, ctx, c_ctx, w_ada, b_ada, w_in, dn_conv_w, dn_a_log, dn_dt_bias, dn_norm_w, hy_conv_w, hy_w1, hy_b1, hy_w2, hy_b2, hy_w3, hy_b3, hy_freq, hy_skip, w_branch_dn, w_branch_hy, w_out, ln1_g, ln1_b, peer_wq, peer_subkeys, peer_u, peer_v, ln2_g, ln2_b)
    return out
```

```python
import functools
import math

import jax
import jax.numpy as jnp
from jax import lax
from jax.experimental import pallas as pl
from jax.experimental.pallas import tpu as pltpu

F32 = jnp.float32
BF16 = jnp.bfloat16

GRID_W = 64
DN_CHUNK = 64
PEER_TOPK = 16
DEPTH = 1
DEEPNORM_ALPHA = (2 * DEPTH) ** 0.25
LN_EPS = 1e-5
RMS_EPS = 1e-6
L2_EPS = 1e-6
HY_DECAY_TARGET = 1e-2
HY_FAST_DECAY = 0.3
HY_SLOW_DECAY = 1.5

LANES = 128
SUBLANES = 8
VMEM_LIMIT_BYTES = 56 * 1024 * 1024

FFT_N2 = 64
FFT_ROW_PAD = 8
FFT_UNROLL = 4


def _cparams(sem):
    return pltpu.CompilerParams(dimension_semantics=sem, vmem_limit_bytes=VMEM_LIMIT_BYTES)


def _silu(x):
    return x * jax.nn.sigmoid(x)


def _ada_kernel(c_ref, w_ref, b_ref, o_ref):
    s = _silu(c_ref[...])
    o_ref[...] = jnp.dot(s.astype(BF16), w_ref[...].astype(BF16), preferred_element_type=F32) + b_ref[...]


def _ada(cc, w, b):
    d, n6 = w.shape
    tn = min(n6, 1536)
    return pl.pallas_call(
        _ada_kernel,
        grid=(n6 // tn,),
        in_specs=[pl.BlockSpec((SUBLANES, d), lambda j: (0, 0)),
                  pl.BlockSpec((d, tn), lambda j: (0, j)),
                  pl.BlockSpec((1, tn), lambda j: (0, j))],
        out_specs=pl.BlockSpec((SUBLANES, tn), lambda j: (0, j)),
        out_shape=jax.ShapeDtypeStruct((SUBLANES, n6), F32),
        compiler_params=_cparams(("arbitrary",)),
    )(cc, w, b)


def _modmm_kernel(x_ref, sc_ref, sh_ref, w_ref, o_ref, h_scr):
    @pl.when(pl.program_id(1) == 0)
    def _():
        h_scr[...] = (x_ref[...] * (1.0 + sc_ref[0]) + sh_ref[0]).astype(BF16)

    o_ref[...] = jnp.dot(h_scr[...], w_ref[...], preferred_element_type=F32).astype(o_ref.dtype)


def _modmm(x2d, sc, sh, w, *, tm, tn, out_dtype=F32):
    m, d = x2d.shape
    ng = w.shape[1]
    r = sc.shape[0]
    tm = min(tm, m // r)
    tn = min(tn, ng)
    tiles_per_mod = (m // r) // tm
    return pl.pallas_call(
        _modmm_kernel,
        grid=(m // tm, ng // tn),
        in_specs=[pl.BlockSpec((tm, d), lambda i, j: (i, 0)),
                  pl.BlockSpec((1, 1, d), lambda i, j: (i // tiles_per_mod, 0, 0)),
                  pl.BlockSpec((1, 1, d), lambda i, j: (i // tiles_per_mod, 0, 0)),
                  pl.BlockSpec((d, tn), lambda i, j: (0, j))],
        out_specs=pl.BlockSpec((tm, tn), lambda i, j: (i, j)),
        out_shape=jax.ShapeDtypeStruct((m, ng), out_dtype),
        scratch_shapes=[pltpu.VMEM((tm, d), BF16)],
        compiler_params=_cparams(("parallel", "arbitrary")),
    )(x2d, sc, sh, w)


def _row_conv(x, cw, pos, row_w):
    rows = x.shape[0]
    k = cw.shape[0]
    pad = k // 2
    acc = None
    for j in range(k):
        off = j - pad
        if off == 0:
            term = x * cw[j:j + 1]
        else:
            shifted = pltpu.roll(x, (-off) % rows, axis=0)
            valid = (pos >= -off) if off < 0 else (pos < row_w - off)
            term = jnp.where(valid, shifted, 0.0) * cw[j:j + 1]
        acc = term if acc is None else acc + term
    return acc


SOLVE_BLOCK = 16


def _tri_inverse_pairs(pairs):
    c = pairs[0][0].shape[0]
    sb = SOLVE_BLOCK
    nb = c // sb
    ri = lax.broadcasted_iota(jnp.int32, (c, 2 * c), 0)
    ci = lax.broadcasted_iota(jnp.int32, (c, 2 * c), 1)
    ci = jnp.where(ci >= c, ci - c, ci)
    lane_lo = lax.broadcasted_iota(jnp.int32, (1, 2 * c), 1) < c
    off_diag = ri // sb != ci // sb
    eye2 = (ri == ci).astype(F32)
    mcats = [jnp.concatenate([m_a, m_b], axis=1) for m_a, m_b in pairs]
    m_offs = [jnp.where(off_diag, m, 0.0).astype(BF16) for m in mcats]
    xs = [[eye2[i * sb:(i + 1) * sb] for i in range(nb)] for _ in pairs]
    for bi in range(nb):
        rs = slice(bi * sb, (bi + 1) * sb)
        blks = [x[bi] for x in xs]
        if bi > 0:
            for s, x in enumerate(xs):
                xcur = jnp.concatenate(x, axis=0)
                xbd = jnp.concatenate([jnp.where(lane_lo, xcur, 0.0), jnp.where(lane_lo, 0.0, xcur)], axis=0)
                blks[s] = blks[s] - jnp.dot(m_offs[s][rs], xbd.astype(BF16), preferred_element_type=F32)
        mrs = [m[rs] for m in mcats]
        for jj in range(sb - 1):
            col = bi * sb + jj
            src = jnp.broadcast_to(jnp.where(lane_lo, col, c + col), (sb, 2 * c))
            for s in range(len(pairs)):
                mult = jnp.take_along_axis(mrs[s], src, axis=1)
                blks[s] = blks[s] - mult * blks[s][jj:jj + 1, :]
        for s, x in enumerate(xs):
            x[bi] = blks[s]
    return [jnp.concatenate(x, axis=0) for x in xs]


def _dn_gates_kernel(ab_ref, alog_ref, dt_ref, gc_ref, beta_ref, *, n_heads):
    tt = ab_ref.shape[1]
    c = DN_CHUNK
    cpos = lax.broadcasted_iota(jnp.int32, (tt, 1), 0) % c
    ab = ab_ref[0]
    z = ab + dt_ref[...]
    softplus = jnp.maximum(z, 0.0) + jnp.log1p(jnp.exp(-jnp.abs(z)))
    g_all = -jnp.exp(alog_ref[...]) * softplus
    gf = g_all
    gb = g_all
    d = 1
    while d < c:
        gf = gf + jnp.where(cpos >= d, pltpu.roll(gf, d, axis=0), 0.0)
        gb = gb + jnp.where(cpos < c - d, pltpu.roll(gb, (tt - d) % tt, axis=0), 0.0)
        d *= 2
    lane = lax.broadcasted_iota(jnp.int32, (1, LANES), 1)
    gc_ref[0] = jnp.where(lane < 2 * n_heads, gf, gb)
    beta_ref[0] = jax.nn.sigmoid(ab)


def _dn_gates(ab, alog_row, dt_row, *, n_heads, tt):
    b, n, _ = ab.shape
    blk = pl.BlockSpec((1, tt, LANES), lambda bi, g: (bi, g, 0))
    row = pl.BlockSpec((1, LANES), lambda bi, g: (0, 0))
    shp = jax.ShapeDtypeStruct((b, n, LANES), F32)
    return pl.pallas_call(
        functools.partial(_dn_gates_kernel, n_heads=n_heads),
        grid=(b, n // tt),
        in_specs=[blk, row, row],
        out_specs=[blk, blk],
        out_shape=[shp, shp],
        compiler_params=_cparams(("parallel", "parallel")),
    )(ab, alog_row, dt_row)


def _dn_prep_kernel(xq_ref, xk_ref, xv_ref, cq_ref, ck_ref, cv_ref, gc_ref, beta_ref,
                    u_ref, w_ref, qg_ref, kd_ref, attn_ref, eg_ref, *, row_w, n_heads, dk, hp):
    tt = xq_ref.shape[1]
    c = DN_CHUNK
    rows = lax.broadcasted_iota(jnp.int32, (tt, 1), 0)
    pos = rows % row_w
    lane = lax.broadcasted_iota(jnp.int32, (1, LANES), 1)
    gc = gc_ref[0]
    beta_all = beta_ref[0]
    eg_ref[...] = jnp.zeros(eg_ref.shape, F32)
    systems = []
    for hh in range(hp):
        systems += _dn_prep_head(pl.program_id(1) * hp + hh, hh, slice(hh * dk, (hh + 1) * dk), pos, lane, gc,
                                 beta_all, xq_ref, xk_ref, xv_ref, cq_ref, ck_ref, cv_ref, qg_ref, kd_ref, attn_ref,
                                 eg_ref, row_w=row_w, n_heads=n_heads, dk=dk, tt=tt)
    tinvs = _tri_inverse_pairs([(m_f, m_bt) for m_f, m_bt, _, _, _, _ in systems])
    zero_rows = jnp.zeros((c, 2 * dk), BF16)
    for tinv, (_, _, rhs_f, rhs_b, sl, hs) in zip(tinvs, systems):
        tinv = tinv.astype(BF16)
        sol_f = jnp.dot(tinv, jnp.concatenate([rhs_f, zero_rows], axis=0), preferred_element_type=F32)
        sol_b = lax.dot_general(tinv, rhs_b, (((0,), (0,)), ((), ())), preferred_element_type=F32)[c:]
        for dr, sol in enumerate((sol_f, sol_b)):
            u_ref[dr, 0, sl, hs] = sol[:, :dk]
            w_ref[dr, 0, sl, hs] = sol[:, dk:].astype(BF16)


def _dn_prep_head(h, hh, hs, pos, lane, gc, beta_all, xq_ref, xk_ref, xv_ref, cq_ref, ck_ref, cv_ref,
                  qg_ref, kd_ref, attn_ref, eg_ref, *, row_w, n_heads, dk, tt):
    c = DN_CHUNK
    systems = []

    def conv_silu(x_ref, cw_ref):
        return _silu(_row_conv(x_ref[0, :, hs], cw_ref[:, hs], pos, row_w))

    def l2n(y):
        return y * lax.rsqrt(jnp.sum(y * y, axis=-1, keepdims=True) + L2_EPS)

    q = l2n(conv_silu(xq_ref, cq_ref)) * (dk ** -0.5)
    k = l2n(conv_silu(xk_ref, ck_ref))
    v = conv_silu(xv_ref, cv_ref)

    def col(a, l):
        return jnp.sum(jnp.where(lane == l, a, 0.0), axis=1, keepdims=True)

    g_cols = (col(gc, h), col(gc, 2 * n_heads + h))
    b_cols = (col(beta_all, n_heads + h), col(beta_all, 3 * n_heads + h))

    ri = lax.broadcasted_iota(jnp.int32, (c, c), 0)
    ci = lax.broadcasted_iota(jnp.int32, (c, c), 1)
    eye = ri == ci
    lower = ri >= ci
    upper = ri <= ci
    strict_lower = ri > ci
    nt = (((1,), (1,)), ((), ()))
    for cc in range(tt // c):
        sl = slice(cc * c, (cc + 1) * c)
        qc, kc, vc = q[sl], k[sl], v[sl]
        kcb = kc.astype(BF16)
        gcols = [g_cols[dr][sl] for dr in range(2)]
        grows = [jnp.sum(jnp.where(eye, g, 0.0), axis=0, keepdims=True) for g in gcols]
        kbs = [kc * b_cols[dr][sl] for dr in range(2)]
        vbs = [vc * b_cols[dr][sl] for dr in range(2)]
        egs = [jnp.exp(g) for g in gcols]
        dif_f = gcols[0] - grows[0]
        dif_b = gcols[1] - grows[1]
        decay_f = jnp.where(lower, jnp.exp(jnp.where(lower, dif_f, 0.0)), 0.0)
        decay_b = jnp.where(upper, jnp.exp(jnp.where(upper, dif_b, 0.0)), 0.0)
        decay_bt = jnp.where(lower, jnp.exp(jnp.where(lower, -dif_b, 0.0)), 0.0)
        kq = lax.dot_general(jnp.concatenate([kbs[0], qc], axis=0).astype(BF16), kcb, nt,
                             preferred_element_type=F32)
        kkt_b = lax.dot_general(kcb, kbs[1].astype(BF16), nt, preferred_element_type=F32)
        m_f = jnp.where(strict_lower, kq[:c] * decay_f, 0.0)
        m_bt = jnp.where(strict_lower, kkt_b * decay_bt, 0.0)
        rhs_f = jnp.concatenate([vbs[0], kbs[0] * egs[0]], axis=1).astype(BF16)
        rhs_b = jnp.concatenate([vbs[1], kbs[1] * egs[1]], axis=1).astype(BF16)
        systems.append((m_f, m_bt, rhs_f, rhs_b, sl, hs))
        attns = (kq[c:] * decay_f, kq[c:] * decay_b)
        for dr in range(2):
            gcol = gcols[dr]
            glast = gcol[c - 1:c] if dr == 0 else gcol[0:1]
            qg_ref[dr, 0, sl, hs] = (qc * egs[dr]).astype(BF16)
            kd_ref[dr, 0, sl, hs] = (kc * jnp.exp(glast - gcol)).astype(BF16)
            attn_ref[dr, 0, hh, sl, :] = attns[dr].astype(BF16)
            eg_ref[dr, 0, hh, 0, cc:cc + 1, :] = jnp.broadcast_to(jnp.exp(glast), (1, LANES))
    return systems


def _dn_prep(qkv, conv_w, ab, alog_row, dt_row, *, n_heads, dk, row_w, tt):
    gc, beta = _dn_gates(ab, alog_row, dt_row, n_heads=n_heads, tt=tt)
    b, n, _ = qkv.shape
    hh = n_heads
    ng = n // tt
    c = DN_CHUNK
    assert tt % row_w == 0 and tt % c == 0 and tt // c <= SUBLANES
    hp = 2 if hh % 2 == 0 else 1
    nhb = hh // hp
    kern = functools.partial(_dn_prep_kernel, row_w=row_w, n_heads=hh, dk=dk, hp=hp)
    xspec = lambda off: pl.BlockSpec((1, tt, hp * dk), lambda bi, h, g: (bi, g, off + h))
    cspec = lambda off: pl.BlockSpec((conv_w.shape[0], hp * dk), lambda bi, h, g: (0, off + h))
    ospec = pl.BlockSpec((2, 1, tt, hp * dk), lambda bi, h, g: (0, bi, g, h))
    return pl.pallas_call(
        kern,
        grid=(b, nhb, ng),
        in_specs=[xspec(0), xspec(nhb), xspec(2 * nhb), cspec(0), cspec(nhb), cspec(2 * nhb),
                  pl.BlockSpec((1, tt, LANES), lambda bi, h, g: (bi, g, 0)),
                  pl.BlockSpec((1, tt, LANES), lambda bi, h, g: (bi, g, 0))],
        out_specs=[ospec, ospec, ospec, ospec,
                   pl.BlockSpec((2, 1, hp, tt, c), lambda bi, h, g: (0, bi, h, g, 0)),
                   pl.BlockSpec((2, 1, hp, 1, SUBLANES, LANES), lambda bi, h, g: (0, bi, h, g, 0, 0))],
        out_shape=[jax.ShapeDtypeStruct((2, b, n, hh * dk), F32),
                   jax.ShapeDtypeStruct((2, b, n, hh * dk), BF16),
                   jax.ShapeDtypeStruct((2, b, n, hh * dk), BF16),
                   jax.ShapeDtypeStruct((2, b, n, hh * dk), BF16),
                   jax.ShapeDtypeStruct((2, b, hh, n, c), BF16),
                   jax.ShapeDtypeStruct((2, b, hh, ng, SUBLANES, LANES), F32)],
        compiler_params=_cparams(("parallel", "parallel", "arbitrary")),
    )(qkv, qkv, qkv, conv_w, conv_w, conv_w, gc, beta)


def _dn_scan_kernel(uf_ref, wf_ref, qf_ref, kf_ref, af_ref, ef_ref,
                    ub_ref, wb_ref, qb_ref, kb_ref, ab_ref, eb_ref, s0_ref,
                    of_ref, ob_ref, sout_ref, s_scr, *, hp, dk):
    g = pl.program_id(2)
    ng = pl.num_programs(2)
    c = DN_CHUNK
    tt = uf_ref.shape[2]
    ncs = tt // c

    @pl.when(g == 0)
    def _():
        s_scr[...] = s0_ref[:, 0]

    refs = ((uf_ref, wf_ref, qf_ref, kf_ref, af_ref, ef_ref, of_ref),
            (ub_ref, wb_ref, qb_ref, kb_ref, ab_ref, eb_ref, ob_ref))
    chains = [(dr, hh) for dr in range(2) for hh in range(hp)]
    state = {ch: s_scr[ch[0], ch[1]] for ch in chains}
    for step in range(ncs):
        rows = {}
        for dr, hh in chains:
            cc = step if dr == 0 else ncs - 1 - step
            rows[dr, hh] = (slice(cc * c, (cc + 1) * c), slice(hh * dk, (hh + 1) * dk), cc)
        r = {}
        for ch in chains:
            w_ref, q_ref = refs[ch[0]][1], refs[ch[0]][2]
            sl, hs, _ = rows[ch]
            wq = jnp.concatenate([w_ref[0, 0, sl, hs], q_ref[0, 0, sl, hs]], axis=0)
            r[ch] = jnp.dot(wq, state[ch].astype(BF16), preferred_element_type=F32)
        vnb = {}
        for ch in chains:
            sl, hs, _ = rows[ch]
            vnb[ch] = (refs[ch[0]][0][0, 0, sl, hs] - r[ch][:c]).astype(BF16)
        for ch in chains:
            _, _, _, k_ref, a_ref, e_ref, o_ref = refs[ch[0]]
            sl, hs, cc = rows[ch]
            o_ref[0, sl, hs] = r[ch][c:] + jnp.dot(a_ref[0, 0, ch[1], sl, :], vnb[ch], preferred_element_type=F32)
            upd = lax.dot_general(k_ref[0, 0, sl, hs], vnb[ch], (((0,), (0,)), ((), ())),
                                  preferred_element_type=F32)
            state[ch] = state[ch] * e_ref[0, 0, ch[1], 0, cc:cc + 1, :] + upd
    for ch in chains:
        s_scr[ch[0], ch[1]] = state[ch]

    @pl.when(g == ng - 1)
    def _():
        sout_ref[:, 0] = s_scr[...]


def _dn_scan(u, w, qg, kd, attn, eg, s0, *, dk, tt, hp):
    _, b, n, hd = u.shape
    hh = hd // dk
    ng = n // tt
    c = DN_CHUNK
    kern = functools.partial(_dn_scan_kernel, hp=hp, dk=dk)

    def specs(dr):
        gi = (lambda g: g) if dr == 0 else (lambda g: ng - 1 - g)
        tok = pl.BlockSpec((1, 1, tt, hp * dk), lambda bi, h, g: (dr, bi, gi(g), h))
        return [tok, tok, tok, tok,
                pl.BlockSpec((1, 1, hp, tt, c), lambda bi, h, g: (dr, bi, h, gi(g), 0)),
                pl.BlockSpec((1, 1, hp, 1, SUBLANES, LANES), lambda bi, h, g: (dr, bi, h, gi(g), 0, 0))]

    sspec = pl.BlockSpec((2, 1, hp, dk, dk), lambda bi, h, g: (0, bi, h, 0, 0))
    return pl.pallas_call(
        kern,
        grid=(b, hh // hp, ng),
        in_specs=specs(0) + specs(1) + [sspec],
        out_specs=[pl.BlockSpec((1, tt, hp * dk), lambda bi, h, g: (bi, g, h)),
                   pl.BlockSpec((1, tt, hp * dk), lambda bi, h, g: (bi, ng - 1 - g, h)),
                   sspec],
        out_shape=[jax.ShapeDtypeStruct((b, n, hd), F32),
                   jax.ShapeDtypeStruct((b, n, hd), F32),
                   jax.ShapeDtypeStruct((2, b, hh, dk, dk), F32)],
        scratch_shapes=[pltpu.VMEM((2, hp, dk, dk), F32)],
        compiler_params=_cparams(("parallel", "parallel", "arbitrary")),
    )(u, w, qg, kd, attn, eg, u, w, qg, kd, attn, eg, s0)


def _hy_hidden_kernel(z_ref, w1_ref, b1_ref, w2_ref, b2_ref, fr_ref, o_ref):
    fr = fr_ref[...]
    h1 = jnp.sin(fr * (jnp.dot(z_ref[...].astype(BF16), w1_ref[...].astype(BF16),
                               preferred_element_type=F32) + b1_ref[...]))
    o_ref[...] = jnp.sin(fr * (jnp.dot(h1.astype(BF16), w2_ref[...].astype(BF16),
                                       preferred_element_type=F32) + b2_ref[...]))


def _hy_filter_kernel(hid_ref, w3_ref, b3_ref, t_ref, delta_ref, o_ref):
    f = jnp.dot(hid_ref[...].astype(BF16), w3_ref[...].astype(BF16), preferred_element_type=F32) + b3_ref[...]
    o_ref[...] = f * jnp.exp(-t_ref[...] * delta_ref[...])


def _hy_filters(n, w1, b1, w2, b2, w3, b3, freq, hy_w):
    emb, hid = w1.shape
    bands = (emb - 1) // 2
    t = jnp.linspace(0.0, 1.0, n, dtype=F32)
    pos = jnp.arange(n, dtype=F32)
    bnd = jnp.linspace(1e-4, bands - 1, bands, dtype=F32)
    ang = (2.0 * math.pi / n) * pos[:, None] * bnd[None, :]
    z = jnp.concatenate([t[:, None], jnp.cos(ang), -jnp.sin(ang)], axis=-1)
    embp = -(-emb // SUBLANES) * SUBLANES
    z = jnp.pad(z, ((0, 0), (0, embp - emb)))
    w1p = jnp.pad(w1, ((0, embp - emb), (0, 0)))
    full = lambda a: pl.BlockSpec(a.shape, lambda: (0,) * a.ndim)
    args = (z, w1p, b1[None, :], w2, b2[None, :], freq[None, :])
    hid2 = pl.pallas_call(
        _hy_hidden_kernel,
        in_specs=[full(a) for a in args],
        out_specs=pl.BlockSpec((n, hid), lambda: (0, 0)),
        out_shape=jax.ShapeDtypeStruct((n, hid), F32),
        compiler_params=pltpu.CompilerParams(vmem_limit_bytes=VMEM_LIMIT_BYTES),
    )(*args)
    nout = w3.shape[1]
    deltas = jnp.abs(jnp.linspace(math.log(HY_DECAY_TARGET) / HY_SLOW_DECAY,
                                  math.log(HY_DECAY_TARGET) / HY_FAST_DECAY, hy_w, dtype=F32))
    delta_row = jnp.tile(deltas, nout // hy_w)[None, :]
    tc = min(nout, 1024)
    return pl.pallas_call(
        _hy_filter_kernel,
        grid=(nout // tc,),
        in_specs=[pl.BlockSpec((n, hid), lambda j: (0, 0)),
                  pl.BlockSpec((hid, tc), lambda j: (0, j)),
                  pl.BlockSpec((1, tc), lambda j: (0, j)),
                  pl.BlockSpec((n, 1), lambda j: (0, 0)),
                  pl.BlockSpec((1, tc), lambda j: (0, j))],
        out_specs=pl.BlockSpec((n, tc), lambda j: (0, j)),
        out_shape=jax.ShapeDtypeStruct((n, nout), F32),
        compiler_params=_cparams(("parallel",)),
    )(hid2, w3, b3[None, :], t[:, None], delta_row)


def _dft_tables(n):
    n2 = FFT_N2
    big = 2 * n
    n1 = big // n2
    k1 = n1 // 2
    two_pi = 2.0 * math.pi

    def cs(idx, period):
        a = (idx % period).astype(F32) * (two_pi / period)
        return jnp.cos(a), jnp.sin(a)

    f1 = jnp.arange(n1, dtype=jnp.int32)
    s1 = jnp.arange(k1, dtype=jnp.int32)
    s2 = jnp.arange(n2, dtype=jnp.int32)
    s_full = s1[None, None, :] * n2 + s2[:, None, None]
    c, s = cs(f1[None, :, None] * s_full, big)
    f1tw = jnp.concatenate([jnp.concatenate([c, s], axis=2), jnp.concatenate([-s, c], axis=2)], axis=1)
    ct, st = jnp.swapaxes(c, 1, 2), jnp.swapaxes(s, 1, 2)
    g1tw = jnp.concatenate([jnp.concatenate([ct, -st], axis=2), jnp.concatenate([st, ct], axis=2)], axis=1) / big
    c2, sn2 = cs(s2[:, None] * s2[None, :], n2)
    f2 = jnp.concatenate([jnp.concatenate([c2, sn2], axis=1), jnp.concatenate([-sn2, c2], axis=1)], axis=0)
    g2 = jnp.concatenate([jnp.concatenate([c2, -sn2], axis=1), jnp.concatenate([sn2, c2], axis=1)], axis=0)
    return f1tw.astype(BF16), f2.astype(BF16), g2.astype(BF16), g1tw.astype(BF16)


def _hy_conv_kernel(a_ref, m_ref, hf_ref, hb_ref, skip_ref, cwa_ref, cwm_ref,
                    f1_ref, f2_ref, g2_ref, g1_ref, o_ref,
                    h_scr, b_scr, zr_scr, zi_scr, *, conv_a, n1, k1):
    n2 = FFT_N2
    k1p = k1 + FFT_ROW_PAD
    pb = 2 * n2 + FFT_ROW_PAD
    pos = lax.broadcasted_iota(jnp.int32, (n2, 1), 0)

    grp = FFT_UNROLL
    cw = b_scr.shape[1]

    def stage1(real_only):
        def body(g, carry):
            s2s = [g * grp + t for t in range(grp)]
            bases = [pl.multiple_of(s2 * k1p, SUBLANES) for s2 in s2s]
            if real_only:
                xs = [zr_scr[pl.ds(base, k1), :].astype(BF16) for base in bases]
                outs = [jnp.dot(f1_ref[s2, :, :k1], x, preferred_element_type=F32) for s2, x in zip(s2s, xs)]
            else:
                xs = [jnp.concatenate([zr_scr[pl.ds(base, k1), :], zi_scr[pl.ds(base, k1), :]], axis=0).astype(BF16)
                      for base in bases]
                outs = [jnp.dot(f1_ref[s2], x, preferred_element_type=F32) for s2, x in zip(s2s, xs)]
            for s2, a in zip(s2s, outs):
                b_scr[pl.ds(s2, n1, stride=pb), :] = a[:n1]
                b_scr[pl.ds(n2 + s2, n1, stride=pb), :] = a[n1:]
            return carry
        lax.fori_loop(0, n2 // grp, body, 0)

    def slab_pairs(g):
        f1s = [[(g * grp + t) * 2 + u for u in range(2)] for t in range(grp)]
        bbs = [[pl.multiple_of(f1 * pb, SUBLANES) for f1 in pair] for pair in f1s]
        hbs = [[pl.multiple_of(f1 * 2 * n2, 2 * n2) for f1 in pair] for pair in f1s]
        return bbs, hbs

    def forward2(bbs):
        bpairs = [jnp.concatenate([b_scr[pl.ds(bb, 2 * n2), :] for bb in pair], axis=1).astype(BF16) for pair in bbs]
        return [jnp.dot(f2_ref[...], bp, preferred_element_type=F32) for bp in bpairs]

    @pl.when(pl.program_id(1) == 0)
    def _():
        for which, ref in ((0, hf_ref), (1, hb_ref)):
            def load(s1, carry, ref=ref):
                r0 = pl.multiple_of(s1 * n2, n2)
                zr_scr[pl.ds(s1, n2, stride=k1p), :] = ref[pl.ds(r0, n2), :]
                return carry
            lax.fori_loop(0, k1, load, 0, unroll=FFT_UNROLL)
            if which == 1:
                zr_scr[0:1, :] = jnp.zeros((1, zr_scr.shape[1]), F32)
            stage1(True)

            def spec(g, carry, which=which):
                bbs, hbs = slab_pairs(g)
                zpairs = forward2(bbs)
                for zpair, hpair in zip(zpairs, hbs):
                    for t, hb_ in enumerate(hpair):
                        zz = zpair[:, t * cw:(t + 1) * cw]
                        if which == 0:
                            h_scr[pl.ds(hb_, 2 * n2), :] = zz
                        else:
                            h_scr[pl.ds(hb_, n2), :] = h_scr[pl.ds(hb_, n2), :] + zz[:n2]
                            h_scr[pl.ds(hb_ + n2, n2), :] = h_scr[pl.ds(hb_ + n2, n2), :] - zz[n2:]
                return carry
            lax.fori_loop(0, n1 // (2 * grp), spec, 0)

    def load_a(s1, carry):
        r0 = pl.multiple_of(s1 * n2, n2)
        for bi, scr in ((0, zr_scr), (1, zi_scr)):
            blk = a_ref[bi, pl.ds(r0, n2), :]
            if conv_a:
                blk = _row_conv(blk, cwa_ref[...], pos, n2)
            scr[pl.ds(s1, n2, stride=k1p), :] = blk
        return carry
    lax.fori_loop(0, k1, load_a, 0, unroll=FFT_UNROLL)

    stage1(False)

    def mid(g, carry):
        bbs, hbs = slab_pairs(g)
        zzs = forward2(bbs)
        yys = []
        for zz, hpair in zip(zzs, hbs):
            zr, zi = zz[:n2], zz[n2:]
            hr = jnp.concatenate([h_scr[pl.ds(hb_, n2), :] for hb_ in hpair], axis=1)
            hi = jnp.concatenate([h_scr[pl.ds(hb_ + n2, n2), :] for hb_ in hpair], axis=1)
            yys.append(jnp.concatenate([zr * hr - zi * hi, zr * hi + zi * hr], axis=0).astype(BF16))
        wws = [jnp.dot(g2_ref[...], yy, preferred_element_type=F32) for yy in yys]
        for ww, pair in zip(wws, bbs):
            for t, bb in enumerate(pair):
                b_scr[pl.ds(bb, 2 * n2), :] = ww[:, t * cw:(t + 1) * cw]
        return carry
    lax.fori_loop(0, n1 // (2 * grp), mid, 0)

    skip = skip_ref[...]

    def last(g, carry):
        s2s = [g * grp + t for t in range(grp)]
        wws = [jnp.concatenate([b_scr[pl.ds(s2, n1, stride=pb), :],
                                b_scr[pl.ds(n2 + s2, n1, stride=pb), :]], axis=0).astype(BF16) for s2 in s2s]
        ys = [jnp.dot(g1_ref[s2], ww, preferred_element_type=F32) for s2, ww in zip(s2s, wws)]
        for s2, y in zip(s2s, ys):
            base = pl.multiple_of(s2 * k1p, SUBLANES)
            zr_scr[pl.ds(base, k1), :] = y[:k1] + zr_scr[pl.ds(base, k1), :] * skip
            zi_scr[pl.ds(base, k1), :] = y[k1:] + zi_scr[pl.ds(base, k1), :] * skip
        return carry
    lax.fori_loop(0, n2 // grp, last, 0)

    def fin(s1, carry):
        r0 = pl.multiple_of(s1 * n2, n2)
        for bi, scr in ((0, zr_scr), (1, zi_scr)):
            mm = _row_conv(m_ref[bi, pl.ds(r0, n2), :], cwm_ref[...], pos, n2)
            o_ref[bi, pl.ds(r0, n2), :] = mm * scr[pl.ds(s1, n2, stride=k1p), :]
        return carry
    lax.fori_loop(0, k1, fin, 0, unroll=FFT_UNROLL)


def _hy_conv(a, a_col0, m, m_col0, filt, f_col_fwd, f_col_bwd, skip, cwa, cwm, tables, *, conv_a, hy_w):
    b, n, _ = a.shape
    assert b % 2 == 0 and GRID_W == FFT_N2 and n % FFT_N2 == 0
    n2 = FFT_N2
    n1 = 2 * n // n2
    k1 = n1 // 2
    k1p = k1 + FFT_ROW_PAD
    pb = 2 * n2 + FFT_ROW_PAD
    cw = LANES
    f1tw, f2, g2, g1tw = tables
    kern = functools.partial(_hy_conv_kernel, conv_a=conv_a, n1=n1, k1=k1)
    once = pl.Buffered(1)
    return pl.pallas_call(
        kern,
        grid=(hy_w // cw, b // 2),
        in_specs=[pl.BlockSpec((2, n, cw), lambda c, p: (p, 0, a_col0 + c), pipeline_mode=once),
                  pl.BlockSpec((2, n, cw), lambda c, p: (p, 0, m_col0 + c), pipeline_mode=once),
                  pl.BlockSpec((n, cw), lambda c, p: (0, f_col_fwd + c), pipeline_mode=once),
                  pl.BlockSpec((n, cw), lambda c, p: (0, f_col_bwd + c), pipeline_mode=once),
                  pl.BlockSpec((1, cw), lambda c, p: (0, c)),
                  pl.BlockSpec((cwa.shape[0], cw), lambda c, p: (0, a_col0 + c if conv_a else 0)),
                  pl.BlockSpec((cwm.shape[0], cw), lambda c, p: (0, m_col0 + c)),
                  pl.BlockSpec(f1tw.shape, lambda c, p: (0, 0, 0), pipeline_mode=once),
                  pl.BlockSpec(f2.shape, lambda c, p: (0, 0)),
                  pl.BlockSpec(g2.shape, lambda c, p: (0, 0)),
                  pl.BlockSpec(g1tw.shape, lambda c, p: (0, 0, 0), pipeline_mode=once)],
        out_specs=pl.BlockSpec((2, n, cw), lambda c, p: (p, 0, c)),
        out_shape=jax.ShapeDtypeStruct((b, n, hy_w), F32),
        scratch_shapes=[pltpu.VMEM((n1 * 2 * n2, cw), F32),
                        pltpu.VMEM((n1 * pb, cw), F32),
                        pltpu.VMEM((n2 * k1p, cw), F32),
                        pltpu.VMEM((n2 * k1p, cw), F32)],
        compiler_params=_cparams(("parallel", "arbitrary")),
    )(a, m, filt, filt, skip, cwa, cwm, f1tw, f2, g2, g1tw)


def _merge_kernel(of_ref, ob_ref, z_ref, y_ref, ga_ref, gb_ref, nw_ref, wa_ref, wh_ref, o_ref, *, dv):
    nw = nw_ref[...]
    heads = []
    for hh in range(of_ref.shape[1] // dv):
        hs = slice(hh * dv, (hh + 1) * dv)
        o = of_ref[:, hs] + ob_ref[:, hs]
        o = o * lax.rsqrt(jnp.mean(o * o, axis=-1, keepdims=True) + RMS_EPS) * nw
        heads.append((o * _silu(z_ref[:, hs])).astype(BF16))
    pa = jnp.dot(jnp.concatenate(heads, axis=1), wa_ref[...], preferred_element_type=F32)
    ph = jnp.dot(y_ref[...].astype(BF16), wh_ref[...], preferred_element_type=F32)
    o_ref[...] = (jax.nn.sigmoid(ga_ref[...]) * pa + jax.nn.sigmoid(gb_ref[...]) * ph).astype(o_ref.dtype)


def _merge(o_f, o_b, z, z_blk, y_hy, gate, norm_w, w_pa, w_ph, *, dv, tm):
    m, dvw = o_f.shape
    d = w_pa.shape[1]
    tm = min(tm, m)
    kern = functools.partial(_merge_kernel, dv=dv)
    row = lambda w_, blk=0: pl.BlockSpec((tm, w_), lambda i: (i, blk))
    const = lambda a: pl.BlockSpec(a.shape, lambda i: (0, 0), pipeline_mode=pl.Buffered(1))
    return pl.pallas_call(
        kern,
        grid=(m // tm,),
        in_specs=[row(dvw), row(dvw), row(dvw, z_blk), row(y_hy.shape[1]), row(d, 0), row(d, 1),
                  pl.BlockSpec((1, dv), lambda i: (0, 0)), const(w_pa), const(w_ph)],
        out_specs=row(d),
        out_shape=jax.ShapeDtypeStruct((m, d), BF16),
        compiler_params=_cparams(("parallel",)),
    )(o_f, o_b, z, y_hy, gate, gate, norm_w, w_pa, w_ph)


def _layer_norm(x, g, b):
    mu = jnp.mean(x, axis=-1, keepdims=True)
    xc = x - mu
    var = jnp.mean(xc * xc, axis=-1, keepdims=True)
    return xc * lax.rsqrt(var + LN_EPS) * g + b


def _outproj_kernel(mg_ref, x_ref, wo_ref, g1_ref, lg_ref, lb_ref, sc_ref, sh_ref, wq_ref, sk_ref,
                    x1_ref, h2_ref, s_ref, *, half):
    mix = jnp.dot(mg_ref[...], wo_ref[...], preferred_element_type=F32)
    x1 = _layer_norm(DEEPNORM_ALPHA * x_ref[...] + g1_ref[0] * mix, lg_ref[...], lb_ref[...])
    x1_ref[...] = x1
    h2 = (x1 * (1.0 + sc_ref[0]) + sh_ref[0]).astype(BF16)
    h2_ref[...] = h2
    qp = jnp.dot(h2, wq_ref[...], preferred_element_type=F32).astype(BF16)
    for j in range(sk_ref.shape[0]):
        js = slice(j * half, (j + 1) * half)
        s_ref[j] = lax.dot_general(sk_ref[j], qp[:, js], (((1,), (1,)), ((), ())), preferred_element_type=F32)


def _outproj(merged, x2d, w_out, g1, ln_g, ln_b, sc2, sh2, wq, subkeys, *, tm):
    m, d = x2d.shape
    r = g1.shape[0]
    tm = min(tm, m // r)
    tpm = (m // r) // tm
    nsk, nkeys, half = subkeys.shape
    assert nkeys == LANES
    kern = functools.partial(_outproj_kernel, half=half)
    row = lambda w_: pl.BlockSpec((tm, w_), lambda i: (i, 0))
    mod = pl.BlockSpec((1, 1, d), lambda i: (i // tpm, 0, 0))
    vec = pl.BlockSpec((1, d), lambda i: (0, 0))
    return pl.pallas_call(
        kern,
        grid=(m // tm,),
        in_specs=[row(d), row(d), pl.BlockSpec(w_out.shape, lambda i: (0, 0), pipeline_mode=pl.Buffered(1)),
                  mod, vec, vec, mod, mod,
                  pl.BlockSpec(wq.shape, lambda i: (0, 0), pipeline_mode=pl.Buffered(1)),
                  pl.BlockSpec(subkeys.shape, lambda i: (0, 0, 0))],
        out_specs=[row(d), row(d), pl.BlockSpec((nsk, nkeys, tm), lambda i: (0, 0, i))],
        out_shape=[jax.ShapeDtypeStruct((m, d), F32),
                   jax.ShapeDtypeStruct((m, d), BF16),
                   jax.ShapeDtypeStruct((nsk, nkeys, m), F32)],
        compiler_params=_cparams(("parallel",)),
    )(merged, x2d, w_out, g1, ln_g, ln_b, sc2, sh2, wq, subkeys)


def _cand_tables():
    pairs = [(a, b) for a in range(PEER_TOPK) for b in range(PEER_TOPK) if (a + 1) * (b + 1) <= PEER_TOPK]
    ia = jnp.array([p[0] for p in pairs], dtype=jnp.int32)
    ib = jnp.array([p[1] for p in pairs], dtype=jnp.int32)
    ncand = len(pairs)
    rows = -(-ncand // (2 * SUBLANES)) * (2 * SUBLANES)
    ranks = jnp.arange(LANES, dtype=jnp.int32)
    p1 = (jnp.pad(ia, (0, rows - ncand), constant_values=-1)[:, None] == ranks[None, :]).astype(F32)
    p2 = (jnp.pad(ib, (0, rows - ncand), constant_values=-1)[:, None] == ranks[None, :]).astype(F32)
    return p1, p2, ncand


def _peer_topk_kernel(s_ref, p1_ref, p2_ref, p1t_ref, c1_ref, r2_ref, e1_ref, e2_ref, *, n_heads, ncand):
    tm = s_ref.shape[2]
    neg = -jnp.inf
    hi = lax.Precision.HIGHEST
    no_rank = float(LANES)

    def pop_max(x, row_id):
        mx = jnp.max(x, axis=0, keepdims=True)
        first = jnp.min(jnp.where(x == mx, row_id, no_rank), axis=0, keepdims=True)
        hit = row_id == first
        return mx, hit, jnp.where(hit, neg, x)

    key_id = lax.broadcasted_iota(jnp.int32, (LANES, tm), 0).astype(F32)
    cand_id = lax.broadcasted_iota(jnp.int32, (p1_ref.shape[0], tm), 0).astype(F32)
    pad_rows = jnp.zeros((LANES - PEER_TOPK, tm), F32)

    def sorted_top(s):
        tops = []
        rank = jnp.full(s.shape, no_rank, F32)
        x = s
        for a in range(PEER_TOPK):
            mx, hit, x = pop_max(x, key_id)
            tops.append(mx)
            rank = jnp.where(hit, float(a), rank)
        return jnp.concatenate(tops + [pad_rows], axis=0), rank

    for h in range(n_heads):
        s1 = s_ref[2 * h]
        s2 = s_ref[2 * h + 1]
        t1, rank1 = sorted_top(s1)
        t2, rank2 = sorted_top(s2)
        cand = (jnp.dot(p1_ref[...], t1, precision=hi, preferred_element_type=F32)
                + jnp.dot(p2_ref[...], t2, precision=hi, preferred_element_type=F32))
        x = jnp.where(cand_id < ncand, cand, neg)
        sel = jnp.zeros(x.shape, F32)
        zsum = None
        cmax = None
        for r in range(PEER_TOPK):
            mx, hit, x = pop_max(x, cand_id)
            sel = jnp.where(hit, 1.0, sel)
            if r == 0:
                cmax = mx
                zsum = jnp.ones_like(mx)
            else:
                zsum = zsum + jnp.exp(mx - cmax)
        cnt = jnp.dot(p1t_ref[...].astype(BF16), sel.astype(BF16), preferred_element_type=F32)
        count1 = jnp.zeros(s1.shape, F32)
        for a in range(PEER_TOPK):
            count1 = jnp.where(rank1 == a, cnt[a:a + 1], count1)
        c1_ref[h] = count1
        r2_ref[h] = rank2.astype(r2_ref.dtype)
        e1_ref[h] = jnp.exp(s1 - t1[0:1]) / zsum
        e2_ref[h] = jnp.exp(s2 - t2[0:1]).astype(e2_ref.dtype)


def _peer_topk(scores, n_heads, *, tm):
    nsk, nkeys, m = scores.shape
    tm = min(tm, m)
    p1, p2, ncand = _cand_tables()
    kern = functools.partial(_peer_topk_kernel, n_heads=n_heads, ncand=ncand)
    full = lambda a: pl.BlockSpec(a.shape, lambda i: (0, 0))
    out = pl.BlockSpec((n_heads, nkeys, tm), lambda i: (0, 0, i))
    shp = lambda dt: jax.ShapeDtypeStruct((n_heads, nkeys, m), dt)
    p1t = p1.T
    return pl.pallas_call(
        kern,
        grid=(m // tm,),
        in_specs=[pl.BlockSpec((nsk, nkeys, tm), lambda i: (0, 0, i)), full(p1), full(p2), full(p1t)],
        out_specs=[out, out, out, out],
        out_shape=[shp(F32), shp(F32), shp(F32), shp(F32)],
        compiler_params=_cparams(("parallel",)),
    )(scores, p1, p2, p1t)


def _peer_dense_kernel(h2t_ref, u_ref, v_ref, r2_ref, e2_ref, c1_ref, e1_ref, x1_ref, g2_ref, lg_ref, lb_ref,
                       o_ref, acc_scr, a_scr, p_scr, r2_scr, e2_scr, *, n_heads, ti):
    e = pl.program_id(1)

    @pl.when(e == 0)
    def _():
        acc_scr[...] = jnp.zeros(acc_scr.shape, F32)
        r2_scr[...] = r2_ref[...].astype(BF16)
        e2_scr[...] = e2_ref[...].astype(BF16)

    a_scr[...] = jnp.dot(u_ref[...], h2t_ref[...], preferred_element_type=F32)
    tm = h2t_ref.shape[1]
    nkeys = r2_ref.shape[1]
    for ii in range(ti):
        rs = slice(ii * nkeys, (ii + 1) * nkeys)
        for cb in range(tm // LANES):
            cs = slice(cb * LANES, (cb + 1) * LANES)
            gate = None
            for h in range(n_heads):
                keep = r2_scr[h, :, cs] < c1_ref[h, ii:ii + 1, cs].astype(BF16)
                term = jnp.where(keep, e2_scr[h, :, cs], jnp.zeros((), BF16)) * e1_ref[h, ii:ii + 1, cs].astype(BF16)
                gate = term if gate is None else gate + term
            a = a_scr[rs, cs]
            act = 0.5 * a * (1.0 + lax.erf(a * (2.0 ** -0.5)))
            p_scr[rs, cs] = gate * act.astype(BF16)
    acc_scr[...] += lax.dot_general(p_scr[...], v_ref[...], (((0,), (0,)), ((), ())), preferred_element_type=F32)

    @pl.when(e == pl.num_programs(1) - 1)
    def _():
        o_ref[...] = _layer_norm(DEEPNORM_ALPHA * x1_ref[...] + g2_ref[0] * acc_scr[...],
                                 lg_ref[...], lb_ref[...])


def _peer_dense(h2t, u_tab, v_tab, rank2, e2, count1, e1, x1, g2, ln_g, ln_b, *, n_heads, ti, tm):
    d, m = h2t.shape
    r = g2.shape[0]
    tm = min(tm, m // r)
    tpm = (m // r) // tm
    ne = u_tab.shape[0]
    nkeys = rank2.shape[1]
    te = ti * nkeys
    kern = functools.partial(_peer_dense_kernel, n_heads=n_heads, ti=ti)
    row = lambda w_: pl.BlockSpec((tm, w_), lambda i, e: (i, 0), pipeline_mode=pl.Buffered(1))
    vec = pl.BlockSpec((1, d), lambda i, e: (0, 0))
    tbl = pl.BlockSpec((te, d), lambda i, e: (e, 0))
    second = pl.BlockSpec((n_heads, nkeys, tm), lambda i, e: (0, 0, i), pipeline_mode=pl.Buffered(1))
    first = pl.BlockSpec((n_heads, ti, tm), lambda i, e: (0, e, i))
    return pl.pallas_call(
        kern,
        grid=(m // tm, ne // te),
        in_specs=[pl.BlockSpec((d, tm), lambda i, e: (0, i), pipeline_mode=pl.Buffered(1)),
                  tbl, tbl, second, second, first, first, row(d),
                  pl.BlockSpec((1, 1, d), lambda i, e: (i // tpm, 0, 0)), vec, vec],
        out_specs=pl.BlockSpec((tm, d), lambda i, e: (i, 0)),
        out_shape=jax.ShapeDtypeStruct((m, d), F32),
        scratch_shapes=[pltpu.VMEM((tm, d), F32), pltpu.VMEM((te, tm), F32), pltpu.VMEM((te, tm), BF16),
                        pltpu.VMEM((n_heads, nkeys, tm), BF16), pltpu.VMEM((n_heads, nkeys, tm), BF16)],
        compiler_params=_cparams(("parallel", "arbitrary")),
    )(h2t, u_tab, v_tab, rank2, e2, count1, e1, x1, g2, ln_g, ln_b)


def _block(x, c, ctx, c_ctx, w_ada, b_ada, w_in, dn_conv_w, dn_a_log, dn_dt_bias, dn_norm_w, hy_conv_w, hy_w1, hy_b1, hy_w2, hy_b2, hy_w3, hy_b3, hy_freq, hy_skip, w_branch_dn, w_branch_hy, w_out, ln1_g, ln1_b, peer_wq, peer_subkeys, peer_u, peer_v, ln2_g, ln2_b):
    b, n, d = x.shape
    n_ctx = ctx.shape[1]
    assert w_ada.shape[0] == DEPTH == 1, "single-layer block: the context stream update is never read"
    hh = dn_a_log.shape[2]
    dk = dn_conv_w.shape[2] // (3 * hh)
    qk_w = hh * dk
    hy_w = hy_skip.shape[2]
    p_heads, _, nkeys, half = peer_subkeys.shape[1:]
    assert dk == LANES and nkeys == LANES and 4 * hh <= LANES and n % GRID_W == 0
    off_z = 3 * qk_w
    off_ab = 4 * qk_w
    off_hy = off_ab + 4 * hh
    off_gate = off_hy + 3 * hy_w
    assert w_in.shape[2] == off_gate + 2 * d

    cc = jnp.zeros((SUBLANES, d), F32).at[:b].set(c).at[b].set(c_ctx)
    mods = _ada(cc, w_ada[0], b_ada[0][None, :])
    sh1, sc1, g1, sh2, sc2, g2 = [mods[:b, i * d:(i + 1) * d][:, None, :] for i in range(6)]
    sh1c, sc1c = [mods[b:b + 1, i * d:(i + 1) * d][:, None, :] for i in range(2)]

    w = w_in[0]
    w_qkvz = w[:, :off_ab].astype(BF16)
    w_ab = jnp.pad(w[:, off_ab:off_hy], ((0, 0), (0, LANES - 4 * hh))).astype(BF16)
    w_hy = w[:, off_hy:off_gate].astype(BF16)
    w_gate = w[:, off_gate:].astype(BF16)

    x2d = x.reshape(b * n, d)
    c2d = ctx.reshape(b * n_ctx, d)
    proj_qkvz = _modmm(x2d, sc1, sh1, w_qkvz, tm=512, tn=2048)
    proj_ab = _modmm(x2d, sc1, sh1, w_ab, tm=512, tn=LANES)
    proj_hy = _modmm(x2d, sc1, sh1, w_hy, tm=512, tn=2048)
    proj_gate = _modmm(x2d, sc1, sh1, w_gate, tm=512, tn=2048)
    projc_qkvz = _modmm(c2d, sc1c, sh1c, w_qkvz, tm=256, tn=2048)
    projc_ab = _modmm(c2d, sc1c, sh1c, w_ab, tm=256, tn=LANES)

    def decay_row(a):
        row = jnp.concatenate([a, jnp.zeros_like(a)], axis=1).reshape(-1)
        return jnp.pad(row, (0, LANES - 4 * hh))[None, :]

    alog_row = decay_row(dn_a_log[0])
    dt_row = decay_row(dn_dt_bias[0])

    conv_w = dn_conv_w[0]
    prep_c = _dn_prep(projc_qkvz.reshape(b, n_ctx, -1), conv_w, projc_ab.reshape(b, n_ctx, LANES),
                      alog_row, dt_row, n_heads=hh, dk=dk, row_w=n_ctx, tt=n_ctx)
    prep_l = _dn_prep(proj_qkvz.reshape(b, n, -1), conv_w, proj_ab.reshape(b, n, LANES),
                      alog_row, dt_row, n_heads=hh, dk=dk, row_w=GRID_W, tt=4 * GRID_W)
    hp = 4 if hh % 4 == 0 else (2 if hh % 2 == 0 else 1)
    s_zero = jnp.zeros((2, b, hh, dk, dk), F32)
    _, _, s_ctx = _dn_scan(*prep_c, s_zero, dk=dk, tt=n_ctx, hp=hp)
    o_f, o_b, _ = _dn_scan(*prep_l, s_ctx, dk=dk, tt=4 * GRID_W, hp=hp)

    filt = _hy_filters(n, hy_w1[0], hy_b1[0], hy_w2[0], hy_b2[0], hy_w3[0], hy_b3[0], hy_freq[0], hy_w)
    tables = _dft_tables(n)
    hy3 = proj_hy.reshape(b, n, 3 * hy_w)
    cwh = hy_conv_w[0]
    cb = hy_w // LANES
    y1 = _hy_conv(hy3, 0, hy3, cb, filt, 0, 2 * cb, hy_skip[0, 0:1], cwh, cwh, tables, conv_a=True, hy_w=hy_w)
    y_hy = _hy_conv(y1, 0, hy3, 2 * cb, filt, cb, 3 * cb, hy_skip[0, 1:2], cwh, cwh, tables, conv_a=False, hy_w=hy_w)

    merged = _merge(o_f.reshape(b * n, qk_w), o_b.reshape(b * n, qk_w), proj_qkvz, off_z // qk_w,
                    y_hy.reshape(b * n, hy_w), proj_gate, dn_norm_w[0][None, :],
                    w_branch_dn[0].astype(BF16), w_branch_hy[0].astype(BF16), dv=dk, tm=256)
    sk = peer_subkeys[0].reshape(p_heads * 2, nkeys, half).astype(BF16)
    x1, h2, scores = _outproj(merged, x2d, w_out[0].astype(BF16), g1, ln1_g[0][None, :], ln1_b[0][None, :],
                              sc2, sh2, peer_wq[0].astype(BF16), sk, tm=256)

    count1, rank2, e1, e2 = _peer_topk(scores, p_heads, tm=256)
    out = _peer_dense(h2.T, peer_u[0].astype(BF16), peer_v[0].astype(BF16), rank2, e2, count1, e1,
                      x1, g2, ln2_g[0][None, :], ln2_b[0][None, :], n_heads=p_heads, ti=SUBLANES, tm=512)
    stages = dict(o_f=o_f, o_b=o_b, s_ctx=s_ctx, filt=filt, y1=y1, y_hy=y_hy, merged=merged, x1=x1, scores=scores,
                  count1=count1, rank2=rank2, e1=e1, e2=e2)
    return out.reshape(b, n, d), stages


def kernel(x, c, ctx, c_ctx, w_ada, b_ada, w_in, dn_conv_w, dn_a_log, dn_dt_bias, dn_norm_w, hy_conv_w, hy_w1, hy_b1, hy_w2, hy_b2, hy_w3, hy_b3, hy_freq, hy_skip, w_branch_dn, w_branch_hy, w_out, ln1_g, ln1_b, peer_wq, peer_subkeys, peer_u, peer_v, ln2_g, ln2_b):
    out, _ = _block(x, c, ctx, c_ctx, w_ada, b_ada, w_in, dn_conv_w, dn_a_log, dn_dt_bias, dn_norm_w, hy_conv_w, hy_w1, hy_b1, hy_w2, hy_b2, hy_w3, hy_b3, hy_freq, hy_skip, w_branch_dn, w_branch_hy, w_out, ln1_g, ln1_b, peer_wq, peer_subkeys, peer_u, peer_v, ln2_g, ln2_b)
    return out
```

```python
import functools
import math

import jax
import jax.numpy as jnp
from jax import lax
from jax.experimental import pallas as pl
from jax.experimental.pallas import tpu as pltpu

F32 = jnp.float32
BF16 = jnp.bfloat16

GRID_W = 64
DN_CHUNK = 64
PEER_TOPK = 16
PEER_BAND = 32
DEPTH = 1
DEEPNORM_ALPHA = (2 * DEPTH) ** 0.25
LN_EPS = 1e-5
RMS_EPS = 1e-6
L2_EPS = 1e-6
HY_DECAY_TARGET = 1e-2
HY_FAST_DECAY = 0.3
HY_SLOW_DECAY = 1.5

LANES = 128
SUBLANES = 8
VMEM_LIMIT_BYTES = 56 * 1024 * 1024

FFT_N2 = 64
FFT_ROW_PAD = 8
FFT_UNROLL = 4


def _cparams(sem):
    return pltpu.CompilerParams(dimension_semantics=sem, vmem_limit_bytes=VMEM_LIMIT_BYTES)


def _silu(x):
    return x * jax.nn.sigmoid(x)


def _ada_kernel(c_ref, w_ref, b_ref, o_ref):
    s = _silu(c_ref[...])
    o_ref[...] = jnp.dot(s.astype(BF16), w_ref[...].astype(BF16), preferred_element_type=F32) + b_ref[...]


def _ada(cc, w, b):
    d, n6 = w.shape
    tn = min(n6, 1536)
    return pl.pallas_call(
        _ada_kernel,
        grid=(n6 // tn,),
        in_specs=[pl.BlockSpec((SUBLANES, d), lambda j: (0, 0)),
                  pl.BlockSpec((d, tn), lambda j: (0, j)),
                  pl.BlockSpec((1, tn), lambda j: (0, j))],
        out_specs=pl.BlockSpec((SUBLANES, tn), lambda j: (0, j)),
        out_shape=jax.ShapeDtypeStruct((SUBLANES, n6), F32),
        compiler_params=_cparams(("arbitrary",)),
    )(cc, w, b)


def _modmm_kernel(x_ref, sc_ref, sh_ref, w_ref, o_ref, h_scr):
    @pl.when(pl.program_id(1) == 0)
    def _():
        h_scr[...] = (x_ref[...] * (1.0 + sc_ref[0]) + sh_ref[0]).astype(BF16)

    o_ref[...] = jnp.dot(h_scr[...], w_ref[...], preferred_element_type=F32).astype(o_ref.dtype)


def _modmm(x2d, sc, sh, w, *, tm, tn, out_dtype=F32):
    m, d = x2d.shape
    ng = w.shape[1]
    r = sc.shape[0]
    tm = min(tm, m // r)
    tn = min(tn, ng)
    tiles_per_mod = (m // r) // tm
    return pl.pallas_call(
        _modmm_kernel,
        grid=(m // tm, ng // tn),
        in_specs=[pl.BlockSpec((tm, d), lambda i, j: (i, 0)),
                  pl.BlockSpec((1, 1, d), lambda i, j: (i // tiles_per_mod, 0, 0)),
                  pl.BlockSpec((1, 1, d), lambda i, j: (i // tiles_per_mod, 0, 0)),
                  pl.BlockSpec((d, tn), lambda i, j: (0, j))],
        out_specs=pl.BlockSpec((tm, tn), lambda i, j: (i, j)),
        out_shape=jax.ShapeDtypeStruct((m, ng), out_dtype),
        scratch_shapes=[pltpu.VMEM((tm, d), BF16)],
        compiler_params=_cparams(("parallel", "arbitrary")),
    )(x2d, sc, sh, w)


def _row_conv(x, cw, pos, row_w):
    rows = x.shape[0]
    k = cw.shape[0]
    pad = k // 2
    acc = None
    for j in range(k):
        off = j - pad
        if off == 0:
            term = x * cw[j:j + 1]
        else:
            shifted = pltpu.roll(x, (-off) % rows, axis=0)
            valid = (pos >= -off) if off < 0 else (pos < row_w - off)
            term = jnp.where(valid, shifted, 0.0) * cw[j:j + 1]
        acc = term if acc is None else acc + term
    return acc


SOLVE_BLOCK = 16


def _tri_inverse_pairs(pairs):
    c = pairs[0][0].shape[0]
    sb = SOLVE_BLOCK
    nb = c // sb
    ri = lax.broadcasted_iota(jnp.int32, (c, 2 * c), 0)
    ci = lax.broadcasted_iota(jnp.int32, (c, 2 * c), 1)
    ci = jnp.where(ci >= c, ci - c, ci)
    lane_lo = lax.broadcasted_iota(jnp.int32, (1, 2 * c), 1) < c
    off_diag = ri // sb != ci // sb
    eye2 = (ri == ci).astype(F32)
    mcats = [jnp.concatenate([m_a, m_b], axis=1) for m_a, m_b in pairs]
    m_offs = [jnp.where(off_diag, m, 0.0).astype(BF16) for m in mcats]
    xs = [[eye2[i * sb:(i + 1) * sb] for i in range(nb)] for _ in pairs]
    for bi in range(nb):
        rs = slice(bi * sb, (bi + 1) * sb)
        blks = [x[bi] for x in xs]
        if bi > 0:
            for s, x in enumerate(xs):
                xcur = jnp.concatenate(x, axis=0)
                xbd = jnp.concatenate([jnp.where(lane_lo, xcur, 0.0), jnp.where(lane_lo, 0.0, xcur)], axis=0)
                blks[s] = blks[s] - jnp.dot(m_offs[s][rs], xbd.astype(BF16), preferred_element_type=F32)
        mrs = [m[rs] for m in mcats]
        for jj in range(sb - 1):
            col = bi * sb + jj
            src = jnp.broadcast_to(jnp.where(lane_lo, col, c + col), (sb, 2 * c))
            for s in range(len(pairs)):
                mult = jnp.take_along_axis(mrs[s], src, axis=1)
                blks[s] = blks[s] - mult * blks[s][jj:jj + 1, :]
        for s, x in enumerate(xs):
            x[bi] = blks[s]
    return [jnp.concatenate(x, axis=0) for x in xs]


def _dn_gates_kernel(ab_ref, alog_ref, dt_ref, gc_ref, beta_ref, *, n_heads):
    tt = ab_ref.shape[1]
    c = DN_CHUNK
    cpos = lax.broadcasted_iota(jnp.int32, (tt, 1), 0) % c
    ab = ab_ref[0]
    z = ab + dt_ref[...]
    softplus = jnp.maximum(z, 0.0) + jnp.log1p(jnp.exp(-jnp.abs(z)))
    g_all = -jnp.exp(alog_ref[...]) * softplus
    gf = g_all
    gb = g_all
    d = 1
    while d < c:
        gf = gf + jnp.where(cpos >= d, pltpu.roll(gf, d, axis=0), 0.0)
        gb = gb + jnp.where(cpos < c - d, pltpu.roll(gb, (tt - d) % tt, axis=0), 0.0)
        d *= 2
    lane = lax.broadcasted_iota(jnp.int32, (1, LANES), 1)
    gc_ref[0] = jnp.where(lane < 2 * n_heads, gf, gb)
    beta_ref[0] = jax.nn.sigmoid(ab)


def _dn_gates(ab, alog_row, dt_row, *, n_heads, tt):
    b, n, _ = ab.shape
    blk = pl.BlockSpec((1, tt, LANES), lambda bi, g: (bi, g, 0))
    row = pl.BlockSpec((1, LANES), lambda bi, g: (0, 0))
    shp = jax.ShapeDtypeStruct((b, n, LANES), F32)
    return pl.pallas_call(
        functools.partial(_dn_gates_kernel, n_heads=n_heads),
        grid=(b, n // tt),
        in_specs=[blk, row, row],
        out_specs=[blk, blk],
        out_shape=[shp, shp],
        compiler_params=_cparams(("parallel", "parallel")),
    )(ab, alog_row, dt_row)


def _dn_prep_kernel(xq_ref, xk_ref, xv_ref, cq_ref, ck_ref, cv_ref, gc_ref, beta_ref,
                    u_ref, w_ref, qg_ref, kd_ref, attn_ref, eg_ref, *, row_w, n_heads, dk, hp):
    tt = xq_ref.shape[1]
    c = DN_CHUNK
    rows = lax.broadcasted_iota(jnp.int32, (tt, 1), 0)
    pos = rows % row_w
    lane = lax.broadcasted_iota(jnp.int32, (1, LANES), 1)
    gc = gc_ref[0]
    beta_all = beta_ref[0]
    eg_ref[...] = jnp.zeros(eg_ref.shape, F32)
    systems = []
    for hh in range(hp):
        systems += _dn_prep_head(pl.program_id(1) * hp + hh, hh, slice(hh * dk, (hh + 1) * dk), pos, lane, gc,
                                 beta_all, xq_ref, xk_ref, xv_ref, cq_ref, ck_ref, cv_ref, qg_ref, kd_ref, attn_ref,
                                 eg_ref, row_w=row_w, n_heads=n_heads, dk=dk, tt=tt)
    tinvs = _tri_inverse_pairs([(m_f, m_bt) for m_f, m_bt, _, _, _, _ in systems])
    zero_rows = jnp.zeros((c, 2 * dk), BF16)
    for tinv, (_, _, rhs_f, rhs_b, sl, hs) in zip(tinvs, systems):
        tinv = tinv.astype(BF16)
        sol_f = jnp.dot(tinv, jnp.concatenate([rhs_f, zero_rows], axis=0), preferred_element_type=F32)
        sol_b = lax.dot_general(tinv, rhs_b, (((0,), (0,)), ((), ())), preferred_element_type=F32)[c:]
        for dr, sol in enumerate((sol_f, sol_b)):
            u_ref[dr, 0, sl, hs] = sol[:, :dk]
            w_ref[dr, 0, sl, hs] = sol[:, dk:].astype(BF16)


def _dn_prep_head(h, hh, hs, pos, lane, gc, beta_all, xq_ref, xk_ref, xv_ref, cq_ref, ck_ref, cv_ref,
                  qg_ref, kd_ref, attn_ref, eg_ref, *, row_w, n_heads, dk, tt):
    c = DN_CHUNK
    systems = []

    def conv_silu(x_ref, cw_ref):
        return _silu(_row_conv(x_ref[0, :, hs], cw_ref[:, hs], pos, row_w))

    def l2n(y):
        return y * lax.rsqrt(jnp.sum(y * y, axis=-1, keepdims=True) + L2_EPS)

    q = l2n(conv_silu(xq_ref, cq_ref)) * (dk ** -0.5)
    k = l2n(conv_silu(xk_ref, ck_ref))
    v = conv_silu(xv_ref, cv_ref)

    def col(a, l):
        return jnp.sum(jnp.where(lane == l, a, 0.0), axis=1, keepdims=True)

    g_cols = (col(gc, h), col(gc, 2 * n_heads + h))
    b_cols = (col(beta_all, n_heads + h), col(beta_all, 3 * n_heads + h))

    ri = lax.broadcasted_iota(jnp.int32, (c, c), 0)
    ci = lax.broadcasted_iota(jnp.int32, (c, c), 1)
    eye = ri == ci
    lower = ri >= ci
    upper = ri <= ci
    strict_lower = ri > ci
    nt = (((1,), (1,)), ((), ()))
    for cc in range(tt // c):
        sl = slice(cc * c, (cc + 1) * c)
        qc, kc, vc = q[sl], k[sl], v[sl]
        kcb = kc.astype(BF16)
        gcols = [g_cols[dr][sl] for dr in range(2)]
        grows = [jnp.sum(jnp.where(eye, g, 0.0), axis=0, keepdims=True) for g in gcols]
        kbs = [kc * b_cols[dr][sl] for dr in range(2)]
        vbs = [vc * b_cols[dr][sl] for dr in range(2)]
        egs = [jnp.exp(g) for g in gcols]
        dif_f = gcols[0] - grows[0]
        dif_b = gcols[1] - grows[1]
        decay_f = jnp.where(lower, jnp.exp(jnp.where(lower, dif_f, 0.0)), 0.0)
        decay_b = jnp.where(upper, jnp.exp(jnp.where(upper, dif_b, 0.0)), 0.0)
        decay_bt = jnp.where(lower, jnp.exp(jnp.where(lower, -dif_b, 0.0)), 0.0)
        kq = lax.dot_general(jnp.concatenate([kbs[0], qc], axis=0).astype(BF16), kcb, nt,
                             preferred_element_type=F32)
        kkt_b = lax.dot_general(kcb, kbs[1].astype(BF16), nt, preferred_element_type=F32)
        m_f = jnp.where(strict_lower, kq[:c] * decay_f, 0.0)
        m_bt = jnp.where(strict_lower, kkt_b * decay_bt, 0.0)
        rhs_f = jnp.concatenate([vbs[0], kbs[0] * egs[0]], axis=1).astype(BF16)
        rhs_b = jnp.concatenate([vbs[1], kbs[1] * egs[1]], axis=1).astype(BF16)
        systems.append((m_f, m_bt, rhs_f, rhs_b, sl, hs))
        attns = (kq[c:] * decay_f, kq[c:] * decay_b)
        for dr in range(2):
            gcol = gcols[dr]
            glast = gcol[c - 1:c] if dr == 0 else gcol[0:1]
            qg_ref[dr, 0, sl, hs] = (qc * egs[dr]).astype(BF16)
            kd_ref[dr, 0, sl, hs] = (kc * jnp.exp(glast - gcol)).astype(BF16)
            attn_ref[dr, 0, hh, sl, :] = attns[dr].astype(BF16)
            eg_ref[dr, 0, hh, 0, cc:cc + 1, :] = jnp.broadcast_to(jnp.exp(glast), (1, LANES))
    return systems


def _dn_prep(qkv, conv_w, ab, alog_row, dt_row, *, n_heads, dk, row_w, tt):
    gc, beta = _dn_gates(ab, alog_row, dt_row, n_heads=n_heads, tt=tt)
    b, n, _ = qkv.shape
    hh = n_heads
    ng = n // tt
    c = DN_CHUNK
    assert tt % row_w == 0 and tt % c == 0 and tt // c <= SUBLANES
    hp = 2 if hh % 2 == 0 else 1
    nhb = hh // hp
    kern = functools.partial(_dn_prep_kernel, row_w=row_w, n_heads=hh, dk=dk, hp=hp)
    xspec = lambda off: pl.BlockSpec((1, tt, hp * dk), lambda bi, h, g: (bi, g, off + h))
    cspec = lambda off: pl.BlockSpec((conv_w.shape[0], hp * dk), lambda bi, h, g: (0, off + h))
    ospec = pl.BlockSpec((2, 1, tt, hp * dk), lambda bi, h, g: (0, bi, g, h))
    return pl.pallas_call(
        kern,
        grid=(b, nhb, ng),
        in_specs=[xspec(0), xspec(nhb), xspec(2 * nhb), cspec(0), cspec(nhb), cspec(2 * nhb),
                  pl.BlockSpec((1, tt, LANES), lambda bi, h, g: (bi, g, 0)),
                  pl.BlockSpec((1, tt, LANES), lambda bi, h, g: (bi, g, 0))],
        out_specs=[ospec, ospec, ospec, ospec,
                   pl.BlockSpec((2, 1, hp, tt, c), lambda bi, h, g: (0, bi, h, g, 0)),
                   pl.BlockSpec((2, 1, hp, 1, SUBLANES, LANES), lambda bi, h, g: (0, bi, h, g, 0, 0))],
        out_shape=[jax.ShapeDtypeStruct((2, b, n, hh * dk), F32),
                   jax.ShapeDtypeStruct((2, b, n, hh * dk), BF16),
                   jax.ShapeDtypeStruct((2, b, n, hh * dk), BF16),
                   jax.ShapeDtypeStruct((2, b, n, hh * dk), BF16),
                   jax.ShapeDtypeStruct((2, b, hh, n, c), BF16),
                   jax.ShapeDtypeStruct((2, b, hh, ng, SUBLANES, LANES), F32)],
        compiler_params=_cparams(("parallel", "parallel", "arbitrary")),
    )(qkv, qkv, qkv, conv_w, conv_w, conv_w, gc, beta)


def _dn_scan_kernel(uf_ref, wf_ref, qf_ref, kf_ref, af_ref, ef_ref,
                    ub_ref, wb_ref, qb_ref, kb_ref, ab_ref, eb_ref, s0_ref,
                    of_ref, ob_ref, sout_ref, s_scr, *, hp, dk):
    g = pl.program_id(2)
    ng = pl.num_programs(2)
    c = DN_CHUNK
    tt = uf_ref.shape[2]
    ncs = tt // c

    @pl.when(g == 0)
    def _():
        s_scr[...] = s0_ref[:, 0]

    refs = ((uf_ref, wf_ref, qf_ref, kf_ref, af_ref, ef_ref, of_ref),
            (ub_ref, wb_ref, qb_ref, kb_ref, ab_ref, eb_ref, ob_ref))
    chains = [(dr, hh) for dr in range(2) for hh in range(hp)]
    state = {ch: s_scr[ch[0], ch[1]] for ch in chains}
    for step in range(ncs):
        rows = {}
        for dr, hh in chains:
            cc = step if dr == 0 else ncs - 1 - step
            rows[dr, hh] = (slice(cc * c, (cc + 1) * c), slice(hh * dk, (hh + 1) * dk), cc)
        r = {}
        for ch in chains:
            w_ref, q_ref = refs[ch[0]][1], refs[ch[0]][2]
            sl, hs, _ = rows[ch]
            wq = jnp.concatenate([w_ref[0, 0, sl, hs], q_ref[0, 0, sl, hs]], axis=0)
            r[ch] = jnp.dot(wq, state[ch].astype(BF16), preferred_element_type=F32)
        vnb = {}
        for ch in chains:
            sl, hs, _ = rows[ch]
            vnb[ch] = (refs[ch[0]][0][0, 0, sl, hs] - r[ch][:c]).astype(BF16)
        for ch in chains:
            _, _, _, k_ref, a_ref, e_ref, o_ref = refs[ch[0]]
            sl, hs, cc = rows[ch]
            o_ref[0, sl, hs] = r[ch][c:] + jnp.dot(a_ref[0, 0, ch[1], sl, :], vnb[ch], preferred_element_type=F32)
            upd = lax.dot_general(k_ref[0, 0, sl, hs], vnb[ch], (((0,), (0,)), ((), ())),
                                  preferred_element_type=F32)
            state[ch] = state[ch] * e_ref[0, 0, ch[1], 0, cc:cc + 1, :] + upd
    for ch in chains:
        s_scr[ch[0], ch[1]] = state[ch]

    @pl.when(g == ng - 1)
    def _():
        sout_ref[:, 0] = s_scr[...]


def _dn_scan(u, w, qg, kd, attn, eg, s0, *, dk, tt, hp):
    _, b, n, hd = u.shape
    hh = hd // dk
    ng = n // tt
    c = DN_CHUNK
    kern = functools.partial(_dn_scan_kernel, hp=hp, dk=dk)

    def specs(dr):
        gi = (lambda g: g) if dr == 0 else (lambda g: ng - 1 - g)
        tok = pl.BlockSpec((1, 1, tt, hp * dk), lambda bi, h, g: (dr, bi, gi(g), h))
        return [tok, tok, tok, tok,
                pl.BlockSpec((1, 1, hp, tt, c), lambda bi, h, g: (dr, bi, h, gi(g), 0)),
                pl.BlockSpec((1, 1, hp, 1, SUBLANES, LANES), lambda bi, h, g: (dr, bi, h, gi(g), 0, 0))]

    sspec = pl.BlockSpec((2, 1, hp, dk, dk), lambda bi, h, g: (0, bi, h, 0, 0))
    return pl.pallas_call(
        kern,
        grid=(b, hh // hp, ng),
        in_specs=specs(0) + specs(1) + [sspec],
        out_specs=[pl.BlockSpec((1, tt, hp * dk), lambda bi, h, g: (bi, g, h)),
                   pl.BlockSpec((1, tt, hp * dk), lambda bi, h, g: (bi, ng - 1 - g, h)),
                   sspec],
        out_shape=[jax.ShapeDtypeStruct((b, n, hd), F32),
                   jax.ShapeDtypeStruct((b, n, hd), F32),
                   jax.ShapeDtypeStruct((2, b, hh, dk, dk), F32)],
        scratch_shapes=[pltpu.VMEM((2, hp, dk, dk), F32)],
        compiler_params=_cparams(("parallel", "parallel", "arbitrary")),
    )(u, w, qg, kd, attn, eg, u, w, qg, kd, attn, eg, s0)


def _hy_hidden_kernel(z_ref, w1_ref, b1_ref, w2_ref, b2_ref, fr_ref, o_ref):
    fr = fr_ref[...]
    h1 = jnp.sin(fr * (jnp.dot(z_ref[...].astype(BF16), w1_ref[...].astype(BF16),
                               preferred_element_type=F32) + b1_ref[...]))
    o_ref[...] = jnp.sin(fr * (jnp.dot(h1.astype(BF16), w2_ref[...].astype(BF16),
                                       preferred_element_type=F32) + b2_ref[...]))


def _hy_filter_kernel(hid_ref, w3_ref, b3_ref, t_ref, delta_ref, o_ref):
    f = jnp.dot(hid_ref[...].astype(BF16), w3_ref[...].astype(BF16), preferred_element_type=F32) + b3_ref[...]
    o_ref[...] = f * jnp.exp(-t_ref[...] * delta_ref[...])


def _hy_filters(n, w1, b1, w2, b2, w3, b3, freq, hy_w):
    emb, hid = w1.shape
    bands = (emb - 1) // 2
    t = jnp.linspace(0.0, 1.0, n, dtype=F32)
    pos = jnp.arange(n, dtype=F32)
    bnd = jnp.linspace(1e-4, bands - 1, bands, dtype=F32)
    ang = (2.0 * math.pi / n) * pos[:, None] * bnd[None, :]
    z = jnp.concatenate([t[:, None], jnp.cos(ang), -jnp.sin(ang)], axis=-1)
    embp = -(-emb // SUBLANES) * SUBLANES
    z = jnp.pad(z, ((0, 0), (0, embp - emb)))
    w1p = jnp.pad(w1, ((0, embp - emb), (0, 0)))
    full = lambda a: pl.BlockSpec(a.shape, lambda: (0,) * a.ndim)
    args = (z, w1p, b1[None, :], w2, b2[None, :], freq[None, :])
    hid2 = pl.pallas_call(
        _hy_hidden_kernel,
        in_specs=[full(a) for a in args],
        out_specs=pl.BlockSpec((n, hid), lambda: (0, 0)),
        out_shape=jax.ShapeDtypeStruct((n, hid), F32),
        compiler_params=pltpu.CompilerParams(vmem_limit_bytes=VMEM_LIMIT_BYTES),
    )(*args)
    nout = w3.shape[1]
    deltas = jnp.abs(jnp.linspace(math.log(HY_DECAY_TARGET) / HY_SLOW_DECAY,
                                  math.log(HY_DECAY_TARGET) / HY_FAST_DECAY, hy_w, dtype=F32))
    delta_row = jnp.tile(deltas, nout // hy_w)[None, :]
    tc = min(nout, 1024)
    return pl.pallas_call(
        _hy_filter_kernel,
        grid=(nout // tc,),
        in_specs=[pl.BlockSpec((n, hid), lambda j: (0, 0)),
                  pl.BlockSpec((hid, tc), lambda j: (0, j)),
                  pl.BlockSpec((1, tc), lambda j: (0, j)),
                  pl.BlockSpec((n, 1), lambda j: (0, 0)),
                  pl.BlockSpec((1, tc), lambda j: (0, j))],
        out_specs=pl.BlockSpec((n, tc), lambda j: (0, j)),
        out_shape=jax.ShapeDtypeStruct((n, nout), F32),
        compiler_params=_cparams(("parallel",)),
    )(hid2, w3, b3[None, :], t[:, None], delta_row)


def _dft_tables(n):
    n2 = FFT_N2
    big = 2 * n
    n1 = big // n2
    k1 = n1 // 2
    two_pi = 2.0 * math.pi

    def cs(idx, period):
        a = (idx % period).astype(F32) * (two_pi / period)
        return jnp.cos(a), jnp.sin(a)

    f1 = jnp.arange(n1, dtype=jnp.int32)
    s1 = jnp.arange(k1, dtype=jnp.int32)
    s2 = jnp.arange(n2, dtype=jnp.int32)
    s_full = s1[None, None, :] * n2 + s2[:, None, None]
    c, s = cs(f1[None, :, None] * s_full, big)
    f1tw = jnp.concatenate([jnp.concatenate([c, s], axis=2), jnp.concatenate([-s, c], axis=2)], axis=1)
    ct, st = jnp.swapaxes(c, 1, 2), jnp.swapaxes(s, 1, 2)
    g1tw = jnp.concatenate([jnp.concatenate([ct, -st], axis=2), jnp.concatenate([st, ct], axis=2)], axis=1) / big
    c2, sn2 = cs(s2[:, None] * s2[None, :], n2)
    f2 = jnp.concatenate([jnp.concatenate([c2, sn2], axis=1), jnp.concatenate([-sn2, c2], axis=1)], axis=0)
    g2 = jnp.concatenate([jnp.concatenate([c2, -sn2], axis=1), jnp.concatenate([sn2, c2], axis=1)], axis=0)
    return f1tw.astype(BF16), f2.astype(BF16), g2.astype(BF16), g1tw.astype(BF16)


def _hy_conv_kernel(a_ref, m_ref, hf_ref, hb_ref, skip_ref, cwa_ref, cwm_ref,
                    f1_ref, f2_ref, g2_ref, g1_ref, o_ref,
                    h_scr, b_scr, zr_scr, zi_scr, *, conv_a, n1, k1):
    n2 = FFT_N2
    k1p = k1 + FFT_ROW_PAD
    pb = 2 * n2 + FFT_ROW_PAD
    pos = lax.broadcasted_iota(jnp.int32, (n2, 1), 0)

    grp = FFT_UNROLL
    cw = b_scr.shape[1]

    def stage1(real_only):
        def body(g, carry):
            s2s = [g * grp + t for t in range(grp)]
            bases = [pl.multiple_of(s2 * k1p, SUBLANES) for s2 in s2s]
            if real_only:
                xs = [zr_scr[pl.ds(base, k1), :].astype(BF16) for base in bases]
                outs = [jnp.dot(f1_ref[s2, :, :k1], x, preferred_element_type=F32) for s2, x in zip(s2s, xs)]
            else:
                xs = [jnp.concatenate([zr_scr[pl.ds(base, k1), :], zi_scr[pl.ds(base, k1), :]], axis=0).astype(BF16)
                      for base in bases]
                outs = [jnp.dot(f1_ref[s2], x, preferred_element_type=F32) for s2, x in zip(s2s, xs)]
            for s2, a in zip(s2s, outs):
                b_scr[pl.ds(s2, n1, stride=pb), :] = a[:n1]
                b_scr[pl.ds(n2 + s2, n1, stride=pb), :] = a[n1:]
            return carry
        lax.fori_loop(0, n2 // grp, body, 0)

    def slab_pairs(g):
        f1s = [[(g * grp + t) * 2 + u for u in range(2)] for t in range(grp)]
        bbs = [[pl.multiple_of(f1 * pb, SUBLANES) for f1 in pair] for pair in f1s]
        hbs = [[pl.multiple_of(f1 * 2 * n2, 2 * n2) for f1 in pair] for pair in f1s]
        return bbs, hbs

    def forward2(bbs):
        bpairs = [jnp.concatenate([b_scr[pl.ds(bb, 2 * n2), :] for bb in pair], axis=1).astype(BF16) for pair in bbs]
        return [jnp.dot(f2_ref[...], bp, preferred_element_type=F32) for bp in bpairs]

    @pl.when(pl.program_id(1) == 0)
    def _():
        for which, ref in ((0, hf_ref), (1, hb_ref)):
            def load(s1, carry, ref=ref):
                r0 = pl.multiple_of(s1 * n2, n2)
                zr_scr[pl.ds(s1, n2, stride=k1p), :] = ref[pl.ds(r0, n2), :]
                return carry
            lax.fori_loop(0, k1, load, 0, unroll=FFT_UNROLL)
            if which == 1:
                zr_scr[0:1, :] = jnp.zeros((1, zr_scr.shape[1]), F32)
            stage1(True)

            def spec(g, carry, which=which):
                bbs, hbs = slab_pairs(g)
                zpairs = forward2(bbs)
                for zpair, hpair in zip(zpairs, hbs):
                    for t, hb_ in enumerate(hpair):
                        zz = zpair[:, t * cw:(t + 1) * cw]
                        if which == 0:
                            h_scr[pl.ds(hb_, 2 * n2), :] = zz
                        else:
                            h_scr[pl.ds(hb_, n2), :] = h_scr[pl.ds(hb_, n2), :] + zz[:n2]
                            h_scr[pl.ds(hb_ + n2, n2), :] = h_scr[pl.ds(hb_ + n2, n2), :] - zz[n2:]
                return carry
            lax.fori_loop(0, n1 // (2 * grp), spec, 0)

    def load_a(s1, carry):
        r0 = pl.multiple_of(s1 * n2, n2)
        for bi, scr in ((0, zr_scr), (1, zi_scr)):
            blk = a_ref[bi, pl.ds(r0, n2), :]
            if conv_a:
                blk = _row_conv(blk, cwa_ref[...], pos, n2)
            scr[pl.ds(s1, n2, stride=k1p), :] = blk
        return carry
    lax.fori_loop(0, k1, load_a, 0, unroll=FFT_UNROLL)

    stage1(False)

    def mid(g, carry):
        bbs, hbs = slab_pairs(g)
        zzs = forward2(bbs)
        yys = []
        for zz, hpair in zip(zzs, hbs):
            zr, zi = zz[:n2], zz[n2:]
            hr = jnp.concatenate([h_scr[pl.ds(hb_, n2), :] for hb_ in hpair], axis=1)
            hi = jnp.concatenate([h_scr[pl.ds(hb_ + n2, n2), :] for hb_ in hpair], axis=1)
            yys.append(jnp.concatenate([zr * hr - zi * hi, zr * hi + zi * hr], axis=0).astype(BF16))
        wws = [jnp.dot(g2_ref[...], yy, preferred_element_type=F32) for yy in yys]
        for ww, pair in zip(wws, bbs):
            for t, bb in enumerate(pair):
                b_scr[pl.ds(bb, 2 * n2), :] = ww[:, t * cw:(t + 1) * cw]
        return carry
    lax.fori_loop(0, n1 // (2 * grp), mid, 0)

    skip = skip_ref[...]

    def last(g, carry):
        s2s = [g * grp + t for t in range(grp)]
        wws = [jnp.concatenate([b_scr[pl.ds(s2, n1, stride=pb), :],
                                b_scr[pl.ds(n2 + s2, n1, stride=pb), :]], axis=0).astype(BF16) for s2 in s2s]
        ys = [jnp.dot(g1_ref[s2], ww, preferred_element_type=F32) for s2, ww in zip(s2s, wws)]
        for s2, y in zip(s2s, ys):
            base = pl.multiple_of(s2 * k1p, SUBLANES)
            zr_scr[pl.ds(base, k1), :] = y[:k1] + zr_scr[pl.ds(base, k1), :] * skip
            zi_scr[pl.ds(base, k1), :] = y[k1:] + zi_scr[pl.ds(base, k1), :] * skip
        return carry
    lax.fori_loop(0, n2 // grp, last, 0)

    def fin(s1, carry):
        r0 = pl.multiple_of(s1 * n2, n2)
        for bi, scr in ((0, zr_scr), (1, zi_scr)):
            mm = _row_conv(m_ref[bi, pl.ds(r0, n2), :], cwm_ref[...], pos, n2)
            o_ref[bi, pl.ds(r0, n2), :] = mm * scr[pl.ds(s1, n2, stride=k1p), :]
        return carry
    lax.fori_loop(0, k1, fin, 0, unroll=FFT_UNROLL)


def _hy_conv(a, a_col0, m, m_col0, filt, f_col_fwd, f_col_bwd, skip, cwa, cwm, tables, *, conv_a, hy_w):
    b, n, _ = a.shape
    assert b % 2 == 0 and GRID_W == FFT_N2 and n % FFT_N2 == 0
    n2 = FFT_N2
    n1 = 2 * n // n2
    k1 = n1 // 2
    k1p = k1 + FFT_ROW_PAD
    pb = 2 * n2 + FFT_ROW_PAD
    cw = LANES
    f1tw, f2, g2, g1tw = tables
    kern = functools.partial(_hy_conv_kernel, conv_a=conv_a, n1=n1, k1=k1)
    once = pl.Buffered(1)
    return pl.pallas_call(
        kern,
        grid=(hy_w // cw, b // 2),
        in_specs=[pl.BlockSpec((2, n, cw), lambda c, p: (p, 0, a_col0 + c), pipeline_mode=once),
                  pl.BlockSpec((2, n, cw), lambda c, p: (p, 0, m_col0 + c), pipeline_mode=once),
                  pl.BlockSpec((n, cw), lambda c, p: (0, f_col_fwd + c), pipeline_mode=once),
                  pl.BlockSpec((n, cw), lambda c, p: (0, f_col_bwd + c), pipeline_mode=once),
                  pl.BlockSpec((1, cw), lambda c, p: (0, c)),
                  pl.BlockSpec((cwa.shape[0], cw), lambda c, p: (0, a_col0 + c if conv_a else 0)),
                  pl.BlockSpec((cwm.shape[0], cw), lambda c, p: (0, m_col0 + c)),
                  pl.BlockSpec(f1tw.shape, lambda c, p: (0, 0, 0), pipeline_mode=once),
                  pl.BlockSpec(f2.shape, lambda c, p: (0, 0)),
                  pl.BlockSpec(g2.shape, lambda c, p: (0, 0)),
                  pl.BlockSpec(g1tw.shape, lambda c, p: (0, 0, 0), pipeline_mode=once)],
        out_specs=pl.BlockSpec((2, n, cw), lambda c, p: (p, 0, c)),
        out_shape=jax.ShapeDtypeStruct((b, n, hy_w), F32),
        scratch_shapes=[pltpu.VMEM((n1 * 2 * n2, cw), F32),
                        pltpu.VMEM((n1 * pb, cw), F32),
                        pltpu.VMEM((n2 * k1p, cw), F32),
                        pltpu.VMEM((n2 * k1p, cw), F32)],
        compiler_params=_cparams(("parallel", "arbitrary")),
    )(a, m, filt, filt, skip, cwa, cwm, f1tw, f2, g2, g1tw)


def _merge_kernel(of_ref, ob_ref, z_ref, y_ref, ga_ref, gb_ref, nw_ref, wa_ref, wh_ref, o_ref, *, dv):
    nw = nw_ref[...]
    heads = []
    for hh in range(of_ref.shape[1] // dv):
        hs = slice(hh * dv, (hh + 1) * dv)
        o = of_ref[:, hs] + ob_ref[:, hs]
        o = o * lax.rsqrt(jnp.mean(o * o, axis=-1, keepdims=True) + RMS_EPS) * nw
        heads.append((o * _silu(z_ref[:, hs])).astype(BF16))
    pa = jnp.dot(jnp.concatenate(heads, axis=1), wa_ref[...], preferred_element_type=F32)
    ph = jnp.dot(y_ref[...].astype(BF16), wh_ref[...], preferred_element_type=F32)
    o_ref[...] = (jax.nn.sigmoid(ga_ref[...]) * pa + jax.nn.sigmoid(gb_ref[...]) * ph).astype(o_ref.dtype)


def _merge(o_f, o_b, z, z_blk, y_hy, gate, norm_w, w_pa, w_ph, *, dv, tm):
    m, dvw = o_f.shape
    d = w_pa.shape[1]
    tm = min(tm, m)
    kern = functools.partial(_merge_kernel, dv=dv)
    row = lambda w_, blk=0: pl.BlockSpec((tm, w_), lambda i: (i, blk))
    const = lambda a: pl.BlockSpec(a.shape, lambda i: (0, 0), pipeline_mode=pl.Buffered(1))
    return pl.pallas_call(
        kern,
        grid=(m // tm,),
        in_specs=[row(dvw), row(dvw), row(dvw, z_blk), row(y_hy.shape[1]), row(d, 0), row(d, 1),
                  pl.BlockSpec((1, dv), lambda i: (0, 0)), const(w_pa), const(w_ph)],
        out_specs=row(d),
        out_shape=jax.ShapeDtypeStruct((m, d), BF16),
        compiler_params=_cparams(("parallel",)),
    )(o_f, o_b, z, y_hy, gate, gate, norm_w, w_pa, w_ph)


def _layer_norm(x, g, b):
    mu = jnp.mean(x, axis=-1, keepdims=True)
    xc = x - mu
    var = jnp.mean(xc * xc, axis=-1, keepdims=True)
    return xc * lax.rsqrt(var + LN_EPS) * g + b


def _outproj_kernel(mg_ref, x_ref, wo_ref, g1_ref, lg_ref, lb_ref, sc_ref, sh_ref, wq_ref, sk_ref,
                    x1_ref, h2_ref, s_ref, *, half):
    mix = jnp.dot(mg_ref[...], wo_ref[...], preferred_element_type=F32)
    x1 = _layer_norm(DEEPNORM_ALPHA * x_ref[...] + g1_ref[0] * mix, lg_ref[...], lb_ref[...])
    x1_ref[...] = x1
    h2 = (x1 * (1.0 + sc_ref[0]) + sh_ref[0]).astype(BF16)
    h2_ref[...] = h2
    qp = jnp.dot(h2, wq_ref[...], preferred_element_type=F32).astype(BF16)
    for j in range(sk_ref.shape[0]):
        js = slice(j * half, (j + 1) * half)
        s_ref[j] = lax.dot_general(sk_ref[j], qp[:, js], (((1,), (1,)), ((), ())), preferred_element_type=F32)


def _outproj(merged, x2d, w_out, g1, ln_g, ln_b, sc2, sh2, wq, subkeys, *, tm):
    m, d = x2d.shape
    r = g1.shape[0]
    tm = min(tm, m // r)
    tpm = (m // r) // tm
    nsk, nkeys, half = subkeys.shape
    assert nkeys == LANES
    kern = functools.partial(_outproj_kernel, half=half)
    row = lambda w_: pl.BlockSpec((tm, w_), lambda i: (i, 0))
    mod = pl.BlockSpec((1, 1, d), lambda i: (i // tpm, 0, 0))
    vec = pl.BlockSpec((1, d), lambda i: (0, 0))
    return pl.pallas_call(
        kern,
        grid=(m // tm,),
        in_specs=[row(d), row(d), pl.BlockSpec(w_out.shape, lambda i: (0, 0), pipeline_mode=pl.Buffered(1)),
                  mod, vec, vec, mod, mod,
                  pl.BlockSpec(wq.shape, lambda i: (0, 0), pipeline_mode=pl.Buffered(1)),
                  pl.BlockSpec(subkeys.shape, lambda i: (0, 0, 0))],
        out_specs=[row(d), row(d), pl.BlockSpec((nsk, nkeys, tm), lambda i: (0, 0, i))],
        out_shape=[jax.ShapeDtypeStruct((m, d), F32),
                   jax.ShapeDtypeStruct((m, d), BF16),
                   jax.ShapeDtypeStruct((nsk, nkeys, m), F32)],
        compiler_params=_cparams(("parallel",)),
    )(merged, x2d, w_out, g1, ln_g, ln_b, sc2, sh2, wq, subkeys)


def _cand_tables():
    pairs = [(a, b) for a in range(PEER_TOPK) for b in range(PEER_TOPK) if (a + 1) * (b + 1) <= PEER_TOPK]
    ia = jnp.array([p[0] for p in pairs], dtype=jnp.int32)
    ib = jnp.array([p[1] for p in pairs], dtype=jnp.int32)
    ncand = len(pairs)
    rows = -(-ncand // (2 * SUBLANES)) * (2 * SUBLANES)
    ranks = jnp.arange(LANES, dtype=jnp.int32)
    p1 = (jnp.pad(ia, (0, rows - ncand), constant_values=-1)[:, None] == ranks[None, :]).astype(F32)
    p2 = (jnp.pad(ib, (0, rows - ncand), constant_values=-1)[:, None] == ranks[None, :]).astype(F32)
    return p1, p2, ncand


def _peer_topk_kernel(s_ref, p1_ref, p2_ref, p1t_ref, c1_ref, r2_ref, e1_ref, e2_ref, *, n_heads, ncand):
    tm = s_ref.shape[2]
    neg = -jnp.inf
    hi = lax.Precision.HIGHEST
    no_rank = float(LANES)

    def pop_max(x, row_id):
        mx = jnp.max(x, axis=0, keepdims=True)
        first = jnp.min(jnp.where(x == mx, row_id, no_rank), axis=0, keepdims=True)
        hit = row_id == first
        return mx, hit, jnp.where(hit, neg, x)

    key_id = lax.broadcasted_iota(jnp.int32, (LANES, tm), 0).astype(F32)
    cand_id = lax.broadcasted_iota(jnp.int32, (p1_ref.shape[0], tm), 0).astype(F32)
    pad_rows = jnp.zeros((LANES - PEER_TOPK, tm), F32)

    def sorted_top(s):
        tops = []
        rank = jnp.full(s.shape, no_rank, F32)
        x = s
        for a in range(PEER_TOPK):
            mx, hit, x = pop_max(x, key_id)
            tops.append(mx)
            rank = jnp.where(hit, float(a), rank)
        return jnp.concatenate(tops + [pad_rows], axis=0), rank

    for h in range(n_heads):
        s1 = s_ref[2 * h]
        s2 = s_ref[2 * h + 1]
        t1, rank1 = sorted_top(s1)
        t2, rank2 = sorted_top(s2)
        cand = (jnp.dot(p1_ref[...], t1, precision=hi, preferred_element_type=F32)
                + jnp.dot(p2_ref[...], t2, precision=hi, preferred_element_type=F32))
        x = jnp.where(cand_id < ncand, cand, neg)
        sel = jnp.zeros(x.shape, F32)
        zsum = None
        cmax = None
        for r in range(PEER_TOPK):
            mx, hit, x = pop_max(x, cand_id)
            sel = jnp.where(hit, 1.0, sel)
            if r == 0:
                cmax = mx
                zsum = jnp.ones_like(mx)
            else:
                zsum = zsum + jnp.exp(mx - cmax)
        cnt = jnp.dot(p1t_ref[...].astype(BF16), sel.astype(BF16), preferred_element_type=F32)
        count1 = jnp.zeros(s1.shape, F32)
        for a in range(PEER_TOPK):
            count1 = jnp.where(rank1 == a, cnt[a:a + 1], count1)
        c1_ref[h] = count1
        r2_ref[h] = rank2.astype(r2_ref.dtype)
        e1_ref[h] = jnp.exp(s1 - t1[0:1]) / zsum
        e2_ref[h] = jnp.exp(s2 - t2[0:1]).astype(e2_ref.dtype)


def _peer_topk(scores, n_heads, *, tm):
    nsk, nkeys, m = scores.shape
    tm = min(tm, m)
    p1, p2, ncand = _cand_tables()
    kern = functools.partial(_peer_topk_kernel, n_heads=n_heads, ncand=ncand)
    full = lambda a: pl.BlockSpec(a.shape, lambda i: (0, 0))
    out = pl.BlockSpec((n_heads, nkeys, tm), lambda i: (0, 0, i))
    shp = lambda dt: jax.ShapeDtypeStruct((n_heads, nkeys, m), dt)
    p1t = p1.T
    return pl.pallas_call(
        kern,
        grid=(m // tm,),
        in_specs=[pl.BlockSpec((nsk, nkeys, tm), lambda i: (0, 0, i)), full(p1), full(p2), full(p1t)],
        out_specs=[out, out, out, out],
        out_shape=[shp(F32), shp(F32), shp(F32), shp(F32)],
        compiler_params=_cparams(("parallel",)),
    )(scores, p1, p2, p1t)


def _peer_dense_kernel(h2t_ref, u_ref, v_ref, r2_ref, e2_ref, c1_ref, e1_ref, x1_ref, g2_ref, lg_ref, lb_ref,
                       o_ref, acc_scr, a_scr, p_scr, r2_scr, e2_scr, *, n_heads, ti):
    e = pl.program_id(1)

    @pl.when(e == 0)
    def _():
        acc_scr[...] = jnp.zeros(acc_scr.shape, F32)
        r2_scr[...] = r2_ref[...].astype(BF16)
        e2_scr[...] = e2_ref[...].astype(BF16)

    a_scr[...] = jnp.dot(u_ref[...], h2t_ref[...], preferred_element_type=F32)
    tm = h2t_ref.shape[1]
    nkeys = r2_ref.shape[1]
    for ii in range(ti):
        c1_rows = [c1_ref[h, ii:ii + 1, :].astype(BF16) for h in range(n_heads)]
        e1_rows = [e1_ref[h, ii:ii + 1, :].astype(BF16) for h in range(n_heads)]
        for band in range(nkeys // PEER_BAND):
            ks = slice(band * PEER_BAND, (band + 1) * PEER_BAND)
            rs = slice(ii * nkeys + band * PEER_BAND, ii * nkeys + (band + 1) * PEER_BAND)
            gate = None
            for h in range(n_heads):
                keep = r2_scr[h, ks, :] < c1_rows[h]
                term = jnp.where(keep, e2_scr[h, ks, :], jnp.zeros((), BF16)) * e1_rows[h]
                gate = term if gate is None else gate + term
            a = a_scr[rs, :]
            act = 0.5 * a * (1.0 + lax.erf(a * (2.0 ** -0.5)))
            p_scr[rs, :] = gate * act.astype(BF16)
    acc_scr[...] += lax.dot_general(p_scr[...], v_ref[...], (((0,), (0,)), ((), ())), preferred_element_type=F32)

    @pl.when(e == pl.num_programs(1) - 1)
    def _():
        o_ref[...] = _layer_norm(DEEPNORM_ALPHA * x1_ref[...] + g2_ref[0] * acc_scr[...],
                                 lg_ref[...], lb_ref[...])


def _peer_dense(h2t, u_tab, v_tab, rank2, e2, count1, e1, x1, g2, ln_g, ln_b, *, n_heads, ti, tm):
    d, m = h2t.shape
    r = g2.shape[0]
    tm = min(tm, m // r)
    tpm = (m // r) // tm
    ne = u_tab.shape[0]
    nkeys = rank2.shape[1]
    te = ti * nkeys
    kern = functools.partial(_peer_dense_kernel, n_heads=n_heads, ti=ti)
    row = lambda w_: pl.BlockSpec((tm, w_), lambda i, e: (i, 0), pipeline_mode=pl.Buffered(1))
    vec = pl.BlockSpec((1, d), lambda i, e: (0, 0))
    tbl = pl.BlockSpec((te, d), lambda i, e: (e, 0))
    second = pl.BlockSpec((n_heads, nkeys, tm), lambda i, e: (0, 0, i), pipeline_mode=pl.Buffered(1))
    first = pl.BlockSpec((n_heads, ti, tm), lambda i, e: (0, e, i))
    return pl.pallas_call(
        kern,
        grid=(m // tm, ne // te),
        in_specs=[pl.BlockSpec((d, tm), lambda i, e: (0, i), pipeline_mode=pl.Buffered(1)),
                  tbl, tbl, second, second, first, first, row(d),
                  pl.BlockSpec((1, 1, d), lambda i, e: (i // tpm, 0, 0)), vec, vec],
        out_specs=pl.BlockSpec((tm, d), lambda i, e: (i, 0)),
        out_shape=jax.ShapeDtypeStruct((m, d), F32),
        scratch_shapes=[pltpu.VMEM((tm, d), F32), pltpu.VMEM((te, tm), F32), pltpu.VMEM((te, tm), BF16),
                        pltpu.VMEM((n_heads, nkeys, tm), BF16), pltpu.VMEM((n_heads, nkeys, tm), BF16)],
        compiler_params=_cparams(("parallel", "arbitrary")),
    )(h2t, u_tab, v_tab, rank2, e2, count1, e1, x1, g2, ln_g, ln_b)


def _block(x, c, ctx, c_ctx, w_ada, b_ada, w_in, dn_conv_w, dn_a_log, dn_dt_bias, dn_norm_w, hy_conv_w, hy_w1, hy_b1, hy_w2, hy_b2, hy_w3, hy_b3, hy_freq, hy_skip, w_branch_dn, w_branch_hy, w_out, ln1_g, ln1_b, peer_wq, peer_subkeys, peer_u, peer_v, ln2_g, ln2_b):
    b, n, d = x.shape
    n_ctx = ctx.shape[1]
    assert w_ada.shape[0] == DEPTH == 1, "single-layer block: the context stream update is never read"
    hh = dn_a_log.shape[2]
    dk = dn_conv_w.shape[2] // (3 * hh)
    qk_w = hh * dk
    hy_w = hy_skip.shape[2]
    p_heads, _, nkeys, half = peer_subkeys.shape[1:]
    assert dk == LANES and nkeys == LANES and 4 * hh <= LANES and n % GRID_W == 0
    off_z = 3 * qk_w
    off_ab = 4 * qk_w
    off_hy = off_ab + 4 * hh
    off_gate = off_hy + 3 * hy_w
    assert w_in.shape[2] == off_gate + 2 * d

    cc = jnp.zeros((SUBLANES, d), F32).at[:b].set(c).at[b].set(c_ctx)
    mods = _ada(cc, w_ada[0], b_ada[0][None, :])
    sh1, sc1, g1, sh2, sc2, g2 = [mods[:b, i * d:(i + 1) * d][:, None, :] for i in range(6)]
    sh1c, sc1c = [mods[b:b + 1, i * d:(i + 1) * d][:, None, :] for i in range(2)]

    w = w_in[0]
    w_qkvz = w[:, :off_ab].astype(BF16)
    w_ab = jnp.pad(w[:, off_ab:off_hy], ((0, 0), (0, LANES - 4 * hh))).astype(BF16)
    w_hy = w[:, off_hy:off_gate].astype(BF16)
    w_gate = w[:, off_gate:].astype(BF16)

    x2d = x.reshape(b * n, d)
    c2d = ctx.reshape(b * n_ctx, d)
    proj_qkvz = _modmm(x2d, sc1, sh1, w_qkvz, tm=512, tn=2048)
    proj_ab = _modmm(x2d, sc1, sh1, w_ab, tm=512, tn=LANES)
    proj_hy = _modmm(x2d, sc1, sh1, w_hy, tm=512, tn=2048)
    proj_gate = _modmm(x2d, sc1, sh1, w_gate, tm=512, tn=2048)
    projc_qkvz = _modmm(c2d, sc1c, sh1c, w_qkvz, tm=256, tn=2048)
    projc_ab = _modmm(c2d, sc1c, sh1c, w_ab, tm=256, tn=LANES)

    def decay_row(a):
        row = jnp.concatenate([a, jnp.zeros_like(a)], axis=1).reshape(-1)
        return jnp.pad(row, (0, LANES - 4 * hh))[None, :]

    alog_row = decay_row(dn_a_log[0])
    dt_row = decay_row(dn_dt_bias[0])

    conv_w = dn_conv_w[0]
    prep_c = _dn_prep(projc_qkvz.reshape(b, n_ctx, -1), conv_w, projc_ab.reshape(b, n_ctx, LANES),
                      alog_row, dt_row, n_heads=hh, dk=dk, row_w=n_ctx, tt=n_ctx)
    prep_l = _dn_prep(proj_qkvz.reshape(b, n, -1), conv_w, proj_ab.reshape(b, n, LANES),
                      alog_row, dt_row, n_heads=hh, dk=dk, row_w=GRID_W, tt=4 * GRID_W)
    hp = 4 if hh % 4 == 0 else (2 if hh % 2 == 0 else 1)
    s_zero = jnp.zeros((2, b, hh, dk, dk), F32)
    _, _, s_ctx = _dn_scan(*prep_c, s_zero, dk=dk, tt=n_ctx, hp=hp)
    o_f, o_b, _ = _dn_scan(*prep_l, s_ctx, dk=dk, tt=4 * GRID_W, hp=hp)

    filt = _hy_filters(n, hy_w1[0], hy_b1[0], hy_w2[0], hy_b2[0], hy_w3[0], hy_b3[0], hy_freq[0], hy_w)
    tables = _dft_tables(n)
    hy3 = proj_hy.reshape(b, n, 3 * hy_w)
    cwh = hy_conv_w[0]
    cb = hy_w // LANES
    y1 = _hy_conv(hy3, 0, hy3, cb, filt, 0, 2 * cb, hy_skip[0, 0:1], cwh, cwh, tables, conv_a=True, hy_w=hy_w)
    y_hy = _hy_conv(y1, 0, hy3, 2 * cb, filt, cb, 3 * cb, hy_skip[0, 1:2], cwh, cwh, tables, conv_a=False, hy_w=hy_w)

    merged = _merge(o_f.reshape(b * n, qk_w), o_b.reshape(b * n, qk_w), proj_qkvz, off_z // qk_w,
                    y_hy.reshape(b * n, hy_w), proj_gate, dn_norm_w[0][None, :],
                    w_branch_dn[0].astype(BF16), w_branch_hy[0].astype(BF16), dv=dk, tm=256)
    sk = peer_subkeys[0].reshape(p_heads * 2, nkeys, half).astype(BF16)
    x1, h2, scores = _outproj(merged, x2d, w_out[0].astype(BF16), g1, ln1_g[0][None, :], ln1_b[0][None, :],
                              sc2, sh2, peer_wq[0].astype(BF16), sk, tm=256)

    count1, rank2, e1, e2 = _peer_topk(scores, p_heads, tm=256)
    out = _peer_dense(h2.T, peer_u[0].astype(BF16), peer_v[0].astype(BF16), rank2, e2, count1, e1,
                      x1, g2, ln2_g[0][None, :], ln2_b[0][None, :], n_heads=p_heads, ti=SUBLANES, tm=512)
    stages = dict(o_f=o_f, o_b=o_b, s_ctx=s_ctx, filt=filt, y1=y1, y_hy=y_hy, merged=merged, x1=x1, scores=scores,
                  count1=count1, rank2=rank2, e1=e1, e2=e2)
    return out.reshape(b, n, d), stages


def kernel(x, c, ctx, c_ctx, w_ada, b_ada, w_in, dn_conv_w, dn_a_log, dn_dt_bias, dn_norm_w, hy_conv_w, hy_w1, hy_b1, hy_w2, hy_b2, hy_w3, hy_b3, hy_freq, hy_skip, w_branch_dn, w_branch_hy, w_out, ln1_g, ln1_b, peer_wq, peer_subkeys, peer_u, peer_v, ln2_g, ln2_b):
    out, _ = _block(x, c, ctx, c_ctx, w_ada, b_ada, w_in, dn_conv_w, dn_a_log, dn_dt_bias, dn_norm_w, hy_conv_w, hy_w1, hy_b1, hy_w2, hy_b2, hy_w3, hy_b3, hy_freq, hy_skip, w_branch_dn, w_branch_hy, w_out, ln1_g, ln1_b, peer_wq, peer_subkeys, peer_u, peer_v, ln2_g, ln2_b)
    return out
```

```python
import functools
import math

import jax
import jax.numpy as jnp
from jax import lax
from jax.experimental import pallas as pl
from jax.experimental.pallas import tpu as pltpu

F32 = jnp.float32
BF16 = jnp.bfloat16

GRID_W = 64
DN_CHUNK = 64
PEER_TOPK = 16
PEER_BAND = 32
DEPTH = 1
DEEPNORM_ALPHA = (2 * DEPTH) ** 0.25
LN_EPS = 1e-5
RMS_EPS = 1e-6
L2_EPS = 1e-6
HY_DECAY_TARGET = 1e-2
HY_FAST_DECAY = 0.3
HY_SLOW_DECAY = 1.5

LANES = 128
SUBLANES = 8
VMEM_LIMIT_BYTES = 56 * 1024 * 1024

FFT_N2 = 64
FFT_ROW_PAD = 8
FFT_UNROLL = 4


def _cparams(sem):
    return pltpu.CompilerParams(dimension_semantics=sem, vmem_limit_bytes=VMEM_LIMIT_BYTES)


def _silu(x):
    return x * jax.nn.sigmoid(x)


def _ada_kernel(c_ref, w_ref, b_ref, o_ref):
    s = _silu(c_ref[...])
    o_ref[...] = jnp.dot(s.astype(BF16), w_ref[...].astype(BF16), preferred_element_type=F32) + b_ref[...]


def _ada(cc, w, b):
    d, n6 = w.shape
    tn = min(n6, 1536)
    return pl.pallas_call(
        _ada_kernel,
        grid=(n6 // tn,),
        in_specs=[pl.BlockSpec((SUBLANES, d), lambda j: (0, 0)),
                  pl.BlockSpec((d, tn), lambda j: (0, j)),
                  pl.BlockSpec((1, tn), lambda j: (0, j))],
        out_specs=pl.BlockSpec((SUBLANES, tn), lambda j: (0, j)),
        out_shape=jax.ShapeDtypeStruct((SUBLANES, n6), F32),
        compiler_params=_cparams(("arbitrary",)),
    )(cc, w, b)


def _modmm_kernel(x_ref, sc_ref, sh_ref, w_ref, o_ref, h_scr):
    @pl.when(pl.program_id(1) == 0)
    def _():
        h_scr[...] = (x_ref[...] * (1.0 + sc_ref[0]) + sh_ref[0]).astype(BF16)

    o_ref[...] = jnp.dot(h_scr[...], w_ref[...], preferred_element_type=F32).astype(o_ref.dtype)


def _modmm(x2d, sc, sh, w, *, tm, tn, out_dtype=F32):
    m, d = x2d.shape
    ng = w.shape[1]
    r = sc.shape[0]
    tm = min(tm, m // r)
    tn = min(tn, ng)
    tiles_per_mod = (m // r) // tm
    return pl.pallas_call(
        _modmm_kernel,
        grid=(m // tm, ng // tn),
        in_specs=[pl.BlockSpec((tm, d), lambda i, j: (i, 0)),
                  pl.BlockSpec((1, 1, d), lambda i, j: (i // tiles_per_mod, 0, 0)),
                  pl.BlockSpec((1, 1, d), lambda i, j: (i // tiles_per_mod, 0, 0)),
                  pl.BlockSpec((d, tn), lambda i, j: (0, j))],
        out_specs=pl.BlockSpec((tm, tn), lambda i, j: (i, j)),
        out_shape=jax.ShapeDtypeStruct((m, ng), out_dtype),
        scratch_shapes=[pltpu.VMEM((tm, d), BF16)],
        compiler_params=_cparams(("parallel", "arbitrary")),
    )(x2d, sc, sh, w)


def _row_conv(x, cw, pos, row_w):
    rows = x.shape[0]
    k = cw.shape[0]
    pad = k // 2
    acc = None
    for j in range(k):
        off = j - pad
        if off == 0:
            term = x * cw[j:j + 1]
        else:
            shifted = pltpu.roll(x, (-off) % rows, axis=0)
            valid = (pos >= -off) if off < 0 else (pos < row_w - off)
            term = jnp.where(valid, shifted, 0.0) * cw[j:j + 1]
        acc = term if acc is None else acc + term
    return acc


SOLVE_BLOCK = 16


def _tri_inverse_pairs(pairs):
    c = pairs[0][0].shape[0]
    sb = SOLVE_BLOCK
    nb = c // sb
    ri = lax.broadcasted_iota(jnp.int32, (c, 2 * c), 0)
    ci = lax.broadcasted_iota(jnp.int32, (c, 2 * c), 1)
    ci = jnp.where(ci >= c, ci - c, ci)
    lane_lo = lax.broadcasted_iota(jnp.int32, (1, 2 * c), 1) < c
    off_diag = ri // sb != ci // sb
    eye2 = (ri == ci).astype(F32)
    mcats = [jnp.concatenate([m_a, m_b], axis=1) for m_a, m_b in pairs]
    m_offs = [jnp.where(off_diag, m, 0.0).astype(BF16) for m in mcats]
    xs = [[eye2[i * sb:(i + 1) * sb] for i in range(nb)] for _ in pairs]
    for bi in range(nb):
        rs = slice(bi * sb, (bi + 1) * sb)
        blks = [x[bi] for x in xs]
        if bi > 0:
            for s, x in enumerate(xs):
                xcur = jnp.concatenate(x, axis=0)
                xbd = jnp.concatenate([jnp.where(lane_lo, xcur, 0.0), jnp.where(lane_lo, 0.0, xcur)], axis=0)
                blks[s] = blks[s] - jnp.dot(m_offs[s][rs], xbd.astype(BF16), preferred_element_type=F32)
        mrs = [m[rs] for m in mcats]
        for jj in range(sb - 1):
            col = bi * sb + jj
            src = jnp.broadcast_to(jnp.where(lane_lo, col, c + col), (sb, 2 * c))
            for s in range(len(pairs)):
                mult = jnp.take_along_axis(mrs[s], src, axis=1)
                blks[s] = blks[s] - mult * blks[s][jj:jj + 1, :]
        for s, x in enumerate(xs):
            x[bi] = blks[s]
    return [jnp.concatenate(x, axis=0) for x in xs]


def _dn_gates_kernel(ab_ref, alog_ref, dt_ref, gc_ref, beta_ref, *, n_heads):
    tt = ab_ref.shape[1]
    c = DN_CHUNK
    cpos = lax.broadcasted_iota(jnp.int32, (tt, 1), 0) % c
    ab = ab_ref[0]
    z = ab + dt_ref[...]
    softplus = jnp.maximum(z, 0.0) + jnp.log1p(jnp.exp(-jnp.abs(z)))
    g_all = -jnp.exp(alog_ref[...]) * softplus
    gf = g_all
    gb = g_all
    d = 1
    while d < c:
        gf = gf + jnp.where(cpos >= d, pltpu.roll(gf, d, axis=0), 0.0)
        gb = gb + jnp.where(cpos < c - d, pltpu.roll(gb, (tt - d) % tt, axis=0), 0.0)
        d *= 2
    lane = lax.broadcasted_iota(jnp.int32, (1, LANES), 1)
    gc_ref[0] = jnp.where(lane < 2 * n_heads, gf, gb)
    beta_ref[0] = jax.nn.sigmoid(ab)


def _dn_gates(ab, alog_row, dt_row, *, n_heads, tt):
    b, n, _ = ab.shape
    blk = pl.BlockSpec((1, tt, LANES), lambda bi, g: (bi, g, 0))
    row = pl.BlockSpec((1, LANES), lambda bi, g: (0, 0))
    shp = jax.ShapeDtypeStruct((b, n, LANES), F32)
    return pl.pallas_call(
        functools.partial(_dn_gates_kernel, n_heads=n_heads),
        grid=(b, n // tt),
        in_specs=[blk, row, row],
        out_specs=[blk, blk],
        out_shape=[shp, shp],
        compiler_params=_cparams(("parallel", "parallel")),
    )(ab, alog_row, dt_row)


def _dn_prep_kernel(xq_ref, xk_ref, xv_ref, cq_ref, ck_ref, cv_ref, gc_ref, beta_ref,
                    u_ref, w_ref, qg_ref, kd_ref, attn_ref, eg_ref, *, row_w, n_heads, dk, hp):
    tt = xq_ref.shape[1]
    c = DN_CHUNK
    rows = lax.broadcasted_iota(jnp.int32, (tt, 1), 0)
    pos = rows % row_w
    lane = lax.broadcasted_iota(jnp.int32, (1, LANES), 1)
    gc = gc_ref[0]
    beta_all = beta_ref[0]
    eg_ref[...] = jnp.zeros(eg_ref.shape, F32)
    systems = []
    for hh in range(hp):
        systems += _dn_prep_head(pl.program_id(1) * hp + hh, hh, slice(hh * dk, (hh + 1) * dk), pos, lane, gc,
                                 beta_all, xq_ref, xk_ref, xv_ref, cq_ref, ck_ref, cv_ref, qg_ref, kd_ref, attn_ref,
                                 eg_ref, row_w=row_w, n_heads=n_heads, dk=dk, tt=tt)
    tinvs = _tri_inverse_pairs([(m_f, m_bt) for m_f, m_bt, _, _, _, _ in systems])
    zero_rows = jnp.zeros((c, 2 * dk), BF16)
    for tinv, (_, _, rhs_f, rhs_b, sl, hs) in zip(tinvs, systems):
        tinv = tinv.astype(BF16)
        sol_f = jnp.dot(tinv, jnp.concatenate([rhs_f, zero_rows], axis=0), preferred_element_type=F32)
        sol_b = lax.dot_general(tinv, rhs_b, (((0,), (0,)), ((), ())), preferred_element_type=F32)[c:]
        for dr, sol in enumerate((sol_f, sol_b)):
            u_ref[dr, 0, sl, hs] = sol[:, :dk]
            w_ref[dr, 0, sl, hs] = sol[:, dk:].astype(BF16)


def _dn_prep_head(h, hh, hs, pos, lane, gc, beta_all, xq_ref, xk_ref, xv_ref, cq_ref, ck_ref, cv_ref,
                  qg_ref, kd_ref, attn_ref, eg_ref, *, row_w, n_heads, dk, tt):
    c = DN_CHUNK
    systems = []

    def conv_silu(x_ref, cw_ref):
        return _silu(_row_conv(x_ref[0, :, hs], cw_ref[:, hs], pos, row_w))

    def l2n(y):
        return y * lax.rsqrt(jnp.sum(y * y, axis=-1, keepdims=True) + L2_EPS)

    q = l2n(conv_silu(xq_ref, cq_ref)) * (dk ** -0.5)
    k = l2n(conv_silu(xk_ref, ck_ref))
    v = conv_silu(xv_ref, cv_ref)

    def col(a, l):
        return jnp.sum(jnp.where(lane == l, a, 0.0), axis=1, keepdims=True)

    g_cols = (col(gc, h), col(gc, 2 * n_heads + h))
    b_cols = (col(beta_all, n_heads + h), col(beta_all, 3 * n_heads + h))

    ri = lax.broadcasted_iota(jnp.int32, (c, c), 0)
    ci = lax.broadcasted_iota(jnp.int32, (c, c), 1)
    eye = ri == ci
    lower = ri >= ci
    upper = ri <= ci
    strict_lower = ri > ci
    nt = (((1,), (1,)), ((), ()))
    for cc in range(tt // c):
        sl = slice(cc * c, (cc + 1) * c)
        qc, kc, vc = q[sl], k[sl], v[sl]
        kcb = kc.astype(BF16)
        gcols = [g_cols[dr][sl] for dr in range(2)]
        grows = [jnp.sum(jnp.where(eye, g, 0.0), axis=0, keepdims=True) for g in gcols]
        kbs = [kc * b_cols[dr][sl] for dr in range(2)]
        vbs = [vc * b_cols[dr][sl] for dr in range(2)]
        egs = [jnp.exp(g) for g in gcols]
        dif_f = gcols[0] - grows[0]
        dif_b = gcols[1] - grows[1]
        decay_f = jnp.where(lower, jnp.exp(jnp.where(lower, dif_f, 0.0)), 0.0)
        decay_b = jnp.where(upper, jnp.exp(jnp.where(upper, dif_b, 0.0)), 0.0)
        decay_bt = jnp.where(lower, jnp.exp(jnp.where(lower, -dif_b, 0.0)), 0.0)
        kq = lax.dot_general(jnp.concatenate([kbs[0], qc], axis=0).astype(BF16), kcb, nt,
                             preferred_element_type=F32)
        kkt_b = lax.dot_general(kcb, kbs[1].astype(BF16), nt, preferred_element_type=F32)
        m_f = jnp.where(strict_lower, kq[:c] * decay_f, 0.0)
        m_bt = jnp.where(strict_lower, kkt_b * decay_bt, 0.0)
        rhs_f = jnp.concatenate([vbs[0], kbs[0] * egs[0]], axis=1).astype(BF16)
        rhs_b = jnp.concatenate([vbs[1], kbs[1] * egs[1]], axis=1).astype(BF16)
        systems.append((m_f, m_bt, rhs_f, rhs_b, sl, hs))
        attns = (kq[c:] * decay_f, kq[c:] * decay_b)
        for dr in range(2):
            gcol = gcols[dr]
            glast = gcol[c - 1:c] if dr == 0 else gcol[0:1]
            qg_ref[dr, 0, sl, hs] = (qc * egs[dr]).astype(BF16)
            kd_ref[dr, 0, sl, hs] = (kc * jnp.exp(glast - gcol)).astype(BF16)
            attn_ref[dr, 0, hh, sl, :] = attns[dr].astype(BF16)
            eg_ref[dr, 0, hh, 0, cc:cc + 1, :] = jnp.broadcast_to(jnp.exp(glast), (1, LANES))
    return systems


def _dn_prep(qkv, conv_w, ab, alog_row, dt_row, *, n_heads, dk, row_w, tt):
    gc, beta = _dn_gates(ab, alog_row, dt_row, n_heads=n_heads, tt=tt)
    b, n, _ = qkv.shape
    hh = n_heads
    ng = n // tt
    c = DN_CHUNK
    assert tt % row_w == 0 and tt % c == 0 and tt // c <= SUBLANES
    hp = 2 if hh % 2 == 0 else 1
    nhb = hh // hp
    kern = functools.partial(_dn_prep_kernel, row_w=row_w, n_heads=hh, dk=dk, hp=hp)
    xspec = lambda off: pl.BlockSpec((1, tt, hp * dk), lambda bi, h, g: (bi, g, off + h))
    cspec = lambda off: pl.BlockSpec((conv_w.shape[0], hp * dk), lambda bi, h, g: (0, off + h))
    ospec = pl.BlockSpec((2, 1, tt, hp * dk), lambda bi, h, g: (0, bi, g, h))
    return pl.pallas_call(
        kern,
        grid=(b, nhb, ng),
        in_specs=[xspec(0), xspec(nhb), xspec(2 * nhb), cspec(0), cspec(nhb), cspec(2 * nhb),
                  pl.BlockSpec((1, tt, LANES), lambda bi, h, g: (bi, g, 0)),
                  pl.BlockSpec((1, tt, LANES), lambda bi, h, g: (bi, g, 0))],
        out_specs=[ospec, ospec, ospec, ospec,
                   pl.BlockSpec((2, 1, hp, tt, c), lambda bi, h, g: (0, bi, h, g, 0)),
                   pl.BlockSpec((2, 1, hp, 1, SUBLANES, LANES), lambda bi, h, g: (0, bi, h, g, 0, 0))],
        out_shape=[jax.ShapeDtypeStruct((2, b, n, hh * dk), F32),
                   jax.ShapeDtypeStruct((2, b, n, hh * dk), BF16),
                   jax.ShapeDtypeStruct((2, b, n, hh * dk), BF16),
                   jax.ShapeDtypeStruct((2, b, n, hh * dk), BF16),
                   jax.ShapeDtypeStruct((2, b, hh, n, c), BF16),
                   jax.ShapeDtypeStruct((2, b, hh, ng, SUBLANES, LANES), F32)],
        compiler_params=_cparams(("parallel", "parallel", "arbitrary")),
    )(qkv, qkv, qkv, conv_w, conv_w, conv_w, gc, beta)


def _dn_scan_kernel(uf_ref, wf_ref, qf_ref, kf_ref, af_ref, ef_ref,
                    ub_ref, wb_ref, qb_ref, kb_ref, ab_ref, eb_ref, s0_ref,
                    of_ref, ob_ref, sout_ref, s_scr, *, hp, dk):
    g = pl.program_id(2)
    ng = pl.num_programs(2)
    c = DN_CHUNK
    tt = uf_ref.shape[2]
    ncs = tt // c

    @pl.when(g == 0)
    def _():
        s_scr[...] = s0_ref[:, 0]

    refs = ((uf_ref, wf_ref, qf_ref, kf_ref, af_ref, ef_ref, of_ref),
            (ub_ref, wb_ref, qb_ref, kb_ref, ab_ref, eb_ref, ob_ref))
    chains = [(dr, hh) for dr in range(2) for hh in range(hp)]
    state = {ch: s_scr[ch[0], ch[1]] for ch in chains}
    for step in range(ncs):
        rows = {}
        for dr, hh in chains:
            cc = step if dr == 0 else ncs - 1 - step
            rows[dr, hh] = (slice(cc * c, (cc + 1) * c), slice(hh * dk, (hh + 1) * dk), cc)
        r = {}
        for ch in chains:
            w_ref, q_ref = refs[ch[0]][1], refs[ch[0]][2]
            sl, hs, _ = rows[ch]
            wq = jnp.concatenate([w_ref[0, 0, sl, hs], q_ref[0, 0, sl, hs]], axis=0)
            r[ch] = jnp.dot(wq, state[ch].astype(BF16), preferred_element_type=F32)
        vnb = {}
        for ch in chains:
            sl, hs, _ = rows[ch]
            vnb[ch] = (refs[ch[0]][0][0, 0, sl, hs] - r[ch][:c]).astype(BF16)
        for ch in chains:
            _, _, _, k_ref, a_ref, e_ref, o_ref = refs[ch[0]]
            sl, hs, cc = rows[ch]
            o_ref[0, sl, hs] = r[ch][c:] + jnp.dot(a_ref[0, 0, ch[1], sl, :], vnb[ch], preferred_element_type=F32)
            upd = lax.dot_general(k_ref[0, 0, sl, hs], vnb[ch], (((0,), (0,)), ((), ())),
                                  preferred_element_type=F32)
            state[ch] = state[ch] * e_ref[0, 0, ch[1], 0, cc:cc + 1, :] + upd
    for ch in chains:
        s_scr[ch[0], ch[1]] = state[ch]

    @pl.when(g == ng - 1)
    def _():
        sout_ref[:, 0] = s_scr[...]


def _dn_scan(u, w, qg, kd, attn, eg, s0, *, dk, tt, hp):
    _, b, n, hd = u.shape
    hh = hd // dk
    ng = n // tt
    c = DN_CHUNK
    kern = functools.partial(_dn_scan_kernel, hp=hp, dk=dk)

    def specs(dr):
        gi = (lambda g: g) if dr == 0 else (lambda g: ng - 1 - g)
        tok = pl.BlockSpec((1, 1, tt, hp * dk), lambda bi, h, g: (dr, bi, gi(g), h))
        return [tok, tok, tok, tok,
                pl.BlockSpec((1, 1, hp, tt, c), lambda bi, h, g: (dr, bi, h, gi(g), 0)),
                pl.BlockSpec((1, 1, hp, 1, SUBLANES, LANES), lambda bi, h, g: (dr, bi, h, gi(g), 0, 0))]

    sspec = pl.BlockSpec((2, 1, hp, dk, dk), lambda bi, h, g: (0, bi, h, 0, 0))
    return pl.pallas_call(
        kern,
        grid=(b, hh // hp, ng),
        in_specs=specs(0) + specs(1) + [sspec],
        out_specs=[pl.BlockSpec((1, tt, hp * dk), lambda bi, h, g: (bi, g, h)),
                   pl.BlockSpec((1, tt, hp * dk), lambda bi, h, g: (bi, ng - 1 - g, h)),
                   sspec],
        out_shape=[jax.ShapeDtypeStruct((b, n, hd), F32),
                   jax.ShapeDtypeStruct((b, n, hd), F32),
                   jax.ShapeDtypeStruct((2, b, hh, dk, dk), F32)],
        scratch_shapes=[pltpu.VMEM((2, hp, dk, dk), F32)],
        compiler_params=_cparams(("parallel", "parallel", "arbitrary")),
    )(u, w, qg, kd, attn, eg, u, w, qg, kd, attn, eg, s0)


def _hy_hidden_kernel(z_ref, w1_ref, b1_ref, w2_ref, b2_ref, fr_ref, o_ref):
    fr = fr_ref[...]
    h1 = jnp.sin(fr * (jnp.dot(z_ref[...].astype(BF16), w1_ref[...].astype(BF16),
                               preferred_element_type=F32) + b1_ref[...]))
    o_ref[...] = jnp.sin(fr * (jnp.dot(h1.astype(BF16), w2_ref[...].astype(BF16),
                                       preferred_element_type=F32) + b2_ref[...]))


def _hy_filter_kernel(hid_ref, w3_ref, b3_ref, t_ref, delta_ref, o_ref):
    f = jnp.dot(hid_ref[...].astype(BF16), w3_ref[...].astype(BF16), preferred_element_type=F32) + b3_ref[...]
    o_ref[...] = f * jnp.exp(-t_ref[...] * delta_ref[...])


def _hy_filters(n, w1, b1, w2, b2, w3, b3, freq, hy_w):
    emb, hid = w1.shape
    bands = (emb - 1) // 2
    t = jnp.linspace(0.0, 1.0, n, dtype=F32)
    pos = jnp.arange(n, dtype=F32)
    bnd = jnp.linspace(1e-4, bands - 1, bands, dtype=F32)
    ang = (2.0 * math.pi / n) * pos[:, None] * bnd[None, :]
    z = jnp.concatenate([t[:, None], jnp.cos(ang), -jnp.sin(ang)], axis=-1)
    embp = -(-emb // SUBLANES) * SUBLANES
    z = jnp.pad(z, ((0, 0), (0, embp - emb)))
    w1p = jnp.pad(w1, ((0, embp - emb), (0, 0)))
    full = lambda a: pl.BlockSpec(a.shape, lambda: (0,) * a.ndim)
    args = (z, w1p, b1[None, :], w2, b2[None, :], freq[None, :])
    hid2 = pl.pallas_call(
        _hy_hidden_kernel,
        in_specs=[full(a) for a in args],
        out_specs=pl.BlockSpec((n, hid), lambda: (0, 0)),
        out_shape=jax.ShapeDtypeStruct((n, hid), F32),
        compiler_params=pltpu.CompilerParams(vmem_limit_bytes=VMEM_LIMIT_BYTES),
    )(*args)
    nout = w3.shape[1]
    deltas = jnp.abs(jnp.linspace(math.log(HY_DECAY_TARGET) / HY_SLOW_DECAY,
                                  math.log(HY_DECAY_TARGET) / HY_FAST_DECAY, hy_w, dtype=F32))
    delta_row = jnp.tile(deltas, nout // hy_w)[None, :]
    tc = min(nout, 1024)
    return pl.pallas_call(
        _hy_filter_kernel,
        grid=(nout // tc,),
        in_specs=[pl.BlockSpec((n, hid), lambda j: (0, 0)),
                  pl.BlockSpec((hid, tc), lambda j: (0, j)),
                  pl.BlockSpec((1, tc), lambda j: (0, j)),
                  pl.BlockSpec((n, 1), lambda j: (0, 0)),
                  pl.BlockSpec((1, tc), lambda j: (0, j))],
        out_specs=pl.BlockSpec((n, tc), lambda j: (0, j)),
        out_shape=jax.ShapeDtypeStruct((n, nout), F32),
        compiler_params=_cparams(("parallel",)),
    )(hid2, w3, b3[None, :], t[:, None], delta_row)


def _dft_tables(n):
    n2 = FFT_N2
    big = 2 * n
    n1 = big // n2
    k1 = n1 // 2
    two_pi = 2.0 * math.pi

    def cs(idx, period):
        a = (idx % period).astype(F32) * (two_pi / period)
        return jnp.cos(a), jnp.sin(a)

    f1 = jnp.arange(n1, dtype=jnp.int32)
    s1 = jnp.arange(k1, dtype=jnp.int32)
    s2 = jnp.arange(n2, dtype=jnp.int32)
    s_full = s1[None, None, :] * n2 + s2[:, None, None]
    c, s = cs(f1[None, :, None] * s_full, big)
    f1tw = jnp.concatenate([jnp.concatenate([c, s], axis=2), jnp.concatenate([-s, c], axis=2)], axis=1)
    ct, st = jnp.swapaxes(c, 1, 2), jnp.swapaxes(s, 1, 2)
    g1tw = jnp.concatenate([jnp.concatenate([ct, -st], axis=2), jnp.concatenate([st, ct], axis=2)], axis=1) / big
    c2, sn2 = cs(s2[:, None] * s2[None, :], n2)
    f2 = jnp.concatenate([jnp.concatenate([c2, sn2], axis=1), jnp.concatenate([-sn2, c2], axis=1)], axis=0)
    g2 = jnp.concatenate([jnp.concatenate([c2, -sn2], axis=1), jnp.concatenate([sn2, c2], axis=1)], axis=0)
    return f1tw.astype(BF16), f2.astype(BF16), g2.astype(BF16), g1tw.astype(BF16)


def _hy_conv_kernel(a_ref, m_ref, hf_ref, hb_ref, skip_ref, cwa_ref, cwm_ref,
                    f1_ref, f2_ref, g2_ref, g1_ref, o_ref,
                    h_scr, b_scr, zr_scr, zi_scr, *, conv_a, n1, k1):
    n2 = FFT_N2
    k1p = k1 + FFT_ROW_PAD
    pb = 2 * n2 + FFT_ROW_PAD
    pos = lax.broadcasted_iota(jnp.int32, (n2, 1), 0)

    grp = FFT_UNROLL
    cw = b_scr.shape[1]

    def stage1(real_only):
        def body(g, carry):
            s2s = [g * grp + t for t in range(grp)]
            bases = [pl.multiple_of(s2 * k1p, SUBLANES) for s2 in s2s]
            if real_only:
                xs = [zr_scr[pl.ds(base, k1), :].astype(BF16) for base in bases]
                outs = [jnp.dot(f1_ref[s2, :, :k1], x, preferred_element_type=F32) for s2, x in zip(s2s, xs)]
            else:
                xs = [jnp.concatenate([zr_scr[pl.ds(base, k1), :], zi_scr[pl.ds(base, k1), :]], axis=0).astype(BF16)
                      for base in bases]
                outs = [jnp.dot(f1_ref[s2], x, preferred_element_type=F32) for s2, x in zip(s2s, xs)]
            for s2, a in zip(s2s, outs):
                b_scr[pl.ds(s2, n1, stride=pb), :] = a[:n1]
                b_scr[pl.ds(n2 + s2, n1, stride=pb), :] = a[n1:]
            return carry
        lax.fori_loop(0, n2 // grp, body, 0)

    def slab_pairs(g):
        f1s = [[(g * grp + t) * 2 + u for u in range(2)] for t in range(grp)]
        bbs = [[pl.multiple_of(f1 * pb, SUBLANES) for f1 in pair] for pair in f1s]
        hbs = [[pl.multiple_of(f1 * 2 * n2, 2 * n2) for f1 in pair] for pair in f1s]
        return bbs, hbs

    def forward2(bbs):
        bpairs = [jnp.concatenate([b_scr[pl.ds(bb, 2 * n2), :] for bb in pair], axis=1).astype(BF16) for pair in bbs]
        return [jnp.dot(f2_ref[...], bp, preferred_element_type=F32) for bp in bpairs]

    @pl.when(pl.program_id(1) == 0)
    def _():
        for which, ref in ((0, hf_ref), (1, hb_ref)):
            def load(s1, carry, ref=ref):
                r0 = pl.multiple_of(s1 * n2, n2)
                zr_scr[pl.ds(s1, n2, stride=k1p), :] = ref[pl.ds(r0, n2), :]
                return carry
            lax.fori_loop(0, k1, load, 0, unroll=FFT_UNROLL)
            if which == 1:
                zr_scr[0:1, :] = jnp.zeros((1, zr_scr.shape[1]), F32)
            stage1(True)

            def spec(g, carry, which=which):
                bbs, hbs = slab_pairs(g)
                zpairs = forward2(bbs)
                for zpair, hpair in zip(zpairs, hbs):
                    for t, hb_ in enumerate(hpair):
                        zz = zpair[:, t * cw:(t + 1) * cw]
                        if which == 0:
                            h_scr[pl.ds(hb_, 2 * n2), :] = zz
                        else:
                            h_scr[pl.ds(hb_, n2), :] = h_scr[pl.ds(hb_, n2), :] + zz[:n2]
                            h_scr[pl.ds(hb_ + n2, n2), :] = h_scr[pl.ds(hb_ + n2, n2), :] - zz[n2:]
                return carry
            lax.fori_loop(0, n1 // (2 * grp), spec, 0)

    def load_a(s1, carry):
        r0 = pl.multiple_of(s1 * n2, n2)
        for bi, scr in ((0, zr_scr), (1, zi_scr)):
            blk = a_ref[bi, pl.ds(r0, n2), :]
            if conv_a:
                blk = _row_conv(blk, cwa_ref[...], pos, n2)
            scr[pl.ds(s1, n2, stride=k1p), :] = blk
        return carry
    lax.fori_loop(0, k1, load_a, 0, unroll=FFT_UNROLL)

    stage1(False)

    def mid(g, carry):
        bbs, hbs = slab_pairs(g)
        zzs = forward2(bbs)
        yys = []
        for zz, hpair in zip(zzs, hbs):
            zr, zi = zz[:n2], zz[n2:]
            hr = jnp.concatenate([h_scr[pl.ds(hb_, n2), :] for hb_ in hpair], axis=1)
            hi = jnp.concatenate([h_scr[pl.ds(hb_ + n2, n2), :] for hb_ in hpair], axis=1)
            yys.append(jnp.concatenate([zr * hr - zi * hi, zr * hi + zi * hr], axis=0).astype(BF16))
        wws = [jnp.dot(g2_ref[...], yy, preferred_element_type=F32) for yy in yys]
        for ww, pair in zip(wws, bbs):
            for t, bb in enumerate(pair):
                b_scr[pl.ds(bb, 2 * n2), :] = ww[:, t * cw:(t + 1) * cw]
        return carry
    lax.fori_loop(0, n1 // (2 * grp), mid, 0)

    skip = skip_ref[...]

    def last(g, carry):
        s2s = [g * grp + t for t in range(grp)]
        wws = [jnp.concatenate([b_scr[pl.ds(s2, n1, stride=pb), :],
                                b_scr[pl.ds(n2 + s2, n1, stride=pb), :]], axis=0).astype(BF16) for s2 in s2s]
        ys = [jnp.dot(g1_ref[s2], ww, preferred_element_type=F32) for s2, ww in zip(s2s, wws)]
        for s2, y in zip(s2s, ys):
            base = pl.multiple_of(s2 * k1p, SUBLANES)
            zr_scr[pl.ds(base, k1), :] = y[:k1] + zr_scr[pl.ds(base, k1), :] * skip
            zi_scr[pl.ds(base, k1), :] = y[k1:] + zi_scr[pl.ds(base, k1), :] * skip
        return carry
    lax.fori_loop(0, n2 // grp, last, 0)

    def fin(s1, carry):
        r0 = pl.multiple_of(s1 * n2, n2)
        for bi, scr in ((0, zr_scr), (1, zi_scr)):
            mm = _row_conv(m_ref[bi, pl.ds(r0, n2), :], cwm_ref[...], pos, n2)
            o_ref[bi, pl.ds(r0, n2), :] = mm * scr[pl.ds(s1, n2, stride=k1p), :]
        return carry
    lax.fori_loop(0, k1, fin, 0, unroll=FFT_UNROLL)


def _hy_conv(a, a_col0, m, m_col0, filt, f_col_fwd, f_col_bwd, skip, cwa, cwm, tables, *, conv_a, hy_w):
    b, n, _ = a.shape
    assert b % 2 == 0 and GRID_W == FFT_N2 and n % FFT_N2 == 0
    n2 = FFT_N2
    n1 = 2 * n // n2
    k1 = n1 // 2
    k1p = k1 + FFT_ROW_PAD
    pb = 2 * n2 + FFT_ROW_PAD
    cw = LANES
    f1tw, f2, g2, g1tw = tables
    kern = functools.partial(_hy_conv_kernel, conv_a=conv_a, n1=n1, k1=k1)
    once = pl.Buffered(1)
    return pl.pallas_call(
        kern,
        grid=(hy_w // cw, b // 2),
        in_specs=[pl.BlockSpec((2, n, cw), lambda c, p: (p, 0, a_col0 + c), pipeline_mode=once),
                  pl.BlockSpec((2, n, cw), lambda c, p: (p, 0, m_col0 + c), pipeline_mode=once),
                  pl.BlockSpec((n, cw), lambda c, p: (0, f_col_fwd + c), pipeline_mode=once),
                  pl.BlockSpec((n, cw), lambda c, p: (0, f_col_bwd + c), pipeline_mode=once),
                  pl.BlockSpec((1, cw), lambda c, p: (0, c)),
                  pl.BlockSpec((cwa.shape[0], cw), lambda c, p: (0, a_col0 + c if conv_a else 0)),
                  pl.BlockSpec((cwm.shape[0], cw), lambda c, p: (0, m_col0 + c)),
                  pl.BlockSpec(f1tw.shape, lambda c, p: (0, 0, 0), pipeline_mode=once),
                  pl.BlockSpec(f2.shape, lambda c, p: (0, 0)),
                  pl.BlockSpec(g2.shape, lambda c, p: (0, 0)),
                  pl.BlockSpec(g1tw.shape, lambda c, p: (0, 0, 0), pipeline_mode=once)],
        out_specs=pl.BlockSpec((2, n, cw), lambda c, p: (p, 0, c)),
        out_shape=jax.ShapeDtypeStruct((b, n, hy_w), F32),
        scratch_shapes=[pltpu.VMEM((n1 * 2 * n2, cw), F32),
                        pltpu.VMEM((n1 * pb, cw), F32),
                        pltpu.VMEM((n2 * k1p, cw), F32),
                        pltpu.VMEM((n2 * k1p, cw), F32)],
        compiler_params=_cparams(("parallel", "arbitrary")),
    )(a, m, filt, filt, skip, cwa, cwm, f1tw, f2, g2, g1tw)


def _merge_kernel(of_ref, ob_ref, z_ref, y_ref, ga_ref, gb_ref, nw_ref, wa_ref, wh_ref, o_ref, *, dv):
    nw = nw_ref[...]
    heads = []
    for hh in range(of_ref.shape[1] // dv):
        hs = slice(hh * dv, (hh + 1) * dv)
        o = of_ref[:, hs] + ob_ref[:, hs]
        o = o * lax.rsqrt(jnp.mean(o * o, axis=-1, keepdims=True) + RMS_EPS) * nw
        heads.append((o * _silu(z_ref[:, hs])).astype(BF16))
    pa = jnp.dot(jnp.concatenate(heads, axis=1), wa_ref[...], preferred_element_type=F32)
    ph = jnp.dot(y_ref[...].astype(BF16), wh_ref[...], preferred_element_type=F32)
    o_ref[...] = (jax.nn.sigmoid(ga_ref[...]) * pa + jax.nn.sigmoid(gb_ref[...]) * ph).astype(o_ref.dtype)


def _merge(o_f, o_b, z, z_blk, y_hy, gate, norm_w, w_pa, w_ph, *, dv, tm):
    m, dvw = o_f.shape
    d = w_pa.shape[1]
    tm = min(tm, m)
    kern = functools.partial(_merge_kernel, dv=dv)
    row = lambda w_, blk=0: pl.BlockSpec((tm, w_), lambda i: (i, blk))
    const = lambda a: pl.BlockSpec(a.shape, lambda i: (0, 0), pipeline_mode=pl.Buffered(1))
    return pl.pallas_call(
        kern,
        grid=(m // tm,),
        in_specs=[row(dvw), row(dvw), row(dvw, z_blk), row(y_hy.shape[1]), row(d, 0), row(d, 1),
                  pl.BlockSpec((1, dv), lambda i: (0, 0)), const(w_pa), const(w_ph)],
        out_specs=row(d),
        out_shape=jax.ShapeDtypeStruct((m, d), BF16),
        compiler_params=_cparams(("parallel",)),
    )(o_f, o_b, z, y_hy, gate, gate, norm_w, w_pa, w_ph)


def _layer_norm(x, g, b):
    mu = jnp.mean(x, axis=-1, keepdims=True)
    xc = x - mu
    var = jnp.mean(xc * xc, axis=-1, keepdims=True)
    return xc * lax.rsqrt(var + LN_EPS) * g + b


def _outproj_kernel(mg_ref, x_ref, wo_ref, g1_ref, lg_ref, lb_ref, sc_ref, sh_ref, wq_ref, sk_ref,
                    x1_ref, h2_ref, s_ref, *, half):
    mix = jnp.dot(mg_ref[...], wo_ref[...], preferred_element_type=F32)
    x1 = _layer_norm(DEEPNORM_ALPHA * x_ref[...] + g1_ref[0] * mix, lg_ref[...], lb_ref[...])
    x1_ref[...] = x1
    h2 = (x1 * (1.0 + sc_ref[0]) + sh_ref[0]).astype(BF16)
    h2_ref[...] = h2
    qp = jnp.dot(h2, wq_ref[...], preferred_element_type=F32).astype(BF16)
    for j in range(sk_ref.shape[0]):
        js = slice(j * half, (j + 1) * half)
        s_ref[j] = lax.dot_general(sk_ref[j], qp[:, js], (((1,), (1,)), ((), ())), preferred_element_type=F32)


def _outproj(merged, x2d, w_out, g1, ln_g, ln_b, sc2, sh2, wq, subkeys, *, tm):
    m, d = x2d.shape
    r = g1.shape[0]
    tm = min(tm, m // r)
    tpm = (m // r) // tm
    nsk, nkeys, half = subkeys.shape
    assert nkeys == LANES
    kern = functools.partial(_outproj_kernel, half=half)
    row = lambda w_: pl.BlockSpec((tm, w_), lambda i: (i, 0))
    mod = pl.BlockSpec((1, 1, d), lambda i: (i // tpm, 0, 0))
    vec = pl.BlockSpec((1, d), lambda i: (0, 0))
    return pl.pallas_call(
        kern,
        grid=(m // tm,),
        in_specs=[row(d), row(d), pl.BlockSpec(w_out.shape, lambda i: (0, 0), pipeline_mode=pl.Buffered(1)),
                  mod, vec, vec, mod, mod,
                  pl.BlockSpec(wq.shape, lambda i: (0, 0), pipeline_mode=pl.Buffered(1)),
                  pl.BlockSpec(subkeys.shape, lambda i: (0, 0, 0))],
        out_specs=[row(d), row(d), pl.BlockSpec((nsk, nkeys, tm), lambda i: (0, 0, i))],
        out_shape=[jax.ShapeDtypeStruct((m, d), F32),
                   jax.ShapeDtypeStruct((m, d), BF16),
                   jax.ShapeDtypeStruct((nsk, nkeys, m), F32)],
        compiler_params=_cparams(("parallel",)),
    )(merged, x2d, w_out, g1, ln_g, ln_b, sc2, sh2, wq, subkeys)


def _cand_tables():
    pairs = [(a, b) for a in range(PEER_TOPK) for b in range(PEER_TOPK) if (a + 1) * (b + 1) <= PEER_TOPK]
    ia = jnp.array([p[0] for p in pairs], dtype=jnp.int32)
    ib = jnp.array([p[1] for p in pairs], dtype=jnp.int32)
    ncand = len(pairs)
    rows = -(-ncand // (2 * SUBLANES)) * (2 * SUBLANES)
    ranks = jnp.arange(LANES, dtype=jnp.int32)
    p1 = (jnp.pad(ia, (0, rows - ncand), constant_values=-1)[:, None] == ranks[None, :]).astype(F32)
    p2 = (jnp.pad(ib, (0, rows - ncand), constant_values=-1)[:, None] == ranks[None, :]).astype(F32)
    return p1, p2, ncand


def _peer_topk_kernel(s_ref, p1_ref, p2_ref, p1t_ref, c1_ref, r2_ref, e1_ref, e2_ref, *, n_heads, ncand):
    tm = s_ref.shape[2]
    neg = -jnp.inf
    hi = lax.Precision.HIGHEST
    no_rank = float(LANES)

    def pop_max(x, row_id):
        mx = jnp.max(x, axis=0, keepdims=True)
        first = jnp.min(jnp.where(x == mx, row_id, no_rank), axis=0, keepdims=True)
        hit = row_id == first
        return mx, hit, jnp.where(hit, neg, x)

    key_id = lax.broadcasted_iota(jnp.int32, (LANES, tm), 0).astype(F32)
    cand_id = lax.broadcasted_iota(jnp.int32, (p1_ref.shape[0], tm), 0).astype(F32)
    pad_rows = jnp.zeros((LANES - PEER_TOPK, tm), F32)

    def sorted_top(s):
        tops = []
        rank = jnp.full(s.shape, no_rank, F32)
        x = s
        for a in range(PEER_TOPK):
            mx, hit, x = pop_max(x, key_id)
            tops.append(mx)
            rank = jnp.where(hit, float(a), rank)
        return jnp.concatenate(tops + [pad_rows], axis=0), rank

    for h in range(n_heads):
        s1 = s_ref[2 * h]
        s2 = s_ref[2 * h + 1]
        t1, rank1 = sorted_top(s1)
        t2, rank2 = sorted_top(s2)
        cand = (jnp.dot(p1_ref[...], t1, precision=hi, preferred_element_type=F32)
                + jnp.dot(p2_ref[...], t2, precision=hi, preferred_element_type=F32))
        x = jnp.where(cand_id < ncand, cand, neg)
        sel = jnp.zeros(x.shape, F32)
        zsum = None
        cmax = None
        for r in range(PEER_TOPK):
            mx, hit, x = pop_max(x, cand_id)
            sel = jnp.where(hit, 1.0, sel)
            if r == 0:
                cmax = mx
                zsum = jnp.ones_like(mx)
            else:
                zsum = zsum + jnp.exp(mx - cmax)
        cnt = jnp.dot(p1t_ref[...].astype(BF16), sel.astype(BF16), preferred_element_type=F32)
        count1 = jnp.zeros(s1.shape, F32)
        for a in range(PEER_TOPK):
            count1 = jnp.where(rank1 == a, cnt[a:a + 1], count1)
        c1_ref[h] = count1
        r2_ref[h] = rank2.astype(r2_ref.dtype)
        e1_ref[h] = jnp.exp(s1 - t1[0:1]) / zsum
        e2_ref[h] = jnp.exp(s2 - t2[0:1]).astype(e2_ref.dtype)


def _peer_topk(scores, n_heads, *, tm):
    nsk, nkeys, m = scores.shape
    tm = min(tm, m)
    p1, p2, ncand = _cand_tables()
    kern = functools.partial(_peer_topk_kernel, n_heads=n_heads, ncand=ncand)
    full = lambda a: pl.BlockSpec(a.shape, lambda i: (0, 0))
    out = pl.BlockSpec((n_heads, nkeys, tm), lambda i: (0, 0, i))
    shp = lambda dt: jax.ShapeDtypeStruct((n_heads, nkeys, m), dt)
    p1t = p1.T
    return pl.pallas_call(
        kern,
        grid=(m // tm,),
        in_specs=[pl.BlockSpec((nsk, nkeys, tm), lambda i: (0, 0, i)), full(p1), full(p2), full(p1t)],
        out_specs=[out, out, out, out],
        out_shape=[shp(F32), shp(F32), shp(F32), shp(F32)],
        compiler_params=_cparams(("parallel",)),
    )(scores, p1, p2, p1t)


def _peer_dense_kernel(h2t_ref, u_ref, vt_ref, r2_ref, e2_ref, c1_ref, e1_ref, x1_ref, g2_ref, lg_ref, lb_ref,
                       o_ref, acc_scr, a_scr, p_scr, r2_scr, e2_scr, *, n_heads, ti):
    e = pl.program_id(1)

    @pl.when(e == 0)
    def _():
        acc_scr[...] = jnp.zeros(acc_scr.shape, F32)
        r2_scr[...] = r2_ref[...].astype(BF16)
        e2_scr[...] = e2_ref[...].astype(BF16)

    a_scr[...] = jnp.dot(u_ref[...], h2t_ref[...], preferred_element_type=F32)
    tm = h2t_ref.shape[1]
    nkeys = r2_ref.shape[1]
    for ii in range(ti):
        c1_rows = [c1_ref[h, ii:ii + 1, :].astype(BF16) for h in range(n_heads)]
        e1_rows = [e1_ref[h, ii:ii + 1, :].astype(BF16) for h in range(n_heads)]
        for band in range(nkeys // PEER_BAND):
            ks = slice(band * PEER_BAND, (band + 1) * PEER_BAND)
            rs = slice(ii * nkeys + band * PEER_BAND, ii * nkeys + (band + 1) * PEER_BAND)
            gate = None
            for h in range(n_heads):
                keep = r2_scr[h, ks, :] < c1_rows[h]
                term = jnp.where(keep, e2_scr[h, ks, :], jnp.zeros((), BF16)) * e1_rows[h]
                gate = term if gate is None else gate + term
            a = a_scr[rs, :]
            act = 0.5 * a * (1.0 + lax.erf(a * (2.0 ** -0.5)))
            p_scr[rs, :] = gate * act.astype(BF16)
    acc_scr[...] += jnp.dot(vt_ref[...], p_scr[...], preferred_element_type=F32)

    @pl.when(e == pl.num_programs(1) - 1)
    def _():
        o_ref[...] = _layer_norm(DEEPNORM_ALPHA * x1_ref[...] + g2_ref[0] * acc_scr[...].T,
                                 lg_ref[...], lb_ref[...])


def _peer_dense(h2t, u_tab, v_tab, rank2, e2, count1, e1, x1, g2, ln_g, ln_b, *, n_heads, ti, tm):
    d, m = h2t.shape
    r = g2.shape[0]
    tm = min(tm, m // r)
    tpm = (m // r) // tm
    ne = u_tab.shape[0]
    nkeys = rank2.shape[1]
    te = ti * nkeys
    kern = functools.partial(_peer_dense_kernel, n_heads=n_heads, ti=ti)
    row = lambda w_: pl.BlockSpec((tm, w_), lambda i, e: (i, 0), pipeline_mode=pl.Buffered(1))
    vec = pl.BlockSpec((1, d), lambda i, e: (0, 0))
    tbl = pl.BlockSpec((te, d), lambda i, e: (e, 0))
    second = pl.BlockSpec((n_heads, nkeys, tm), lambda i, e: (0, 0, i), pipeline_mode=pl.Buffered(1))
    first = pl.BlockSpec((n_heads, ti, tm), lambda i, e: (0, e, i))
    return pl.pallas_call(
        kern,
        grid=(m // tm, ne // te),
        in_specs=[pl.BlockSpec((d, tm), lambda i, e: (0, i), pipeline_mode=pl.Buffered(1)),
                  tbl, pl.BlockSpec((d, te), lambda i, e: (0, e)), second, second, first, first, row(d),
                  pl.BlockSpec((1, 1, d), lambda i, e: (i // tpm, 0, 0)), vec, vec],
        out_specs=pl.BlockSpec((tm, d), lambda i, e: (i, 0)),
        out_shape=jax.ShapeDtypeStruct((m, d), F32),
        scratch_shapes=[pltpu.VMEM((d, tm), F32), pltpu.VMEM((te, tm), F32), pltpu.VMEM((te, tm), BF16),
                        pltpu.VMEM((n_heads, nkeys, tm), BF16), pltpu.VMEM((n_heads, nkeys, tm), BF16)],
        compiler_params=_cparams(("parallel", "arbitrary")),
    )(h2t, u_tab, v_tab, rank2, e2, count1, e1, x1, g2, ln_g, ln_b)


def _block(x, c, ctx, c_ctx, w_ada, b_ada, w_in, dn_conv_w, dn_a_log, dn_dt_bias, dn_norm_w, hy_conv_w, hy_w1, hy_b1, hy_w2, hy_b2, hy_w3, hy_b3, hy_freq, hy_skip, w_branch_dn, w_branch_hy, w_out, ln1_g, ln1_b, peer_wq, peer_subkeys, peer_u, peer_v, ln2_g, ln2_b):
    b, n, d = x.shape
    n_ctx = ctx.shape[1]
    assert w_ada.shape[0] == DEPTH == 1, "single-layer block: the context stream update is never read"
    hh = dn_a_log.shape[2]
    dk = dn_conv_w.shape[2] // (3 * hh)
    qk_w = hh * dk
    hy_w = hy_skip.shape[2]
    p_heads, _, nkeys, half = peer_subkeys.shape[1:]
    assert dk == LANES and nkeys == LANES and 4 * hh <= LANES and n % GRID_W == 0
    off_z = 3 * qk_w
    off_ab = 4 * qk_w
    off_hy = off_ab + 4 * hh
    off_gate = off_hy + 3 * hy_w
    assert w_in.shape[2] == off_gate + 2 * d

    cc = jnp.zeros((SUBLANES, d), F32).at[:b].set(c).at[b].set(c_ctx)
    mods = _ada(cc, w_ada[0], b_ada[0][None, :])
    sh1, sc1, g1, sh2, sc2, g2 = [mods[:b, i * d:(i + 1) * d][:, None, :] for i in range(6)]
    sh1c, sc1c = [mods[b:b + 1, i * d:(i + 1) * d][:, None, :] for i in range(2)]

    w = w_in[0]
    w_qkvz = w[:, :off_ab].astype(BF16)
    w_ab = jnp.pad(w[:, off_ab:off_hy], ((0, 0), (0, LANES - 4 * hh))).astype(BF16)
    w_hy = w[:, off_hy:off_gate].astype(BF16)
    w_gate = w[:, off_gate:].astype(BF16)

    x2d = x.reshape(b * n, d)
    c2d = ctx.reshape(b * n_ctx, d)
    proj_qkvz = _modmm(x2d, sc1, sh1, w_qkvz, tm=512, tn=2048)
    proj_ab = _modmm(x2d, sc1, sh1, w_ab, tm=512, tn=LANES)
    proj_hy = _modmm(x2d, sc1, sh1, w_hy, tm=512, tn=2048)
    proj_gate = _modmm(x2d, sc1, sh1, w_gate, tm=512, tn=2048)
    projc_qkvz = _modmm(c2d, sc1c, sh1c, w_qkvz, tm=256, tn=2048)
    projc_ab = _modmm(c2d, sc1c, sh1c, w_ab, tm=256, tn=LANES)

    def decay_row(a):
        row = jnp.concatenate([a, jnp.zeros_like(a)], axis=1).reshape(-1)
        return jnp.pad(row, (0, LANES - 4 * hh))[None, :]

    alog_row = decay_row(dn_a_log[0])
    dt_row = decay_row(dn_dt_bias[0])

    conv_w = dn_conv_w[0]
    prep_c = _dn_prep(projc_qkvz.reshape(b, n_ctx, -1), conv_w, projc_ab.reshape(b, n_ctx, LANES),
                      alog_row, dt_row, n_heads=hh, dk=dk, row_w=n_ctx, tt=n_ctx)
    prep_l = _dn_prep(proj_qkvz.reshape(b, n, -1), conv_w, proj_ab.reshape(b, n, LANES),
                      alog_row, dt_row, n_heads=hh, dk=dk, row_w=GRID_W, tt=4 * GRID_W)
    hp = 4 if hh % 4 == 0 else (2 if hh % 2 == 0 else 1)
    s_zero = jnp.zeros((2, b, hh, dk, dk), F32)
    _, _, s_ctx = _dn_scan(*prep_c, s_zero, dk=dk, tt=n_ctx, hp=hp)
    o_f, o_b, _ = _dn_scan(*prep_l, s_ctx, dk=dk, tt=4 * GRID_W, hp=hp)

    filt = _hy_filters(n, hy_w1[0], hy_b1[0], hy_w2[0], hy_b2[0], hy_w3[0], hy_b3[0], hy_freq[0], hy_w)
    tables = _dft_tables(n)
    hy3 = proj_hy.reshape(b, n, 3 * hy_w)
    cwh = hy_conv_w[0]
    cb = hy_w // LANES
    y1 = _hy_conv(hy3, 0, hy3, cb, filt, 0, 2 * cb, hy_skip[0, 0:1], cwh, cwh, tables, conv_a=True, hy_w=hy_w)
    y_hy = _hy_conv(y1, 0, hy3, 2 * cb, filt, cb, 3 * cb, hy_skip[0, 1:2], cwh, cwh, tables, conv_a=False, hy_w=hy_w)

    merged = _merge(o_f.reshape(b * n, qk_w), o_b.reshape(b * n, qk_w), proj_qkvz, off_z // qk_w,
                    y_hy.reshape(b * n, hy_w), proj_gate, dn_norm_w[0][None, :],
                    w_branch_dn[0].astype(BF16), w_branch_hy[0].astype(BF16), dv=dk, tm=256)
    sk = peer_subkeys[0].reshape(p_heads * 2, nkeys, half).astype(BF16)
    x1, h2, scores = _outproj(merged, x2d, w_out[0].astype(BF16), g1, ln1_g[0][None, :], ln1_b[0][None, :],
                              sc2, sh2, peer_wq[0].astype(BF16), sk, tm=256)

    count1, rank2, e1, e2 = _peer_topk(scores, p_heads, tm=256)
    out = _peer_dense(h2.T, peer_u[0].astype(BF16), peer_v[0].T.astype(BF16), rank2, e2, count1, e1,
                      x1, g2, ln2_g[0][None, :], ln2_b[0][None, :], n_heads=p_heads, ti=SUBLANES, tm=512)
    stages = dict(o_f=o_f, o_b=o_b, s_ctx=s_ctx, filt=filt, y1=y1, y_hy=y_hy, merged=merged, x1=x1, scores=scores,
                  count1=count1, rank2=rank2, e1=e1, e2=e2)
    return out.reshape(b, n, d), stages


def kernel(x, c, ctx, c_ctx, w_ada, b_ada, w_in, dn_conv_w, dn_a_log, dn_dt_bias, dn_norm_w, hy_conv_w, hy_w1, hy_b1, hy_w2, hy_b2, hy_w3, hy_b3, hy_freq, hy_skip, w_branch_dn, w_branch_hy, w_out, ln1_g, ln1_b, peer_wq, peer_subkeys, peer_u, peer_v, ln2_g, ln2_b):
    out, _ = _block(x, c, ctx, c_ctx, w_ada, b_ada, w_in, dn_conv_w, dn_a_log, dn_dt_bias, dn_norm_w, hy_conv_w, hy_w1, hy_b1, hy_w2, hy_b2, hy_w3, hy_b3, hy_freq, hy_skip, w_branch_dn, w_branch_hy, w_out, ln1_g, ln1_b, peer_wq, peer_subkeys, peer_u, peer_v, ln2_g, ln2_b)
    return out
```

```python
import functools
import math

import jax
import jax.numpy as jnp
from jax import lax
from jax.experimental import pallas as pl
from jax.experimental.pallas import tpu as pltpu

F32 = jnp.float32
BF16 = jnp.bfloat16

GRID_W = 64
DN_CHUNK = 64
PEER_TOPK = 16
PEER_BAND = 32
DEPTH = 1
DEEPNORM_ALPHA = (2 * DEPTH) ** 0.25
LN_EPS = 1e-5
RMS_EPS = 1e-6
L2_EPS = 1e-6
HY_DECAY_TARGET = 1e-2
HY_FAST_DECAY = 0.3
HY_SLOW_DECAY = 1.5

LANES = 128
SUBLANES = 8
VMEM_LIMIT_BYTES = 56 * 1024 * 1024

TM_PROJ, TN_PROJ = 512, 2048
TT_DELTA = 4 * DN_CHUNK
TM_ROW = 256
TM_PEER, TI_PEER = 512, 8

FFT_N2 = 64
FFT_ROW_PAD = 8
FFT_UNROLL = 4


def _cparams(sem):
    return pltpu.CompilerParams(dimension_semantics=sem, vmem_limit_bytes=VMEM_LIMIT_BYTES)


def _silu(x):
    return x * jax.nn.sigmoid(x)


def _ada_kernel(c_ref, w_ref, b_ref, o_ref):
    s = _silu(c_ref[...])
    o_ref[...] = jnp.dot(s.astype(BF16), w_ref[...].astype(BF16), preferred_element_type=F32) + b_ref[...]


def _ada(cc, w, b):
    d, n6 = w.shape
    tn = min(n6, 1536)
    return pl.pallas_call(
        _ada_kernel,
        grid=(n6 // tn,),
        in_specs=[pl.BlockSpec((SUBLANES, d), lambda j: (0, 0)),
                  pl.BlockSpec((d, tn), lambda j: (0, j)),
                  pl.BlockSpec((1, tn), lambda j: (0, j))],
        out_specs=pl.BlockSpec((SUBLANES, tn), lambda j: (0, j)),
        out_shape=jax.ShapeDtypeStruct((SUBLANES, n6), F32),
        compiler_params=_cparams(("arbitrary",)),
    )(cc, w, b)


def _modmm_kernel(x_ref, sc_ref, sh_ref, w_ref, o_ref, h_scr):
    @pl.when(pl.program_id(1) == 0)
    def _():
        h_scr[...] = (x_ref[...] * (1.0 + sc_ref[0]) + sh_ref[0]).astype(BF16)

    o_ref[...] = jnp.dot(h_scr[...], w_ref[...], preferred_element_type=F32).astype(o_ref.dtype)


def _modmm(x2d, sc, sh, w, *, tm, tn, out_dtype=F32):
    m, d = x2d.shape
    ng = w.shape[1]
    r = sc.shape[0]
    tm = min(tm, m // r)
    tn = min(tn, ng)
    tiles_per_mod = (m // r) // tm
    return pl.pallas_call(
        _modmm_kernel,
        grid=(m // tm, ng // tn),
        in_specs=[pl.BlockSpec((tm, d), lambda i, j: (i, 0)),
                  pl.BlockSpec((1, 1, d), lambda i, j: (i // tiles_per_mod, 0, 0)),
                  pl.BlockSpec((1, 1, d), lambda i, j: (i // tiles_per_mod, 0, 0)),
                  pl.BlockSpec((d, tn), lambda i, j: (0, j))],
        out_specs=pl.BlockSpec((tm, tn), lambda i, j: (i, j)),
        out_shape=jax.ShapeDtypeStruct((m, ng), out_dtype),
        scratch_shapes=[pltpu.VMEM((tm, d), BF16)],
        compiler_params=_cparams(("parallel", "arbitrary")),
    )(x2d, sc, sh, w)


def _row_conv(x, cw, pos, row_w):
    rows = x.shape[0]
    k = cw.shape[0]
    pad = k // 2
    acc = None
    for j in range(k):
        off = j - pad
        if off == 0:
            term = x * cw[j:j + 1]
        else:
            shifted = pltpu.roll(x, (-off) % rows, axis=0)
            valid = (pos >= -off) if off < 0 else (pos < row_w - off)
            term = jnp.where(valid, shifted, 0.0) * cw[j:j + 1]
        acc = term if acc is None else acc + term
    return acc


SOLVE_BLOCK = 16


def _tri_inverse_pairs(pairs):
    c = pairs[0][0].shape[0]
    sb = SOLVE_BLOCK
    nb = c // sb
    ri = lax.broadcasted_iota(jnp.int32, (c, 2 * c), 0)
    ci = lax.broadcasted_iota(jnp.int32, (c, 2 * c), 1)
    ci = jnp.where(ci >= c, ci - c, ci)
    lane_lo = lax.broadcasted_iota(jnp.int32, (1, 2 * c), 1) < c
    off_diag = ri // sb != ci // sb
    eye2 = (ri == ci).astype(F32)
    mcats = [jnp.concatenate([m_a, m_b], axis=1) for m_a, m_b in pairs]
    m_offs = [jnp.where(off_diag, m, 0.0).astype(BF16) for m in mcats]
    xs = [[eye2[i * sb:(i + 1) * sb] for i in range(nb)] for _ in pairs]
    for bi in range(nb):
        rs = slice(bi * sb, (bi + 1) * sb)
        blks = [x[bi] for x in xs]
        if bi > 0:
            for s, x in enumerate(xs):
                xcur = jnp.concatenate(x, axis=0)
                xbd = jnp.concatenate([jnp.where(lane_lo, xcur, 0.0), jnp.where(lane_lo, 0.0, xcur)], axis=0)
                blks[s] = blks[s] - jnp.dot(m_offs[s][rs], xbd.astype(BF16), preferred_element_type=F32)
        mrs = [m[rs] for m in mcats]
        for jj in range(sb - 1):
            col = bi * sb + jj
            src = jnp.broadcast_to(jnp.where(lane_lo, col, c + col), (sb, 2 * c))
            for s in range(len(pairs)):
                mult = jnp.take_along_axis(mrs[s], src, axis=1)
                blks[s] = blks[s] - mult * blks[s][jj:jj + 1, :]
        for s, x in enumerate(xs):
            x[bi] = blks[s]
    return [jnp.concatenate(x, axis=0) for x in xs]


def _dn_gates_kernel(ab_ref, alog_ref, dt_ref, gc_ref, beta_ref, *, n_heads):
    tt = ab_ref.shape[1]
    c = DN_CHUNK
    cpos = lax.broadcasted_iota(jnp.int32, (tt, 1), 0) % c
    ab = ab_ref[0]
    z = ab + dt_ref[...]
    softplus = jnp.maximum(z, 0.0) + jnp.log1p(jnp.exp(-jnp.abs(z)))
    g_all = -jnp.exp(alog_ref[...]) * softplus
    gf = g_all
    gb = g_all
    d = 1
    while d < c:
        gf = gf + jnp.where(cpos >= d, pltpu.roll(gf, d, axis=0), 0.0)
        gb = gb + jnp.where(cpos < c - d, pltpu.roll(gb, (tt - d) % tt, axis=0), 0.0)
        d *= 2
    lane = lax.broadcasted_iota(jnp.int32, (1, LANES), 1)
    gc_ref[0] = jnp.where(lane < 2 * n_heads, gf, gb)
    beta_ref[0] = jax.nn.sigmoid(ab)


def _dn_gates(ab, alog_row, dt_row, *, n_heads, tt):
    b, n, _ = ab.shape
    blk = pl.BlockSpec((1, tt, LANES), lambda bi, g: (bi, g, 0))
    row = pl.BlockSpec((1, LANES), lambda bi, g: (0, 0))
    shp = jax.ShapeDtypeStruct((b, n, LANES), F32)
    return pl.pallas_call(
        functools.partial(_dn_gates_kernel, n_heads=n_heads),
        grid=(b, n // tt),
        in_specs=[blk, row, row],
        out_specs=[blk, blk],
        out_shape=[shp, shp],
        compiler_params=_cparams(("parallel", "parallel")),
    )(ab, alog_row, dt_row)


def _dn_prep_kernel(xq_ref, xk_ref, xv_ref, cq_ref, ck_ref, cv_ref, gc_ref, beta_ref,
                    u_ref, w_ref, qg_ref, kd_ref, attn_ref, eg_ref, *, row_w, n_heads, dk, hp):
    tt = xq_ref.shape[1]
    c = DN_CHUNK
    rows = lax.broadcasted_iota(jnp.int32, (tt, 1), 0)
    pos = rows % row_w
    lane = lax.broadcasted_iota(jnp.int32, (1, LANES), 1)
    gc = gc_ref[0]
    beta_all = beta_ref[0]
    eg_ref[...] = jnp.zeros(eg_ref.shape, F32)
    systems = []
    for hh in range(hp):
        systems += _dn_prep_head(pl.program_id(1) * hp + hh, hh, slice(hh * dk, (hh + 1) * dk), pos, lane, gc,
                                 beta_all, xq_ref, xk_ref, xv_ref, cq_ref, ck_ref, cv_ref, qg_ref, kd_ref, attn_ref,
                                 eg_ref, row_w=row_w, n_heads=n_heads, dk=dk, tt=tt)
    tinvs = _tri_inverse_pairs([(m_f, m_bt) for m_f, m_bt, _, _, _, _ in systems])
    zero_rows = jnp.zeros((c, 2 * dk), BF16)
    for tinv, (_, _, rhs_f, rhs_b, sl, hs) in zip(tinvs, systems):
        tinv = tinv.astype(BF16)
        sol_f = jnp.dot(tinv, jnp.concatenate([rhs_f, zero_rows], axis=0), preferred_element_type=F32)
        sol_b = lax.dot_general(tinv, rhs_b, (((0,), (0,)), ((), ())), preferred_element_type=F32)[c:]
        for dr, sol in enumerate((sol_f, sol_b)):
            u_ref[dr, 0, sl, hs] = sol[:, :dk]
            w_ref[dr, 0, sl, hs] = sol[:, dk:].astype(BF16)


def _dn_prep_head(h, hh, hs, pos, lane, gc, beta_all, xq_ref, xk_ref, xv_ref, cq_ref, ck_ref, cv_ref,
                  qg_ref, kd_ref, attn_ref, eg_ref, *, row_w, n_heads, dk, tt):
    c = DN_CHUNK
    systems = []

    def conv_silu(x_ref, cw_ref):
        return _silu(_row_conv(x_ref[0, :, hs], cw_ref[:, hs], pos, row_w))

    def l2n(y):
        return y * lax.rsqrt(jnp.sum(y * y, axis=-1, keepdims=True) + L2_EPS)

    q = l2n(conv_silu(xq_ref, cq_ref)) * (dk ** -0.5)
    k = l2n(conv_silu(xk_ref, ck_ref))
    v = conv_silu(xv_ref, cv_ref)

    def col(a, l):
        return jnp.sum(jnp.where(lane == l, a, 0.0), axis=1, keepdims=True)

    g_cols = (col(gc, h), col(gc, 2 * n_heads + h))
    b_cols = (col(beta_all, n_heads + h), col(beta_all, 3 * n_heads + h))

    ri = lax.broadcasted_iota(jnp.int32, (c, c), 0)
    ci = lax.broadcasted_iota(jnp.int32, (c, c), 1)
    eye = ri == ci
    lower = ri >= ci
    upper = ri <= ci
    strict_lower = ri > ci
    nt = (((1,), (1,)), ((), ()))
    for cc in range(tt // c):
        sl = slice(cc * c, (cc + 1) * c)
        qc, kc, vc = q[sl], k[sl], v[sl]
        kcb = kc.astype(BF16)
        gcols = [g_cols[dr][sl] for dr in range(2)]
        grows = [jnp.sum(jnp.where(eye, g, 0.0), axis=0, keepdims=True) for g in gcols]
        kbs = [kc * b_cols[dr][sl] for dr in range(2)]
        vbs = [vc * b_cols[dr][sl] for dr in range(2)]
        egs = [jnp.exp(g) for g in gcols]
        dif_f = gcols[0] - grows[0]
        dif_b = gcols[1] - grows[1]
        decay_f = jnp.where(lower, jnp.exp(jnp.where(lower, dif_f, 0.0)), 0.0)
        decay_b = jnp.where(upper, jnp.exp(jnp.where(upper, dif_b, 0.0)), 0.0)
        decay_bt = jnp.where(lower, jnp.exp(jnp.where(lower, -dif_b, 0.0)), 0.0)
        kq = lax.dot_general(jnp.concatenate([kbs[0], qc], axis=0).astype(BF16), kcb, nt,
                             preferred_element_type=F32)
        kkt_b = lax.dot_general(kcb, kbs[1].astype(BF16), nt, preferred_element_type=F32)
        m_f = jnp.where(strict_lower, kq[:c] * decay_f, 0.0)
        m_bt = jnp.where(strict_lower, kkt_b * decay_bt, 0.0)
        rhs_f = jnp.concatenate([vbs[0], kbs[0] * egs[0]], axis=1).astype(BF16)
        rhs_b = jnp.concatenate([vbs[1], kbs[1] * egs[1]], axis=1).astype(BF16)
        systems.append((m_f, m_bt, rhs_f, rhs_b, sl, hs))
        attns = (kq[c:] * decay_f, kq[c:] * decay_b)
        for dr in range(2):
            gcol = gcols[dr]
            glast = gcol[c - 1:c] if dr == 0 else gcol[0:1]
            qg_ref[dr, 0, sl, hs] = (qc * egs[dr]).astype(BF16)
            kd_ref[dr, 0, sl, hs] = (kc * jnp.exp(glast - gcol)).astype(BF16)
            attn_ref[dr, 0, hh, sl, :] = attns[dr].astype(BF16)
            eg_ref[dr, 0, hh, 0, cc:cc + 1, :] = jnp.broadcast_to(jnp.exp(glast), (1, LANES))
    return systems


def _dn_prep(qkv, conv_w, ab, alog_row, dt_row, *, n_heads, dk, row_w, tt):
    gc, beta = _dn_gates(ab, alog_row, dt_row, n_heads=n_heads, tt=tt)
    b, n, _ = qkv.shape
    hh = n_heads
    ng = n // tt
    c = DN_CHUNK
    assert tt % row_w == 0 and tt % c == 0 and tt // c <= SUBLANES
    hp = 2 if hh % 2 == 0 else 1
    nhb = hh // hp
    kern = functools.partial(_dn_prep_kernel, row_w=row_w, n_heads=hh, dk=dk, hp=hp)
    xspec = lambda off: pl.BlockSpec((1, tt, hp * dk), lambda bi, h, g: (bi, g, off + h))
    cspec = lambda off: pl.BlockSpec((conv_w.shape[0], hp * dk), lambda bi, h, g: (0, off + h))
    ospec = pl.BlockSpec((2, 1, tt, hp * dk), lambda bi, h, g: (0, bi, g, h))
    return pl.pallas_call(
        kern,
        grid=(b, nhb, ng),
        in_specs=[xspec(0), xspec(nhb), xspec(2 * nhb), cspec(0), cspec(nhb), cspec(2 * nhb),
                  pl.BlockSpec((1, tt, LANES), lambda bi, h, g: (bi, g, 0)),
                  pl.BlockSpec((1, tt, LANES), lambda bi, h, g: (bi, g, 0))],
        out_specs=[ospec, ospec, ospec, ospec,
                   pl.BlockSpec((2, 1, hp, tt, c), lambda bi, h, g: (0, bi, h, g, 0)),
                   pl.BlockSpec((2, 1, hp, 1, SUBLANES, LANES), lambda bi, h, g: (0, bi, h, g, 0, 0))],
        out_shape=[jax.ShapeDtypeStruct((2, b, n, hh * dk), F32),
                   jax.ShapeDtypeStruct((2, b, n, hh * dk), BF16),
                   jax.ShapeDtypeStruct((2, b, n, hh * dk), BF16),
                   jax.ShapeDtypeStruct((2, b, n, hh * dk), BF16),
                   jax.ShapeDtypeStruct((2, b, hh, n, c), BF16),
                   jax.ShapeDtypeStruct((2, b, hh, ng, SUBLANES, LANES), F32)],
        compiler_params=_cparams(("parallel", "parallel", "arbitrary")),
    )(qkv, qkv, qkv, conv_w, conv_w, conv_w, gc, beta)


def _dn_scan_kernel(uf_ref, wf_ref, qf_ref, kf_ref, af_ref, ef_ref,
                    ub_ref, wb_ref, qb_ref, kb_ref, ab_ref, eb_ref, s0_ref,
                    of_ref, ob_ref, sout_ref, s_scr, *, hp, dk):
    g = pl.program_id(2)
    ng = pl.num_programs(2)
    c = DN_CHUNK
    tt = uf_ref.shape[2]
    ncs = tt // c

    @pl.when(g == 0)
    def _():
        s_scr[...] = s0_ref[:, 0]

    refs = ((uf_ref, wf_ref, qf_ref, kf_ref, af_ref, ef_ref, of_ref),
            (ub_ref, wb_ref, qb_ref, kb_ref, ab_ref, eb_ref, ob_ref))
    chains = [(dr, hh) for dr in range(2) for hh in range(hp)]
    state = {ch: s_scr[ch[0], ch[1]] for ch in chains}
    for step in range(ncs):
        rows = {}
        for dr, hh in chains:
            cc = step if dr == 0 else ncs - 1 - step
            rows[dr, hh] = (slice(cc * c, (cc + 1) * c), slice(hh * dk, (hh + 1) * dk), cc)
        wide = {}
        for dr in range(2):
            sl = rows[dr, 0][0]
            wide[dr] = [refs[dr][i][0, 0, sl, :] for i in range(4)]
        r = {}
        for ch in chains:
            _, hs, _ = rows[ch]
            wq = jnp.concatenate([wide[ch[0]][1][:, hs], wide[ch[0]][2][:, hs]], axis=0)
            r[ch] = jnp.dot(wq, state[ch].astype(BF16), preferred_element_type=F32)
        vnb = {}
        for ch in chains:
            _, hs, _ = rows[ch]
            vnb[ch] = (wide[ch[0]][0][:, hs] - r[ch][:c]).astype(BF16)
        outs = {}
        for ch in chains:
            a_ref, e_ref = refs[ch[0]][4], refs[ch[0]][5]
            sl, hs, cc = rows[ch]
            outs[ch] = r[ch][c:] + jnp.dot(a_ref[0, 0, ch[1], sl, :], vnb[ch], preferred_element_type=F32)
            upd = lax.dot_general(wide[ch[0]][3][:, hs], vnb[ch], (((0,), (0,)), ((), ())),
                                  preferred_element_type=F32)
            state[ch] = state[ch] * e_ref[0, 0, ch[1], 0, cc:cc + 1, :] + upd
        for dr in range(2):
            refs[dr][6][0, rows[dr, 0][0], :] = jnp.concatenate([outs[dr, hh] for hh in range(hp)], axis=1)
    for ch in chains:
        s_scr[ch[0], ch[1]] = state[ch]

    @pl.when(g == ng - 1)
    def _():
        sout_ref[:, 0] = s_scr[...]


def _dn_scan(u, w, qg, kd, attn, eg, s0, *, dk, tt, hp):
    _, b, n, hd = u.shape
    hh = hd // dk
    ng = n // tt
    c = DN_CHUNK
    kern = functools.partial(_dn_scan_kernel, hp=hp, dk=dk)

    def specs(dr):
        gi = (lambda g: g) if dr == 0 else (lambda g: ng - 1 - g)
        tok = pl.BlockSpec((1, 1, tt, hp * dk), lambda bi, h, g: (dr, bi, gi(g), h))
        return [tok, tok, tok, tok,
                pl.BlockSpec((1, 1, hp, tt, c), lambda bi, h, g: (dr, bi, h, gi(g), 0)),
                pl.BlockSpec((1, 1, hp, 1, SUBLANES, LANES), lambda bi, h, g: (dr, bi, h, gi(g), 0, 0))]

    sspec = pl.BlockSpec((2, 1, hp, dk, dk), lambda bi, h, g: (0, bi, h, 0, 0))
    return pl.pallas_call(
        kern,
        grid=(b, hh // hp, ng),
        in_specs=specs(0) + specs(1) + [sspec],
        out_specs=[pl.BlockSpec((1, tt, hp * dk), lambda bi, h, g: (bi, g, h)),
                   pl.BlockSpec((1, tt, hp * dk), lambda bi, h, g: (bi, ng - 1 - g, h)),
                   sspec],
        out_shape=[jax.ShapeDtypeStruct((b, n, hd), F32),
                   jax.ShapeDtypeStruct((b, n, hd), F32),
                   jax.ShapeDtypeStruct((2, b, hh, dk, dk), F32)],
        scratch_shapes=[pltpu.VMEM((2, hp, dk, dk), F32)],
        compiler_params=_cparams(("parallel", "parallel", "arbitrary")),
    )(u, w, qg, kd, attn, eg, u, w, qg, kd, attn, eg, s0)


def _hy_hidden_kernel(z_ref, w1_ref, b1_ref, w2_ref, b2_ref, fr_ref, o_ref):
    fr = fr_ref[...]
    h1 = jnp.sin(fr * (jnp.dot(z_ref[...].astype(BF16), w1_ref[...].astype(BF16),
                               preferred_element_type=F32) + b1_ref[...]))
    o_ref[...] = jnp.sin(fr * (jnp.dot(h1.astype(BF16), w2_ref[...].astype(BF16),
                                       preferred_element_type=F32) + b2_ref[...]))


def _hy_filter_kernel(hid_ref, w3_ref, b3_ref, t_ref, delta_ref, o_ref):
    f = jnp.dot(hid_ref[...].astype(BF16), w3_ref[...].astype(BF16), preferred_element_type=F32) + b3_ref[...]
    o_ref[...] = f * jnp.exp(-t_ref[...] * delta_ref[...])


def _hy_filters(n, w1, b1, w2, b2, w3, b3, freq, hy_w):
    emb, hid = w1.shape
    bands = (emb - 1) // 2
    t = jnp.linspace(0.0, 1.0, n, dtype=F32)
    pos = jnp.arange(n, dtype=F32)
    bnd = jnp.linspace(1e-4, bands - 1, bands, dtype=F32)
    ang = (2.0 * math.pi / n) * pos[:, None] * bnd[None, :]
    z = jnp.concatenate([t[:, None], jnp.cos(ang), -jnp.sin(ang)], axis=-1)
    embp = -(-emb // SUBLANES) * SUBLANES
    z = jnp.pad(z, ((0, 0), (0, embp - emb)))
    w1p = jnp.pad(w1, ((0, embp - emb), (0, 0)))
    full = lambda a: pl.BlockSpec(a.shape, lambda: (0,) * a.ndim)
    args = (z, w1p, b1[None, :], w2, b2[None, :], freq[None, :])
    hid2 = pl.pallas_call(
        _hy_hidden_kernel,
        in_specs=[full(a) for a in args],
        out_specs=pl.BlockSpec((n, hid), lambda: (0, 0)),
        out_shape=jax.ShapeDtypeStruct((n, hid), F32),
        compiler_params=pltpu.CompilerParams(vmem_limit_bytes=VMEM_LIMIT_BYTES),
    )(*args)
    nout = w3.shape[1]
    deltas = jnp.abs(jnp.linspace(math.log(HY_DECAY_TARGET) / HY_SLOW_DECAY,
                                  math.log(HY_DECAY_TARGET) / HY_FAST_DECAY, hy_w, dtype=F32))
    delta_row = jnp.tile(deltas, nout // hy_w)[None, :]
    tc = min(nout, 1024)
    return pl.pallas_call(
        _hy_filter_kernel,
        grid=(nout // tc,),
        in_specs=[pl.BlockSpec((n, hid), lambda j: (0, 0)),
                  pl.BlockSpec((hid, tc), lambda j: (0, j)),
                  pl.BlockSpec((1, tc), lambda j: (0, j)),
                  pl.BlockSpec((n, 1), lambda j: (0, 0)),
                  pl.BlockSpec((1, tc), lambda j: (0, j))],
        out_specs=pl.BlockSpec((n, tc), lambda j: (0, j)),
        out_shape=jax.ShapeDtypeStruct((n, nout), F32),
        compiler_params=_cparams(("parallel",)),
    )(hid2, w3, b3[None, :], t[:, None], delta_row)


def _dft_tables(n):
    n2 = FFT_N2
    big = 2 * n
    n1 = big // n2
    k1 = n1 // 2
    two_pi = 2.0 * math.pi

    def cs(idx, period):
        a = (idx % period).astype(F32) * (two_pi / period)
        return jnp.cos(a), jnp.sin(a)

    f1 = jnp.arange(n1, dtype=jnp.int32)
    s1 = jnp.arange(k1, dtype=jnp.int32)
    s2 = jnp.arange(n2, dtype=jnp.int32)
    s_full = s1[None, None, :] * n2 + s2[:, None, None]
    c, s = cs(f1[None, :, None] * s_full, big)
    f1tw = jnp.concatenate([jnp.concatenate([c, s], axis=2), jnp.concatenate([-s, c], axis=2)], axis=1)
    ct, st = jnp.swapaxes(c, 1, 2), jnp.swapaxes(s, 1, 2)
    g1tw = jnp.concatenate([jnp.concatenate([ct, -st], axis=2), jnp.concatenate([st, ct], axis=2)], axis=1) / big
    c2, sn2 = cs(s2[:, None] * s2[None, :], n2)
    f2 = jnp.concatenate([jnp.concatenate([c2, sn2], axis=1), jnp.concatenate([-sn2, c2], axis=1)], axis=0)
    g2 = jnp.concatenate([jnp.concatenate([c2, -sn2], axis=1), jnp.concatenate([sn2, c2], axis=1)], axis=0)
    return f1tw.astype(BF16), f2.astype(BF16), g2.astype(BF16), g1tw.astype(BF16)


def _hy_conv_kernel(a_ref, m_ref, hf_ref, hb_ref, skip_ref, cwa_ref, cwm_ref,
                    f1_ref, f2_ref, g2_ref, g1_ref, o_ref,
                    h_scr, b_scr, zr_scr, zi_scr, *, conv_a, n1, k1):
    n2 = FFT_N2
    k1p = k1 + FFT_ROW_PAD
    pb = 2 * n2 + FFT_ROW_PAD
    pos = lax.broadcasted_iota(jnp.int32, (n2, 1), 0)

    grp = FFT_UNROLL
    cw = b_scr.shape[1]

    def stage1(real_only):
        def body(g, carry):
            s2s = [g * grp + t for t in range(grp)]
            bases = [pl.multiple_of(s2 * k1p, SUBLANES) for s2 in s2s]
            if real_only:
                xs = [zr_scr[pl.ds(base, k1), :].astype(BF16) for base in bases]
                outs = [jnp.dot(f1_ref[s2, :, :k1], x, preferred_element_type=F32) for s2, x in zip(s2s, xs)]
            else:
                xs = [jnp.concatenate([zr_scr[pl.ds(base, k1), :], zi_scr[pl.ds(base, k1), :]], axis=0).astype(BF16)
                      for base in bases]
                outs = [jnp.dot(f1_ref[s2], x, preferred_element_type=F32) for s2, x in zip(s2s, xs)]
            for s2, a in zip(s2s, outs):
                b_scr[pl.ds(s2, n1, stride=pb), :] = a[:n1]
                b_scr[pl.ds(n2 + s2, n1, stride=pb), :] = a[n1:]
            return carry
        lax.fori_loop(0, n2 // grp, body, 0)

    def slab_pairs(g):
        f1s = [[(g * grp + t) * 2 + u for u in range(2)] for t in range(grp)]
        bbs = [[pl.multiple_of(f1 * pb, SUBLANES) for f1 in pair] for pair in f1s]
        hbs = [[pl.multiple_of(f1 * 2 * n2, 2 * n2) for f1 in pair] for pair in f1s]
        return bbs, hbs

    def forward2(bbs):
        bpairs = [jnp.concatenate([b_scr[pl.ds(bb, 2 * n2), :] for bb in pair], axis=1).astype(BF16) for pair in bbs]
        return [jnp.dot(f2_ref[...], bp, preferred_element_type=F32) for bp in bpairs]

    @pl.when(pl.program_id(1) == 0)
    def _():
        for which, ref in ((0, hf_ref), (1, hb_ref)):
            def load(s1, carry, ref=ref):
                r0 = pl.multiple_of(s1 * n2, n2)
                zr_scr[pl.ds(s1, n2, stride=k1p), :] = ref[pl.ds(r0, n2), :]
                return carry
            lax.fori_loop(0, k1, load, 0, unroll=FFT_UNROLL)
            if which == 1:
                zr_scr[0:1, :] = jnp.zeros((1, zr_scr.shape[1]), F32)
            stage1(True)

            def spec(g, carry, which=which):
                bbs, hbs = slab_pairs(g)
                zpairs = forward2(bbs)
                for zpair, hpair in zip(zpairs, hbs):
                    for t, hb_ in enumerate(hpair):
                        zz = zpair[:, t * cw:(t + 1) * cw]
                        if which == 0:
                            h_scr[pl.ds(hb_, 2 * n2), :] = zz
                        else:
                            h_scr[pl.ds(hb_, n2), :] = h_scr[pl.ds(hb_, n2), :] + zz[:n2]
                            h_scr[pl.ds(hb_ + n2, n2), :] = h_scr[pl.ds(hb_ + n2, n2), :] - zz[n2:]
                return carry
            lax.fori_loop(0, n1 // (2 * grp), spec, 0)

    def load_a(s1, carry):
        r0 = pl.multiple_of(s1 * n2, n2)
        for bi, scr in ((0, zr_scr), (1, zi_scr)):
            blk = a_ref[bi, pl.ds(r0, n2), :]
            if conv_a:
                blk = _row_conv(blk, cwa_ref[...], pos, n2)
            scr[pl.ds(s1, n2, stride=k1p), :] = blk
        return carry
    lax.fori_loop(0, k1, load_a, 0, unroll=FFT_UNROLL)

    stage1(False)

    def mid(g, carry):
        bbs, hbs = slab_pairs(g)
        zzs = forward2(bbs)
        yys = []
        for zz, hpair in zip(zzs, hbs):
            zr, zi = zz[:n2], zz[n2:]
            hr = jnp.concatenate([h_scr[pl.ds(hb_, n2), :] for hb_ in hpair], axis=1)
            hi = jnp.concatenate([h_scr[pl.ds(hb_ + n2, n2), :] for hb_ in hpair], axis=1)
            yys.append(jnp.concatenate([zr * hr - zi * hi, zr * hi + zi * hr], axis=0).astype(BF16))
        wws = [jnp.dot(g2_ref[...], yy, preferred_element_type=F32) for yy in yys]
        for ww, pair in zip(wws, bbs):
            for t, bb in enumerate(pair):
                b_scr[pl.ds(bb, 2 * n2), :] = ww[:, t * cw:(t + 1) * cw]
        return carry
    lax.fori_loop(0, n1 // (2 * grp), mid, 0)

    skip = skip_ref[...]

    def last(g, carry):
        s2s = [g * grp + t for t in range(grp)]
        wws = [jnp.concatenate([b_scr[pl.ds(s2, n1, stride=pb), :],
                                b_scr[pl.ds(n2 + s2, n1, stride=pb), :]], axis=0).astype(BF16) for s2 in s2s]
        ys = [jnp.dot(g1_ref[s2], ww, preferred_element_type=F32) for s2, ww in zip(s2s, wws)]
        for s2, y in zip(s2s, ys):
            base = pl.multiple_of(s2 * k1p, SUBLANES)
            zr_scr[pl.ds(base, k1), :] = y[:k1] + zr_scr[pl.ds(base, k1), :] * skip
            zi_scr[pl.ds(base, k1), :] = y[k1:] + zi_scr[pl.ds(base, k1), :] * skip
        return carry
    lax.fori_loop(0, n2 // grp, last, 0)

    def fin(s1, carry):
        r0 = pl.multiple_of(s1 * n2, n2)
        for bi, scr in ((0, zr_scr), (1, zi_scr)):
            mm = _row_conv(m_ref[bi, pl.ds(r0, n2), :], cwm_ref[...], pos, n2)
            o_ref[bi, pl.ds(r0, n2), :] = mm * scr[pl.ds(s1, n2, stride=k1p), :]
        return carry
    lax.fori_loop(0, k1, fin, 0, unroll=FFT_UNROLL)


def _hy_conv(a, a_col0, m, m_col0, filt, f_col_fwd, f_col_bwd, skip, cwa, cwm, tables, *, conv_a, hy_w):
    b, n, _ = a.shape
    assert b % 2 == 0 and GRID_W == FFT_N2 and n % FFT_N2 == 0
    n2 = FFT_N2
    n1 = 2 * n // n2
    k1 = n1 // 2
    k1p = k1 + FFT_ROW_PAD
    pb = 2 * n2 + FFT_ROW_PAD
    cw = LANES
    f1tw, f2, g2, g1tw = tables
    kern = functools.partial(_hy_conv_kernel, conv_a=conv_a, n1=n1, k1=k1)
    once = pl.Buffered(1)
    return pl.pallas_call(
        kern,
        grid=(hy_w // cw, b // 2),
        in_specs=[pl.BlockSpec((2, n, cw), lambda c, p: (p, 0, a_col0 + c), pipeline_mode=once),
                  pl.BlockSpec((2, n, cw), lambda c, p: (p, 0, m_col0 + c), pipeline_mode=once),
                  pl.BlockSpec((n, cw), lambda c, p: (0, f_col_fwd + c), pipeline_mode=once),
                  pl.BlockSpec((n, cw), lambda c, p: (0, f_col_bwd + c), pipeline_mode=once),
                  pl.BlockSpec((1, cw), lambda c, p: (0, c)),
                  pl.BlockSpec((cwa.shape[0], cw), lambda c, p: (0, a_col0 + c if conv_a else 0)),
                  pl.BlockSpec((cwm.shape[0], cw), lambda c, p: (0, m_col0 + c)),
                  pl.BlockSpec(f1tw.shape, lambda c, p: (0, 0, 0), pipeline_mode=once),
                  pl.BlockSpec(f2.shape, lambda c, p: (0, 0)),
                  pl.BlockSpec(g2.shape, lambda c, p: (0, 0)),
                  pl.BlockSpec(g1tw.shape, lambda c, p: (0, 0, 0), pipeline_mode=once)],
        out_specs=pl.BlockSpec((2, n, cw), lambda c, p: (p, 0, c)),
        out_shape=jax.ShapeDtypeStruct((b, n, hy_w), F32),
        scratch_shapes=[pltpu.VMEM((n1 * 2 * n2, cw), F32),
                        pltpu.VMEM((n1 * pb, cw), F32),
                        pltpu.VMEM((n2 * k1p, cw), F32),
                        pltpu.VMEM((n2 * k1p, cw), F32)],
        compiler_params=_cparams(("parallel", "arbitrary")),
    )(a, m, filt, filt, skip, cwa, cwm, f1tw, f2, g2, g1tw)


def _merge_kernel(of_ref, ob_ref, z_ref, y_ref, ga_ref, gb_ref, nw_ref, wa_ref, wh_ref, o_ref, *, dv):
    nw = nw_ref[...]
    o_all = of_ref[...] + ob_ref[...]
    gate_all = _silu(z_ref[...])
    heads = []
    for hh in range(of_ref.shape[1] // dv):
        hs = slice(hh * dv, (hh + 1) * dv)
        o = o_all[:, hs]
        o = o * lax.rsqrt(jnp.mean(o * o, axis=-1, keepdims=True) + RMS_EPS) * nw
        heads.append((o * gate_all[:, hs]).astype(BF16))
    pa = jnp.dot(jnp.concatenate(heads, axis=1), wa_ref[...], preferred_element_type=F32)
    ph = jnp.dot(y_ref[...].astype(BF16), wh_ref[...], preferred_element_type=F32)
    o_ref[...] = (jax.nn.sigmoid(ga_ref[...]) * pa + jax.nn.sigmoid(gb_ref[...]) * ph).astype(o_ref.dtype)


def _merge(o_f, o_b, z, z_blk, y_hy, gate, norm_w, w_pa, w_ph, *, dv, tm):
    m, dvw = o_f.shape
    d = w_pa.shape[1]
    tm = min(tm, m)
    kern = functools.partial(_merge_kernel, dv=dv)
    row = lambda w_, blk=0: pl.BlockSpec((tm, w_), lambda i: (i, blk))
    const = lambda a: pl.BlockSpec(a.shape, lambda i: (0, 0), pipeline_mode=pl.Buffered(1))
    return pl.pallas_call(
        kern,
        grid=(m // tm,),
        in_specs=[row(dvw), row(dvw), row(dvw, z_blk), row(y_hy.shape[1]), row(d, 0), row(d, 1),
                  pl.BlockSpec((1, dv), lambda i: (0, 0)), const(w_pa), const(w_ph)],
        out_specs=row(d),
        out_shape=jax.ShapeDtypeStruct((m, d), BF16),
        compiler_params=_cparams(("parallel",)),
    )(o_f, o_b, z, y_hy, gate, gate, norm_w, w_pa, w_ph)


def _layer_norm(x, g, b):
    mu = jnp.mean(x, axis=-1, keepdims=True)
    xc = x - mu
    var = jnp.mean(xc * xc, axis=-1, keepdims=True)
    return xc * lax.rsqrt(var + LN_EPS) * g + b


def _outproj_kernel(mg_ref, x_ref, wo_ref, g1_ref, lg_ref, lb_ref, sc_ref, sh_ref, wq_ref, sk_ref,
                    x1_ref, h2_ref, s_ref, *, half):
    mix = jnp.dot(mg_ref[...], wo_ref[...], preferred_element_type=F32)
    x1 = _layer_norm(DEEPNORM_ALPHA * x_ref[...] + g1_ref[0] * mix, lg_ref[...], lb_ref[...])
    x1_ref[...] = x1
    h2 = (x1 * (1.0 + sc_ref[0]) + sh_ref[0]).astype(BF16)
    h2_ref[...] = h2
    qp = jnp.dot(h2, wq_ref[...], preferred_element_type=F32).astype(BF16)
    for j in range(sk_ref.shape[0]):
        js = slice(j * half, (j + 1) * half)
        s_ref[j] = lax.dot_general(sk_ref[j], qp[:, js], (((1,), (1,)), ((), ())), preferred_element_type=F32)


def _outproj(merged, x2d, w_out, g1, ln_g, ln_b, sc2, sh2, wq, subkeys, *, tm):
    m, d = x2d.shape
    r = g1.shape[0]
    tm = min(tm, m // r)
    tpm = (m // r) // tm
    nsk, nkeys, half = subkeys.shape
    assert nkeys == LANES
    kern = functools.partial(_outproj_kernel, half=half)
    row = lambda w_: pl.BlockSpec((tm, w_), lambda i: (i, 0))
    mod = pl.BlockSpec((1, 1, d), lambda i: (i // tpm, 0, 0))
    vec = pl.BlockSpec((1, d), lambda i: (0, 0))
    return pl.pallas_call(
        kern,
        grid=(m // tm,),
        in_specs=[row(d), row(d), pl.BlockSpec(w_out.shape, lambda i: (0, 0), pipeline_mode=pl.Buffered(1)),
                  mod, vec, vec, mod, mod,
                  pl.BlockSpec(wq.shape, lambda i: (0, 0), pipeline_mode=pl.Buffered(1)),
                  pl.BlockSpec(subkeys.shape, lambda i: (0, 0, 0))],
        out_specs=[row(d), row(d), pl.BlockSpec((nsk, nkeys, tm), lambda i: (0, 0, i))],
        out_shape=[jax.ShapeDtypeStruct((m, d), F32),
                   jax.ShapeDtypeStruct((m, d), BF16),
                   jax.ShapeDtypeStruct((nsk, nkeys, m), F32)],
        compiler_params=_cparams(("parallel",)),
    )(merged, x2d, w_out, g1, ln_g, ln_b, sc2, sh2, wq, subkeys)


def _cand_tables():
    pairs = [(a, b) for a in range(PEER_TOPK) for b in range(PEER_TOPK) if (a + 1) * (b + 1) <= PEER_TOPK]
    ia = jnp.array([p[0] for p in pairs], dtype=jnp.int32)
    ib = jnp.array([p[1] for p in pairs], dtype=jnp.int32)
    ncand = len(pairs)
    rows = -(-ncand // (2 * SUBLANES)) * (2 * SUBLANES)
    ranks = jnp.arange(LANES, dtype=jnp.int32)
    p1 = (jnp.pad(ia, (0, rows - ncand), constant_values=-1)[:, None] == ranks[None, :]).astype(F32)
    p2 = (jnp.pad(ib, (0, rows - ncand), constant_values=-1)[:, None] == ranks[None, :]).astype(F32)
    return p1, p2, ncand


def _peer_topk_kernel(s_ref, p1_ref, p2_ref, p1t_ref, c1_ref, r2_ref, e1_ref, e2_ref, *, n_heads, ncand):
    tm = s_ref.shape[2]
    neg = -jnp.inf
    hi = lax.Precision.HIGHEST
    no_rank = float(LANES)

    def pop_max(x, row_id):
        mx = jnp.max(x, axis=0, keepdims=True)
        first = jnp.min(jnp.where(x == mx, row_id, no_rank), axis=0, keepdims=True)
        hit = row_id == first
        return mx, hit, jnp.where(hit, neg, x)

    key_id = lax.broadcasted_iota(jnp.int32, (LANES, tm), 0).astype(F32)
    cand_id = lax.broadcasted_iota(jnp.int32, (p1_ref.shape[0], tm), 0).astype(F32)
    pad_rows = jnp.zeros((LANES - PEER_TOPK, tm), F32)

    def sorted_top(s):
        tops = []
        rank = jnp.full(s.shape, no_rank, F32)
        x = s
        for a in range(PEER_TOPK):
            mx, hit, x = pop_max(x, key_id)
            tops.append(mx)
            rank = jnp.where(hit, float(a), rank)
        return jnp.concatenate(tops + [pad_rows], axis=0), rank

    for h in range(n_heads):
        s1 = s_ref[2 * h]
        s2 = s_ref[2 * h + 1]
        t1, rank1 = sorted_top(s1)
        t2, rank2 = sorted_top(s2)
        cand = (jnp.dot(p1_ref[...], t1, precision=hi, preferred_element_type=F32)
                + jnp.dot(p2_ref[...], t2, precision=hi, preferred_element_type=F32))
        x = jnp.where(cand_id < ncand, cand, neg)
        sel = jnp.zeros(x.shape, F32)
        zsum = None
        cmax = None
        for r in range(PEER_TOPK):
            mx, hit, x = pop_max(x, cand_id)
            sel = jnp.where(hit, 1.0, sel)
            if r == 0:
                cmax = mx
                zsum = jnp.ones_like(mx)
            else:
                zsum = zsum + jnp.exp(mx - cmax)
        cnt = jnp.dot(p1t_ref[...].astype(BF16), sel.astype(BF16), preferred_element_type=F32)
        count1 = jnp.zeros(s1.shape, F32)
        for a in range(PEER_TOPK):
            count1 = jnp.where(rank1 == a, cnt[a:a + 1], count1)
        c1_ref[h] = count1
        r2_ref[h] = rank2.astype(r2_ref.dtype)
        e1_ref[h] = jnp.exp(s1 - t1[0:1]) / zsum
        e2_ref[h] = jnp.exp(s2 - t2[0:1]).astype(e2_ref.dtype)


def _peer_topk(scores, n_heads, *, tm):
    nsk, nkeys, m = scores.shape
    tm = min(tm, m)
    p1, p2, ncand = _cand_tables()
    kern = functools.partial(_peer_topk_kernel, n_heads=n_heads, ncand=ncand)
    full = lambda a: pl.BlockSpec(a.shape, lambda i: (0, 0))
    out = pl.BlockSpec((n_heads, nkeys, tm), lambda i: (0, 0, i))
    shp = lambda dt: jax.ShapeDtypeStruct((n_heads, nkeys, m), dt)
    p1t = p1.T
    return pl.pallas_call(
        kern,
        grid=(m // tm,),
        in_specs=[pl.BlockSpec((nsk, nkeys, tm), lambda i: (0, 0, i)), full(p1), full(p2), full(p1t)],
        out_specs=[out, out, out, out],
        out_shape=[shp(F32), shp(F32), shp(F32), shp(F32)],
        compiler_params=_cparams(("parallel",)),
    )(scores, p1, p2, p1t)


def _peer_dense_kernel(h2t_ref, u_ref, vt_ref, r2_ref, e2_ref, c1_ref, e1_ref, x1_ref, g2_ref, lg_ref, lb_ref,
                       o_ref, acc_scr, a_scr, p_scr, r2_scr, e2_scr, *, n_heads, ti):
    e = pl.program_id(1)

    @pl.when(e == 0)
    def _():
        acc_scr[...] = jnp.zeros(acc_scr.shape, F32)
        r2_scr[...] = r2_ref[...].astype(BF16)
        e2_scr[...] = e2_ref[...].astype(BF16)

    a_scr[...] = jnp.dot(u_ref[...], h2t_ref[...], preferred_element_type=F32)
    tm = h2t_ref.shape[1]
    nkeys = r2_ref.shape[1]
    for ii in range(ti):
        c1_rows = [c1_ref[h, ii:ii + 1, :].astype(BF16) for h in range(n_heads)]
        e1_rows = [e1_ref[h, ii:ii + 1, :].astype(BF16) for h in range(n_heads)]
        for band in range(nkeys // PEER_BAND):
            ks = slice(band * PEER_BAND, (band + 1) * PEER_BAND)
            rs = slice(ii * nkeys + band * PEER_BAND, ii * nkeys + (band + 1) * PEER_BAND)
            gate = None
            for h in range(n_heads):
                keep = r2_scr[h, ks, :] < c1_rows[h]
                term = jnp.where(keep, e2_scr[h, ks, :], jnp.zeros((), BF16)) * e1_rows[h]
                gate = term if gate is None else gate + term
            a = a_scr[rs, :]
            act = 0.5 * a * (1.0 + lax.erf(a * (2.0 ** -0.5)))
            p_scr[rs, :] = gate * act.astype(BF16)
    acc_scr[...] += jnp.dot(vt_ref[...], p_scr[...], preferred_element_type=F32)

    @pl.when(e == pl.num_programs(1) - 1)
    def _():
        o_ref[...] = _layer_norm(DEEPNORM_ALPHA * x1_ref[...] + g2_ref[0] * acc_scr[...].T,
                                 lg_ref[...], lb_ref[...])


def _peer_dense(h2t, u_tab, vt_tab, rank2, e2, count1, e1, x1, g2, ln_g, ln_b, *, n_heads, ti, tm):
    d, m = h2t.shape
    r = g2.shape[0]
    tm = min(tm, m // r)
    tpm = (m // r) // tm
    ne = u_tab.shape[0]
    nkeys = rank2.shape[1]
    te = ti * nkeys
    kern = functools.partial(_peer_dense_kernel, n_heads=n_heads, ti=ti)
    row = lambda w_: pl.BlockSpec((tm, w_), lambda i, e: (i, 0), pipeline_mode=pl.Buffered(1))
    vec = pl.BlockSpec((1, d), lambda i, e: (0, 0))
    tbl = pl.BlockSpec((te, d), lambda i, e: (e, 0))
    second = pl.BlockSpec((n_heads, nkeys, tm), lambda i, e: (0, 0, i), pipeline_mode=pl.Buffered(1))
    first = pl.BlockSpec((n_heads, ti, tm), lambda i, e: (0, e, i))
    return pl.pallas_call(
        kern,
        grid=(m // tm, ne // te),
        in_specs=[pl.BlockSpec((d, tm), lambda i, e: (0, i), pipeline_mode=pl.Buffered(1)),
                  tbl, pl.BlockSpec((d, te), lambda i, e: (0, e)), second, second, first, first, row(d),
                  pl.BlockSpec((1, 1, d), lambda i, e: (i // tpm, 0, 0)), vec, vec],
        out_specs=pl.BlockSpec((tm, d), lambda i, e: (i, 0)),
        out_shape=jax.ShapeDtypeStruct((m, d), F32),
        scratch_shapes=[pltpu.VMEM((d, tm), F32), pltpu.VMEM((te, tm), F32), pltpu.VMEM((te, tm), BF16),
                        pltpu.VMEM((n_heads, nkeys, tm), BF16), pltpu.VMEM((n_heads, nkeys, tm), BF16)],
        compiler_params=_cparams(("parallel", "arbitrary")),
    )(h2t, u_tab, vt_tab, rank2, e2, count1, e1, x1, g2, ln_g, ln_b)


def _block(x, c, ctx, c_ctx, w_ada, b_ada, w_in, dn_conv_w, dn_a_log, dn_dt_bias, dn_norm_w, hy_conv_w, hy_w1, hy_b1, hy_w2, hy_b2, hy_w3, hy_b3, hy_freq, hy_skip, w_branch_dn, w_branch_hy, w_out, ln1_g, ln1_b, peer_wq, peer_subkeys, peer_u, peer_v, ln2_g, ln2_b):
    b, n, d = x.shape
    n_ctx = ctx.shape[1]
    assert w_ada.shape[0] == DEPTH == 1, "single-layer block: the context stream update is never read"
    hh = dn_a_log.shape[2]
    dk = dn_conv_w.shape[2] // (3 * hh)
    qk_w = hh * dk
    hy_w = hy_skip.shape[2]
    p_heads, _, nkeys, half = peer_subkeys.shape[1:]
    assert dk == LANES and nkeys == LANES and 4 * hh <= LANES and n % GRID_W == 0
    off_z = 3 * qk_w
    off_ab = 4 * qk_w
    off_hy = off_ab + 4 * hh
    off_gate = off_hy + 3 * hy_w
    assert w_in.shape[2] == off_gate + 2 * d

    cc = jnp.zeros((SUBLANES, d), F32).at[:b].set(c).at[b].set(c_ctx)
    mods = _ada(cc, w_ada[0], b_ada[0][None, :])
    sh1, sc1, g1, sh2, sc2, g2 = [mods[:b, i * d:(i + 1) * d][:, None, :] for i in range(6)]
    sh1c, sc1c = [mods[b:b + 1, i * d:(i + 1) * d][:, None, :] for i in range(2)]

    w = w_in[0]
    w_qkvz = w[:, :off_ab].astype(BF16)
    w_ab = jnp.pad(w[:, off_ab:off_hy], ((0, 0), (0, LANES - 4 * hh))).astype(BF16)
    w_hy = w[:, off_hy:off_gate].astype(BF16)
    w_gate = w[:, off_gate:].astype(BF16)

    x2d = x.reshape(b * n, d)
    c2d = ctx.reshape(b * n_ctx, d)
    proj_qkvz = _modmm(x2d, sc1, sh1, w_qkvz, tm=TM_PROJ, tn=TN_PROJ)
    proj_ab = _modmm(x2d, sc1, sh1, w_ab, tm=TM_PROJ, tn=LANES)
    proj_hy = _modmm(x2d, sc1, sh1, w_hy, tm=TM_PROJ, tn=TN_PROJ)
    proj_gate = _modmm(x2d, sc1, sh1, w_gate, tm=TM_PROJ, tn=TN_PROJ)
    projc_qkvz = _modmm(c2d, sc1c, sh1c, w_qkvz, tm=TM_ROW, tn=TN_PROJ)
    projc_ab = _modmm(c2d, sc1c, sh1c, w_ab, tm=TM_ROW, tn=LANES)

    def decay_row(a):
        row = jnp.concatenate([a, jnp.zeros_like(a)], axis=1).reshape(-1)
        return jnp.pad(row, (0, LANES - 4 * hh))[None, :]

    alog_row = decay_row(dn_a_log[0])
    dt_row = decay_row(dn_dt_bias[0])

    conv_w = dn_conv_w[0]
    prep_c = _dn_prep(projc_qkvz.reshape(b, n_ctx, -1), conv_w, projc_ab.reshape(b, n_ctx, LANES),
                      alog_row, dt_row, n_heads=hh, dk=dk, row_w=n_ctx, tt=n_ctx)
    prep_l = _dn_prep(proj_qkvz.reshape(b, n, -1), conv_w, proj_ab.reshape(b, n, LANES),
                      alog_row, dt_row, n_heads=hh, dk=dk, row_w=GRID_W, tt=TT_DELTA)
    hp = 4 if hh % 4 == 0 else (2 if hh % 2 == 0 else 1)
    s_zero = jnp.zeros((2, b, hh, dk, dk), F32)
    _, _, s_ctx = _dn_scan(*prep_c, s_zero, dk=dk, tt=n_ctx, hp=hp)
    o_f, o_b, _ = _dn_scan(*prep_l, s_ctx, dk=dk, tt=TT_DELTA, hp=hp)

    filt = _hy_filters(n, hy_w1[0], hy_b1[0], hy_w2[0], hy_b2[0], hy_w3[0], hy_b3[0], hy_freq[0], hy_w)
    tables = _dft_tables(n)
    hy3 = proj_hy.reshape(b, n, 3 * hy_w)
    cwh = hy_conv_w[0]
    cb = hy_w // LANES
    y1 = _hy_conv(hy3, 0, hy3, cb, filt, 0, 2 * cb, hy_skip[0, 0:1], cwh, cwh, tables, conv_a=True, hy_w=hy_w)
    y_hy = _hy_conv(y1, 0, hy3, 2 * cb, filt, cb, 3 * cb, hy_skip[0, 1:2], cwh, cwh, tables, conv_a=False, hy_w=hy_w)

    merged = _merge(o_f.reshape(b * n, qk_w), o_b.reshape(b * n, qk_w), proj_qkvz, off_z // qk_w,
                    y_hy.reshape(b * n, hy_w), proj_gate, dn_norm_w[0][None, :],
                    w_branch_dn[0].astype(BF16), w_branch_hy[0].astype(BF16), dv=dk, tm=TM_ROW)
    sk = peer_subkeys[0].reshape(p_heads * 2, nkeys, half).astype(BF16)
    x1, h2, scores = _outproj(merged, x2d, w_out[0].astype(BF16), g1, ln1_g[0][None, :], ln1_b[0][None, :],
                              sc2, sh2, peer_wq[0].astype(BF16), sk, tm=TM_ROW)

    count1, rank2, e1, e2 = _peer_topk(scores, p_heads, tm=TM_ROW)
    out = _peer_dense(h2.T, peer_u[0].astype(BF16), peer_v[0].T.astype(BF16), rank2, e2, count1, e1,
                      x1, g2, ln2_g[0][None, :], ln2_b[0][None, :], n_heads=p_heads, ti=TI_PEER, tm=TM_PEER)
    stages = dict(o_f=o_f, o_b=o_b, s_ctx=s_ctx, filt=filt, y1=y1, y_hy=y_hy, merged=merged, x1=x1, scores=scores,
                  count1=count1, rank2=rank2, e1=e1, e2=e2)
    return out.reshape(b, n, d), stages


def kernel(x, c, ctx, c_ctx, w_ada, b_ada, w_in, dn_conv_w, dn_a_log, dn_dt_bias, dn_norm_w, hy_conv_w, hy_w1, hy_b1, hy_w2, hy_b2, hy_w3, hy_b3, hy_freq, hy_skip, w_branch_dn, w_branch_hy, w_out, ln1_g, ln1_b, peer_wq, peer_subkeys, peer_u, peer_v, ln2_g, ln2_b):
    out, _ = _block(x, c, ctx, c_ctx, w_ada, b_ada, w_in, dn_conv_w, dn_a_log, dn_dt_bias, dn_norm_w, hy_conv_w, hy_w1, hy_b1, hy_w2, hy_b2, hy_w3, hy_b3, hy_freq, hy_skip, w_branch_dn, w_branch_hy, w_out, ln1_g, ln1_b, peer_wq, peer_subkeys, peer_u, peer_v, ln2_g, ln2_b)
    return out
```

```python
import functools
import math

import jax
import jax.numpy as jnp
from jax import lax
from jax.experimental import pallas as pl
from jax.experimental.pallas import tpu as pltpu

F32 = jnp.float32
BF16 = jnp.bfloat16

GRID_W = 64
DN_CHUNK = 64
PEER_TOPK = 16
PEER_BAND = 32
DEPTH = 1
DEEPNORM_ALPHA = (2 * DEPTH) ** 0.25
LN_EPS = 1e-5
RMS_EPS = 1e-6
L2_EPS = 1e-6
HY_DECAY_TARGET = 1e-2
HY_FAST_DECAY = 0.3
HY_SLOW_DECAY = 1.5

LANES = 128
SUBLANES = 8
VMEM_LIMIT_BYTES = 56 * 1024 * 1024

TM_PROJ, TN_PROJ = 512, 2048
TT_DELTA = 4 * DN_CHUNK
TM_ROW = 256
TM_PEER, TI_PEER = 512, 8

FFT_N2 = 64
FFT_ROW_PAD = 8
FFT_UNROLL = 4


def _cparams(sem):
    return pltpu.CompilerParams(dimension_semantics=sem, vmem_limit_bytes=VMEM_LIMIT_BYTES)


def _silu(x):
    return x * jax.nn.sigmoid(x)


def _ada_kernel(c_ref, w_ref, b_ref, o_ref):
    s = _silu(c_ref[...])
    o_ref[...] = jnp.dot(s.astype(BF16), w_ref[...].astype(BF16), preferred_element_type=F32) + b_ref[...]


def _ada(cc, w, b):
    d, n6 = w.shape
    tn = min(n6, 1536)
    return pl.pallas_call(
        _ada_kernel,
        grid=(n6 // tn,),
        in_specs=[pl.BlockSpec((SUBLANES, d), lambda j: (0, 0)),
                  pl.BlockSpec((d, tn), lambda j: (0, j)),
                  pl.BlockSpec((1, tn), lambda j: (0, j))],
        out_specs=pl.BlockSpec((SUBLANES, tn), lambda j: (0, j)),
        out_shape=jax.ShapeDtypeStruct((SUBLANES, n6), F32),
        compiler_params=_cparams(("arbitrary",)),
    )(cc, w, b)


def _modmm_kernel(x_ref, sc_ref, sh_ref, w_ref, o_ref, h_scr):
    @pl.when(pl.program_id(1) == 0)
    def _():
        h_scr[...] = (x_ref[...] * (1.0 + sc_ref[0]) + sh_ref[0]).astype(BF16)

    o_ref[...] = jnp.dot(h_scr[...], w_ref[...], preferred_element_type=F32).astype(o_ref.dtype)


def _modmm(x2d, sc, sh, w, *, tm, tn, out_dtype=F32):
    m, d = x2d.shape
    ng = w.shape[1]
    r = sc.shape[0]
    tm = min(tm, m // r)
    tn = min(tn, ng)
    tiles_per_mod = (m // r) // tm
    return pl.pallas_call(
        _modmm_kernel,
        grid=(m // tm, ng // tn),
        in_specs=[pl.BlockSpec((tm, d), lambda i, j: (i, 0)),
                  pl.BlockSpec((1, 1, d), lambda i, j: (i // tiles_per_mod, 0, 0)),
                  pl.BlockSpec((1, 1, d), lambda i, j: (i // tiles_per_mod, 0, 0)),
                  pl.BlockSpec((d, tn), lambda i, j: (0, j))],
        out_specs=pl.BlockSpec((tm, tn), lambda i, j: (i, j)),
        out_shape=jax.ShapeDtypeStruct((m, ng), out_dtype),
        scratch_shapes=[pltpu.VMEM((tm, d), BF16)],
        compiler_params=_cparams(("parallel", "arbitrary")),
    )(x2d, sc, sh, w)


def _row_conv(x, cw, pos, row_w):
    rows = x.shape[0]
    k = cw.shape[0]
    pad = k // 2
    acc = None
    for j in range(k):
        off = j - pad
        if off == 0:
            term = x * cw[j:j + 1]
        else:
            shifted = pltpu.roll(x, (-off) % rows, axis=0)
            valid = (pos >= -off) if off < 0 else (pos < row_w - off)
            term = jnp.where(valid, shifted, 0.0) * cw[j:j + 1]
        acc = term if acc is None else acc + term
    return acc


SOLVE_BLOCK = 16


def _tri_inverse_pairs(pairs):
    c = pairs[0][0].shape[0]
    sb = SOLVE_BLOCK
    nb = c // sb
    ri = lax.broadcasted_iota(jnp.int32, (c, 2 * c), 0)
    ci = lax.broadcasted_iota(jnp.int32, (c, 2 * c), 1)
    ci = jnp.where(ci >= c, ci - c, ci)
    lane_lo = lax.broadcasted_iota(jnp.int32, (1, 2 * c), 1) < c
    off_diag = ri // sb != ci // sb
    eye2 = (ri == ci).astype(F32)
    mcats = [jnp.concatenate([m_a, m_b], axis=1) for m_a, m_b in pairs]
    m_offs = [jnp.where(off_diag, m, 0.0).astype(BF16) for m in mcats]
    xs = [[eye2[i * sb:(i + 1) * sb] for i in range(nb)] for _ in pairs]
    for bi in range(nb):
        rs = slice(bi * sb, (bi + 1) * sb)
        blks = [x[bi] for x in xs]
        if bi > 0:
            for s, x in enumerate(xs):
                xcur = jnp.concatenate(x, axis=0)
                xbd = jnp.concatenate([jnp.where(lane_lo, xcur, 0.0), jnp.where(lane_lo, 0.0, xcur)], axis=0)
                blks[s] = blks[s] - jnp.dot(m_offs[s][rs], xbd.astype(BF16), preferred_element_type=F32)
        mrs = [m[rs] for m in mcats]
        for jj in range(sb - 1):
            col = bi * sb + jj
            src = jnp.broadcast_to(jnp.where(lane_lo, col, c + col), (sb, 2 * c))
            for s in range(len(pairs)):
                mult = jnp.take_along_axis(mrs[s], src, axis=1)
                blks[s] = blks[s] - mult * blks[s][jj:jj + 1, :]
        for s, x in enumerate(xs):
            x[bi] = blks[s]
    return [jnp.concatenate(x, axis=0) for x in xs]


def _dn_gates_kernel(ab_ref, alog_ref, dt_ref, gc_ref, beta_ref, *, n_heads):
    tt = ab_ref.shape[1]
    c = DN_CHUNK
    cpos = lax.broadcasted_iota(jnp.int32, (tt, 1), 0) % c
    ab = ab_ref[0]
    z = ab + dt_ref[...]
    softplus = jnp.maximum(z, 0.0) + jnp.log1p(jnp.exp(-jnp.abs(z)))
    g_all = -jnp.exp(alog_ref[...]) * softplus
    gf = g_all
    gb = g_all
    d = 1
    while d < c:
        gf = gf + jnp.where(cpos >= d, pltpu.roll(gf, d, axis=0), 0.0)
        gb = gb + jnp.where(cpos < c - d, pltpu.roll(gb, (tt - d) % tt, axis=0), 0.0)
        d *= 2
    lane = lax.broadcasted_iota(jnp.int32, (1, LANES), 1)
    gc_ref[0] = jnp.where(lane < 2 * n_heads, gf, gb)
    beta_ref[0] = jax.nn.sigmoid(ab)


def _dn_gates(ab, alog_row, dt_row, *, n_heads, tt):
    b, n, _ = ab.shape
    blk = pl.BlockSpec((1, tt, LANES), lambda bi, g: (bi, g, 0))
    row = pl.BlockSpec((1, LANES), lambda bi, g: (0, 0))
    shp = jax.ShapeDtypeStruct((b, n, LANES), F32)
    return pl.pallas_call(
        functools.partial(_dn_gates_kernel, n_heads=n_heads),
        grid=(b, n // tt),
        in_specs=[blk, row, row],
        out_specs=[blk, blk],
        out_shape=[shp, shp],
        compiler_params=_cparams(("parallel", "parallel")),
    )(ab, alog_row, dt_row)


def _dn_prep_kernel(xq_ref, xk_ref, xv_ref, cq_ref, ck_ref, cv_ref, gc_ref, beta_ref,
                    u_ref, w_ref, qg_ref, kd_ref, attn_ref, eg_ref, *, row_w, n_heads, dk, hp):
    tt = xq_ref.shape[1]
    c = DN_CHUNK
    pos = lax.broadcasted_iota(jnp.int32, (tt, 1), 0) % row_w
    lane = lax.broadcasted_iota(jnp.int32, (1, LANES), 1)
    gc = gc_ref[0]
    beta_all = beta_ref[0]
    eg_ref[...] = jnp.zeros(eg_ref.shape, F32)
    ri = lax.broadcasted_iota(jnp.int32, (c, c), 0)
    ci = lax.broadcasted_iota(jnp.int32, (c, c), 1)
    eye = ri == ci
    lower = ri >= ci
    upper = ri <= ci
    strict_lower = ri > ci
    nt = (((1,), (1,)), ((), ()))

    def col(a, l):
        return jnp.sum(jnp.where(lane == l, a, 0.0), axis=1, keepdims=True)

    def l2n(y):
        return y * lax.rsqrt(jnp.sum(y * y, axis=-1, keepdims=True) + L2_EPS)

    units = []
    for hh in range(hp):
        h = pl.program_id(1) * hp + hh
        hs = slice(hh * dk, (hh + 1) * dk)
        conv_silu = lambda x_ref, cw_ref: _silu(_row_conv(x_ref[0, :, hs], cw_ref[:, hs], pos, row_w))
        q = l2n(conv_silu(xq_ref, cq_ref)) * (dk ** -0.5)
        k = l2n(conv_silu(xk_ref, ck_ref))
        v = conv_silu(xv_ref, cv_ref)
        g_cols = (col(gc, h), col(gc, 2 * n_heads + h))
        b_cols = (col(beta_all, n_heads + h), col(beta_all, 3 * n_heads + h))
        for cc in range(tt // c):
            sl = slice(cc * c, (cc + 1) * c)
            units.append(dict(hh=hh, hs=hs, cc=cc, sl=sl, q=q[sl], k=k[sl], v=v[sl],
                              g=[g_cols[dr][sl] for dr in range(2)], b=[b_cols[dr][sl] for dr in range(2)]))

    for un in units:
        qc, kc, vc, gcols = un["q"], un["k"], un["v"], un["g"]
        grows = [jnp.sum(jnp.where(eye, g, 0.0), axis=0, keepdims=True) for g in gcols]
        kbs = [kc * un["b"][dr] for dr in range(2)]
        vbs = [vc * un["b"][dr] for dr in range(2)]
        egs = [jnp.exp(g) for g in gcols]
        dif_f = gcols[0] - grows[0]
        dif_b = gcols[1] - grows[1]
        un["decay_f"] = jnp.where(lower, jnp.exp(jnp.where(lower, dif_f, 0.0)), 0.0)
        un["decay_b"] = jnp.where(upper, jnp.exp(jnp.where(upper, dif_b, 0.0)), 0.0)
        un["decay_bt"] = jnp.where(lower, jnp.exp(jnp.where(lower, -dif_b, 0.0)), 0.0)
        un["kcb"] = kc.astype(BF16)
        un["lhs"] = jnp.concatenate([kbs[0], qc], axis=0).astype(BF16)
        un["kb_b"] = kbs[1].astype(BF16)
        un["rhs_f"] = jnp.concatenate([vbs[0], kbs[0] * egs[0]], axis=1).astype(BF16)
        un["rhs_b"] = jnp.concatenate([vbs[1], kbs[1] * egs[1]], axis=1).astype(BF16)
        for dr in range(2):
            gcol = gcols[dr]
            glast = gcol[c - 1:c] if dr == 0 else gcol[0:1]
            qg_ref[dr, 0, un["sl"], un["hs"]] = (qc * egs[dr]).astype(BF16)
            kd_ref[dr, 0, un["sl"], un["hs"]] = (kc * jnp.exp(glast - gcol)).astype(BF16)
            eg_ref[dr, 0, un["hh"], 0, un["cc"]:un["cc"] + 1, :] = jnp.broadcast_to(jnp.exp(glast), (1, LANES))

    for un in units:
        un["kq"] = lax.dot_general(un["lhs"], un["kcb"], nt, preferred_element_type=F32)
        un["kkt_b"] = lax.dot_general(un["kcb"], un["kb_b"], nt, preferred_element_type=F32)

    for un in units:
        un["m_f"] = jnp.where(strict_lower, un["kq"][:c] * un["decay_f"], 0.0)
        un["m_bt"] = jnp.where(strict_lower, un["kkt_b"] * un["decay_bt"], 0.0)
        attn_ref[0, 0, un["hh"], un["sl"], :] = (un["kq"][c:] * un["decay_f"]).astype(BF16)
        attn_ref[1, 0, un["hh"], un["sl"], :] = (un["kq"][c:] * un["decay_b"]).astype(BF16)

    tinvs = _tri_inverse_pairs([(un["m_f"], un["m_bt"]) for un in units])

    zero_rows = jnp.zeros((c, 2 * dk), BF16)
    sols = []
    for tinv, un in zip(tinvs, units):
        tinv = tinv.astype(BF16)
        sol_f = jnp.dot(tinv, jnp.concatenate([un["rhs_f"], zero_rows], axis=0), preferred_element_type=F32)
        sol_b = lax.dot_general(tinv, un["rhs_b"], (((0,), (0,)), ((), ())), preferred_element_type=F32)[c:]
        sols.append((sol_f, sol_b))
    for (sol_f, sol_b), un in zip(sols, units):
        for dr, sol in enumerate((sol_f, sol_b)):
            u_ref[dr, 0, un["sl"], un["hs"]] = sol[:, :dk]
            w_ref[dr, 0, un["sl"], un["hs"]] = sol[:, dk:].astype(BF16)


def _dn_prep(qkv, conv_w, ab, alog_row, dt_row, *, n_heads, dk, row_w, tt):
    gc, beta = _dn_gates(ab, alog_row, dt_row, n_heads=n_heads, tt=tt)
    b, n, _ = qkv.shape
    hh = n_heads
    ng = n // tt
    c = DN_CHUNK
    assert tt % row_w == 0 and tt % c == 0 and tt // c <= SUBLANES
    hp = 2 if hh % 2 == 0 else 1
    nhb = hh // hp
    kern = functools.partial(_dn_prep_kernel, row_w=row_w, n_heads=hh, dk=dk, hp=hp)
    xspec = lambda off: pl.BlockSpec((1, tt, hp * dk), lambda bi, h, g: (bi, g, off + h))
    cspec = lambda off: pl.BlockSpec((conv_w.shape[0], hp * dk), lambda bi, h, g: (0, off + h))
    ospec = pl.BlockSpec((2, 1, tt, hp * dk), lambda bi, h, g: (0, bi, g, h))
    return pl.pallas_call(
        kern,
        grid=(b, nhb, ng),
        in_specs=[xspec(0), xspec(nhb), xspec(2 * nhb), cspec(0), cspec(nhb), cspec(2 * nhb),
                  pl.BlockSpec((1, tt, LANES), lambda bi, h, g: (bi, g, 0)),
                  pl.BlockSpec((1, tt, LANES), lambda bi, h, g: (bi, g, 0))],
        out_specs=[ospec, ospec, ospec, ospec,
                   pl.BlockSpec((2, 1, hp, tt, c), lambda bi, h, g: (0, bi, h, g, 0)),
                   pl.BlockSpec((2, 1, hp, 1, SUBLANES, LANES), lambda bi, h, g: (0, bi, h, g, 0, 0))],
        out_shape=[jax.ShapeDtypeStruct((2, b, n, hh * dk), F32),
                   jax.ShapeDtypeStruct((2, b, n, hh * dk), BF16),
                   jax.ShapeDtypeStruct((2, b, n, hh * dk), BF16),
                   jax.ShapeDtypeStruct((2, b, n, hh * dk), BF16),
                   jax.ShapeDtypeStruct((2, b, hh, n, c), BF16),
                   jax.ShapeDtypeStruct((2, b, hh, ng, SUBLANES, LANES), F32)],
        compiler_params=_cparams(("parallel", "parallel", "arbitrary")),
    )(qkv, qkv, qkv, conv_w, conv_w, conv_w, gc, beta)


def _dn_scan_kernel(uf_ref, wf_ref, qf_ref, kf_ref, af_ref, ef_ref,
                    ub_ref, wb_ref, qb_ref, kb_ref, ab_ref, eb_ref, s0_ref,
                    of_ref, ob_ref, sout_ref, s_scr, *, hp, dk):
    g = pl.program_id(2)
    ng = pl.num_programs(2)
    c = DN_CHUNK
    tt = uf_ref.shape[2]
    ncs = tt // c

    @pl.when(g == 0)
    def _():
        s_scr[...] = s0_ref[:, 0]

    refs = ((uf_ref, wf_ref, qf_ref, kf_ref, af_ref, ef_ref, of_ref),
            (ub_ref, wb_ref, qb_ref, kb_ref, ab_ref, eb_ref, ob_ref))
    chains = [(dr, hh) for dr in range(2) for hh in range(hp)]
    state = {ch: s_scr[ch[0], ch[1]] for ch in chains}
    for step in range(ncs):
        rows = {}
        for dr, hh in chains:
            cc = step if dr == 0 else ncs - 1 - step
            rows[dr, hh] = (slice(cc * c, (cc + 1) * c), slice(hh * dk, (hh + 1) * dk), cc)
        wide = {}
        for dr in range(2):
            sl = rows[dr, 0][0]
            wide[dr] = [refs[dr][i][0, 0, sl, :] for i in range(4)]
        r = {}
        for ch in chains:
            _, hs, _ = rows[ch]
            wq = jnp.concatenate([wide[ch[0]][1][:, hs], wide[ch[0]][2][:, hs]], axis=0)
            r[ch] = jnp.dot(wq, state[ch].astype(BF16), preferred_element_type=F32)
        vnb = {}
        for ch in chains:
            _, hs, _ = rows[ch]
            vnb[ch] = (wide[ch[0]][0][:, hs] - r[ch][:c]).astype(BF16)
        outs = {}
        for ch in chains:
            a_ref, e_ref = refs[ch[0]][4], refs[ch[0]][5]
            sl, hs, cc = rows[ch]
            outs[ch] = r[ch][c:] + jnp.dot(a_ref[0, 0, ch[1], sl, :], vnb[ch], preferred_element_type=F32)
            upd = lax.dot_general(wide[ch[0]][3][:, hs], vnb[ch], (((0,), (0,)), ((), ())),
                                  preferred_element_type=F32)
            state[ch] = state[ch] * e_ref[0, 0, ch[1], 0, cc:cc + 1, :] + upd
        for dr in range(2):
            refs[dr][6][0, rows[dr, 0][0], :] = jnp.concatenate([outs[dr, hh] for hh in range(hp)], axis=1)
    for ch in chains:
        s_scr[ch[0], ch[1]] = state[ch]

    @pl.when(g == ng - 1)
    def _():
        sout_ref[:, 0] = s_scr[...]


def _dn_scan(u, w, qg, kd, attn, eg, s0, *, dk, tt, hp):
    _, b, n, hd = u.shape
    hh = hd // dk
    ng = n // tt
    c = DN_CHUNK
    kern = functools.partial(_dn_scan_kernel, hp=hp, dk=dk)

    def specs(dr):
        gi = (lambda g: g) if dr == 0 else (lambda g: ng - 1 - g)
        tok = pl.BlockSpec((1, 1, tt, hp * dk), lambda bi, h, g: (dr, bi, gi(g), h))
        return [tok, tok, tok, tok,
                pl.BlockSpec((1, 1, hp, tt, c), lambda bi, h, g: (dr, bi, h, gi(g), 0)),
                pl.BlockSpec((1, 1, hp, 1, SUBLANES, LANES), lambda bi, h, g: (dr, bi, h, gi(g), 0, 0))]

    sspec = pl.BlockSpec((2, 1, hp, dk, dk), lambda bi, h, g: (0, bi, h, 0, 0))
    return pl.pallas_call(
        kern,
        grid=(b, hh // hp, ng),
        in_specs=specs(0) + specs(1) + [sspec],
        out_specs=[pl.BlockSpec((1, tt, hp * dk), lambda bi, h, g: (bi, g, h)),
                   pl.BlockSpec((1, tt, hp * dk), lambda bi, h, g: (bi, ng - 1 - g, h)),
                   sspec],
        out_shape=[jax.ShapeDtypeStruct((b, n, hd), F32),
                   jax.ShapeDtypeStruct((b, n, hd), F32),
                   jax.ShapeDtypeStruct((2, b, hh, dk, dk), F32)],
        scratch_shapes=[pltpu.VMEM((2, hp, dk, dk), F32)],
        compiler_params=_cparams(("parallel", "parallel", "arbitrary")),
    )(u, w, qg, kd, attn, eg, u, w, qg, kd, attn, eg, s0)


def _hy_hidden_kernel(z_ref, w1_ref, b1_ref, w2_ref, b2_ref, fr_ref, o_ref):
    fr = fr_ref[...]
    h1 = jnp.sin(fr * (jnp.dot(z_ref[...].astype(BF16), w1_ref[...].astype(BF16),
                               preferred_element_type=F32) + b1_ref[...]))
    o_ref[...] = jnp.sin(fr * (jnp.dot(h1.astype(BF16), w2_ref[...].astype(BF16),
                                       preferred_element_type=F32) + b2_ref[...]))


def _hy_filter_kernel(hid_ref, w3_ref, b3_ref, t_ref, delta_ref, o_ref):
    f = jnp.dot(hid_ref[...].astype(BF16), w3_ref[...].astype(BF16), preferred_element_type=F32) + b3_ref[...]
    o_ref[...] = f * jnp.exp(-t_ref[...] * delta_ref[...])


def _hy_filters(n, w1, b1, w2, b2, w3, b3, freq, hy_w):
    emb, hid = w1.shape
    bands = (emb - 1) // 2
    t = jnp.linspace(0.0, 1.0, n, dtype=F32)
    pos = jnp.arange(n, dtype=F32)
    bnd = jnp.linspace(1e-4, bands - 1, bands, dtype=F32)
    ang = (2.0 * math.pi / n) * pos[:, None] * bnd[None, :]
    z = jnp.concatenate([t[:, None], jnp.cos(ang), -jnp.sin(ang)], axis=-1)
    embp = -(-emb // SUBLANES) * SUBLANES
    z = jnp.pad(z, ((0, 0), (0, embp - emb)))
    w1p = jnp.pad(w1, ((0, embp - emb), (0, 0)))
    full = lambda a: pl.BlockSpec(a.shape, lambda: (0,) * a.ndim)
    args = (z, w1p, b1[None, :], w2, b2[None, :], freq[None, :])
    hid2 = pl.pallas_call(
        _hy_hidden_kernel,
        in_specs=[full(a) for a in args],
        out_specs=pl.BlockSpec((n, hid), lambda: (0, 0)),
        out_shape=jax.ShapeDtypeStruct((n, hid), F32),
        compiler_params=pltpu.CompilerParams(vmem_limit_bytes=VMEM_LIMIT_BYTES),
    )(*args)
    nout = w3.shape[1]
    deltas = jnp.abs(jnp.linspace(math.log(HY_DECAY_TARGET) / HY_SLOW_DECAY,
                                  math.log(HY_DECAY_TARGET) / HY_FAST_DECAY, hy_w, dtype=F32))
    delta_row = jnp.tile(deltas, nout // hy_w)[None, :]
    tc = min(nout, 1024)
    return pl.pallas_call(
        _hy_filter_kernel,
        grid=(nout // tc,),
        in_specs=[pl.BlockSpec((n, hid), lambda j: (0, 0)),
                  pl.BlockSpec((hid, tc), lambda j: (0, j)),
                  pl.BlockSpec((1, tc), lambda j: (0, j)),
                  pl.BlockSpec((n, 1), lambda j: (0, 0)),
                  pl.BlockSpec((1, tc), lambda j: (0, j))],
        out_specs=pl.BlockSpec((n, tc), lambda j: (0, j)),
        out_shape=jax.ShapeDtypeStruct((n, nout), F32),
        compiler_params=_cparams(("parallel",)),
    )(hid2, w3, b3[None, :], t[:, None], delta_row)


def _dft_tables(n):
    n2 = FFT_N2
    big = 2 * n
    n1 = big // n2
    k1 = n1 // 2
    two_pi = 2.0 * math.pi

    def cs(idx, period):
        a = (idx % period).astype(F32) * (two_pi / period)
        return jnp.cos(a), jnp.sin(a)

    f1 = jnp.arange(n1, dtype=jnp.int32)
    s1 = jnp.arange(k1, dtype=jnp.int32)
    s2 = jnp.arange(n2, dtype=jnp.int32)
    s_full = s1[None, None, :] * n2 + s2[:, None, None]
    c, s = cs(f1[None, :, None] * s_full, big)
    f1tw = jnp.concatenate([jnp.concatenate([c, s], axis=2), jnp.concatenate([-s, c], axis=2)], axis=1)
    ct, st = jnp.swapaxes(c, 1, 2), jnp.swapaxes(s, 1, 2)
    g1tw = jnp.concatenate([jnp.concatenate([ct, -st], axis=2), jnp.concatenate([st, ct], axis=2)], axis=1) / big
    c2, sn2 = cs(s2[:, None] * s2[None, :], n2)
    f2 = jnp.concatenate([jnp.concatenate([c2, sn2], axis=1), jnp.concatenate([-sn2, c2], axis=1)], axis=0)
    g2 = jnp.concatenate([jnp.concatenate([c2, -sn2], axis=1), jnp.concatenate([sn2, c2], axis=1)], axis=0)
    return f1tw.astype(BF16), f2.astype(BF16), g2.astype(BF16), g1tw.astype(BF16)


def _hy_conv_kernel(a_ref, m_ref, hf_ref, hb_ref, skip_ref, cwa_ref, cwm_ref,
                    f1_ref, f2_ref, g2_ref, g1_ref, o_ref,
                    h_scr, b_scr, zr_scr, zi_scr, *, conv_a, n1, k1):
    n2 = FFT_N2
    k1p = k1 + FFT_ROW_PAD
    pb = 2 * n2 + FFT_ROW_PAD
    pos = lax.broadcasted_iota(jnp.int32, (n2, 1), 0)

    grp = FFT_UNROLL
    cw = b_scr.shape[1]

    def stage1(real_only):
        def body(g, carry):
            s2s = [g * grp + t for t in range(grp)]
            bases = [pl.multiple_of(s2 * k1p, SUBLANES) for s2 in s2s]
            if real_only:
                xs = [zr_scr[pl.ds(base, k1), :].astype(BF16) for base in bases]
                outs = [jnp.dot(f1_ref[s2, :, :k1], x, preferred_element_type=F32) for s2, x in zip(s2s, xs)]
            else:
                xs = [jnp.concatenate([zr_scr[pl.ds(base, k1), :], zi_scr[pl.ds(base, k1), :]], axis=0).astype(BF16)
                      for base in bases]
                outs = [jnp.dot(f1_ref[s2], x, preferred_element_type=F32) for s2, x in zip(s2s, xs)]
            for s2, a in zip(s2s, outs):
                b_scr[pl.ds(s2, n1, stride=pb), :] = a[:n1]
                b_scr[pl.ds(n2 + s2, n1, stride=pb), :] = a[n1:]
            return carry
        lax.fori_loop(0, n2 // grp, body, 0)

    def slab_pairs(g):
        f1s = [[(g * grp + t) * 2 + u for u in range(2)] for t in range(grp)]
        bbs = [[pl.multiple_of(f1 * pb, SUBLANES) for f1 in pair] for pair in f1s]
        hbs = [[pl.multiple_of(f1 * 2 * n2, 2 * n2) for f1 in pair] for pair in f1s]
        return bbs, hbs

    def forward2(bbs):
        bpairs = [jnp.concatenate([b_scr[pl.ds(bb, 2 * n2), :] for bb in pair], axis=1).astype(BF16) for pair in bbs]
        return [jnp.dot(f2_ref[...], bp, preferred_element_type=F32) for bp in bpairs]

    @pl.when(pl.program_id(1) == 0)
    def _():
        for which, ref in ((0, hf_ref), (1, hb_ref)):
            def load(s1, carry, ref=ref):
                r0 = pl.multiple_of(s1 * n2, n2)
                zr_scr[pl.ds(s1, n2, stride=k1p), :] = ref[pl.ds(r0, n2), :]
                return carry
            lax.fori_loop(0, k1, load, 0, unroll=FFT_UNROLL)
            if which == 1:
                zr_scr[0:1, :] = jnp.zeros((1, zr_scr.shape[1]), F32)
            stage1(True)

            def spec(g, carry, which=which):
                bbs, hbs = slab_pairs(g)
                zpairs = forward2(bbs)
                for zpair, hpair in zip(zpairs, hbs):
                    for t, hb_ in enumerate(hpair):
                        zz = zpair[:, t * cw:(t + 1) * cw]
                        if which == 0:
                            h_scr[pl.ds(hb_, 2 * n2), :] = zz
                        else:
                            h_scr[pl.ds(hb_, n2), :] = h_scr[pl.ds(hb_, n2), :] + zz[:n2]
                            h_scr[pl.ds(hb_ + n2, n2), :] = h_scr[pl.ds(hb_ + n2, n2), :] - zz[n2:]
                return carry
            lax.fori_loop(0, n1 // (2 * grp), spec, 0)

    def load_a(s1, carry):
        r0 = pl.multiple_of(s1 * n2, n2)
        for bi, scr in ((0, zr_scr), (1, zi_scr)):
            blk = a_ref[bi, pl.ds(r0, n2), :]
            if conv_a:
                blk = _row_conv(blk, cwa_ref[...], pos, n2)
            scr[pl.ds(s1, n2, stride=k1p), :] = blk
        return carry
    lax.fori_loop(0, k1, load_a, 0, unroll=FFT_UNROLL)

    stage1(False)

    def mid(g, carry):
        bbs, hbs = slab_pairs(g)
        zzs = forward2(bbs)
        yys = []
        for zz, hpair in zip(zzs, hbs):
            zr, zi = zz[:n2], zz[n2:]
            hr = jnp.concatenate([h_scr[pl.ds(hb_, n2), :] for hb_ in hpair], axis=1)
            hi = jnp.concatenate([h_scr[pl.ds(hb_ + n2, n2), :] for hb_ in hpair], axis=1)
            yys.append(jnp.concatenate([zr * hr - zi * hi, zr * hi + zi * hr], axis=0).astype(BF16))
        wws = [jnp.dot(g2_ref[...], yy, preferred_element_type=F32) for yy in yys]
        for ww, pair in zip(wws, bbs):
            for t, bb in enumerate(pair):
                b_scr[pl.ds(bb, 2 * n2), :] = ww[:, t * cw:(t + 1) * cw]
        return carry
    lax.fori_loop(0, n1 // (2 * grp), mid, 0)

    skip = skip_ref[...]

    def last(g, carry):
        s2s = [g * grp + t for t in range(grp)]
        wws = [jnp.concatenate([b_scr[pl.ds(s2, n1, stride=pb), :],
                                b_scr[pl.ds(n2 + s2, n1, stride=pb), :]], axis=0).astype(BF16) for s2 in s2s]
        ys = [jnp.dot(g1_ref[s2], ww, preferred_element_type=F32) for s2, ww in zip(s2s, wws)]
        for s2, y in zip(s2s, ys):
            base = pl.multiple_of(s2 * k1p, SUBLANES)
            zr_scr[pl.ds(base, k1), :] = y[:k1] + zr_scr[pl.ds(base, k1), :] * skip
            zi_scr[pl.ds(base, k1), :] = y[k1:] + zi_scr[pl.ds(base, k1), :] * skip
        return carry
    lax.fori_loop(0, n2 // grp, last, 0)

    def fin(s1, carry):
        r0 = pl.multiple_of(s1 * n2, n2)
        for bi, scr in ((0, zr_scr), (1, zi_scr)):
            mm = _row_conv(m_ref[bi, pl.ds(r0, n2), :], cwm_ref[...], pos, n2)
            o_ref[bi, pl.ds(r0, n2), :] = mm * scr[pl.ds(s1, n2, stride=k1p), :]
        return carry
    lax.fori_loop(0, k1, fin, 0, unroll=FFT_UNROLL)


def _hy_conv(a, a_col0, m, m_col0, filt, f_col_fwd, f_col_bwd, skip, cwa, cwm, tables, *, conv_a, hy_w):
    b, n, _ = a.shape
    assert b % 2 == 0 and GRID_W == FFT_N2 and n % FFT_N2 == 0
    n2 = FFT_N2
    n1 = 2 * n // n2
    k1 = n1 // 2
    k1p = k1 + FFT_ROW_PAD
    pb = 2 * n2 + FFT_ROW_PAD
    cw = LANES
    f1tw, f2, g2, g1tw = tables
    kern = functools.partial(_hy_conv_kernel, conv_a=conv_a, n1=n1, k1=k1)
    once = pl.Buffered(1)
    return pl.pallas_call(
        kern,
        grid=(hy_w // cw, b // 2),
        in_specs=[pl.BlockSpec((2, n, cw), lambda c, p: (p, 0, a_col0 + c), pipeline_mode=once),
                  pl.BlockSpec((2, n, cw), lambda c, p: (p, 0, m_col0 + c), pipeline_mode=once),
                  pl.BlockSpec((n, cw), lambda c, p: (0, f_col_fwd + c), pipeline_mode=once),
                  pl.BlockSpec((n, cw), lambda c, p: (0, f_col_bwd + c), pipeline_mode=once),
                  pl.BlockSpec((1, cw), lambda c, p: (0, c)),
                  pl.BlockSpec((cwa.shape[0], cw), lambda c, p: (0, a_col0 + c if conv_a else 0)),
                  pl.BlockSpec((cwm.shape[0], cw), lambda c, p: (0, m_col0 + c)),
                  pl.BlockSpec(f1tw.shape, lambda c, p: (0, 0, 0), pipeline_mode=once),
                  pl.BlockSpec(f2.shape, lambda c, p: (0, 0)),
                  pl.BlockSpec(g2.shape, lambda c, p: (0, 0)),
                  pl.BlockSpec(g1tw.shape, lambda c, p: (0, 0, 0), pipeline_mode=once)],
        out_specs=pl.BlockSpec((2, n, cw), lambda c, p: (p, 0, c)),
        out_shape=jax.ShapeDtypeStruct((b, n, hy_w), F32),
        scratch_shapes=[pltpu.VMEM((n1 * 2 * n2, cw), F32),
                        pltpu.VMEM((n1 * pb, cw), F32),
                        pltpu.VMEM((n2 * k1p, cw), F32),
                        pltpu.VMEM((n2 * k1p, cw), F32)],
        compiler_params=_cparams(("parallel", "arbitrary")),
    )(a, m, filt, filt, skip, cwa, cwm, f1tw, f2, g2, g1tw)


def _merge_kernel(of_ref, ob_ref, z_ref, y_ref, ga_ref, gb_ref, nw_ref, wa_ref, wh_ref, o_ref, *, dv):
    nw = nw_ref[...]
    o_all = of_ref[...] + ob_ref[...]
    gate_all = _silu(z_ref[...])
    heads = []
    for hh in range(of_ref.shape[1] // dv):
        hs = slice(hh * dv, (hh + 1) * dv)
        o = o_all[:, hs]
        o = o * lax.rsqrt(jnp.mean(o * o, axis=-1, keepdims=True) + RMS_EPS) * nw
        heads.append((o * gate_all[:, hs]).astype(BF16))
    pa = jnp.dot(jnp.concatenate(heads, axis=1), wa_ref[...], preferred_element_type=F32)
    ph = jnp.dot(y_ref[...].astype(BF16), wh_ref[...], preferred_element_type=F32)
    o_ref[...] = (jax.nn.sigmoid(ga_ref[...]) * pa + jax.nn.sigmoid(gb_ref[...]) * ph).astype(o_ref.dtype)


def _merge(o_f, o_b, z, z_blk, y_hy, gate, norm_w, w_pa, w_ph, *, dv, tm):
    m, dvw = o_f.shape
    d = w_pa.shape[1]
    tm = min(tm, m)
    kern = functools.partial(_merge_kernel, dv=dv)
    row = lambda w_, blk=0: pl.BlockSpec((tm, w_), lambda i: (i, blk))
    const = lambda a: pl.BlockSpec(a.shape, lambda i: (0, 0), pipeline_mode=pl.Buffered(1))
    return pl.pallas_call(
        kern,
        grid=(m // tm,),
        in_specs=[row(dvw), row(dvw), row(dvw, z_blk), row(y_hy.shape[1]), row(d, 0), row(d, 1),
                  pl.BlockSpec((1, dv), lambda i: (0, 0)), const(w_pa), const(w_ph)],
        out_specs=row(d),
        out_shape=jax.ShapeDtypeStruct((m, d), BF16),
        compiler_params=_cparams(("parallel",)),
    )(o_f, o_b, z, y_hy, gate, gate, norm_w, w_pa, w_ph)


def _layer_norm(x, g, b):
    mu = jnp.mean(x, axis=-1, keepdims=True)
    xc = x - mu
    var = jnp.mean(xc * xc, axis=-1, keepdims=True)
    return xc * lax.rsqrt(var + LN_EPS) * g + b


def _outproj_kernel(mg_ref, x_ref, wo_ref, g1_ref, lg_ref, lb_ref, sc_ref, sh_ref, wq_ref, sk_ref,
                    x1_ref, h2_ref, s_ref, *, half):
    mix = jnp.dot(mg_ref[...], wo_ref[...], preferred_element_type=F32)
    x1 = _layer_norm(DEEPNORM_ALPHA * x_ref[...] + g1_ref[0] * mix, lg_ref[...], lb_ref[...])
    x1_ref[...] = x1
    h2 = (x1 * (1.0 + sc_ref[0]) + sh_ref[0]).astype(BF16)
    h2_ref[...] = h2
    qp = jnp.dot(h2, wq_ref[...], preferred_element_type=F32).astype(BF16)
    for j in range(sk_ref.shape[0]):
        js = slice(j * half, (j + 1) * half)
        s_ref[j] = lax.dot_general(sk_ref[j], qp[:, js], (((1,), (1,)), ((), ())), preferred_element_type=F32)


def _outproj(merged, x2d, w_out, g1, ln_g, ln_b, sc2, sh2, wq, subkeys, *, tm):
    m, d = x2d.shape
    r = g1.shape[0]
    tm = min(tm, m // r)
    tpm = (m // r) // tm
    nsk, nkeys, half = subkeys.shape
    assert nkeys == LANES
    kern = functools.partial(_outproj_kernel, half=half)
    row = lambda w_: pl.BlockSpec((tm, w_), lambda i: (i, 0))
    mod = pl.BlockSpec((1, 1, d), lambda i: (i // tpm, 0, 0))
    vec = pl.BlockSpec((1, d), lambda i: (0, 0))
    return pl.pallas_call(
        kern,
        grid=(m // tm,),
        in_specs=[row(d), row(d), pl.BlockSpec(w_out.shape, lambda i: (0, 0), pipeline_mode=pl.Buffered(1)),
                  mod, vec, vec, mod, mod,
                  pl.BlockSpec(wq.shape, lambda i: (0, 0), pipeline_mode=pl.Buffered(1)),
                  pl.BlockSpec(subkeys.shape, lambda i: (0, 0, 0))],
        out_specs=[row(d), row(d), pl.BlockSpec((nsk, nkeys, tm), lambda i: (0, 0, i))],
        out_shape=[jax.ShapeDtypeStruct((m, d), F32),
                   jax.ShapeDtypeStruct((m, d), BF16),
                   jax.ShapeDtypeStruct((nsk, nkeys, m), F32)],
        compiler_params=_cparams(("parallel",)),
    )(merged, x2d, w_out, g1, ln_g, ln_b, sc2, sh2, wq, subkeys)


def _cand_tables():
    pairs = [(a, b) for a in range(PEER_TOPK) for b in range(PEER_TOPK) if (a + 1) * (b + 1) <= PEER_TOPK]
    ia = jnp.array([p[0] for p in pairs], dtype=jnp.int32)
    ib = jnp.array([p[1] for p in pairs], dtype=jnp.int32)
    ncand = len(pairs)
    rows = -(-ncand // (2 * SUBLANES)) * (2 * SUBLANES)
    ranks = jnp.arange(LANES, dtype=jnp.int32)
    p1 = (jnp.pad(ia, (0, rows - ncand), constant_values=-1)[:, None] == ranks[None, :]).astype(F32)
    p2 = (jnp.pad(ib, (0, rows - ncand), constant_values=-1)[:, None] == ranks[None, :]).astype(F32)
    return p1, p2, ncand


def _peer_topk_kernel(s_ref, p1_ref, p2_ref, p1t_ref, c1_ref, r2_ref, e1_ref, e2_ref, *, n_heads, ncand):
    tm = s_ref.shape[2]
    neg = -jnp.inf
    hi = lax.Precision.HIGHEST
    no_rank = float(LANES)

    def pop_max(x, row_id):
        mx = jnp.max(x, axis=0, keepdims=True)
        first = jnp.min(jnp.where(x == mx, row_id, no_rank), axis=0, keepdims=True)
        hit = row_id == first
        return mx, hit, jnp.where(hit, neg, x)

    key_id = lax.broadcasted_iota(jnp.int32, (LANES, tm), 0).astype(F32)
    cand_id = lax.broadcasted_iota(jnp.int32, (p1_ref.shape[0], tm), 0).astype(F32)
    pad_rows = jnp.zeros((LANES - PEER_TOPK, tm), F32)

    def sorted_top(s):
        tops = []
        rank = jnp.full(s.shape, no_rank, F32)
        x = s
        for a in range(PEER_TOPK):
            mx, hit, x = pop_max(x, key_id)
            tops.append(mx)
            rank = jnp.where(hit, float(a), rank)
        return jnp.concatenate(tops + [pad_rows], axis=0), rank

    for h in range(n_heads):
        s1 = s_ref[2 * h]
        s2 = s_ref[2 * h + 1]
        t1, rank1 = sorted_top(s1)
        t2, rank2 = sorted_top(s2)
        cand = (jnp.dot(p1_ref[...], t1, precision=hi, preferred_element_type=F32)
                + jnp.dot(p2_ref[...], t2, precision=hi, preferred_element_type=F32))
        x = jnp.where(cand_id < ncand, cand, neg)
        sel = jnp.zeros(x.shape, F32)
        zsum = None
        cmax = None
        for r in range(PEER_TOPK):
            mx, hit, x = pop_max(x, cand_id)
            sel = jnp.where(hit, 1.0, sel)
            if r == 0:
                cmax = mx
                zsum = jnp.ones_like(mx)
            else:
                zsum = zsum + jnp.exp(mx - cmax)
        cnt = jnp.dot(p1t_ref[...].astype(BF16), sel.astype(BF16), preferred_element_type=F32)
        count1 = jnp.zeros(s1.shape, F32)
        for a in range(PEER_TOPK):
            count1 = jnp.where(rank1 == a, cnt[a:a + 1], count1)
        c1_ref[h] = count1
        r2_ref[h] = rank2.astype(r2_ref.dtype)
        e1_ref[h] = jnp.exp(s1 - t1[0:1]) / zsum
        e2_ref[h] = jnp.exp(s2 - t2[0:1]).astype(e2_ref.dtype)


def _peer_topk(scores, n_heads, *, tm):
    nsk, nkeys, m = scores.shape
    tm = min(tm, m)
    p1, p2, ncand = _cand_tables()
    kern = functools.partial(_peer_topk_kernel, n_heads=n_heads, ncand=ncand)
    full = lambda a: pl.BlockSpec(a.shape, lambda i: (0, 0))
    out = pl.BlockSpec((n_heads, nkeys, tm), lambda i: (0, 0, i))
    shp = lambda dt: jax.ShapeDtypeStruct((n_heads, nkeys, m), dt)
    p1t = p1.T
    return pl.pallas_call(
        kern,
        grid=(m // tm,),
        in_specs=[pl.BlockSpec((nsk, nkeys, tm), lambda i: (0, 0, i)), full(p1), full(p2), full(p1t)],
        out_specs=[out, out, out, out],
        out_shape=[shp(F32), shp(F32), shp(F32), shp(F32)],
        compiler_params=_cparams(("parallel",)),
    )(scores, p1, p2, p1t)


def _peer_dense_kernel(h2t_ref, u_ref, vt_ref, r2_ref, e2_ref, c1_ref, e1_ref, x1_ref, g2_ref, lg_ref, lb_ref,
                       o_ref, acc_scr, a_scr, p_scr, r2_scr, e2_scr, *, n_heads, ti):
    e = pl.program_id(1)

    @pl.when(e == 0)
    def _():
        acc_scr[...] = jnp.zeros(acc_scr.shape, F32)
        r2_scr[...] = r2_ref[...].astype(BF16)
        e2_scr[...] = e2_ref[...].astype(BF16)

    a_scr[...] = jnp.dot(u_ref[...], h2t_ref[...], preferred_element_type=F32)
    nkeys = r2_ref.shape[1]
    for ii in range(ti):
        c1_rows = [c1_ref[h, ii:ii + 1, :].astype(BF16) for h in range(n_heads)]
        e1_rows = [e1_ref[h, ii:ii + 1, :].astype(BF16) for h in range(n_heads)]
        for band in range(nkeys // PEER_BAND):
            ks = slice(band * PEER_BAND, (band + 1) * PEER_BAND)
            rs = slice(ii * nkeys + band * PEER_BAND, ii * nkeys + (band + 1) * PEER_BAND)
            gate = None
            for h in range(n_heads):
                keep = r2_scr[h, ks, :] < c1_rows[h]
                term = jnp.where(keep, e2_scr[h, ks, :], jnp.zeros((), BF16)) * e1_rows[h]
                gate = term if gate is None else gate + term
            a = a_scr[rs, :]
            act = 0.5 * a * (1.0 + lax.erf(a * (2.0 ** -0.5)))
            p_scr[rs, :] = gate * act.astype(BF16)
    acc_scr[...] += jnp.dot(vt_ref[...], p_scr[...], preferred_element_type=F32)

    @pl.when(e == pl.num_programs(1) - 1)
    def _():
        o_ref[...] = _layer_norm(DEEPNORM_ALPHA * x1_ref[...] + g2_ref[0] * acc_scr[...].T,
                                 lg_ref[...], lb_ref[...])


def _peer_dense(h2t, u_tab, vt_tab, rank2, e2, count1, e1, x1, g2, ln_g, ln_b, *, n_heads, ti, tm):
    d, m = h2t.shape
    r = g2.shape[0]
    tm = min(tm, m // r)
    tpm = (m // r) // tm
    ne = u_tab.shape[0]
    nkeys = rank2.shape[1]
    te = ti * nkeys
    kern = functools.partial(_peer_dense_kernel, n_heads=n_heads, ti=ti)
    row = lambda w_: pl.BlockSpec((tm, w_), lambda i, e: (i, 0), pipeline_mode=pl.Buffered(1))
    vec = pl.BlockSpec((1, d), lambda i, e: (0, 0))
    tbl = pl.BlockSpec((te, d), lambda i, e: (e, 0))
    second = pl.BlockSpec((n_heads, nkeys, tm), lambda i, e: (0, 0, i), pipeline_mode=pl.Buffered(1))
    first = pl.BlockSpec((n_heads, ti, tm), lambda i, e: (0, e, i))
    return pl.pallas_call(
        kern,
        grid=(m // tm, ne // te),
        in_specs=[pl.BlockSpec((d, tm), lambda i, e: (0, i), pipeline_mode=pl.Buffered(1)),
                  tbl, pl.BlockSpec((d, te), lambda i, e: (0, e)), second, second, first, first, row(d),
                  pl.BlockSpec((1, 1, d), lambda i, e: (i // tpm, 0, 0)), vec, vec],
        out_specs=pl.BlockSpec((tm, d), lambda i, e: (i, 0)),
        out_shape=jax.ShapeDtypeStruct((m, d), F32),
        scratch_shapes=[pltpu.VMEM((d, tm), F32), pltpu.VMEM((te, tm), F32), pltpu.VMEM((te, tm), BF16),
                        pltpu.VMEM((n_heads, nkeys, tm), BF16), pltpu.VMEM((n_heads, nkeys, tm), BF16)],
        compiler_params=_cparams(("parallel", "arbitrary")),
    )(h2t, u_tab, vt_tab, rank2, e2, count1, e1, x1, g2, ln_g, ln_b)


def _block(x, c, ctx, c_ctx, w_ada, b_ada, w_in, dn_conv_w, dn_a_log, dn_dt_bias, dn_norm_w, hy_conv_w, hy_w1, hy_b1, hy_w2, hy_b2, hy_w3, hy_b3, hy_freq, hy_skip, w_branch_dn, w_branch_hy, w_out, ln1_g, ln1_b, peer_wq, peer_subkeys, peer_u, peer_v, ln2_g, ln2_b):
    b, n, d = x.shape
    n_ctx = ctx.shape[1]
    assert w_ada.shape[0] == DEPTH == 1, "single-layer block: the context stream update is never read"
    hh = dn_a_log.shape[2]
    dk = dn_conv_w.shape[2] // (3 * hh)
    qk_w = hh * dk
    hy_w = hy_skip.shape[2]
    p_heads, _, nkeys, half = peer_subkeys.shape[1:]
    assert dk == LANES and nkeys == LANES and 4 * hh <= LANES and n % GRID_W == 0
    off_z = 3 * qk_w
    off_ab = 4 * qk_w
    off_hy = off_ab + 4 * hh
    off_gate = off_hy + 3 * hy_w
    assert w_in.shape[2] == off_gate + 2 * d

    cc = jnp.zeros((SUBLANES, d), F32).at[:b].set(c).at[b].set(c_ctx)
    mods = _ada(cc, w_ada[0], b_ada[0][None, :])
    sh1, sc1, g1, sh2, sc2, g2 = [mods[:b, i * d:(i + 1) * d][:, None, :] for i in range(6)]
    sh1c, sc1c = [mods[b:b + 1, i * d:(i + 1) * d][:, None, :] for i in range(2)]

    w = w_in[0]
    w_qkvz = w[:, :off_ab].astype(BF16)
    w_ab = jnp.pad(w[:, off_ab:off_hy], ((0, 0), (0, LANES - 4 * hh))).astype(BF16)
    w_hy = w[:, off_hy:off_gate].astype(BF16)
    w_gate = w[:, off_gate:].astype(BF16)

    x2d = x.reshape(b * n, d)
    c2d = ctx.reshape(b * n_ctx, d)
    proj_qkvz = _modmm(x2d, sc1, sh1, w_qkvz, tm=TM_PROJ, tn=TN_PROJ)
    proj_ab = _modmm(x2d, sc1, sh1, w_ab, tm=TM_PROJ, tn=LANES)
    proj_hy = _modmm(x2d, sc1, sh1, w_hy, tm=TM_PROJ, tn=TN_PROJ)
    proj_gate = _modmm(x2d, sc1, sh1, w_gate, tm=TM_PROJ, tn=TN_PROJ)
    projc_qkvz = _modmm(c2d, sc1c, sh1c, w_qkvz, tm=TM_ROW, tn=TN_PROJ)
    projc_ab = _modmm(c2d, sc1c, sh1c, w_ab, tm=TM_ROW, tn=LANES)

    def decay_row(a):
        row = jnp.concatenate([a, jnp.zeros_like(a)], axis=1).reshape(-1)
        return jnp.pad(row, (0, LANES - 4 * hh))[None, :]

    alog_row = decay_row(dn_a_log[0])
    dt_row = decay_row(dn_dt_bias[0])

    conv_w = dn_conv_w[0]
    prep_c = _dn_prep(projc_qkvz.reshape(b, n_ctx, -1), conv_w, projc_ab.reshape(b, n_ctx, LANES),
                      alog_row, dt_row, n_heads=hh, dk=dk, row_w=n_ctx, tt=n_ctx)
    prep_l = _dn_prep(proj_qkvz.reshape(b, n, -1), conv_w, proj_ab.reshape(b, n, LANES),
                      alog_row, dt_row, n_heads=hh, dk=dk, row_w=GRID_W, tt=TT_DELTA)
    hp = 4 if hh % 4 == 0 else (2 if hh % 2 == 0 else 1)
    s_zero = jnp.zeros((2, b, hh, dk, dk), F32)
    _, _, s_ctx = _dn_scan(*prep_c, s_zero, dk=dk, tt=n_ctx, hp=hp)
    o_f, o_b, _ = _dn_scan(*prep_l, s_ctx, dk=dk, tt=TT_DELTA, hp=hp)

    filt = _hy_filters(n, hy_w1[0], hy_b1[0], hy_w2[0], hy_b2[0], hy_w3[0], hy_b3[0], hy_freq[0], hy_w)
    tables = _dft_tables(n)
    hy3 = proj_hy.reshape(b, n, 3 * hy_w)
    cwh = hy_conv_w[0]
    cb = hy_w // LANES
    y1 = _hy_conv(hy3, 0, hy3, cb, filt, 0, 2 * cb, hy_skip[0, 0:1], cwh, cwh, tables, conv_a=True, hy_w=hy_w)
    y_hy = _hy_conv(y1, 0, hy3, 2 * cb, filt, cb, 3 * cb, hy_skip[0, 1:2], cwh, cwh, tables, conv_a=False, hy_w=hy_w)

    merged = _merge(o_f.reshape(b * n, qk_w), o_b.reshape(b * n, qk_w), proj_qkvz, off_z // qk_w,
                    y_hy.reshape(b * n, hy_w), proj_gate, dn_norm_w[0][None, :],
                    w_branch_dn[0].astype(BF16), w_branch_hy[0].astype(BF16), dv=dk, tm=TM_ROW)
    sk = peer_subkeys[0].reshape(p_heads * 2, nkeys, half).astype(BF16)
    x1, h2, scores = _outproj(merged, x2d, w_out[0].astype(BF16), g1, ln1_g[0][None, :], ln1_b[0][None, :],
                              sc2, sh2, peer_wq[0].astype(BF16), sk, tm=TM_ROW)

    count1, rank2, e1, e2 = _peer_topk(scores, p_heads, tm=TM_ROW)
    out = _peer_dense(h2.T, peer_u[0].astype(BF16), peer_v[0].T.astype(BF16), rank2, e2, count1, e1,
                      x1, g2, ln2_g[0][None, :], ln2_b[0][None, :], n_heads=p_heads, ti=TI_PEER, tm=TM_PEER)
    stages = dict(o_f=o_f, o_b=o_b, s_ctx=s_ctx, filt=filt, y1=y1, y_hy=y_hy, merged=merged, x1=x1, scores=scores,
                  count1=count1, rank2=rank2, e1=e1, e2=e2)
    return out.reshape(b, n, d), stages


def kernel(x, c, ctx, c_ctx, w_ada, b_ada, w_in, dn_conv_w, dn_a_log, dn_dt_bias, dn_norm_w, hy_conv_w, hy_w1, hy_b1, hy_w2, hy_b2, hy_w3, hy_b3, hy_freq, hy_skip, w_branch_dn, w_branch_hy, w_out, ln1_g, ln1_b, peer_wq, peer_subkeys, peer_u, peer_v, ln2_g, ln2_b):
    out, _ = _block(x, c, ctx, c_ctx, w_ada, b_ada, w_in, dn_conv_w, dn_a_log, dn_dt_bias, dn_norm_w, hy_conv_w, hy_w1, hy_b1, hy_w2, hy_b2, hy_w3, hy_b3, hy_freq, hy_skip, w_branch_dn, w_branch_hy, w_out, ln1_g, ln1_b, peer_wq, peer_subkeys, peer_u, peer_v, ln2_g, ln2_b)
    return out
```

```python
import functools
import math

import jax
import jax.numpy as jnp
from jax import lax
from jax.experimental import pallas as pl
from jax.experimental.pallas import tpu as pltpu

F32 = jnp.float32
BF16 = jnp.bfloat16

GRID_W = 64
DN_CHUNK = 64
PEER_TOPK = 16
PEER_BAND = 32
DEPTH = 1
DEEPNORM_ALPHA = (2 * DEPTH) ** 0.25
LN_EPS = 1e-5
RMS_EPS = 1e-6
L2_EPS = 1e-6
HY_DECAY_TARGET = 1e-2
HY_FAST_DECAY = 0.3
HY_SLOW_DECAY = 1.5

LANES = 128
SUBLANES = 8
VMEM_LIMIT_BYTES = 56 * 1024 * 1024

TM_PROJ, TN_PROJ = 512, 2048
TT_DELTA = 4 * DN_CHUNK
TM_ROW = 256
TM_PEER, TI_PEER = 512, 8

FFT_N2 = 64
FFT_ROW_PAD = 8
FFT_UNROLL = 16


def _cparams(sem):
    return pltpu.CompilerParams(dimension_semantics=sem, vmem_limit_bytes=VMEM_LIMIT_BYTES)


def _silu(x):
    return x * jax.nn.sigmoid(x)


def _ada_kernel(c_ref, w_ref, b_ref, o_ref):
    s = _silu(c_ref[...])
    o_ref[...] = jnp.dot(s.astype(BF16), w_ref[...].astype(BF16), preferred_element_type=F32) + b_ref[...]


def _ada(cc, w, b):
    d, n6 = w.shape
    tn = min(n6, 1536)
    return pl.pallas_call(
        _ada_kernel,
        grid=(n6 // tn,),
        in_specs=[pl.BlockSpec((SUBLANES, d), lambda j: (0, 0)),
                  pl.BlockSpec((d, tn), lambda j: (0, j)),
                  pl.BlockSpec((1, tn), lambda j: (0, j))],
        out_specs=pl.BlockSpec((SUBLANES, tn), lambda j: (0, j)),
        out_shape=jax.ShapeDtypeStruct((SUBLANES, n6), F32),
        compiler_params=_cparams(("arbitrary",)),
    )(cc, w, b)


def _modmm_kernel(x_ref, sc_ref, sh_ref, w_ref, o_ref, h_scr):
    @pl.when(pl.program_id(1) == 0)
    def _():
        h_scr[...] = (x_ref[...] * (1.0 + sc_ref[0]) + sh_ref[0]).astype(BF16)

    o_ref[...] = jnp.dot(h_scr[...], w_ref[...], preferred_element_type=F32).astype(o_ref.dtype)


def _modmm(x2d, sc, sh, w, *, tm, tn, out_dtype=F32):
    m, d = x2d.shape
    ng = w.shape[1]
    r = sc.shape[0]
    tm = min(tm, m // r)
    tn = min(tn, ng)
    tiles_per_mod = (m // r) // tm
    return pl.pallas_call(
        _modmm_kernel,
        grid=(m // tm, ng // tn),
        in_specs=[pl.BlockSpec((tm, d), lambda i, j: (i, 0)),
                  pl.BlockSpec((1, 1, d), lambda i, j: (i // tiles_per_mod, 0, 0)),
                  pl.BlockSpec((1, 1, d), lambda i, j: (i // tiles_per_mod, 0, 0)),
                  pl.BlockSpec((d, tn), lambda i, j: (0, j))],
        out_specs=pl.BlockSpec((tm, tn), lambda i, j: (i, j)),
        out_shape=jax.ShapeDtypeStruct((m, ng), out_dtype),
        scratch_shapes=[pltpu.VMEM((tm, d), BF16)],
        compiler_params=_cparams(("parallel", "arbitrary")),
    )(x2d, sc, sh, w)


def _row_conv(x, cw, pos, row_w):
    rows = x.shape[0]
    k = cw.shape[0]
    pad = k // 2
    acc = None
    for j in range(k):
        off = j - pad
        if off == 0:
            term = x * cw[j:j + 1]
        else:
            shifted = pltpu.roll(x, (-off) % rows, axis=0)
            valid = (pos >= -off) if off < 0 else (pos < row_w - off)
            term = jnp.where(valid, shifted, 0.0) * cw[j:j + 1]
        acc = term if acc is None else acc + term
    return acc


SOLVE_BLOCK = 16


def _tri_inverse_pairs(pairs):
    c = pairs[0][0].shape[0]
    sb = SOLVE_BLOCK
    nb = c // sb
    ri = lax.broadcasted_iota(jnp.int32, (c, 2 * c), 0)
    ci = lax.broadcasted_iota(jnp.int32, (c, 2 * c), 1)
    ci = jnp.where(ci >= c, ci - c, ci)
    lane_lo = lax.broadcasted_iota(jnp.int32, (1, 2 * c), 1) < c
    off_diag = ri // sb != ci // sb
    eye2 = (ri == ci).astype(F32)
    mcats = [jnp.concatenate([m_a, m_b], axis=1) for m_a, m_b in pairs]
    m_offs = [jnp.where(off_diag, m, 0.0).astype(BF16) for m in mcats]
    xs = [[eye2[i * sb:(i + 1) * sb] for i in range(nb)] for _ in pairs]
    for bi in range(nb):
        rs = slice(bi * sb, (bi + 1) * sb)
        blks = [x[bi] for x in xs]
        if bi > 0:
            for s, x in enumerate(xs):
                xcur = jnp.concatenate(x, axis=0)
                xbd = jnp.concatenate([jnp.where(lane_lo, xcur, 0.0), jnp.where(lane_lo, 0.0, xcur)], axis=0)
                blks[s] = blks[s] - jnp.dot(m_offs[s][rs], xbd.astype(BF16), preferred_element_type=F32)
        mrs = [m[rs] for m in mcats]
        for jj in range(sb - 1):
            col = bi * sb + jj
            src = jnp.broadcast_to(jnp.where(lane_lo, col, c + col), (sb, 2 * c))
            for s in range(len(pairs)):
                mult = jnp.take_along_axis(mrs[s], src, axis=1)
                blks[s] = blks[s] - mult * blks[s][jj:jj + 1, :]
        for s, x in enumerate(xs):
            x[bi] = blks[s]
    return [jnp.concatenate(x, axis=0) for x in xs]


def _dn_gates_kernel(ab_ref, alog_ref, dt_ref, gc_ref, beta_ref, *, n_heads):
    tt = ab_ref.shape[1]
    c = DN_CHUNK
    cpos = lax.broadcasted_iota(jnp.int32, (tt, 1), 0) % c
    ab = ab_ref[0]
    z = ab + dt_ref[...]
    softplus = jnp.maximum(z, 0.0) + jnp.log1p(jnp.exp(-jnp.abs(z)))
    g_all = -jnp.exp(alog_ref[...]) * softplus
    gf = g_all
    gb = g_all
    d = 1
    while d < c:
        gf = gf + jnp.where(cpos >= d, pltpu.roll(gf, d, axis=0), 0.0)
        gb = gb + jnp.where(cpos < c - d, pltpu.roll(gb, (tt - d) % tt, axis=0), 0.0)
        d *= 2
    lane = lax.broadcasted_iota(jnp.int32, (1, LANES), 1)
    gc_ref[0] = jnp.where(lane < 2 * n_heads, gf, gb)
    beta_ref[0] = jax.nn.sigmoid(ab)


def _dn_gates(ab, alog_row, dt_row, *, n_heads, tt):
    b, n, _ = ab.shape
    blk = pl.BlockSpec((1, tt, LANES), lambda bi, g: (bi, g, 0))
    row = pl.BlockSpec((1, LANES), lambda bi, g: (0, 0))
    shp = jax.ShapeDtypeStruct((b, n, LANES), F32)
    return pl.pallas_call(
        functools.partial(_dn_gates_kernel, n_heads=n_heads),
        grid=(b, n // tt),
        in_specs=[blk, row, row],
        out_specs=[blk, blk],
        out_shape=[shp, shp],
        compiler_params=_cparams(("parallel", "parallel")),
    )(ab, alog_row, dt_row)


def _dn_prep_kernel(xq_ref, xk_ref, xv_ref, cq_ref, ck_ref, cv_ref, gc_ref, beta_ref,
                    u_ref, w_ref, qg_ref, kd_ref, attn_ref, eg_ref, *, row_w, n_heads, dk, hp):
    tt = xq_ref.shape[1]
    c = DN_CHUNK
    pos = lax.broadcasted_iota(jnp.int32, (tt, 1), 0) % row_w
    lane = lax.broadcasted_iota(jnp.int32, (1, LANES), 1)
    gc = gc_ref[0]
    beta_all = beta_ref[0]
    eg_ref[...] = jnp.zeros(eg_ref.shape, F32)
    ri = lax.broadcasted_iota(jnp.int32, (c, c), 0)
    ci = lax.broadcasted_iota(jnp.int32, (c, c), 1)
    eye = ri == ci
    lower = ri >= ci
    upper = ri <= ci
    strict_lower = ri > ci
    nt = (((1,), (1,)), ((), ()))

    def col(a, l):
        return jnp.sum(jnp.where(lane == l, a, 0.0), axis=1, keepdims=True)

    def l2n(y):
        return y * lax.rsqrt(jnp.sum(y * y, axis=-1, keepdims=True) + L2_EPS)

    units = []
    for hh in range(hp):
        h = pl.program_id(1) * hp + hh
        hs = slice(hh * dk, (hh + 1) * dk)
        conv_silu = lambda x_ref, cw_ref: _silu(_row_conv(x_ref[0, :, hs], cw_ref[:, hs], pos, row_w))
        q = l2n(conv_silu(xq_ref, cq_ref)) * (dk ** -0.5)
        k = l2n(conv_silu(xk_ref, ck_ref))
        v = conv_silu(xv_ref, cv_ref)
        g_cols = (col(gc, h), col(gc, 2 * n_heads + h))
        b_cols = (col(beta_all, n_heads + h), col(beta_all, 3 * n_heads + h))
        for cc in range(tt // c):
            sl = slice(cc * c, (cc + 1) * c)
            units.append(dict(hh=hh, hs=hs, cc=cc, sl=sl, q=q[sl], k=k[sl], v=v[sl],
                              g=[g_cols[dr][sl] for dr in range(2)], b=[b_cols[dr][sl] for dr in range(2)]))

    for un in units:
        qc, kc, vc, gcols = un["q"], un["k"], un["v"], un["g"]
        grows = [jnp.sum(jnp.where(eye, g, 0.0), axis=0, keepdims=True) for g in gcols]
        kbs = [kc * un["b"][dr] for dr in range(2)]
        vbs = [vc * un["b"][dr] for dr in range(2)]
        egs = [jnp.exp(g) for g in gcols]
        dif_f = gcols[0] - grows[0]
        dif_b = gcols[1] - grows[1]
        un["decay_f"] = jnp.where(lower, jnp.exp(jnp.where(lower, dif_f, 0.0)), 0.0)
        un["decay_b"] = jnp.where(upper, jnp.exp(jnp.where(upper, dif_b, 0.0)), 0.0)
        un["decay_bt"] = jnp.where(lower, jnp.exp(jnp.where(lower, -dif_b, 0.0)), 0.0)
        un["kcb"] = kc.astype(BF16)
        un["lhs"] = jnp.concatenate([kbs[0], qc], axis=0).astype(BF16)
        un["kb_b"] = kbs[1].astype(BF16)
        un["rhs_f"] = jnp.concatenate([vbs[0], kbs[0] * egs[0]], axis=1).astype(BF16)
        un["rhs_b"] = jnp.concatenate([vbs[1], kbs[1] * egs[1]], axis=1).astype(BF16)
        for dr in range(2):
            gcol = gcols[dr]
            glast = gcol[c - 1:c] if dr == 0 else gcol[0:1]
            qg_ref[dr, 0, un["sl"], un["hs"]] = (qc * egs[dr]).astype(BF16)
            kd_ref[dr, 0, un["sl"], un["hs"]] = (kc * jnp.exp(glast - gcol)).astype(BF16)
            eg_ref[dr, 0, un["hh"], 0, un["cc"]:un["cc"] + 1, :] = jnp.broadcast_to(jnp.exp(glast), (1, LANES))

    for un in units:
        un["kq"] = lax.dot_general(un["lhs"], un["kcb"], nt, preferred_element_type=F32)
        un["kkt_b"] = lax.dot_general(un["kcb"], un["kb_b"], nt, preferred_element_type=F32)

    for un in units:
        un["m_f"] = jnp.where(strict_lower, un["kq"][:c] * un["decay_f"], 0.0)
        un["m_bt"] = jnp.where(strict_lower, un["kkt_b"] * un["decay_bt"], 0.0)
        attn_ref[0, 0, un["hh"], un["sl"], :] = (un["kq"][c:] * un["decay_f"]).astype(BF16)
        attn_ref[1, 0, un["hh"], un["sl"], :] = (un["kq"][c:] * un["decay_b"]).astype(BF16)

    tinvs = _tri_inverse_pairs([(un["m_f"], un["m_bt"]) for un in units])

    zero_rows = jnp.zeros((c, 2 * dk), BF16)
    sols = []
    for tinv, un in zip(tinvs, units):
        tinv = tinv.astype(BF16)
        sol_f = jnp.dot(tinv, jnp.concatenate([un["rhs_f"], zero_rows], axis=0), preferred_element_type=F32)
        sol_b = lax.dot_general(tinv, un["rhs_b"], (((0,), (0,)), ((), ())), preferred_element_type=F32)[c:]
        sols.append((sol_f, sol_b))
    for (sol_f, sol_b), un in zip(sols, units):
        for dr, sol in enumerate((sol_f, sol_b)):
            u_ref[dr, 0, un["sl"], un["hs"]] = sol[:, :dk]
            w_ref[dr, 0, un["sl"], un["hs"]] = sol[:, dk:].astype(BF16)


def _dn_prep(qkv, conv_w, ab, alog_row, dt_row, *, n_heads, dk, row_w, tt):
    gc, beta = _dn_gates(ab, alog_row, dt_row, n_heads=n_heads, tt=tt)
    b, n, _ = qkv.shape
    hh = n_heads
    ng = n // tt
    c = DN_CHUNK
    assert tt % row_w == 0 and tt % c == 0 and tt // c <= SUBLANES
    hp = 2 if hh % 2 == 0 else 1
    nhb = hh // hp
    kern = functools.partial(_dn_prep_kernel, row_w=row_w, n_heads=hh, dk=dk, hp=hp)
    xspec = lambda off: pl.BlockSpec((1, tt, hp * dk), lambda bi, h, g: (bi, g, off + h))
    cspec = lambda off: pl.BlockSpec((conv_w.shape[0], hp * dk), lambda bi, h, g: (0, off + h))
    ospec = pl.BlockSpec((2, 1, tt, hp * dk), lambda bi, h, g: (0, bi, g, h))
    return pl.pallas_call(
        kern,
        grid=(b, nhb, ng),
        in_specs=[xspec(0), xspec(nhb), xspec(2 * nhb), cspec(0), cspec(nhb), cspec(2 * nhb),
                  pl.BlockSpec((1, tt, LANES), lambda bi, h, g: (bi, g, 0)),
                  pl.BlockSpec((1, tt, LANES), lambda bi, h, g: (bi, g, 0))],
        out_specs=[ospec, ospec, ospec, ospec,
                   pl.BlockSpec((2, 1, hp, tt, c), lambda bi, h, g: (0, bi, h, g, 0)),
                   pl.BlockSpec((2, 1, hp, 1, SUBLANES, LANES), lambda bi, h, g: (0, bi, h, g, 0, 0))],
        out_shape=[jax.ShapeDtypeStruct((2, b, n, hh * dk), F32),
                   jax.ShapeDtypeStruct((2, b, n, hh * dk), BF16),
                   jax.ShapeDtypeStruct((2, b, n, hh * dk), BF16),
                   jax.ShapeDtypeStruct((2, b, n, hh * dk), BF16),
                   jax.ShapeDtypeStruct((2, b, hh, n, c), BF16),
                   jax.ShapeDtypeStruct((2, b, hh, ng, SUBLANES, LANES), F32)],
        compiler_params=_cparams(("parallel", "parallel", "arbitrary")),
    )(qkv, qkv, qkv, conv_w, conv_w, conv_w, gc, beta)


def _dn_scan_kernel(uf_ref, wf_ref, qf_ref, kf_ref, af_ref, ef_ref,
                    ub_ref, wb_ref, qb_ref, kb_ref, ab_ref, eb_ref, s0_ref,
                    of_ref, ob_ref, sout_ref, s_scr, *, hp, dk):
    g = pl.program_id(2)
    ng = pl.num_programs(2)
    c = DN_CHUNK
    tt = uf_ref.shape[2]
    ncs = tt // c

    @pl.when(g == 0)
    def _():
        s_scr[...] = s0_ref[:, 0]

    refs = ((uf_ref, wf_ref, qf_ref, kf_ref, af_ref, ef_ref, of_ref),
            (ub_ref, wb_ref, qb_ref, kb_ref, ab_ref, eb_ref, ob_ref))
    chains = [(dr, hh) for dr in range(2) for hh in range(hp)]
    state = {ch: s_scr[ch[0], ch[1]] for ch in chains}
    for step in range(ncs):
        rows = {}
        for dr, hh in chains:
            cc = step if dr == 0 else ncs - 1 - step
            rows[dr, hh] = (slice(cc * c, (cc + 1) * c), slice(hh * dk, (hh + 1) * dk), cc)
        wide = {}
        for dr in range(2):
            sl = rows[dr, 0][0]
            wide[dr] = [refs[dr][i][0, 0, sl, :] for i in range(4)]
        r = {}
        for ch in chains:
            _, hs, _ = rows[ch]
            wq = jnp.concatenate([wide[ch[0]][1][:, hs], wide[ch[0]][2][:, hs]], axis=0)
            r[ch] = jnp.dot(wq, state[ch].astype(BF16), preferred_element_type=F32)
        vnb = {}
        for ch in chains:
            _, hs, _ = rows[ch]
            vnb[ch] = (wide[ch[0]][0][:, hs] - r[ch][:c]).astype(BF16)
        outs = {}
        for ch in chains:
            a_ref, e_ref = refs[ch[0]][4], refs[ch[0]][5]
            sl, hs, cc = rows[ch]
            outs[ch] = r[ch][c:] + jnp.dot(a_ref[0, 0, ch[1], sl, :], vnb[ch], preferred_element_type=F32)
            upd = lax.dot_general(wide[ch[0]][3][:, hs], vnb[ch], (((0,), (0,)), ((), ())),
                                  preferred_element_type=F32)
            state[ch] = state[ch] * e_ref[0, 0, ch[1], 0, cc:cc + 1, :] + upd
        for dr in range(2):
            refs[dr][6][0, rows[dr, 0][0], :] = jnp.concatenate([outs[dr, hh] for hh in range(hp)], axis=1)
    for ch in chains:
        s_scr[ch[0], ch[1]] = state[ch]

    @pl.when(g == ng - 1)
    def _():
        sout_ref[:, 0] = s_scr[...]


def _dn_scan(u, w, qg, kd, attn, eg, s0, *, dk, tt, hp):
    _, b, n, hd = u.shape
    hh = hd // dk
    ng = n // tt
    c = DN_CHUNK
    kern = functools.partial(_dn_scan_kernel, hp=hp, dk=dk)

    def specs(dr):
        gi = (lambda g: g) if dr == 0 else (lambda g: ng - 1 - g)
        tok = pl.BlockSpec((1, 1, tt, hp * dk), lambda bi, h, g: (dr, bi, gi(g), h))
        return [tok, tok, tok, tok,
                pl.BlockSpec((1, 1, hp, tt, c), lambda bi, h, g: (dr, bi, h, gi(g), 0)),
                pl.BlockSpec((1, 1, hp, 1, SUBLANES, LANES), lambda bi, h, g: (dr, bi, h, gi(g), 0, 0))]

    sspec = pl.BlockSpec((2, 1, hp, dk, dk), lambda bi, h, g: (0, bi, h, 0, 0))
    return pl.pallas_call(
        kern,
        grid=(b, hh // hp, ng),
        in_specs=specs(0) + specs(1) + [sspec],
        out_specs=[pl.BlockSpec((1, tt, hp * dk), lambda bi, h, g: (bi, g, h)),
                   pl.BlockSpec((1, tt, hp * dk), lambda bi, h, g: (bi, ng - 1 - g, h)),
                   sspec],
        out_shape=[jax.ShapeDtypeStruct((b, n, hd), F32),
                   jax.ShapeDtypeStruct((b, n, hd), F32),
                   jax.ShapeDtypeStruct((2, b, hh, dk, dk), F32)],
        scratch_shapes=[pltpu.VMEM((2, hp, dk, dk), F32)],
        compiler_params=_cparams(("parallel", "parallel", "arbitrary")),
    )(u, w, qg, kd, attn, eg, u, w, qg, kd, attn, eg, s0)


def _hy_hidden_kernel(z_ref, w1_ref, b1_ref, w2_ref, b2_ref, fr_ref, o_ref):
    fr = fr_ref[...]
    h1 = jnp.sin(fr * (jnp.dot(z_ref[...].astype(BF16), w1_ref[...].astype(BF16),
                               preferred_element_type=F32) + b1_ref[...]))
    o_ref[...] = jnp.sin(fr * (jnp.dot(h1.astype(BF16), w2_ref[...].astype(BF16),
                                       preferred_element_type=F32) + b2_ref[...]))


def _hy_filter_kernel(hid_ref, w3_ref, b3_ref, t_ref, delta_ref, o_ref):
    f = jnp.dot(hid_ref[...].astype(BF16), w3_ref[...].astype(BF16), preferred_element_type=F32) + b3_ref[...]
    o_ref[...] = f * jnp.exp(-t_ref[...] * delta_ref[...])


def _hy_filters(n, w1, b1, w2, b2, w3, b3, freq, hy_w):
    emb, hid = w1.shape
    bands = (emb - 1) // 2
    t = jnp.linspace(0.0, 1.0, n, dtype=F32)
    pos = jnp.arange(n, dtype=F32)
    bnd = jnp.linspace(1e-4, bands - 1, bands, dtype=F32)
    ang = (2.0 * math.pi / n) * pos[:, None] * bnd[None, :]
    z = jnp.concatenate([t[:, None], jnp.cos(ang), -jnp.sin(ang)], axis=-1)
    embp = -(-emb // SUBLANES) * SUBLANES
    z = jnp.pad(z, ((0, 0), (0, embp - emb)))
    w1p = jnp.pad(w1, ((0, embp - emb), (0, 0)))
    full = lambda a: pl.BlockSpec(a.shape, lambda: (0,) * a.ndim)
    args = (z, w1p, b1[None, :], w2, b2[None, :], freq[None, :])
    hid2 = pl.pallas_call(
        _hy_hidden_kernel,
        in_specs=[full(a) for a in args],
        out_specs=pl.BlockSpec((n, hid), lambda: (0, 0)),
        out_shape=jax.ShapeDtypeStruct((n, hid), F32),
        compiler_params=pltpu.CompilerParams(vmem_limit_bytes=VMEM_LIMIT_BYTES),
    )(*args)
    nout = w3.shape[1]
    deltas = jnp.abs(jnp.linspace(math.log(HY_DECAY_TARGET) / HY_SLOW_DECAY,
                                  math.log(HY_DECAY_TARGET) / HY_FAST_DECAY, hy_w, dtype=F32))
    delta_row = jnp.tile(deltas, nout // hy_w)[None, :]
    tc = min(nout, 1024)
    return pl.pallas_call(
        _hy_filter_kernel,
        grid=(nout // tc,),
        in_specs=[pl.BlockSpec((n, hid), lambda j: (0, 0)),
                  pl.BlockSpec((hid, tc), lambda j: (0, j)),
                  pl.BlockSpec((1, tc), lambda j: (0, j)),
                  pl.BlockSpec((n, 1), lambda j: (0, 0)),
                  pl.BlockSpec((1, tc), lambda j: (0, j))],
        out_specs=pl.BlockSpec((n, tc), lambda j: (0, j)),
        out_shape=jax.ShapeDtypeStruct((n, nout), F32),
        compiler_params=_cparams(("parallel",)),
    )(hid2, w3, b3[None, :], t[:, None], delta_row)


def _dft_tables(n):
    n2 = FFT_N2
    big = 2 * n
    n1 = big // n2
    k1 = n1 // 2
    two_pi = 2.0 * math.pi

    def cs(idx, period):
        a = (idx % period).astype(F32) * (two_pi / period)
        return jnp.cos(a), jnp.sin(a)

    f1 = jnp.arange(n1, dtype=jnp.int32)
    s1 = jnp.arange(k1, dtype=jnp.int32)
    s2 = jnp.arange(n2, dtype=jnp.int32)
    s_full = s1[None, None, :] * n2 + s2[:, None, None]
    c, s = cs(f1[None, :, None] * s_full, big)
    f1tw = jnp.concatenate([jnp.concatenate([c, s], axis=2), jnp.concatenate([-s, c], axis=2)], axis=1)
    ct, st = jnp.swapaxes(c, 1, 2), jnp.swapaxes(s, 1, 2)
    g1tw = jnp.concatenate([jnp.concatenate([ct, -st], axis=2), jnp.concatenate([st, ct], axis=2)], axis=1) / big
    c2, sn2 = cs(s2[:, None] * s2[None, :], n2)
    f2 = jnp.concatenate([jnp.concatenate([c2, sn2], axis=1), jnp.concatenate([-sn2, c2], axis=1)], axis=0)
    g2 = jnp.concatenate([jnp.concatenate([c2, -sn2], axis=1), jnp.concatenate([sn2, c2], axis=1)], axis=0)
    return f1tw.astype(BF16), f2.astype(BF16), g2.astype(BF16), g1tw.astype(BF16)


def _hy_conv_kernel(a_ref, m_ref, hf_ref, hb_ref, skip_ref, cwa_ref, cwm_ref,
                    f1_ref, f2_ref, g2_ref, g1_ref, o_ref,
                    h_scr, b_scr, zr_scr, zi_scr, *, conv_a, n1, k1):
    n2 = FFT_N2
    k1p = k1 + FFT_ROW_PAD
    pb = 2 * n2 + FFT_ROW_PAD
    pos = lax.broadcasted_iota(jnp.int32, (n2, 1), 0)

    grp = min(FFT_UNROLL, n1 // 2)
    cw = b_scr.shape[1]

    def stage1(real_only):
        def body(g, carry):
            s2s = [g * grp + t for t in range(grp)]
            bases = [pl.multiple_of(s2 * k1p, SUBLANES) for s2 in s2s]
            if real_only:
                xs = [zr_scr[pl.ds(base, k1), :].astype(BF16) for base in bases]
                outs = [jnp.dot(f1_ref[s2, :, :k1], x, preferred_element_type=F32) for s2, x in zip(s2s, xs)]
            else:
                xs = [jnp.concatenate([zr_scr[pl.ds(base, k1), :], zi_scr[pl.ds(base, k1), :]], axis=0).astype(BF16)
                      for base in bases]
                outs = [jnp.dot(f1_ref[s2], x, preferred_element_type=F32) for s2, x in zip(s2s, xs)]
            for s2, a in zip(s2s, outs):
                b_scr[pl.ds(s2, n1, stride=pb), :] = a[:n1]
                b_scr[pl.ds(n2 + s2, n1, stride=pb), :] = a[n1:]
            return carry
        lax.fori_loop(0, n2 // grp, body, 0)

    def slab_pairs(g):
        f1s = [[(g * grp + t) * 2 + u for u in range(2)] for t in range(grp)]
        bbs = [[pl.multiple_of(f1 * pb, SUBLANES) for f1 in pair] for pair in f1s]
        hbs = [[pl.multiple_of(f1 * 2 * n2, 2 * n2) for f1 in pair] for pair in f1s]
        return bbs, hbs

    def forward2(bbs):
        bpairs = [jnp.concatenate([b_scr[pl.ds(bb, 2 * n2), :] for bb in pair], axis=1).astype(BF16) for pair in bbs]
        return [jnp.dot(f2_ref[...], bp, preferred_element_type=F32) for bp in bpairs]

    @pl.when(pl.program_id(1) == 0)
    def _():
        for which, ref in ((0, hf_ref), (1, hb_ref)):
            def load(s1, carry, ref=ref):
                r0 = pl.multiple_of(s1 * n2, n2)
                zr_scr[pl.ds(s1, n2, stride=k1p), :] = ref[pl.ds(r0, n2), :]
                return carry
            lax.fori_loop(0, k1, load, 0, unroll=FFT_UNROLL)
            if which == 1:
                zr_scr[0:1, :] = jnp.zeros((1, zr_scr.shape[1]), F32)
            stage1(True)

            def spec(g, carry, which=which):
                bbs, hbs = slab_pairs(g)
                zpairs = forward2(bbs)
                for zpair, hpair in zip(zpairs, hbs):
                    for t, hb_ in enumerate(hpair):
                        zz = zpair[:, t * cw:(t + 1) * cw]
                        if which == 0:
                            h_scr[pl.ds(hb_, 2 * n2), :] = zz
                        else:
                            h_scr[pl.ds(hb_, n2), :] = h_scr[pl.ds(hb_, n2), :] + zz[:n2]
                            h_scr[pl.ds(hb_ + n2, n2), :] = h_scr[pl.ds(hb_ + n2, n2), :] - zz[n2:]
                return carry
            lax.fori_loop(0, n1 // (2 * grp), spec, 0)

    def load_a(s1, carry):
        r0 = pl.multiple_of(s1 * n2, n2)
        for bi, scr in ((0, zr_scr), (1, zi_scr)):
            blk = a_ref[bi, pl.ds(r0, n2), :]
            if conv_a:
                blk = _row_conv(blk, cwa_ref[...], pos, n2)
            scr[pl.ds(s1, n2, stride=k1p), :] = blk
        return carry
    lax.fori_loop(0, k1, load_a, 0, unroll=FFT_UNROLL)

    stage1(False)

    def mid(g, carry):
        bbs, hbs = slab_pairs(g)
        zzs = forward2(bbs)
        yys = []
        for zz, hpair in zip(zzs, hbs):
            zr, zi = zz[:n2], zz[n2:]
            hr = jnp.concatenate([h_scr[pl.ds(hb_, n2), :] for hb_ in hpair], axis=1)
            hi = jnp.concatenate([h_scr[pl.ds(hb_ + n2, n2), :] for hb_ in hpair], axis=1)
            yys.append(jnp.concatenate([zr * hr - zi * hi, zr * hi + zi * hr], axis=0).astype(BF16))
        wws = [jnp.dot(g2_ref[...], yy, preferred_element_type=F32) for yy in yys]
        for ww, pair in zip(wws, bbs):
            for t, bb in enumerate(pair):
                b_scr[pl.ds(bb, 2 * n2), :] = ww[:, t * cw:(t + 1) * cw]
        return carry
    lax.fori_loop(0, n1 // (2 * grp), mid, 0)

    skip = skip_ref[...]

    def last(g, carry):
        s2s = [g * grp + t for t in range(grp)]
        wws = [jnp.concatenate([b_scr[pl.ds(s2, n1, stride=pb), :],
                                b_scr[pl.ds(n2 + s2, n1, stride=pb), :]], axis=0).astype(BF16) for s2 in s2s]
        ys = [jnp.dot(g1_ref[s2], ww, preferred_element_type=F32) for s2, ww in zip(s2s, wws)]
        for s2, y in zip(s2s, ys):
            base = pl.multiple_of(s2 * k1p, SUBLANES)
            zr_scr[pl.ds(base, k1), :] = y[:k1] + zr_scr[pl.ds(base, k1), :] * skip
            zi_scr[pl.ds(base, k1), :] = y[k1:] + zi_scr[pl.ds(base, k1), :] * skip
        return carry
    lax.fori_loop(0, n2 // grp, last, 0)

    def fin(s1, carry):
        r0 = pl.multiple_of(s1 * n2, n2)
        for bi, scr in ((0, zr_scr), (1, zi_scr)):
            mm = _row_conv(m_ref[bi, pl.ds(r0, n2), :], cwm_ref[...], pos, n2)
            o_ref[bi, pl.ds(r0, n2), :] = mm * scr[pl.ds(s1, n2, stride=k1p), :]
        return carry
    lax.fori_loop(0, k1, fin, 0, unroll=FFT_UNROLL)


def _hy_conv(a, a_col0, m, m_col0, filt, f_col_fwd, f_col_bwd, skip, cwa, cwm, tables, *, conv_a, hy_w):
    b, n, _ = a.shape
    assert b % 2 == 0 and GRID_W == FFT_N2 and n % FFT_N2 == 0
    n2 = FFT_N2
    n1 = 2 * n // n2
    k1 = n1 // 2
    k1p = k1 + FFT_ROW_PAD
    pb = 2 * n2 + FFT_ROW_PAD
    cw = LANES
    f1tw, f2, g2, g1tw = tables
    kern = functools.partial(_hy_conv_kernel, conv_a=conv_a, n1=n1, k1=k1)
    once = pl.Buffered(1)
    return pl.pallas_call(
        kern,
        grid=(hy_w // cw, b // 2),
        in_specs=[pl.BlockSpec((2, n, cw), lambda c, p: (p, 0, a_col0 + c), pipeline_mode=once),
                  pl.BlockSpec((2, n, cw), lambda c, p: (p, 0, m_col0 + c), pipeline_mode=once),
                  pl.BlockSpec((n, cw), lambda c, p: (0, f_col_fwd + c), pipeline_mode=once),
                  pl.BlockSpec((n, cw), lambda c, p: (0, f_col_bwd + c), pipeline_mode=once),
                  pl.BlockSpec((1, cw), lambda c, p: (0, c)),
                  pl.BlockSpec((cwa.shape[0], cw), lambda c, p: (0, a_col0 + c if conv_a else 0)),
                  pl.BlockSpec((cwm.shape[0], cw), lambda c, p: (0, m_col0 + c)),
                  pl.BlockSpec(f1tw.shape, lambda c, p: (0, 0, 0), pipeline_mode=once),
                  pl.BlockSpec(f2.shape, lambda c, p: (0, 0)),
                  pl.BlockSpec(g2.shape, lambda c, p: (0, 0)),
                  pl.BlockSpec(g1tw.shape, lambda c, p: (0, 0, 0), pipeline_mode=once)],
        out_specs=pl.BlockSpec((2, n, cw), lambda c, p: (p, 0, c)),
        out_shape=jax.ShapeDtypeStruct((b, n, hy_w), F32),
        scratch_shapes=[pltpu.VMEM((n1 * 2 * n2, cw), F32),
                        pltpu.VMEM((n1 * pb, cw), F32),
                        pltpu.VMEM((n2 * k1p, cw), F32),
                        pltpu.VMEM((n2 * k1p, cw), F32)],
        compiler_params=_cparams(("parallel", "arbitrary")),
    )(a, m, filt, filt, skip, cwa, cwm, f1tw, f2, g2, g1tw)


def _merge_kernel(of_ref, ob_ref, z_ref, y_ref, ga_ref, gb_ref, nw_ref, wa_ref, wh_ref, o_ref, *, dv):
    nw = nw_ref[...]
    o_all = of_ref[...] + ob_ref[...]
    gate_all = _silu(z_ref[...])
    heads = []
    for hh in range(of_ref.shape[1] // dv):
        hs = slice(hh * dv, (hh + 1) * dv)
        o = o_all[:, hs]
        o = o * lax.rsqrt(jnp.mean(o * o, axis=-1, keepdims=True) + RMS_EPS) * nw
        heads.append((o * gate_all[:, hs]).astype(BF16))
    pa = jnp.dot(jnp.concatenate(heads, axis=1), wa_ref[...], preferred_element_type=F32)
    ph = jnp.dot(y_ref[...].astype(BF16), wh_ref[...], preferred_element_type=F32)
    o_ref[...] = (jax.nn.sigmoid(ga_ref[...]) * pa + jax.nn.sigmoid(gb_ref[...]) * ph).astype(o_ref.dtype)


def _merge(o_f, o_b, z, z_blk, y_hy, gate, norm_w, w_pa, w_ph, *, dv, tm):
    m, dvw = o_f.shape
    d = w_pa.shape[1]
    tm = min(tm, m)
    kern = functools.partial(_merge_kernel, dv=dv)
    row = lambda w_, blk=0: pl.BlockSpec((tm, w_), lambda i: (i, blk))
    const = lambda a: pl.BlockSpec(a.shape, lambda i: (0, 0), pipeline_mode=pl.Buffered(1))
    return pl.pallas_call(
        kern,
        grid=(m // tm,),
        in_specs=[row(dvw), row(dvw), row(dvw, z_blk), row(y_hy.shape[1]), row(d, 0), row(d, 1),
                  pl.BlockSpec((1, dv), lambda i: (0, 0)), const(w_pa), const(w_ph)],
        out_specs=row(d),
        out_shape=jax.ShapeDtypeStruct((m, d), BF16),
        compiler_params=_cparams(("parallel",)),
    )(o_f, o_b, z, y_hy, gate, gate, norm_w, w_pa, w_ph)


def _layer_norm(x, g, b):
    mu = jnp.mean(x, axis=-1, keepdims=True)
    xc = x - mu
    var = jnp.mean(xc * xc, axis=-1, keepdims=True)
    return xc * lax.rsqrt(var + LN_EPS) * g + b


def _outproj_kernel(mg_ref, x_ref, wo_ref, g1_ref, lg_ref, lb_ref, sc_ref, sh_ref, wq_ref, sk_ref,
                    x1_ref, h2_ref, s_ref, *, half):
    mix = jnp.dot(mg_ref[...], wo_ref[...], preferred_element_type=F32)
    x1 = _layer_norm(DEEPNORM_ALPHA * x_ref[...] + g1_ref[0] * mix, lg_ref[...], lb_ref[...])
    x1_ref[...] = x1
    h2 = (x1 * (1.0 + sc_ref[0]) + sh_ref[0]).astype(BF16)
    h2_ref[...] = h2
    qp = jnp.dot(h2, wq_ref[...], preferred_element_type=F32).astype(BF16)
    for j in range(sk_ref.shape[0]):
        js = slice(j * half, (j + 1) * half)
        s_ref[j] = lax.dot_general(sk_ref[j], qp[:, js], (((1,), (1,)), ((), ())), preferred_element_type=F32)


def _outproj(merged, x2d, w_out, g1, ln_g, ln_b, sc2, sh2, wq, subkeys, *, tm):
    m, d = x2d.shape
    r = g1.shape[0]
    tm = min(tm, m // r)
    tpm = (m // r) // tm
    nsk, nkeys, half = subkeys.shape
    assert nkeys == LANES
    kern = functools.partial(_outproj_kernel, half=half)
    row = lambda w_: pl.BlockSpec((tm, w_), lambda i: (i, 0))
    mod = pl.BlockSpec((1, 1, d), lambda i: (i // tpm, 0, 0))
    vec = pl.BlockSpec((1, d), lambda i: (0, 0))
    return pl.pallas_call(
        kern,
        grid=(m // tm,),
        in_specs=[row(d), row(d), pl.BlockSpec(w_out.shape, lambda i: (0, 0), pipeline_mode=pl.Buffered(1)),
                  mod, vec, vec, mod, mod,
                  pl.BlockSpec(wq.shape, lambda i: (0, 0), pipeline_mode=pl.Buffered(1)),
                  pl.BlockSpec(subkeys.shape, lambda i: (0, 0, 0))],
        out_specs=[row(d), row(d), pl.BlockSpec((nsk, nkeys, tm), lambda i: (0, 0, i))],
        out_shape=[jax.ShapeDtypeStruct((m, d), F32),
                   jax.ShapeDtypeStruct((m, d), BF16),
                   jax.ShapeDtypeStruct((nsk, nkeys, m), F32)],
        compiler_params=_cparams(("parallel",)),
    )(merged, x2d, w_out, g1, ln_g, ln_b, sc2, sh2, wq, subkeys)


def _cand_tables():
    pairs = [(a, b) for a in range(PEER_TOPK) for b in range(PEER_TOPK) if (a + 1) * (b + 1) <= PEER_TOPK]
    ia = jnp.array([p[0] for p in pairs], dtype=jnp.int32)
    ib = jnp.array([p[1] for p in pairs], dtype=jnp.int32)
    ncand = len(pairs)
    rows = -(-ncand // (2 * SUBLANES)) * (2 * SUBLANES)
    ranks = jnp.arange(LANES, dtype=jnp.int32)
    p1 = (jnp.pad(ia, (0, rows - ncand), constant_values=-1)[:, None] == ranks[None, :]).astype(F32)
    p2 = (jnp.pad(ib, (0, rows - ncand), constant_values=-1)[:, None] == ranks[None, :]).astype(F32)
    return p1, p2, ncand


def _peer_topk_kernel(s_ref, p1_ref, p2_ref, p1t_ref, c1_ref, r2_ref, e1_ref, e2_ref, *, n_heads, ncand):
    tm = s_ref.shape[2]
    neg = -jnp.inf
    hi = lax.Precision.HIGHEST
    no_rank = float(LANES)

    def pop_max(x, row_id):
        mx = jnp.max(x, axis=0, keepdims=True)
        first = jnp.min(jnp.where(x == mx, row_id, no_rank), axis=0, keepdims=True)
        hit = row_id == first
        return mx, hit, jnp.where(hit, neg, x)

    key_id = lax.broadcasted_iota(jnp.int32, (LANES, tm), 0).astype(F32)
    cand_id = lax.broadcasted_iota(jnp.int32, (p1_ref.shape[0], tm), 0).astype(F32)
    pad_rows = jnp.zeros((LANES - PEER_TOPK, tm), F32)

    def sorted_top(s):
        tops = []
        rank = jnp.full(s.shape, no_rank, F32)
        x = s
        for a in range(PEER_TOPK):
            mx, hit, x = pop_max(x, key_id)
            tops.append(mx)
            rank = jnp.where(hit, float(a), rank)
        return jnp.concatenate(tops + [pad_rows], axis=0), rank

    for h in range(n_heads):
        s1 = s_ref[2 * h]
        s2 = s_ref[2 * h + 1]
        t1, rank1 = sorted_top(s1)
        t2, rank2 = sorted_top(s2)
        cand = (jnp.dot(p1_ref[...], t1, precision=hi, preferred_element_type=F32)
                + jnp.dot(p2_ref[...], t2, precision=hi, preferred_element_type=F32))
        x = jnp.where(cand_id < ncand, cand, neg)
        sel = jnp.zeros(x.shape, F32)
        zsum = None
        cmax = None
        for r in range(PEER_TOPK):
            mx, hit, x = pop_max(x, cand_id)
            sel = jnp.where(hit, 1.0, sel)
            if r == 0:
                cmax = mx
                zsum = jnp.ones_like(mx)
            else:
                zsum = zsum + jnp.exp(mx - cmax)
        cnt = jnp.dot(p1t_ref[...].astype(BF16), sel.astype(BF16), preferred_element_type=F32)
        count1 = jnp.zeros(s1.shape, F32)
        for a in range(PEER_TOPK):
            count1 = jnp.where(rank1 == a, cnt[a:a + 1], count1)
        c1_ref[h] = count1
        r2_ref[h] = rank2.astype(r2_ref.dtype)
        e1_ref[h] = jnp.exp(s1 - t1[0:1]) / zsum
        e2_ref[h] = jnp.exp(s2 - t2[0:1]).astype(e2_ref.dtype)


def _peer_topk(scores, n_heads, *, tm):
    nsk, nkeys, m = scores.shape
    tm = min(tm, m)
    p1, p2, ncand = _cand_tables()
    kern = functools.partial(_peer_topk_kernel, n_heads=n_heads, ncand=ncand)
    full = lambda a: pl.BlockSpec(a.shape, lambda i: (0, 0))
    out = pl.BlockSpec((n_heads, nkeys, tm), lambda i: (0, 0, i))
    shp = lambda dt: jax.ShapeDtypeStruct((n_heads, nkeys, m), dt)
    p1t = p1.T
    return pl.pallas_call(
        kern,
        grid=(m // tm,),
        in_specs=[pl.BlockSpec((nsk, nkeys, tm), lambda i: (0, 0, i)), full(p1), full(p2), full(p1t)],
        out_specs=[out, out, out, out],
        out_shape=[shp(F32), shp(F32), shp(F32), shp(F32)],
        compiler_params=_cparams(("parallel",)),
    )(scores, p1, p2, p1t)


def _peer_dense_kernel(h2t_ref, u_ref, vt_ref, r2_ref, e2_ref, c1_ref, e1_ref, x1_ref, g2_ref, lg_ref, lb_ref,
                       o_ref, acc_scr, a_scr, p_scr, r2_scr, e2_scr, *, n_heads, ti):
    e = pl.program_id(1)

    @pl.when(e == 0)
    def _():
        acc_scr[...] = jnp.zeros(acc_scr.shape, F32)
        r2_scr[...] = r2_ref[...].astype(BF16)
        e2_scr[...] = e2_ref[...].astype(BF16)

    a_scr[...] = jnp.dot(u_ref[...], h2t_ref[...], preferred_element_type=F32)
    nkeys = r2_ref.shape[1]
    for ii in range(ti):
        c1_rows = [c1_ref[h, ii:ii + 1, :].astype(BF16) for h in range(n_heads)]
        e1_rows = [e1_ref[h, ii:ii + 1, :].astype(BF16) for h in range(n_heads)]
        for band in range(nkeys // PEER_BAND):
            ks = slice(band * PEER_BAND, (band + 1) * PEER_BAND)
            rs = slice(ii * nkeys + band * PEER_BAND, ii * nkeys + (band + 1) * PEER_BAND)
            gate = None
            for h in range(n_heads):
                keep = r2_scr[h, ks, :] < c1_rows[h]
                term = jnp.where(keep, e2_scr[h, ks, :], jnp.zeros((), BF16)) * e1_rows[h]
                gate = term if gate is None else gate + term
            a = a_scr[rs, :]
            act = 0.5 * a * (1.0 + lax.erf(a * (2.0 ** -0.5)))
            p_scr[rs, :] = gate * act.astype(BF16)
    acc_scr[...] += jnp.dot(vt_ref[...], p_scr[...], preferred_element_type=F32)

    @pl.when(e == pl.num_programs(1) - 1)
    def _():
        o_ref[...] = _layer_norm(DEEPNORM_ALPHA * x1_ref[...] + g2_ref[0] * acc_scr[...].T,
                                 lg_ref[...], lb_ref[...])


def _peer_dense(h2t, u_tab, vt_tab, rank2, e2, count1, e1, x1, g2, ln_g, ln_b, *, n_heads, ti, tm):
    d, m = h2t.shape
    r = g2.shape[0]
    tm = min(tm, m // r)
    tpm = (m // r) // tm
    ne = u_tab.shape[0]
    nkeys = rank2.shape[1]
    te = ti * nkeys
    kern = functools.partial(_peer_dense_kernel, n_heads=n_heads, ti=ti)
    row = lambda w_: pl.BlockSpec((tm, w_), lambda i, e: (i, 0), pipeline_mode=pl.Buffered(1))
    vec = pl.BlockSpec((1, d), lambda i, e: (0, 0))
    tbl = pl.BlockSpec((te, d), lambda i, e: (e, 0))
    second = pl.BlockSpec((n_heads, nkeys, tm), lambda i, e: (0, 0, i), pipeline_mode=pl.Buffered(1))
    first = pl.BlockSpec((n_heads, ti, tm), lambda i, e: (0, e, i))
    return pl.pallas_call(
        kern,
        grid=(m // tm, ne // te),
        in_specs=[pl.BlockSpec((d, tm), lambda i, e: (0, i), pipeline_mode=pl.Buffered(1)),
                  tbl, pl.BlockSpec((d, te), lambda i, e: (0, e)), second, second, first, first, row(d),
                  pl.BlockSpec((1, 1, d), lambda i, e: (i // tpm, 0, 0)), vec, vec],
        out_specs=pl.BlockSpec((tm, d), lambda i, e: (i, 0)),
        out_shape=jax.ShapeDtypeStruct((m, d), F32),
        scratch_shapes=[pltpu.VMEM((d, tm), F32), pltpu.VMEM((te, tm), F32), pltpu.VMEM((te, tm), BF16),
                        pltpu.VMEM((n_heads, nkeys, tm), BF16), pltpu.VMEM((n_heads, nkeys, tm), BF16)],
        compiler_params=_cparams(("parallel", "arbitrary")),
    )(h2t, u_tab, vt_tab, rank2, e2, count1, e1, x1, g2, ln_g, ln_b)


def _block(x, c, ctx, c_ctx, w_ada, b_ada, w_in, dn_conv_w, dn_a_log, dn_dt_bias, dn_norm_w, hy_conv_w, hy_w1, hy_b1, hy_w2, hy_b2, hy_w3, hy_b3, hy_freq, hy_skip, w_branch_dn, w_branch_hy, w_out, ln1_g, ln1_b, peer_wq, peer_subkeys, peer_u, peer_v, ln2_g, ln2_b):
    b, n, d = x.shape
    n_ctx = ctx.shape[1]
    assert w_ada.shape[0] == DEPTH == 1, "single-layer block: the context stream update is never read"
    hh = dn_a_log.shape[2]
    dk = dn_conv_w.shape[2] // (3 * hh)
    qk_w = hh * dk
    hy_w = hy_skip.shape[2]
    p_heads, _, nkeys, half = peer_subkeys.shape[1:]
    assert dk == LANES and nkeys == LANES and 4 * hh <= LANES and n % GRID_W == 0
    off_z = 3 * qk_w
    off_ab = 4 * qk_w
    off_hy = off_ab + 4 * hh
    off_gate = off_hy + 3 * hy_w
    assert w_in.shape[2] == off_gate + 2 * d

    cc = jnp.zeros((SUBLANES, d), F32).at[:b].set(c).at[b].set(c_ctx)
    mods = _ada(cc, w_ada[0], b_ada[0][None, :])
    sh1, sc1, g1, sh2, sc2, g2 = [mods[:b, i * d:(i + 1) * d][:, None, :] for i in range(6)]
    sh1c, sc1c = [mods[b:b + 1, i * d:(i + 1) * d][:, None, :] for i in range(2)]

    w = w_in[0]
    w_qkvz = w[:, :off_ab].astype(BF16)
    w_ab = jnp.pad(w[:, off_ab:off_hy], ((0, 0), (0, LANES - 4 * hh))).astype(BF16)
    w_hy = w[:, off_hy:off_gate].astype(BF16)
    w_gate = w[:, off_gate:].astype(BF16)

    x2d = x.reshape(b * n, d)
    c2d = ctx.reshape(b * n_ctx, d)
    proj_qkvz = _modmm(x2d, sc1, sh1, w_qkvz, tm=TM_PROJ, tn=TN_PROJ)
    proj_ab = _modmm(x2d, sc1, sh1, w_ab, tm=TM_PROJ, tn=LANES)
    proj_hy = _modmm(x2d, sc1, sh1, w_hy, tm=TM_PROJ, tn=TN_PROJ)
    proj_gate = _modmm(x2d, sc1, sh1, w_gate, tm=TM_PROJ, tn=TN_PROJ)
    projc_qkvz = _modmm(c2d, sc1c, sh1c, w_qkvz, tm=TM_ROW, tn=TN_PROJ)
    projc_ab = _modmm(c2d, sc1c, sh1c, w_ab, tm=TM_ROW, tn=LANES)

    def decay_row(a):
        row = jnp.concatenate([a, jnp.zeros_like(a)], axis=1).reshape(-1)
        return jnp.pad(row, (0, LANES - 4 * hh))[None, :]

    alog_row = decay_row(dn_a_log[0])
    dt_row = decay_row(dn_dt_bias[0])

    conv_w = dn_conv_w[0]
    prep_c = _dn_prep(projc_qkvz.reshape(b, n_ctx, -1), conv_w, projc_ab.reshape(b, n_ctx, LANES),
                      alog_row, dt_row, n_heads=hh, dk=dk, row_w=n_ctx, tt=n_ctx)
    prep_l = _dn_prep(proj_qkvz.reshape(b, n, -1), conv_w, proj_ab.reshape(b, n, LANES),
                      alog_row, dt_row, n_heads=hh, dk=dk, row_w=GRID_W, tt=TT_DELTA)
    hp = 4 if hh % 4 == 0 else (2 if hh % 2 == 0 else 1)
    s_zero = jnp.zeros((2, b, hh, dk, dk), F32)
    _, _, s_ctx = _dn_scan(*prep_c, s_zero, dk=dk, tt=n_ctx, hp=hp)
    o_f, o_b, _ = _dn_scan(*prep_l, s_ctx, dk=dk, tt=TT_DELTA, hp=hp)

    filt = _hy_filters(n, hy_w1[0], hy_b1[0], hy_w2[0], hy_b2[0], hy_w3[0], hy_b3[0], hy_freq[0], hy_w)
    tables = _dft_tables(n)
    hy3 = proj_hy.reshape(b, n, 3 * hy_w)
    cwh = hy_conv_w[0]
    cb = hy_w // LANES
    y1 = _hy_conv(hy3, 0, hy3, cb, filt, 0, 2 * cb, hy_skip[0, 0:1], cwh, cwh, tables, conv_a=True, hy_w=hy_w)
    y_hy = _hy_conv(y1, 0, hy3, 2 * cb, filt, cb, 3 * cb, hy_skip[0, 1:2], cwh, cwh, tables, conv_a=False, hy_w=hy_w)

    merged = _merge(o_f.reshape(b * n, qk_w), o_b.reshape(b * n, qk_w), proj_qkvz, off_z // qk_w,
                    y_hy.reshape(b * n, hy_w), proj_gate, dn_norm_w[0][None, :],
                    w_branch_dn[0].astype(BF16), w_branch_hy[0].astype(BF16), dv=dk, tm=TM_ROW)
    sk = peer_subkeys[0].reshape(p_heads * 2, nkeys, half).astype(BF16)
    x1, h2, scores = _outproj(merged, x2d, w_out[0].astype(BF16), g1, ln1_g[0][None, :], ln1_b[0][None, :],
                              sc2, sh2, peer_wq[0].astype(BF16), sk, tm=TM_ROW)

    count1, rank2, e1, e2 = _peer_topk(scores, p_heads, tm=TM_ROW)
    out = _peer_dense(h2.T, peer_u[0].astype(BF16), peer_v[0].T.astype(BF16), rank2, e2, count1, e1,
                      x1, g2, ln2_g[0][None, :], ln2_b[0][None, :], n_heads=p_heads, ti=TI_PEER, tm=TM_PEER)
    stages = dict(o_f=o_f, o_b=o_b, s_ctx=s_ctx, filt=filt, y1=y1, y_hy=y_hy, merged=merged, x1=x1, scores=scores,
                  count1=count1, rank2=rank2, e1=e1, e2=e2)
    return out.reshape(b, n, d), stages


def kernel(x, c, ctx, c_ctx, w_ada, b_ada, w_in, dn_conv_w, dn_a_log, dn_dt_bias, dn_norm_w, hy_conv_w, hy_w1, hy_b1, hy_w2, hy_b2, hy_w3, hy_b3, hy_freq, hy_skip, w_branch_dn, w_branch_hy, w_out, ln1_g, ln1_b, peer_wq, peer_subkeys, peer_u, peer_v, ln2_g, ln2_b):
    out, _ = _block(x, c, ctx, c_ctx, w_ada, b_ada, w_in, dn_conv_w, dn_a_log, dn_dt_bias, dn_norm_w, hy_conv_w, hy_w1, hy_b1, hy_w2, hy_b2, hy_w3, hy_b3, hy_freq, hy_skip, w_branch_dn, w_branch_hy, w_out, ln1_g, ln1_b, peer_wq, peer_subkeys, peer_u, peer_v, ln2_g, ln2_b)
    return out
```

```python
import functools
import math

import jax
import jax.numpy as jnp
from jax import lax
from jax.experimental import pallas as pl
from jax.experimental.pallas import tpu as pltpu

F32 = jnp.float32
BF16 = jnp.bfloat16

GRID_W = 64
DN_CHUNK = 64
PEER_TOPK = 16
PEER_BAND = 32
DEPTH = 1
DEEPNORM_ALPHA = (2 * DEPTH) ** 0.25
LN_EPS = 1e-5
RMS_EPS = 1e-6
L2_EPS = 1e-6
HY_DECAY_TARGET = 1e-2
HY_FAST_DECAY = 0.3
HY_SLOW_DECAY = 1.5

LANES = 128
SUBLANES = 8
VMEM_LIMIT_BYTES = 56 * 1024 * 1024

TM_PROJ, TN_PROJ = 512, 2048
TT_DELTA = 8 * DN_CHUNK
TM_ROW = 256
TM_PEER, TI_PEER = 512, 8

FFT_N2 = 64
FFT_ROW_PAD = 8
FFT_UNROLL = 32


def _cparams(sem):
    return pltpu.CompilerParams(dimension_semantics=sem, vmem_limit_bytes=VMEM_LIMIT_BYTES)


def _silu(x):
    return x * jax.nn.sigmoid(x)


def _ada_kernel(c_ref, w_ref, b_ref, o_ref):
    s = _silu(c_ref[...])
    o_ref[...] = jnp.dot(s.astype(BF16), w_ref[...].astype(BF16), preferred_element_type=F32) + b_ref[...]


def _ada(cc, w, b):
    d, n6 = w.shape
    tn = min(n6, 1536)
    return pl.pallas_call(
        _ada_kernel,
        grid=(n6 // tn,),
        in_specs=[pl.BlockSpec((SUBLANES, d), lambda j: (0, 0)),
                  pl.BlockSpec((d, tn), lambda j: (0, j)),
                  pl.BlockSpec((1, tn), lambda j: (0, j))],
        out_specs=pl.BlockSpec((SUBLANES, tn), lambda j: (0, j)),
        out_shape=jax.ShapeDtypeStruct((SUBLANES, n6), F32),
        compiler_params=_cparams(("arbitrary",)),
    )(cc, w, b)


def _modmm_kernel(x_ref, sc_ref, sh_ref, w_ref, o_ref, h_scr):
    @pl.when(pl.program_id(1) == 0)
    def _():
        h_scr[...] = (x_ref[...] * (1.0 + sc_ref[0]) + sh_ref[0]).astype(BF16)

    o_ref[...] = jnp.dot(h_scr[...], w_ref[...], preferred_element_type=F32).astype(o_ref.dtype)


def _modmm(x2d, sc, sh, w, *, tm, tn, out_dtype=F32):
    m, d = x2d.shape
    ng = w.shape[1]
    r = sc.shape[0]
    tm = min(tm, m // r)
    tn = min(tn, ng)
    tiles_per_mod = (m // r) // tm
    return pl.pallas_call(
        _modmm_kernel,
        grid=(m // tm, ng // tn),
        in_specs=[pl.BlockSpec((tm, d), lambda i, j: (i, 0)),
                  pl.BlockSpec((1, 1, d), lambda i, j: (i // tiles_per_mod, 0, 0)),
                  pl.BlockSpec((1, 1, d), lambda i, j: (i // tiles_per_mod, 0, 0)),
                  pl.BlockSpec((d, tn), lambda i, j: (0, j))],
        out_specs=pl.BlockSpec((tm, tn), lambda i, j: (i, j)),
        out_shape=jax.ShapeDtypeStruct((m, ng), out_dtype),
        scratch_shapes=[pltpu.VMEM((tm, d), BF16)],
        compiler_params=_cparams(("parallel", "arbitrary")),
    )(x2d, sc, sh, w)


def _row_conv(x, cw, pos, row_w):
    rows = x.shape[0]
    k = cw.shape[0]
    pad = k // 2
    acc = None
    for j in range(k):
        off = j - pad
        if off == 0:
            term = x * cw[j:j + 1]
        else:
            shifted = pltpu.roll(x, (-off) % rows, axis=0)
            valid = (pos >= -off) if off < 0 else (pos < row_w - off)
            term = jnp.where(valid, shifted, 0.0) * cw[j:j + 1]
        acc = term if acc is None else acc + term
    return acc


SOLVE_BLOCK = 16


def _tri_inverse_pairs(pairs):
    c = pairs[0][0].shape[0]
    sb = SOLVE_BLOCK
    nb = c // sb
    ri = lax.broadcasted_iota(jnp.int32, (c, 2 * c), 0)
    ci = lax.broadcasted_iota(jnp.int32, (c, 2 * c), 1)
    ci = jnp.where(ci >= c, ci - c, ci)
    lane_lo = lax.broadcasted_iota(jnp.int32, (1, 2 * c), 1) < c
    off_diag = ri // sb != ci // sb
    eye2 = (ri == ci).astype(F32)
    mcats = [jnp.concatenate([m_a, m_b], axis=1) for m_a, m_b in pairs]
    m_offs = [jnp.where(off_diag, m, 0.0).astype(BF16) for m in mcats]
    xs = [[eye2[i * sb:(i + 1) * sb] for i in range(nb)] for _ in pairs]
    for bi in range(nb):
        rs = slice(bi * sb, (bi + 1) * sb)
        blks = [x[bi] for x in xs]
        if bi > 0:
            for s, x in enumerate(xs):
                xcur = jnp.concatenate(x, axis=0)
                xbd = jnp.concatenate([jnp.where(lane_lo, xcur, 0.0), jnp.where(lane_lo, 0.0, xcur)], axis=0)
                blks[s] = blks[s] - jnp.dot(m_offs[s][rs], xbd.astype(BF16), preferred_element_type=F32)
        mrs = [m[rs] for m in mcats]
        for jj in range(sb - 1):
            col = bi * sb + jj
            src = jnp.broadcast_to(jnp.where(lane_lo, col, c + col), (sb, 2 * c))
            for s in range(len(pairs)):
                mult = jnp.take_along_axis(mrs[s], src, axis=1)
                blks[s] = blks[s] - mult * blks[s][jj:jj + 1, :]
        for s, x in enumerate(xs):
            x[bi] = blks[s]
    return [jnp.concatenate(x, axis=0) for x in xs]


def _dn_gates_kernel(ab_ref, alog_ref, dt_ref, gc_ref, beta_ref, *, n_heads):
    tt = ab_ref.shape[1]
    c = DN_CHUNK
    cpos = lax.broadcasted_iota(jnp.int32, (tt, 1), 0) % c
    ab = ab_ref[0]
    z = ab + dt_ref[...]
    softplus = jnp.maximum(z, 0.0) + jnp.log1p(jnp.exp(-jnp.abs(z)))
    g_all = -jnp.exp(alog_ref[...]) * softplus
    gf = g_all
    gb = g_all
    d = 1
    while d < c:
        gf = gf + jnp.where(cpos >= d, pltpu.roll(gf, d, axis=0), 0.0)
        gb = gb + jnp.where(cpos < c - d, pltpu.roll(gb, (tt - d) % tt, axis=0), 0.0)
        d *= 2
    lane = lax.broadcasted_iota(jnp.int32, (1, LANES), 1)
    gc_ref[0] = jnp.where(lane < 2 * n_heads, gf, gb)
    beta_ref[0] = jax.nn.sigmoid(ab)


def _dn_gates(ab, alog_row, dt_row, *, n_heads, tt):
    b, n, _ = ab.shape
    blk = pl.BlockSpec((1, tt, LANES), lambda bi, g: (bi, g, 0))
    row = pl.BlockSpec((1, LANES), lambda bi, g: (0, 0))
    shp = jax.ShapeDtypeStruct((b, n, LANES), F32)
    return pl.pallas_call(
        functools.partial(_dn_gates_kernel, n_heads=n_heads),
        grid=(b, n // tt),
        in_specs=[blk, row, row],
        out_specs=[blk, blk],
        out_shape=[shp, shp],
        compiler_params=_cparams(("parallel", "parallel")),
    )(ab, alog_row, dt_row)


def _dn_prep_kernel(xq_ref, xk_ref, xv_ref, cq_ref, ck_ref, cv_ref, gc_ref, beta_ref,
                    u_ref, w_ref, qg_ref, kd_ref, attn_ref, eg_ref, *, row_w, n_heads, dk, hp):
    tt = xq_ref.shape[1]
    c = DN_CHUNK
    pos = lax.broadcasted_iota(jnp.int32, (tt, 1), 0) % row_w
    lane = lax.broadcasted_iota(jnp.int32, (1, LANES), 1)
    gc = gc_ref[0]
    beta_all = beta_ref[0]
    eg_ref[...] = jnp.zeros(eg_ref.shape, F32)
    ri = lax.broadcasted_iota(jnp.int32, (c, c), 0)
    ci = lax.broadcasted_iota(jnp.int32, (c, c), 1)
    eye = ri == ci
    lower = ri >= ci
    upper = ri <= ci
    strict_lower = ri > ci
    nt = (((1,), (1,)), ((), ()))

    def col(a, l):
        return jnp.sum(jnp.where(lane == l, a, 0.0), axis=1, keepdims=True)

    def l2n(y):
        return y * lax.rsqrt(jnp.sum(y * y, axis=-1, keepdims=True) + L2_EPS)

    units = []
    for hh in range(hp):
        h = pl.program_id(1) * hp + hh
        hs = slice(hh * dk, (hh + 1) * dk)
        conv_silu = lambda x_ref, cw_ref: _silu(_row_conv(x_ref[0, :, hs], cw_ref[:, hs], pos, row_w))
        q = l2n(conv_silu(xq_ref, cq_ref)) * (dk ** -0.5)
        k = l2n(conv_silu(xk_ref, ck_ref))
        v = conv_silu(xv_ref, cv_ref)
        g_cols = (col(gc, h), col(gc, 2 * n_heads + h))
        b_cols = (col(beta_all, n_heads + h), col(beta_all, 3 * n_heads + h))
        for cc in range(tt // c):
            sl = slice(cc * c, (cc + 1) * c)
            units.append(dict(hh=hh, hs=hs, cc=cc, sl=sl, q=q[sl], k=k[sl], v=v[sl],
                              g=[g_cols[dr][sl] for dr in range(2)], b=[b_cols[dr][sl] for dr in range(2)]))

    for un in units:
        qc, kc, vc, gcols = un["q"], un["k"], un["v"], un["g"]
        grows = [jnp.sum(jnp.where(eye, g, 0.0), axis=0, keepdims=True) for g in gcols]
        kbs = [kc * un["b"][dr] for dr in range(2)]
        vbs = [vc * un["b"][dr] for dr in range(2)]
        egs = [jnp.exp(g) for g in gcols]
        dif_f = gcols[0] - grows[0]
        dif_b = gcols[1] - grows[1]
        un["decay_f"] = jnp.where(lower, jnp.exp(jnp.where(lower, dif_f, 0.0)), 0.0)
        un["decay_b"] = jnp.where(upper, jnp.exp(jnp.where(upper, dif_b, 0.0)), 0.0)
        un["decay_bt"] = jnp.where(lower, jnp.exp(jnp.where(lower, -dif_b, 0.0)), 0.0)
        un["kcb"] = kc.astype(BF16)
        un["lhs"] = jnp.concatenate([kbs[0], qc], axis=0).astype(BF16)
        un["kb_b"] = kbs[1].astype(BF16)
        un["rhs_f"] = jnp.concatenate([vbs[0], kbs[0] * egs[0]], axis=1).astype(BF16)
        un["rhs_b"] = jnp.concatenate([vbs[1], kbs[1] * egs[1]], axis=1).astype(BF16)
        for dr in range(2):
            gcol = gcols[dr]
            glast = gcol[c - 1:c] if dr == 0 else gcol[0:1]
            qg_ref[dr, 0, un["sl"], un["hs"]] = (qc * egs[dr]).astype(BF16)
            kd_ref[dr, 0, un["sl"], un["hs"]] = (kc * jnp.exp(glast - gcol)).astype(BF16)
            eg_ref[dr, 0, un["hh"], 0, un["cc"]:un["cc"] + 1, :] = jnp.broadcast_to(jnp.exp(glast), (1, LANES))

    for un in units:
        un["kq"] = lax.dot_general(un["lhs"], un["kcb"], nt, preferred_element_type=F32)
        un["kkt_b"] = lax.dot_general(un["kcb"], un["kb_b"], nt, preferred_element_type=F32)

    for un in units:
        un["m_f"] = jnp.where(strict_lower, un["kq"][:c] * un["decay_f"], 0.0)
        un["m_bt"] = jnp.where(strict_lower, un["kkt_b"] * un["decay_bt"], 0.0)
        attn_ref[0, 0, un["hh"], un["sl"], :] = (un["kq"][c:] * un["decay_f"]).astype(BF16)
        attn_ref[1, 0, un["hh"], un["sl"], :] = (un["kq"][c:] * un["decay_b"]).astype(BF16)

    tinvs = _tri_inverse_pairs([(un["m_f"], un["m_bt"]) for un in units])

    zero_rows = jnp.zeros((c, 2 * dk), BF16)
    sols = []
    for tinv, un in zip(tinvs, units):
        tinv = tinv.astype(BF16)
        sol_f = jnp.dot(tinv, jnp.concatenate([un["rhs_f"], zero_rows], axis=0), preferred_element_type=F32)
        sol_b = lax.dot_general(tinv, un["rhs_b"], (((0,), (0,)), ((), ())), preferred_element_type=F32)[c:]
        sols.append((sol_f, sol_b))
    for (sol_f, sol_b), un in zip(sols, units):
        for dr, sol in enumerate((sol_f, sol_b)):
            u_ref[dr, 0, un["sl"], un["hs"]] = sol[:, :dk]
            w_ref[dr, 0, un["sl"], un["hs"]] = sol[:, dk:].astype(BF16)


def _dn_prep(qkv, conv_w, ab, alog_row, dt_row, *, n_heads, dk, row_w, tt):
    gc, beta = _dn_gates(ab, alog_row, dt_row, n_heads=n_heads, tt=tt)
    b, n, _ = qkv.shape
    hh = n_heads
    ng = n // tt
    c = DN_CHUNK
    assert tt % row_w == 0 and tt % c == 0 and tt // c <= SUBLANES
    hp = 2 if hh % 2 == 0 else 1
    nhb = hh // hp
    kern = functools.partial(_dn_prep_kernel, row_w=row_w, n_heads=hh, dk=dk, hp=hp)
    xspec = lambda off: pl.BlockSpec((1, tt, hp * dk), lambda bi, h, g: (bi, g, off + h))
    cspec = lambda off: pl.BlockSpec((conv_w.shape[0], hp * dk), lambda bi, h, g: (0, off + h))
    ospec = pl.BlockSpec((2, 1, tt, hp * dk), lambda bi, h, g: (0, bi, g, h))
    return pl.pallas_call(
        kern,
        grid=(b, nhb, ng),
        in_specs=[xspec(0), xspec(nhb), xspec(2 * nhb), cspec(0), cspec(nhb), cspec(2 * nhb),
                  pl.BlockSpec((1, tt, LANES), lambda bi, h, g: (bi, g, 0)),
                  pl.BlockSpec((1, tt, LANES), lambda bi, h, g: (bi, g, 0))],
        out_specs=[ospec, ospec, ospec, ospec,
                   pl.BlockSpec((2, 1, hp, tt, c), lambda bi, h, g: (0, bi, h, g, 0)),
                   pl.BlockSpec((2, 1, hp, 1, SUBLANES, LANES), lambda bi, h, g: (0, bi, h, g, 0, 0))],
        out_shape=[jax.ShapeDtypeStruct((2, b, n, hh * dk), F32),
                   jax.ShapeDtypeStruct((2, b, n, hh * dk), BF16),
                   jax.ShapeDtypeStruct((2, b, n, hh * dk), BF16),
                   jax.ShapeDtypeStruct((2, b, n, hh * dk), BF16),
                   jax.ShapeDtypeStruct((2, b, hh, n, c), BF16),
                   jax.ShapeDtypeStruct((2, b, hh, ng, SUBLANES, LANES), F32)],
        compiler_params=_cparams(("parallel", "parallel", "arbitrary")),
    )(qkv, qkv, qkv, conv_w, conv_w, conv_w, gc, beta)


def _dn_scan_kernel(uf_ref, wf_ref, qf_ref, kf_ref, af_ref, ef_ref,
                    ub_ref, wb_ref, qb_ref, kb_ref, ab_ref, eb_ref, s0_ref,
                    of_ref, ob_ref, sout_ref, s_scr, *, hp, dk):
    g = pl.program_id(2)
    ng = pl.num_programs(2)
    c = DN_CHUNK
    tt = uf_ref.shape[2]
    ncs = tt // c

    @pl.when(g == 0)
    def _():
        s_scr[...] = s0_ref[:, 0]

    refs = ((uf_ref, wf_ref, qf_ref, kf_ref, af_ref, ef_ref, of_ref),
            (ub_ref, wb_ref, qb_ref, kb_ref, ab_ref, eb_ref, ob_ref))
    chains = [(dr, hh) for dr in range(2) for hh in range(hp)]
    state = {ch: s_scr[ch[0], ch[1]] for ch in chains}
    for step in range(ncs):
        rows = {}
        for dr, hh in chains:
            cc = step if dr == 0 else ncs - 1 - step
            rows[dr, hh] = (slice(cc * c, (cc + 1) * c), slice(hh * dk, (hh + 1) * dk), cc)
        wide = {}
        for dr in range(2):
            sl = rows[dr, 0][0]
            wide[dr] = [refs[dr][i][0, 0, sl, :] for i in range(4)]
        r = {}
        for ch in chains:
            _, hs, _ = rows[ch]
            wq = jnp.concatenate([wide[ch[0]][1][:, hs], wide[ch[0]][2][:, hs]], axis=0)
            r[ch] = jnp.dot(wq, state[ch].astype(BF16), preferred_element_type=F32)
        vnb = {}
        for ch in chains:
            _, hs, _ = rows[ch]
            vnb[ch] = (wide[ch[0]][0][:, hs] - r[ch][:c]).astype(BF16)
        outs = {}
        for ch in chains:
            a_ref, e_ref = refs[ch[0]][4], refs[ch[0]][5]
            sl, hs, cc = rows[ch]
            outs[ch] = r[ch][c:] + jnp.dot(a_ref[0, 0, ch[1], sl, :], vnb[ch], preferred_element_type=F32)
            upd = lax.dot_general(wide[ch[0]][3][:, hs], vnb[ch], (((0,), (0,)), ((), ())),
                                  preferred_element_type=F32)
            state[ch] = state[ch] * e_ref[0, 0, ch[1], 0, cc:cc + 1, :] + upd
        for dr in range(2):
            refs[dr][6][0, rows[dr, 0][0], :] = jnp.concatenate([outs[dr, hh] for hh in range(hp)], axis=1)
    for ch in chains:
        s_scr[ch[0], ch[1]] = state[ch]

    @pl.when(g == ng - 1)
    def _():
        sout_ref[:, 0] = s_scr[...]


def _dn_scan(u, w, qg, kd, attn, eg, s0, *, dk, tt, hp):
    _, b, n, hd = u.shape
    hh = hd // dk
    ng = n // tt
    c = DN_CHUNK
    kern = functools.partial(_dn_scan_kernel, hp=hp, dk=dk)

    def specs(dr):
        gi = (lambda g: g) if dr == 0 else (lambda g: ng - 1 - g)
        tok = pl.BlockSpec((1, 1, tt, hp * dk), lambda bi, h, g: (dr, bi, gi(g), h))
        return [tok, tok, tok, tok,
                pl.BlockSpec((1, 1, hp, tt, c), lambda bi, h, g: (dr, bi, h, gi(g), 0)),
                pl.BlockSpec((1, 1, hp, 1, SUBLANES, LANES), lambda bi, h, g: (dr, bi, h, gi(g), 0, 0))]

    sspec = pl.BlockSpec((2, 1, hp, dk, dk), lambda bi, h, g: (0, bi, h, 0, 0))
    return pl.pallas_call(
        kern,
        grid=(b, hh // hp, ng),
        in_specs=specs(0) + specs(1) + [sspec],
        out_specs=[pl.BlockSpec((1, tt, hp * dk), lambda bi, h, g: (bi, g, h)),
                   pl.BlockSpec((1, tt, hp * dk), lambda bi, h, g: (bi, ng - 1 - g, h)),
                   sspec],
        out_shape=[jax.ShapeDtypeStruct((b, n, hd), F32),
                   jax.ShapeDtypeStruct((b, n, hd), F32),
                   jax.ShapeDtypeStruct((2, b, hh, dk, dk), F32)],
        scratch_shapes=[pltpu.VMEM((2, hp, dk, dk), F32)],
        compiler_params=_cparams(("parallel", "parallel", "arbitrary")),
    )(u, w, qg, kd, attn, eg, u, w, qg, kd, attn, eg, s0)


def _hy_hidden_kernel(z_ref, w1_ref, b1_ref, w2_ref, b2_ref, fr_ref, o_ref):
    fr = fr_ref[...]
    h1 = jnp.sin(fr * (jnp.dot(z_ref[...].astype(BF16), w1_ref[...].astype(BF16),
                               preferred_element_type=F32) + b1_ref[...]))
    o_ref[...] = jnp.sin(fr * (jnp.dot(h1.astype(BF16), w2_ref[...].astype(BF16),
                                       preferred_element_type=F32) + b2_ref[...]))


def _hy_filter_kernel(hid_ref, w3_ref, b3_ref, t_ref, delta_ref, o_ref):
    f = jnp.dot(hid_ref[...].astype(BF16), w3_ref[...].astype(BF16), preferred_element_type=F32) + b3_ref[...]
    o_ref[...] = f * jnp.exp(-t_ref[...] * delta_ref[...])


def _hy_filters(n, w1, b1, w2, b2, w3, b3, freq, hy_w):
    emb, hid = w1.shape
    bands = (emb - 1) // 2
    t = jnp.linspace(0.0, 1.0, n, dtype=F32)
    pos = jnp.arange(n, dtype=F32)
    bnd = jnp.linspace(1e-4, bands - 1, bands, dtype=F32)
    ang = (2.0 * math.pi / n) * pos[:, None] * bnd[None, :]
    z = jnp.concatenate([t[:, None], jnp.cos(ang), -jnp.sin(ang)], axis=-1)
    embp = -(-emb // SUBLANES) * SUBLANES
    z = jnp.pad(z, ((0, 0), (0, embp - emb)))
    w1p = jnp.pad(w1, ((0, embp - emb), (0, 0)))
    full = lambda a: pl.BlockSpec(a.shape, lambda: (0,) * a.ndim)
    args = (z, w1p, b1[None, :], w2, b2[None, :], freq[None, :])
    hid2 = pl.pallas_call(
        _hy_hidden_kernel,
        in_specs=[full(a) for a in args],
        out_specs=pl.BlockSpec((n, hid), lambda: (0, 0)),
        out_shape=jax.ShapeDtypeStruct((n, hid), F32),
        compiler_params=pltpu.CompilerParams(vmem_limit_bytes=VMEM_LIMIT_BYTES),
    )(*args)
    nout = w3.shape[1]
    deltas = jnp.abs(jnp.linspace(math.log(HY_DECAY_TARGET) / HY_SLOW_DECAY,
                                  math.log(HY_DECAY_TARGET) / HY_FAST_DECAY, hy_w, dtype=F32))
    delta_row = jnp.tile(deltas, nout // hy_w)[None, :]
    tc = min(nout, 1024)
    return pl.pallas_call(
        _hy_filter_kernel,
        grid=(nout // tc,),
        in_specs=[pl.BlockSpec((n, hid), lambda j: (0, 0)),
                  pl.BlockSpec((hid, tc), lambda j: (0, j)),
                  pl.BlockSpec((1, tc), lambda j: (0, j)),
                  pl.BlockSpec((n, 1), lambda j: (0, 0)),
                  pl.BlockSpec((1, tc), lambda j: (0, j))],
        out_specs=pl.BlockSpec((n, tc), lambda j: (0, j)),
        out_shape=jax.ShapeDtypeStruct((n, nout), F32),
        compiler_params=_cparams(("parallel",)),
    )(hid2, w3, b3[None, :], t[:, None], delta_row)


def _dft_tables(n):
    n2 = FFT_N2
    big = 2 * n
    n1 = big // n2
    k1 = n1 // 2
    two_pi = 2.0 * math.pi

    def cs(idx, period):
        a = (idx % period).astype(F32) * (two_pi / period)
        return jnp.cos(a), jnp.sin(a)

    f1 = jnp.arange(n1, dtype=jnp.int32)
    s1 = jnp.arange(k1, dtype=jnp.int32)
    s2 = jnp.arange(n2, dtype=jnp.int32)
    s_full = s1[None, None, :] * n2 + s2[:, None, None]
    c, s = cs(f1[None, :, None] * s_full, big)
    f1tw = jnp.concatenate([jnp.concatenate([c, s], axis=2), jnp.concatenate([-s, c], axis=2)], axis=1)
    ct, st = jnp.swapaxes(c, 1, 2), jnp.swapaxes(s, 1, 2)
    g1tw = jnp.concatenate([jnp.concatenate([ct, -st], axis=2), jnp.concatenate([st, ct], axis=2)], axis=1) / big
    c2, sn2 = cs(s2[:, None] * s2[None, :], n2)
    f2 = jnp.concatenate([jnp.concatenate([c2, sn2], axis=1), jnp.concatenate([-sn2, c2], axis=1)], axis=0)
    g2 = jnp.concatenate([jnp.concatenate([c2, -sn2], axis=1), jnp.concatenate([sn2, c2], axis=1)], axis=0)
    return f1tw.astype(BF16), f2.astype(BF16), g2.astype(BF16), g1tw.astype(BF16)


def _hy_conv_kernel(a_ref, m_ref, hf_ref, hb_ref, skip_ref, cwa_ref, cwm_ref,
                    f1_ref, f2_ref, g2_ref, g1_ref, o_ref,
                    h_scr, b_scr, zr_scr, zi_scr, *, conv_a, n1, k1):
    n2 = FFT_N2
    k1p = k1 + FFT_ROW_PAD
    pb = 2 * n2 + FFT_ROW_PAD
    pos = lax.broadcasted_iota(jnp.int32, (n2, 1), 0)

    grp = min(FFT_UNROLL, n1 // 2)
    cw = b_scr.shape[1]

    def stage1(real_only):
        def body(g, carry):
            s2s = [g * grp + t for t in range(grp)]
            bases = [pl.multiple_of(s2 * k1p, SUBLANES) for s2 in s2s]
            if real_only:
                xs = [zr_scr[pl.ds(base, k1), :].astype(BF16) for base in bases]
                outs = [jnp.dot(f1_ref[s2, :, :k1], x, preferred_element_type=F32) for s2, x in zip(s2s, xs)]
            else:
                xs = [jnp.concatenate([zr_scr[pl.ds(base, k1), :], zi_scr[pl.ds(base, k1), :]], axis=0).astype(BF16)
                      for base in bases]
                outs = [jnp.dot(f1_ref[s2], x, preferred_element_type=F32) for s2, x in zip(s2s, xs)]
            for s2, a in zip(s2s, outs):
                b_scr[pl.ds(s2, n1, stride=pb), :] = a[:n1]
                b_scr[pl.ds(n2 + s2, n1, stride=pb), :] = a[n1:]
            return carry
        lax.fori_loop(0, n2 // grp, body, 0)

    def slab_pairs(g):
        f1s = [[(g * grp + t) * 2 + u for u in range(2)] for t in range(grp)]
        bbs = [[pl.multiple_of(f1 * pb, SUBLANES) for f1 in pair] for pair in f1s]
        hbs = [[pl.multiple_of(f1 * 2 * n2, 2 * n2) for f1 in pair] for pair in f1s]
        return bbs, hbs

    def forward2(bbs):
        bpairs = [jnp.concatenate([b_scr[pl.ds(bb, 2 * n2), :] for bb in pair], axis=1).astype(BF16) for pair in bbs]
        return [jnp.dot(f2_ref[...], bp, preferred_element_type=F32) for bp in bpairs]

    @pl.when(pl.program_id(1) == 0)
    def _():
        for which, ref in ((0, hf_ref), (1, hb_ref)):
            def load(s1, carry, ref=ref):
                r0 = pl.multiple_of(s1 * n2, n2)
                zr_scr[pl.ds(s1, n2, stride=k1p), :] = ref[pl.ds(r0, n2), :]
                return carry
            lax.fori_loop(0, k1, load, 0, unroll=FFT_UNROLL)
            if which == 1:
                zr_scr[0:1, :] = jnp.zeros((1, zr_scr.shape[1]), F32)
            stage1(True)

            def spec(g, carry, which=which):
                bbs, hbs = slab_pairs(g)
                zpairs = forward2(bbs)
                for zpair, hpair in zip(zpairs, hbs):
                    for t, hb_ in enumerate(hpair):
                        zz = zpair[:, t * cw:(t + 1) * cw]
                        if which == 0:
                            h_scr[pl.ds(hb_, 2 * n2), :] = zz
                        else:
                            h_scr[pl.ds(hb_, n2), :] = h_scr[pl.ds(hb_, n2), :] + zz[:n2]
                            h_scr[pl.ds(hb_ + n2, n2), :] = h_scr[pl.ds(hb_ + n2, n2), :] - zz[n2:]
                return carry
            lax.fori_loop(0, n1 // (2 * grp), spec, 0)

    def load_a(s1, carry):
        r0 = pl.multiple_of(s1 * n2, n2)
        for bi, scr in ((0, zr_scr), (1, zi_scr)):
            blk = a_ref[bi, pl.ds(r0, n2), :]
            if conv_a:
                blk = _row_conv(blk, cwa_ref[...], pos, n2)
            scr[pl.ds(s1, n2, stride=k1p), :] = blk
        return carry
    lax.fori_loop(0, k1, load_a, 0, unroll=FFT_UNROLL)

    stage1(False)

    def mid(g, carry):
        bbs, hbs = slab_pairs(g)
        zzs = forward2(bbs)
        yys = []
        for zz, hpair in zip(zzs, hbs):
            zr, zi = zz[:n2], zz[n2:]
            hr = jnp.concatenate([h_scr[pl.ds(hb_, n2), :] for hb_ in hpair], axis=1)
            hi = jnp.concatenate([h_scr[pl.ds(hb_ + n2, n2), :] for hb_ in hpair], axis=1)
            yys.append(jnp.concatenate([zr * hr - zi * hi, zr * hi + zi * hr], axis=0).astype(BF16))
        wws = [jnp.dot(g2_ref[...], yy, preferred_element_type=F32) for yy in yys]
        for ww, pair in zip(wws, bbs):
            for t, bb in enumerate(pair):
                b_scr[pl.ds(bb, 2 * n2), :] = ww[:, t * cw:(t + 1) * cw]
        return carry
    lax.fori_loop(0, n1 // (2 * grp), mid, 0)

    skip = skip_ref[...]

    def last(g, carry):
        s2s = [g * grp + t for t in range(grp)]
        wws = [jnp.concatenate([b_scr[pl.ds(s2, n1, stride=pb), :],
                                b_scr[pl.ds(n2 + s2, n1, stride=pb), :]], axis=0).astype(BF16) for s2 in s2s]
        ys = [jnp.dot(g1_ref[s2], ww, preferred_element_type=F32) for s2, ww in zip(s2s, wws)]
        for s2, y in zip(s2s, ys):
            base = pl.multiple_of(s2 * k1p, SUBLANES)
            zr_scr[pl.ds(base, k1), :] = y[:k1] + zr_scr[pl.ds(base, k1), :] * skip
            zi_scr[pl.ds(base, k1), :] = y[k1:] + zi_scr[pl.ds(base, k1), :] * skip
        return carry
    lax.fori_loop(0, n2 // grp, last, 0)

    def fin(s1, carry):
        r0 = pl.multiple_of(s1 * n2, n2)
        for bi, scr in ((0, zr_scr), (1, zi_scr)):
            mm = _row_conv(m_ref[bi, pl.ds(r0, n2), :], cwm_ref[...], pos, n2)
            o_ref[bi, pl.ds(r0, n2), :] = mm * scr[pl.ds(s1, n2, stride=k1p), :]
        return carry
    lax.fori_loop(0, k1, fin, 0, unroll=FFT_UNROLL)


def _hy_conv(a, a_col0, m, m_col0, filt, f_col_fwd, f_col_bwd, skip, cwa, cwm, tables, *, conv_a, hy_w):
    b, n, _ = a.shape
    assert b % 2 == 0 and GRID_W == FFT_N2 and n % FFT_N2 == 0
    n2 = FFT_N2
    n1 = 2 * n // n2
    k1 = n1 // 2
    k1p = k1 + FFT_ROW_PAD
    pb = 2 * n2 + FFT_ROW_PAD
    cw = LANES
    f1tw, f2, g2, g1tw = tables
    kern = functools.partial(_hy_conv_kernel, conv_a=conv_a, n1=n1, k1=k1)
    once = pl.Buffered(1)
    return pl.pallas_call(
        kern,
        grid=(hy_w // cw, b // 2),
        in_specs=[pl.BlockSpec((2, n, cw), lambda c, p: (p, 0, a_col0 + c), pipeline_mode=once),
                  pl.BlockSpec((2, n, cw), lambda c, p: (p, 0, m_col0 + c), pipeline_mode=once),
                  pl.BlockSpec((n, cw), lambda c, p: (0, f_col_fwd + c), pipeline_mode=once),
                  pl.BlockSpec((n, cw), lambda c, p: (0, f_col_bwd + c), pipeline_mode=once),
                  pl.BlockSpec((1, cw), lambda c, p: (0, c)),
                  pl.BlockSpec((cwa.shape[0], cw), lambda c, p: (0, a_col0 + c if conv_a else 0)),
                  pl.BlockSpec((cwm.shape[0], cw), lambda c, p: (0, m_col0 + c)),
                  pl.BlockSpec(f1tw.shape, lambda c, p: (0, 0, 0), pipeline_mode=once),
                  pl.BlockSpec(f2.shape, lambda c, p: (0, 0)),
                  pl.BlockSpec(g2.shape, lambda c, p: (0, 0)),
                  pl.BlockSpec(g1tw.shape, lambda c, p: (0, 0, 0), pipeline_mode=once)],
        out_specs=pl.BlockSpec((2, n, cw), lambda c, p: (p, 0, c)),
        out_shape=jax.ShapeDtypeStruct((b, n, hy_w), F32),
        scratch_shapes=[pltpu.VMEM((n1 * 2 * n2, cw), F32),
                        pltpu.VMEM((n1 * pb, cw), F32),
                        pltpu.VMEM((n2 * k1p, cw), F32),
                        pltpu.VMEM((n2 * k1p, cw), F32)],
        compiler_params=_cparams(("parallel", "arbitrary")),
    )(a, m, filt, filt, skip, cwa, cwm, f1tw, f2, g2, g1tw)


def _merge_kernel(of_ref, ob_ref, z_ref, y_ref, ga_ref, gb_ref, nw_ref, wa_ref, wh_ref, o_ref, *, dv):
    nw = nw_ref[...]
    o_all = of_ref[...] + ob_ref[...]
    gate_all = _silu(z_ref[...])
    heads = []
    for hh in range(of_ref.shape[1] // dv):
        hs = slice(hh * dv, (hh + 1) * dv)
        o = o_all[:, hs]
        o = o * lax.rsqrt(jnp.mean(o * o, axis=-1, keepdims=True) + RMS_EPS) * nw
        heads.append((o * gate_all[:, hs]).astype(BF16))
    pa = jnp.dot(jnp.concatenate(heads, axis=1), wa_ref[...], preferred_element_type=F32)
    ph = jnp.dot(y_ref[...].astype(BF16), wh_ref[...], preferred_element_type=F32)
    o_ref[...] = (jax.nn.sigmoid(ga_ref[...]) * pa + jax.nn.sigmoid(gb_ref[...]) * ph).astype(o_ref.dtype)


def _merge(o_f, o_b, z, z_blk, y_hy, gate, norm_w, w_pa, w_ph, *, dv, tm):
    m, dvw = o_f.shape
    d = w_pa.shape[1]
    tm = min(tm, m)
    kern = functools.partial(_merge_kernel, dv=dv)
    row = lambda w_, blk=0: pl.BlockSpec((tm, w_), lambda i: (i, blk))
    const = lambda a: pl.BlockSpec(a.shape, lambda i: (0, 0), pipeline_mode=pl.Buffered(1))
    return pl.pallas_call(
        kern,
        grid=(m // tm,),
        in_specs=[row(dvw), row(dvw), row(dvw, z_blk), row(y_hy.shape[1]), row(d, 0), row(d, 1),
                  pl.BlockSpec((1, dv), lambda i: (0, 0)), const(w_pa), const(w_ph)],
        out_specs=row(d),
        out_shape=jax.ShapeDtypeStruct((m, d), BF16),
        compiler_params=_cparams(("parallel",)),
    )(o_f, o_b, z, y_hy, gate, gate, norm_w, w_pa, w_ph)


def _layer_norm(x, g, b):
    mu = jnp.mean(x, axis=-1, keepdims=True)
    xc = x - mu
    var = jnp.mean(xc * xc, axis=-1, keepdims=True)
    return xc * lax.rsqrt(var + LN_EPS) * g + b


def _outproj_kernel(mg_ref, x_ref, wo_ref, g1_ref, lg_ref, lb_ref, sc_ref, sh_ref, wq_ref, sk_ref,
                    x1_ref, h2_ref, s_ref, *, half):
    mix = jnp.dot(mg_ref[...], wo_ref[...], preferred_element_type=F32)
    x1 = _layer_norm(DEEPNORM_ALPHA * x_ref[...] + g1_ref[0] * mix, lg_ref[...], lb_ref[...])
    x1_ref[...] = x1
    h2 = (x1 * (1.0 + sc_ref[0]) + sh_ref[0]).astype(BF16)
    h2_ref[...] = h2
    qp = jnp.dot(h2, wq_ref[...], preferred_element_type=F32).astype(BF16)
    for j in range(sk_ref.shape[0]):
        js = slice(j * half, (j + 1) * half)
        s_ref[j] = lax.dot_general(sk_ref[j], qp[:, js], (((1,), (1,)), ((), ())), preferred_element_type=F32)


def _outproj(merged, x2d, w_out, g1, ln_g, ln_b, sc2, sh2, wq, subkeys, *, tm):
    m, d = x2d.shape
    r = g1.shape[0]
    tm = min(tm, m // r)
    tpm = (m // r) // tm
    nsk, nkeys, half = subkeys.shape
    assert nkeys == LANES
    kern = functools.partial(_outproj_kernel, half=half)
    row = lambda w_: pl.BlockSpec((tm, w_), lambda i: (i, 0))
    mod = pl.BlockSpec((1, 1, d), lambda i: (i // tpm, 0, 0))
    vec = pl.BlockSpec((1, d), lambda i: (0, 0))
    return pl.pallas_call(
        kern,
        grid=(m // tm,),
        in_specs=[row(d), row(d), pl.BlockSpec(w_out.shape, lambda i: (0, 0), pipeline_mode=pl.Buffered(1)),
                  mod, vec, vec, mod, mod,
                  pl.BlockSpec(wq.shape, lambda i: (0, 0), pipeline_mode=pl.Buffered(1)),
                  pl.BlockSpec(subkeys.shape, lambda i: (0, 0, 0))],
        out_specs=[row(d), row(d), pl.BlockSpec((nsk, nkeys, tm), lambda i: (0, 0, i))],
        out_shape=[jax.ShapeDtypeStruct((m, d), F32),
                   jax.ShapeDtypeStruct((m, d), BF16),
                   jax.ShapeDtypeStruct((nsk, nkeys, m), F32)],
        compiler_params=_cparams(("parallel",)),
    )(merged, x2d, w_out, g1, ln_g, ln_b, sc2, sh2, wq, subkeys)


def _cand_tables():
    pairs = [(a, b) for a in range(PEER_TOPK) for b in range(PEER_TOPK) if (a + 1) * (b + 1) <= PEER_TOPK]
    ia = jnp.array([p[0] for p in pairs], dtype=jnp.int32)
    ib = jnp.array([p[1] for p in pairs], dtype=jnp.int32)
    ncand = len(pairs)
    rows = -(-ncand // (2 * SUBLANES)) * (2 * SUBLANES)
    ranks = jnp.arange(LANES, dtype=jnp.int32)
    p1 = (jnp.pad(ia, (0, rows - ncand), constant_values=-1)[:, None] == ranks[None, :]).astype(F32)
    p2 = (jnp.pad(ib, (0, rows - ncand), constant_values=-1)[:, None] == ranks[None, :]).astype(F32)
    return p1, p2, ncand


def _peer_topk_kernel(s_ref, p1_ref, p2_ref, p1t_ref, c1_ref, r2_ref, e1_ref, e2_ref, *, n_heads, ncand):
    tm = s_ref.shape[2]
    neg = -jnp.inf
    hi = lax.Precision.HIGHEST
    no_rank = float(LANES)

    def pop_max(x, row_id):
        mx = jnp.max(x, axis=0, keepdims=True)
        first = jnp.min(jnp.where(x == mx, row_id, no_rank), axis=0, keepdims=True)
        hit = row_id == first
        return mx, hit, jnp.where(hit, neg, x)

    key_id = lax.broadcasted_iota(jnp.int32, (LANES, tm), 0).astype(F32)
    cand_id = lax.broadcasted_iota(jnp.int32, (p1_ref.shape[0], tm), 0).astype(F32)
    pad_rows = jnp.zeros((LANES - PEER_TOPK, tm), F32)

    def sorted_top(s):
        tops = []
        rank = jnp.full(s.shape, no_rank, F32)
        x = s
        for a in range(PEER_TOPK):
            mx, hit, x = pop_max(x, key_id)
            tops.append(mx)
            rank = jnp.where(hit, float(a), rank)
        return jnp.concatenate(tops + [pad_rows], axis=0), rank

    for h in range(n_heads):
        s1 = s_ref[2 * h]
        s2 = s_ref[2 * h + 1]
        t1, rank1 = sorted_top(s1)
        t2, rank2 = sorted_top(s2)
        cand = (jnp.dot(p1_ref[...], t1, precision=hi, preferred_element_type=F32)
                + jnp.dot(p2_ref[...], t2, precision=hi, preferred_element_type=F32))
        x = jnp.where(cand_id < ncand, cand, neg)
        sel = jnp.zeros(x.shape, F32)
        zsum = None
        cmax = None
        for r in range(PEER_TOPK):
            mx, hit, x = pop_max(x, cand_id)
            sel = jnp.where(hit, 1.0, sel)
            if r == 0:
                cmax = mx
                zsum = jnp.ones_like(mx)
            else:
                zsum = zsum + jnp.exp(mx - cmax)
        cnt = jnp.dot(p1t_ref[...].astype(BF16), sel.astype(BF16), preferred_element_type=F32)
        count1 = jnp.zeros(s1.shape, F32)
        for a in range(PEER_TOPK):
            count1 = jnp.where(rank1 == a, cnt[a:a + 1], count1)
        c1_ref[h] = count1
        r2_ref[h] = rank2.astype(r2_ref.dtype)
        e1_ref[h] = jnp.exp(s1 - t1[0:1]) / zsum
        e2_ref[h] = jnp.exp(s2 - t2[0:1]).astype(e2_ref.dtype)


def _peer_topk(scores, n_heads, *, tm):
    nsk, nkeys, m = scores.shape
    tm = min(tm, m)
    p1, p2, ncand = _cand_tables()
    kern = functools.partial(_peer_topk_kernel, n_heads=n_heads, ncand=ncand)
    full = lambda a: pl.BlockSpec(a.shape, lambda i: (0, 0))
    out = pl.BlockSpec((n_heads, nkeys, tm), lambda i: (0, 0, i))
    shp = lambda dt: jax.ShapeDtypeStruct((n_heads, nkeys, m), dt)
    p1t = p1.T
    return pl.pallas_call(
        kern,
        grid=(m // tm,),
        in_specs=[pl.BlockSpec((nsk, nkeys, tm), lambda i: (0, 0, i)), full(p1), full(p2), full(p1t)],
        out_specs=[out, out, out, out],
        out_shape=[shp(F32), shp(F32), shp(F32), shp(F32)],
        compiler_params=_cparams(("parallel",)),
    )(scores, p1, p2, p1t)


def _peer_dense_kernel(h2t_ref, u_ref, vt_ref, r2_ref, e2_ref, c1_ref, e1_ref, x1_ref, g2_ref, lg_ref, lb_ref,
                       o_ref, acc_scr, a_scr, p_scr, r2_scr, e2_scr, *, n_heads, ti):
    e = pl.program_id(1)

    @pl.when(e == 0)
    def _():
        acc_scr[...] = jnp.zeros(acc_scr.shape, F32)
        r2_scr[...] = r2_ref[...].astype(BF16)
        e2_scr[...] = e2_ref[...].astype(BF16)

    a_scr[...] = jnp.dot(u_ref[...], h2t_ref[...], preferred_element_type=F32)
    nkeys = r2_ref.shape[1]
    for ii in range(ti):
        c1_rows = [c1_ref[h, ii:ii + 1, :].astype(BF16) for h in range(n_heads)]
        e1_rows = [e1_ref[h, ii:ii + 1, :].astype(BF16) for h in range(n_heads)]
        for band in range(nkeys // PEER_BAND):
            ks = slice(band * PEER_BAND, (band + 1) * PEER_BAND)
            rs = slice(ii * nkeys + band * PEER_BAND, ii * nkeys + (band + 1) * PEER_BAND)
            gate = None
            for h in range(n_heads):
                keep = r2_scr[h, ks, :] < c1_rows[h]
                term = jnp.where(keep, e2_scr[h, ks, :], jnp.zeros((), BF16)) * e1_rows[h]
                gate = term if gate is None else gate + term
            a = a_scr[rs, :]
            act = 0.5 * a * (1.0 + lax.erf(a * (2.0 ** -0.5)))
            p_scr[rs, :] = gate * act.astype(BF16)
    acc_scr[...] += jnp.dot(vt_ref[...], p_scr[...], preferred_element_type=F32)

    @pl.when(e == pl.num_programs(1) - 1)
    def _():
        o_ref[...] = _layer_norm(DEEPNORM_ALPHA * x1_ref[...] + g2_ref[0] * acc_scr[...].T,
                                 lg_ref[...], lb_ref[...])


def _peer_dense(h2t, u_tab, vt_tab, rank2, e2, count1, e1, x1, g2, ln_g, ln_b, *, n_heads, ti, tm):
    d, m = h2t.shape
    r = g2.shape[0]
    tm = min(tm, m // r)
    tpm = (m // r) // tm
    ne = u_tab.shape[0]
    nkeys = rank2.shape[1]
    te = ti * nkeys
    kern = functools.partial(_peer_dense_kernel, n_heads=n_heads, ti=ti)
    row = lambda w_: pl.BlockSpec((tm, w_), lambda i, e: (i, 0), pipeline_mode=pl.Buffered(1))
    vec = pl.BlockSpec((1, d), lambda i, e: (0, 0))
    tbl = pl.BlockSpec((te, d), lambda i, e: (e, 0))
    second = pl.BlockSpec((n_heads, nkeys, tm), lambda i, e: (0, 0, i), pipeline_mode=pl.Buffered(1))
    first = pl.BlockSpec((n_heads, ti, tm), lambda i, e: (0, e, i))
    return pl.pallas_call(
        kern,
        grid=(m // tm, ne // te),
        in_specs=[pl.BlockSpec((d, tm), lambda i, e: (0, i), pipeline_mode=pl.Buffered(1)),
                  tbl, pl.BlockSpec((d, te), lambda i, e: (0, e)), second, second, first, first, row(d),
                  pl.BlockSpec((1, 1, d), lambda i, e: (i // tpm, 0, 0)), vec, vec],
        out_specs=pl.BlockSpec((tm, d), lambda i, e: (i, 0)),
        out_shape=jax.ShapeDtypeStruct((m, d), F32),
        scratch_shapes=[pltpu.VMEM((d, tm), F32), pltpu.VMEM((te, tm), F32), pltpu.VMEM((te, tm), BF16),
                        pltpu.VMEM((n_heads, nkeys, tm), BF16), pltpu.VMEM((n_heads, nkeys, tm), BF16)],
        compiler_params=_cparams(("parallel", "arbitrary")),
    )(h2t, u_tab, vt_tab, rank2, e2, count1, e1, x1, g2, ln_g, ln_b)


def _block(x, c, ctx, c_ctx, w_ada, b_ada, w_in, dn_conv_w, dn_a_log, dn_dt_bias, dn_norm_w, hy_conv_w, hy_w1, hy_b1, hy_w2, hy_b2, hy_w3, hy_b3, hy_freq, hy_skip, w_branch_dn, w_branch_hy, w_out, ln1_g, ln1_b, peer_wq, peer_subkeys, peer_u, peer_v, ln2_g, ln2_b):
    b, n, d = x.shape
    n_ctx = ctx.shape[1]
    assert w_ada.shape[0] == DEPTH == 1, "single-layer block: the context stream update is never read"
    hh = dn_a_log.shape[2]
    dk = dn_conv_w.shape[2] // (3 * hh)
    qk_w = hh * dk
    hy_w = hy_skip.shape[2]
    p_heads, _, nkeys, half = peer_subkeys.shape[1:]
    assert dk == LANES and nkeys == LANES and 4 * hh <= LANES and n % GRID_W == 0
    off_z = 3 * qk_w
    off_ab = 4 * qk_w
    off_hy = off_ab + 4 * hh
    off_gate = off_hy + 3 * hy_w
    assert w_in.shape[2] == off_gate + 2 * d

    cc = jnp.zeros((SUBLANES, d), F32).at[:b].set(c).at[b].set(c_ctx)
    mods = _ada(cc, w_ada[0], b_ada[0][None, :])
    sh1, sc1, g1, sh2, sc2, g2 = [mods[:b, i * d:(i + 1) * d][:, None, :] for i in range(6)]
    sh1c, sc1c = [mods[b:b + 1, i * d:(i + 1) * d][:, None, :] for i in range(2)]

    w = w_in[0]
    w_qkvz = w[:, :off_ab].astype(BF16)
    w_ab = jnp.pad(w[:, off_ab:off_hy], ((0, 0), (0, LANES - 4 * hh))).astype(BF16)
    w_hy = w[:, off_hy:off_gate].astype(BF16)
    w_gate = w[:, off_gate:].astype(BF16)

    x2d = x.reshape(b * n, d)
    c2d = ctx.reshape(b * n_ctx, d)
    proj_qkvz = _modmm(x2d, sc1, sh1, w_qkvz, tm=TM_PROJ, tn=TN_PROJ)
    proj_ab = _modmm(x2d, sc1, sh1, w_ab, tm=TM_PROJ, tn=LANES)
    proj_hy = _modmm(x2d, sc1, sh1, w_hy, tm=TM_PROJ, tn=TN_PROJ)
    proj_gate = _modmm(x2d, sc1, sh1, w_gate, tm=TM_PROJ, tn=TN_PROJ)
    projc_qkvz = _modmm(c2d, sc1c, sh1c, w_qkvz, tm=TM_ROW, tn=TN_PROJ)
    projc_ab = _modmm(c2d, sc1c, sh1c, w_ab, tm=TM_ROW, tn=LANES)

    def decay_row(a):
        row = jnp.concatenate([a, jnp.zeros_like(a)], axis=1).reshape(-1)
        return jnp.pad(row, (0, LANES - 4 * hh))[None, :]

    alog_row = decay_row(dn_a_log[0])
    dt_row = decay_row(dn_dt_bias[0])

    conv_w = dn_conv_w[0]
    prep_c = _dn_prep(projc_qkvz.reshape(b, n_ctx, -1), conv_w, projc_ab.reshape(b, n_ctx, LANES),
                      alog_row, dt_row, n_heads=hh, dk=dk, row_w=n_ctx, tt=n_ctx)
    prep_l = _dn_prep(proj_qkvz.reshape(b, n, -1), conv_w, proj_ab.reshape(b, n, LANES),
                      alog_row, dt_row, n_heads=hh, dk=dk, row_w=GRID_W, tt=TT_DELTA)
    hp = 4 if hh % 4 == 0 else (2 if hh % 2 == 0 else 1)
    s_zero = jnp.zeros((2, b, hh, dk, dk), F32)
    _, _, s_ctx = _dn_scan(*prep_c, s_zero, dk=dk, tt=n_ctx, hp=hp)
    o_f, o_b, _ = _dn_scan(*prep_l, s_ctx, dk=dk, tt=TT_DELTA, hp=hp)

    filt = _hy_filters(n, hy_w1[0], hy_b1[0], hy_w2[0], hy_b2[0], hy_w3[0], hy_b3[0], hy_freq[0], hy_w)
    tables = _dft_tables(n)
    hy3 = proj_hy.reshape(b, n, 3 * hy_w)
    cwh = hy_conv_w[0]
    cb = hy_w // LANES
    y1 = _hy_conv(hy3, 0, hy3, cb, filt, 0, 2 * cb, hy_skip[0, 0:1], cwh, cwh, tables, conv_a=True, hy_w=hy_w)
    y_hy = _hy_conv(y1, 0, hy3, 2 * cb, filt, cb, 3 * cb, hy_skip[0, 1:2], cwh, cwh, tables, conv_a=False, hy_w=hy_w)

    merged = _merge(o_f.reshape(b * n, qk_w), o_b.reshape(b * n, qk_w), proj_qkvz, off_z // qk_w,
                    y_hy.reshape(b * n, hy_w), proj_gate, dn_norm_w[0][None, :],
                    w_branch_dn[0].astype(BF16), w_branch_hy[0].astype(BF16), dv=dk, tm=TM_ROW)
    sk = peer_subkeys[0].reshape(p_heads * 2, nkeys, half).astype(BF16)
    x1, h2, scores = _outproj(merged, x2d, w_out[0].astype(BF16), g1, ln1_g[0][None, :], ln1_b[0][None, :],
                              sc2, sh2, peer_wq[0].astype(BF16), sk, tm=TM_ROW)

    count1, rank2, e1, e2 = _peer_topk(scores, p_heads, tm=TM_ROW)
    out = _peer_dense(h2.T, peer_u[0].astype(BF16), peer_v[0].T.astype(BF16), rank2, e2, count1, e1,
                      x1, g2, ln2_g[0][None, :], ln2_b[0][None, :], n_heads=p_heads, ti=TI_PEER, tm=TM_PEER)
    stages = dict(o_f=o_f, o_b=o_b, s_ctx=s_ctx, filt=filt, y1=y1, y_hy=y_hy, merged=merged, x1=x1, scores=scores,
                  count1=count1, rank2=rank2, e1=e1, e2=e2)
    return out.reshape(b, n, d), stages


def kernel(x, c, ctx, c_ctx, w_ada, b_ada, w_in, dn_conv_w, dn_a_log, dn_dt_bias, dn_norm_w, hy_conv_w, hy_w1, hy_b1, hy_w2, hy_b2, hy_w3, hy_b3, hy_freq, hy_skip, w_branch_dn, w_branch_hy, w_out, ln1_g, ln1_b, peer_wq, peer_subkeys, peer_u, peer_v, ln2_g, ln2_b):
    out, _ = _block(x, c, ctx, c_ctx, w_ada, b_ada, w_in, dn_conv_w, dn_a_log, dn_dt_bias, dn_norm_w, hy_conv_w, hy_w1, hy_b1, hy_w2, hy_b2, hy_w3, hy_b3, hy_freq, hy_skip, w_branch_dn, w_branch_hy, w_out, ln1_g, ln1_b, peer_wq, peer_subkeys, peer_u, peer_v, ln2_g, ln2_b)
    return out
```

```python
import functools
import math

import jax
import jax.numpy as jnp
from jax import lax
from jax.experimental import pallas as pl
from jax.experimental.pallas import tpu as pltpu

F32 = jnp.float32
BF16 = jnp.bfloat16

GRID_W = 64
DN_CHUNK = 64
PEER_TOPK = 16
PEER_BAND = 32
DEPTH = 1
DEEPNORM_ALPHA = (2 * DEPTH) ** 0.25
LN_EPS = 1e-5
RMS_EPS = 1e-6
L2_EPS = 1e-6
HY_DECAY_TARGET = 1e-2
HY_FAST_DECAY = 0.3
HY_SLOW_DECAY = 1.5

LANES = 128
SUBLANES = 8
VMEM_LIMIT_BYTES = 56 * 1024 * 1024

TM_PROJ, TN_PROJ = 512, 2048
TT_DELTA = 8 * DN_CHUNK
TM_ROW = 256
TM_PEER, TI_PEER = 512, 8

FFT_N2 = 64
FFT_ROW_PAD = 8
FFT_UNROLL = 32


def _cparams(sem):
    return pltpu.CompilerParams(dimension_semantics=sem, vmem_limit_bytes=VMEM_LIMIT_BYTES)


def _silu(x):
    return x * jax.nn.sigmoid(x)


def _ada_kernel(c_ref, w_ref, b_ref, o_ref):
    s = _silu(c_ref[...])
    o_ref[...] = jnp.dot(s.astype(BF16), w_ref[...].astype(BF16), preferred_element_type=F32) + b_ref[...]


def _ada(cc, w, b):
    d, n6 = w.shape
    tn = min(n6, 1536)
    return pl.pallas_call(
        _ada_kernel,
        grid=(n6 // tn,),
        in_specs=[pl.BlockSpec((SUBLANES, d), lambda j: (0, 0)),
                  pl.BlockSpec((d, tn), lambda j: (0, j)),
                  pl.BlockSpec((1, tn), lambda j: (0, j))],
        out_specs=pl.BlockSpec((SUBLANES, tn), lambda j: (0, j)),
        out_shape=jax.ShapeDtypeStruct((SUBLANES, n6), F32),
        compiler_params=_cparams(("arbitrary",)),
    )(cc, w, b)


def _modmm_kernel(x_ref, sc_ref, sh_ref, w_ref, o_ref, h_scr):
    @pl.when(pl.program_id(1) == 0)
    def _():
        h_scr[...] = (x_ref[...] * (1.0 + sc_ref[0]) + sh_ref[0]).astype(BF16)

    o_ref[...] = jnp.dot(h_scr[...], w_ref[...], preferred_element_type=F32).astype(o_ref.dtype)


def _modmm(x2d, sc, sh, w, *, tm, tn, out_dtype=F32):
    m, d = x2d.shape
    ng = w.shape[1]
    r = sc.shape[0]
    tm = min(tm, m // r)
    tn = min(tn, ng)
    tiles_per_mod = (m // r) // tm
    return pl.pallas_call(
        _modmm_kernel,
        grid=(m // tm, ng // tn),
        in_specs=[pl.BlockSpec((tm, d), lambda i, j: (i, 0)),
                  pl.BlockSpec((1, 1, d), lambda i, j: (i // tiles_per_mod, 0, 0)),
                  pl.BlockSpec((1, 1, d), lambda i, j: (i // tiles_per_mod, 0, 0)),
                  pl.BlockSpec((d, tn), lambda i, j: (0, j))],
        out_specs=pl.BlockSpec((tm, tn), lambda i, j: (i, j)),
        out_shape=jax.ShapeDtypeStruct((m, ng), out_dtype),
        scratch_shapes=[pltpu.VMEM((tm, d), BF16)],
        compiler_params=_cparams(("parallel", "arbitrary")),
    )(x2d, sc, sh, w)


def _row_conv(x, cw, pos, row_w):
    rows = x.shape[0]
    k = cw.shape[0]
    pad = k // 2
    acc = None
    for j in range(k):
        off = j - pad
        if off == 0:
            term = x * cw[j:j + 1]
        else:
            shifted = pltpu.roll(x, (-off) % rows, axis=0)
            valid = (pos >= -off) if off < 0 else (pos < row_w - off)
            term = jnp.where(valid, shifted, 0.0) * cw[j:j + 1]
        acc = term if acc is None else acc + term
    return acc


SOLVE_BLOCK = 16


def _tri_inverse_pairs(pairs):
    c = pairs[0][0].shape[0]
    sb = SOLVE_BLOCK
    nb = c // sb
    ri = lax.broadcasted_iota(jnp.int32, (c, 2 * c), 0)
    ci = lax.broadcasted_iota(jnp.int32, (c, 2 * c), 1)
    ci = jnp.where(ci >= c, ci - c, ci)
    lane_lo = lax.broadcasted_iota(jnp.int32, (1, 2 * c), 1) < c
    off_diag = ri // sb != ci // sb
    eye2 = (ri == ci).astype(F32)
    mcats = [jnp.concatenate([m_a, m_b], axis=1) for m_a, m_b in pairs]
    m_offs = [jnp.where(off_diag, m, 0.0).astype(BF16) for m in mcats]
    xs = [[eye2[i * sb:(i + 1) * sb] for i in range(nb)] for _ in pairs]
    for bi in range(nb):
        rs = slice(bi * sb, (bi + 1) * sb)
        blks = [x[bi] for x in xs]
        if bi > 0:
            for s, x in enumerate(xs):
                xcur = jnp.concatenate(x, axis=0)
                xbd = jnp.concatenate([jnp.where(lane_lo, xcur, 0.0), jnp.where(lane_lo, 0.0, xcur)], axis=0)
                blks[s] = blks[s] - jnp.dot(m_offs[s][rs], xbd.astype(BF16), preferred_element_type=F32)
        mrs = [m[rs] for m in mcats]
        for jj in range(sb - 1):
            col = bi * sb + jj
            src = jnp.broadcast_to(jnp.where(lane_lo, col, c + col), (sb, 2 * c))
            for s in range(len(pairs)):
                mult = jnp.take_along_axis(mrs[s], src, axis=1)
                blks[s] = blks[s] - mult * blks[s][jj:jj + 1, :]
        for s, x in enumerate(xs):
            x[bi] = blks[s]
    return [jnp.concatenate(x, axis=0) for x in xs]


def _dn_gates_kernel(ab_ref, alog_ref, dt_ref, gc_ref, beta_ref, *, n_heads):
    tt = ab_ref.shape[1]
    c = DN_CHUNK
    cpos = lax.broadcasted_iota(jnp.int32, (tt, 1), 0) % c
    ab = ab_ref[0]
    z = ab + dt_ref[...]
    softplus = jnp.maximum(z, 0.0) + jnp.log1p(jnp.exp(-jnp.abs(z)))
    g_all = -jnp.exp(alog_ref[...]) * softplus
    gf = g_all
    gb = g_all
    d = 1
    while d < c:
        gf = gf + jnp.where(cpos >= d, pltpu.roll(gf, d, axis=0), 0.0)
        gb = gb + jnp.where(cpos < c - d, pltpu.roll(gb, (tt - d) % tt, axis=0), 0.0)
        d *= 2
    lane = lax.broadcasted_iota(jnp.int32, (1, LANES), 1)
    gc_ref[0] = jnp.where(lane < 2 * n_heads, gf, gb)
    beta_ref[0] = jax.nn.sigmoid(ab)


def _dn_gates(ab, alog_row, dt_row, *, n_heads, tt):
    b, n, _ = ab.shape
    blk = pl.BlockSpec((1, tt, LANES), lambda bi, g: (bi, g, 0))
    row = pl.BlockSpec((1, LANES), lambda bi, g: (0, 0))
    shp = jax.ShapeDtypeStruct((b, n, LANES), F32)
    return pl.pallas_call(
        functools.partial(_dn_gates_kernel, n_heads=n_heads),
        grid=(b, n // tt),
        in_specs=[blk, row, row],
        out_specs=[blk, blk],
        out_shape=[shp, shp],
        compiler_params=_cparams(("parallel", "parallel")),
    )(ab, alog_row, dt_row)


def _dn_prep_kernel(xq_ref, xk_ref, xv_ref, cq_ref, ck_ref, cv_ref, gc_ref, beta_ref,
                    u_ref, w_ref, qg_ref, kd_ref, attn_ref, eg_ref, *, row_w, n_heads, dk, hp):
    tt = xq_ref.shape[1]
    c = DN_CHUNK
    pos = lax.broadcasted_iota(jnp.int32, (tt, 1), 0) % row_w
    lane = lax.broadcasted_iota(jnp.int32, (1, LANES), 1)
    gc = gc_ref[0]
    beta_all = beta_ref[0]
    eg_ref[...] = jnp.zeros(eg_ref.shape, F32)
    ri = lax.broadcasted_iota(jnp.int32, (c, c), 0)
    ci = lax.broadcasted_iota(jnp.int32, (c, c), 1)
    eye = ri == ci
    lower = ri >= ci
    upper = ri <= ci
    strict_lower = ri > ci
    nt = (((1,), (1,)), ((), ()))

    def col(a, l):
        return jnp.sum(jnp.where(lane == l, a, 0.0), axis=1, keepdims=True)

    def l2n(y):
        return y * lax.rsqrt(jnp.sum(y * y, axis=-1, keepdims=True) + L2_EPS)

    units = []
    for hh in range(hp):
        h = pl.program_id(1) * hp + hh
        hs = slice(hh * dk, (hh + 1) * dk)
        conv_silu = lambda x_ref, cw_ref: _silu(_row_conv(x_ref[0, :, hs], cw_ref[:, hs], pos, row_w))
        q = l2n(conv_silu(xq_ref, cq_ref)) * (dk ** -0.5)
        k = l2n(conv_silu(xk_ref, ck_ref))
        v = conv_silu(xv_ref, cv_ref)
        g_cols = (col(gc, h), col(gc, 2 * n_heads + h))
        b_cols = (col(beta_all, n_heads + h), col(beta_all, 3 * n_heads + h))
        for cc in range(tt // c):
            sl = slice(cc * c, (cc + 1) * c)
            units.append(dict(hh=hh, hs=hs, cc=cc, sl=sl, q=q[sl], k=k[sl], v=v[sl],
                              g=[g_cols[dr][sl] for dr in range(2)], b=[b_cols[dr][sl] for dr in range(2)]))

    for un in units:
        qc, kc, vc, gcols = un["q"], un["k"], un["v"], un["g"]
        grows = [jnp.sum(jnp.where(eye, g, 0.0), axis=0, keepdims=True) for g in gcols]
        kbs = [kc * un["b"][dr] for dr in range(2)]
        vbs = [vc * un["b"][dr] for dr in range(2)]
        egs = [jnp.exp(g) for g in gcols]
        dif_f = gcols[0] - grows[0]
        dif_b = gcols[1] - grows[1]
        un["decay_f"] = jnp.where(lower, jnp.exp(jnp.where(lower, dif_f, 0.0)), 0.0)
        un["decay_b"] = jnp.where(upper, jnp.exp(jnp.where(upper, dif_b, 0.0)), 0.0)
        un["decay_bt"] = jnp.where(lower, jnp.exp(jnp.where(lower, -dif_b, 0.0)), 0.0)
        un["kcb"] = kc.astype(BF16)
        un["lhs"] = jnp.concatenate([kbs[0], qc], axis=0).astype(BF16)
        un["kb_b"] = kbs[1].astype(BF16)
        un["rhs_f"] = jnp.concatenate([vbs[0], kbs[0] * egs[0]], axis=1).astype(BF16)
        un["rhs_b"] = jnp.concatenate([vbs[1], kbs[1] * egs[1]], axis=1).astype(BF16)
        for dr in range(2):
            gcol = gcols[dr]
            glast = gcol[c - 1:c] if dr == 0 else gcol[0:1]
            qg_ref[dr, 0, un["sl"], un["hs"]] = (qc * egs[dr]).astype(BF16)
            kd_ref[dr, 0, un["sl"], un["hs"]] = (kc * jnp.exp(glast - gcol)).astype(BF16)
            eg_ref[dr, 0, un["hh"], 0, un["cc"]:un["cc"] + 1, :] = jnp.broadcast_to(jnp.exp(glast), (1, LANES))

    for un in units:
        un["kq"] = lax.dot_general(un["lhs"], un["kcb"], nt, preferred_element_type=F32)
        un["kkt_b"] = lax.dot_general(un["kcb"], un["kb_b"], nt, preferred_element_type=F32)

    for un in units:
        un["m_f"] = jnp.where(strict_lower, un["kq"][:c] * un["decay_f"], 0.0)
        un["m_bt"] = jnp.where(strict_lower, un["kkt_b"] * un["decay_bt"], 0.0)
        attn_ref[0, 0, un["hh"], un["sl"], :] = (un["kq"][c:] * un["decay_f"]).astype(BF16)
        attn_ref[1, 0, un["hh"], un["sl"], :] = (un["kq"][c:] * un["decay_b"]).astype(BF16)

    tinvs = _tri_inverse_pairs([(un["m_f"], un["m_bt"]) for un in units])

    zero_rows = jnp.zeros((c, 2 * dk), BF16)
    sols = []
    for tinv, un in zip(tinvs, units):
        tinv = tinv.astype(BF16)
        sol_f = jnp.dot(tinv, jnp.concatenate([un["rhs_f"], zero_rows], axis=0), preferred_element_type=F32)
        sol_b = lax.dot_general(tinv, un["rhs_b"], (((0,), (0,)), ((), ())), preferred_element_type=F32)[c:]
        sols.append((sol_f, sol_b))
    for (sol_f, sol_b), un in zip(sols, units):
        for dr, sol in enumerate((sol_f, sol_b)):
            u_ref[dr, 0, un["sl"], un["hs"]] = sol[:, :dk]
            w_ref[dr, 0, un["sl"], un["hs"]] = sol[:, dk:].astype(BF16)


def _dn_prep(qkv, conv_w, ab, alog_row, dt_row, *, n_heads, dk, row_w, tt):
    gc, beta = _dn_gates(ab, alog_row, dt_row, n_heads=n_heads, tt=tt)
    b, n, _ = qkv.shape
    hh = n_heads
    ng = n // tt
    c = DN_CHUNK
    assert tt % row_w == 0 and tt % c == 0 and tt // c <= SUBLANES
    hp = 4 if hh % 4 == 0 else (2 if hh % 2 == 0 else 1)
    nhb = hh // hp
    kern = functools.partial(_dn_prep_kernel, row_w=row_w, n_heads=hh, dk=dk, hp=hp)
    xspec = lambda off: pl.BlockSpec((1, tt, hp * dk), lambda bi, h, g: (bi, g, off + h))
    cspec = lambda off: pl.BlockSpec((conv_w.shape[0], hp * dk), lambda bi, h, g: (0, off + h))
    ospec = pl.BlockSpec((2, 1, tt, hp * dk), lambda bi, h, g: (0, bi, g, h))
    return pl.pallas_call(
        kern,
        grid=(b, nhb, ng),
        in_specs=[xspec(0), xspec(nhb), xspec(2 * nhb), cspec(0), cspec(nhb), cspec(2 * nhb),
                  pl.BlockSpec((1, tt, LANES), lambda bi, h, g: (bi, g, 0)),
                  pl.BlockSpec((1, tt, LANES), lambda bi, h, g: (bi, g, 0))],
        out_specs=[ospec, ospec, ospec, ospec,
                   pl.BlockSpec((2, 1, hp, tt, c), lambda bi, h, g: (0, bi, h, g, 0)),
                   pl.BlockSpec((2, 1, hp, 1, SUBLANES, LANES), lambda bi, h, g: (0, bi, h, g, 0, 0))],
        out_shape=[jax.ShapeDtypeStruct((2, b, n, hh * dk), F32),
                   jax.ShapeDtypeStruct((2, b, n, hh * dk), BF16),
                   jax.ShapeDtypeStruct((2, b, n, hh * dk), BF16),
                   jax.ShapeDtypeStruct((2, b, n, hh * dk), BF16),
                   jax.ShapeDtypeStruct((2, b, hh, n, c), BF16),
                   jax.ShapeDtypeStruct((2, b, hh, ng, SUBLANES, LANES), F32)],
        compiler_params=_cparams(("parallel", "parallel", "arbitrary")),
    )(qkv, qkv, qkv, conv_w, conv_w, conv_w, gc, beta)


def _dn_scan_kernel(uf_ref, wf_ref, qf_ref, kf_ref, af_ref, ef_ref,
                    ub_ref, wb_ref, qb_ref, kb_ref, ab_ref, eb_ref, s0_ref,
                    of_ref, ob_ref, sout_ref, s_scr, *, hp, dk):
    g = pl.program_id(2)
    ng = pl.num_programs(2)
    c = DN_CHUNK
    tt = uf_ref.shape[2]
    ncs = tt // c

    @pl.when(g == 0)
    def _():
        s_scr[...] = s0_ref[:, 0]

    refs = ((uf_ref, wf_ref, qf_ref, kf_ref, af_ref, ef_ref, of_ref),
            (ub_ref, wb_ref, qb_ref, kb_ref, ab_ref, eb_ref, ob_ref))
    chains = [(dr, hh) for dr in range(2) for hh in range(hp)]
    state = {ch: s_scr[ch[0], ch[1]] for ch in chains}
    for step in range(ncs):
        rows = {}
        for dr, hh in chains:
            cc = step if dr == 0 else ncs - 1 - step
            rows[dr, hh] = (slice(cc * c, (cc + 1) * c), slice(hh * dk, (hh + 1) * dk), cc)
        wide = {}
        for dr in range(2):
            sl = rows[dr, 0][0]
            wide[dr] = [refs[dr][i][0, 0, sl, :] for i in range(4)]
        r = {}
        for ch in chains:
            _, hs, _ = rows[ch]
            wq = jnp.concatenate([wide[ch[0]][1][:, hs], wide[ch[0]][2][:, hs]], axis=0)
            r[ch] = jnp.dot(wq, state[ch].astype(BF16), preferred_element_type=F32)
        vnb = {}
        for ch in chains:
            _, hs, _ = rows[ch]
            vnb[ch] = (wide[ch[0]][0][:, hs] - r[ch][:c]).astype(BF16)
        outs = {}
        for ch in chains:
            a_ref, e_ref = refs[ch[0]][4], refs[ch[0]][5]
            sl, hs, cc = rows[ch]
            outs[ch] = r[ch][c:] + jnp.dot(a_ref[0, 0, ch[1], sl, :], vnb[ch], preferred_element_type=F32)
            upd = lax.dot_general(wide[ch[0]][3][:, hs], vnb[ch], (((0,), (0,)), ((), ())),
                                  preferred_element_type=F32)
            state[ch] = state[ch] * e_ref[0, 0, ch[1], 0, cc:cc + 1, :] + upd
        for dr in range(2):
            refs[dr][6][0, rows[dr, 0][0], :] = jnp.concatenate([outs[dr, hh] for hh in range(hp)], axis=1)
    for ch in chains:
        s_scr[ch[0], ch[1]] = state[ch]

    @pl.when(g == ng - 1)
    def _():
        sout_ref[:, 0] = s_scr[...]


def _dn_scan(u, w, qg, kd, attn, eg, s0, *, dk, tt, hp):
    _, b, n, hd = u.shape
    hh = hd // dk
    ng = n // tt
    c = DN_CHUNK
    kern = functools.partial(_dn_scan_kernel, hp=hp, dk=dk)

    def specs(dr):
        gi = (lambda g: g) if dr == 0 else (lambda g: ng - 1 - g)
        tok = pl.BlockSpec((1, 1, tt, hp * dk), lambda bi, h, g: (dr, bi, gi(g), h))
        return [tok, tok, tok, tok,
                pl.BlockSpec((1, 1, hp, tt, c), lambda bi, h, g: (dr, bi, h, gi(g), 0)),
                pl.BlockSpec((1, 1, hp, 1, SUBLANES, LANES), lambda bi, h, g: (dr, bi, h, gi(g), 0, 0))]

    sspec = pl.BlockSpec((2, 1, hp, dk, dk), lambda bi, h, g: (0, bi, h, 0, 0))
    return pl.pallas_call(
        kern,
        grid=(b, hh // hp, ng),
        in_specs=specs(0) + specs(1) + [sspec],
        out_specs=[pl.BlockSpec((1, tt, hp * dk), lambda bi, h, g: (bi, g, h)),
                   pl.BlockSpec((1, tt, hp * dk), lambda bi, h, g: (bi, ng - 1 - g, h)),
                   sspec],
        out_shape=[jax.ShapeDtypeStruct((b, n, hd), F32),
                   jax.ShapeDtypeStruct((b, n, hd), F32),
                   jax.ShapeDtypeStruct((2, b, hh, dk, dk), F32)],
        scratch_shapes=[pltpu.VMEM((2, hp, dk, dk), F32)],
        compiler_params=_cparams(("parallel", "parallel", "arbitrary")),
    )(u, w, qg, kd, attn, eg, u, w, qg, kd, attn, eg, s0)


def _hy_hidden_kernel(z_ref, w1_ref, b1_ref, w2_ref, b2_ref, fr_ref, o_ref):
    fr = fr_ref[...]
    h1 = jnp.sin(fr * (jnp.dot(z_ref[...].astype(BF16), w1_ref[...].astype(BF16),
                               preferred_element_type=F32) + b1_ref[...]))
    o_ref[...] = jnp.sin(fr * (jnp.dot(h1.astype(BF16), w2_ref[...].astype(BF16),
                                       preferred_element_type=F32) + b2_ref[...]))


def _hy_filter_kernel(hid_ref, w3_ref, b3_ref, t_ref, delta_ref, o_ref):
    f = jnp.dot(hid_ref[...].astype(BF16), w3_ref[...].astype(BF16), preferred_element_type=F32) + b3_ref[...]
    o_ref[...] = f * jnp.exp(-t_ref[...] * delta_ref[...])


def _hy_filters(n, w1, b1, w2, b2, w3, b3, freq, hy_w):
    emb, hid = w1.shape
    bands = (emb - 1) // 2
    t = jnp.linspace(0.0, 1.0, n, dtype=F32)
    pos = jnp.arange(n, dtype=F32)
    bnd = jnp.linspace(1e-4, bands - 1, bands, dtype=F32)
    ang = (2.0 * math.pi / n) * pos[:, None] * bnd[None, :]
    z = jnp.concatenate([t[:, None], jnp.cos(ang), -jnp.sin(ang)], axis=-1)
    embp = -(-emb // SUBLANES) * SUBLANES
    z = jnp.pad(z, ((0, 0), (0, embp - emb)))
    w1p = jnp.pad(w1, ((0, embp - emb), (0, 0)))
    full = lambda a: pl.BlockSpec(a.shape, lambda: (0,) * a.ndim)
    args = (z, w1p, b1[None, :], w2, b2[None, :], freq[None, :])
    hid2 = pl.pallas_call(
        _hy_hidden_kernel,
        in_specs=[full(a) for a in args],
        out_specs=pl.BlockSpec((n, hid), lambda: (0, 0)),
        out_shape=jax.ShapeDtypeStruct((n, hid), F32),
        compiler_params=pltpu.CompilerParams(vmem_limit_bytes=VMEM_LIMIT_BYTES),
    )(*args)
    nout = w3.shape[1]
    deltas = jnp.abs(jnp.linspace(math.log(HY_DECAY_TARGET) / HY_SLOW_DECAY,
                                  math.log(HY_DECAY_TARGET) / HY_FAST_DECAY, hy_w, dtype=F32))
    delta_row = jnp.tile(deltas, nout // hy_w)[None, :]
    tc = min(nout, 1024)
    return pl.pallas_call(
        _hy_filter_kernel,
        grid=(nout // tc,),
        in_specs=[pl.BlockSpec((n, hid), lambda j: (0, 0)),
                  pl.BlockSpec((hid, tc), lambda j: (0, j)),
                  pl.BlockSpec((1, tc), lambda j: (0, j)),
                  pl.BlockSpec((n, 1), lambda j: (0, 0)),
                  pl.BlockSpec((1, tc), lambda j: (0, j))],
        out_specs=pl.BlockSpec((n, tc), lambda j: (0, j)),
        out_shape=jax.ShapeDtypeStruct((n, nout), F32),
        compiler_params=_cparams(("parallel",)),
    )(hid2, w3, b3[None, :], t[:, None], delta_row)


def _dft_tables(n):
    n2 = FFT_N2
    big = 2 * n
    n1 = big // n2
    k1 = n1 // 2
    two_pi = 2.0 * math.pi

    def cs(idx, period):
        a = (idx % period).astype(F32) * (two_pi / period)
        return jnp.cos(a), jnp.sin(a)

    f1 = jnp.arange(n1, dtype=jnp.int32)
    s1 = jnp.arange(k1, dtype=jnp.int32)
    s2 = jnp.arange(n2, dtype=jnp.int32)
    s_full = s1[None, None, :] * n2 + s2[:, None, None]
    c, s = cs(f1[None, :, None] * s_full, big)
    f1tw = jnp.concatenate([jnp.concatenate([c, s], axis=2), jnp.concatenate([-s, c], axis=2)], axis=1)
    ct, st = jnp.swapaxes(c, 1, 2), jnp.swapaxes(s, 1, 2)
    g1tw = jnp.concatenate([jnp.concatenate([ct, -st], axis=2), jnp.concatenate([st, ct], axis=2)], axis=1) / big
    c2, sn2 = cs(s2[:, None] * s2[None, :], n2)
    f2 = jnp.concatenate([jnp.concatenate([c2, sn2], axis=1), jnp.concatenate([-sn2, c2], axis=1)], axis=0)
    g2 = jnp.concatenate([jnp.concatenate([c2, -sn2], axis=1), jnp.concatenate([sn2, c2], axis=1)], axis=0)
    return f1tw.astype(BF16), f2.astype(BF16), g2.astype(BF16), g1tw.astype(BF16)


def _hy_conv_kernel(a_ref, m_ref, hf_ref, hb_ref, skip_ref, cwa_ref, cwm_ref,
                    f1_ref, f2_ref, g2_ref, g1_ref, o_ref,
                    h_scr, b_scr, zr_scr, zi_scr, *, conv_a, n1, k1):
    n2 = FFT_N2
    k1p = k1 + FFT_ROW_PAD
    pb = 2 * n2 + FFT_ROW_PAD
    pos = lax.broadcasted_iota(jnp.int32, (n2, 1), 0)

    grp = min(FFT_UNROLL, n1 // 2)
    cw = b_scr.shape[1]

    def stage1(real_only):
        def body(g, carry):
            s2s = [g * grp + t for t in range(grp)]
            bases = [pl.multiple_of(s2 * k1p, SUBLANES) for s2 in s2s]
            if real_only:
                xs = [zr_scr[pl.ds(base, k1), :].astype(BF16) for base in bases]
                outs = [jnp.dot(f1_ref[s2, :, :k1], x, preferred_element_type=F32) for s2, x in zip(s2s, xs)]
            else:
                xs = [jnp.concatenate([zr_scr[pl.ds(base, k1), :], zi_scr[pl.ds(base, k1), :]], axis=0).astype(BF16)
                      for base in bases]
                outs = [jnp.dot(f1_ref[s2], x, preferred_element_type=F32) for s2, x in zip(s2s, xs)]
            for s2, a in zip(s2s, outs):
                b_scr[pl.ds(s2, n1, stride=pb), :] = a[:n1]
                b_scr[pl.ds(n2 + s2, n1, stride=pb), :] = a[n1:]
            return carry
        lax.fori_loop(0, n2 // grp, body, 0)

    def slab_pairs(g):
        f1s = [[(g * grp + t) * 2 + u for u in range(2)] for t in range(grp)]
        bbs = [[pl.multiple_of(f1 * pb, SUBLANES) for f1 in pair] for pair in f1s]
        hbs = [[pl.multiple_of(f1 * 2 * n2, 2 * n2) for f1 in pair] for pair in f1s]
        return bbs, hbs

    def forward2(bbs):
        bpairs = [jnp.concatenate([b_scr[pl.ds(bb, 2 * n2), :] for bb in pair], axis=1).astype(BF16) for pair in bbs]
        return [jnp.dot(f2_ref[...], bp, preferred_element_type=F32) for bp in bpairs]

    @pl.when(pl.program_id(1) == 0)
    def _():
        for which, ref in ((0, hf_ref), (1, hb_ref)):
            def load(s1, carry, ref=ref):
                r0 = pl.multiple_of(s1 * n2, n2)
                zr_scr[pl.ds(s1, n2, stride=k1p), :] = ref[pl.ds(r0, n2), :]
                return carry
            lax.fori_loop(0, k1, load, 0, unroll=FFT_UNROLL)
            if which == 1:
                zr_scr[0:1, :] = jnp.zeros((1, zr_scr.shape[1]), F32)
            stage1(True)

            def spec(g, carry, which=which):
                bbs, hbs = slab_pairs(g)
                zpairs = forward2(bbs)
                for zpair, hpair in zip(zpairs, hbs):
                    for t, hb_ in enumerate(hpair):
                        zz = zpair[:, t * cw:(t + 1) * cw]
                        if which == 0:
                            h_scr[pl.ds(hb_, 2 * n2), :] = zz
                        else:
                            h_scr[pl.ds(hb_, n2), :] = h_scr[pl.ds(hb_, n2), :] + zz[:n2]
                            h_scr[pl.ds(hb_ + n2, n2), :] = h_scr[pl.ds(hb_ + n2, n2), :] - zz[n2:]
                return carry
            lax.fori_loop(0, n1 // (2 * grp), spec, 0)

    def load_a(s1, carry):
        r0 = pl.multiple_of(s1 * n2, n2)
        for bi, scr in ((0, zr_scr), (1, zi_scr)):
            blk = a_ref[bi, pl.ds(r0, n2), :]
            if conv_a:
                blk = _row_conv(blk, cwa_ref[...], pos, n2)
            scr[pl.ds(s1, n2, stride=k1p), :] = blk
        return carry
    lax.fori_loop(0, k1, load_a, 0, unroll=FFT_UNROLL)

    stage1(False)

    def mid(g, carry):
        bbs, hbs = slab_pairs(g)
        zzs = forward2(bbs)
        yys = []
        for zz, hpair in zip(zzs, hbs):
            zr, zi = zz[:n2], zz[n2:]
            hr = jnp.concatenate([h_scr[pl.ds(hb_, n2), :] for hb_ in hpair], axis=1)
            hi = jnp.concatenate([h_scr[pl.ds(hb_ + n2, n2), :] for hb_ in hpair], axis=1)
            yys.append(jnp.concatenate([zr * hr - zi * hi, zr * hi + zi * hr], axis=0).astype(BF16))
        wws = [jnp.dot(g2_ref[...], yy, preferred_element_type=F32) for yy in yys]
        for ww, pair in zip(wws, bbs):
            for t, bb in enumerate(pair):
                b_scr[pl.ds(bb, 2 * n2), :] = ww[:, t * cw:(t + 1) * cw]
        return carry
    lax.fori_loop(0, n1 // (2 * grp), mid, 0)

    skip = skip_ref[...]

    def last(g, carry):
        s2s = [g * grp + t for t in range(grp)]
        wws = [jnp.concatenate([b_scr[pl.ds(s2, n1, stride=pb), :],
                                b_scr[pl.ds(n2 + s2, n1, stride=pb), :]], axis=0).astype(BF16) for s2 in s2s]
        ys = [jnp.dot(g1_ref[s2], ww, preferred_element_type=F32) for s2, ww in zip(s2s, wws)]
        for s2, y in zip(s2s, ys):
            base = pl.multiple_of(s2 * k1p, SUBLANES)
            zr_scr[pl.ds(base, k1), :] = y[:k1] + zr_scr[pl.ds(base, k1), :] * skip
            zi_scr[pl.ds(base, k1), :] = y[k1:] + zi_scr[pl.ds(base, k1), :] * skip
        return carry
    lax.fori_loop(0, n2 // grp, last, 0)

    def fin(s1, carry):
        r0 = pl.multiple_of(s1 * n2, n2)
        for bi, scr in ((0, zr_scr), (1, zi_scr)):
            mm = _row_conv(m_ref[bi, pl.ds(r0, n2), :], cwm_ref[...], pos, n2)
            o_ref[bi, pl.ds(r0, n2), :] = mm * scr[pl.ds(s1, n2, stride=k1p), :]
        return carry
    lax.fori_loop(0, k1, fin, 0, unroll=FFT_UNROLL)


def _hy_conv(a, a_col0, m, m_col0, filt, f_col_fwd, f_col_bwd, skip, cwa, cwm, tables, *, conv_a, hy_w):
    b, n, _ = a.shape
    assert b % 2 == 0 and GRID_W == FFT_N2 and n % FFT_N2 == 0
    n2 = FFT_N2
    n1 = 2 * n // n2
    k1 = n1 // 2
    k1p = k1 + FFT_ROW_PAD
    pb = 2 * n2 + FFT_ROW_PAD
    cw = LANES
    f1tw, f2, g2, g1tw = tables
    kern = functools.partial(_hy_conv_kernel, conv_a=conv_a, n1=n1, k1=k1)
    once = pl.Buffered(1)
    return pl.pallas_call(
        kern,
        grid=(hy_w // cw, b // 2),
        in_specs=[pl.BlockSpec((2, n, cw), lambda c, p: (p, 0, a_col0 + c), pipeline_mode=once),
                  pl.BlockSpec((2, n, cw), lambda c, p: (p, 0, m_col0 + c), pipeline_mode=once),
                  pl.BlockSpec((n, cw), lambda c, p: (0, f_col_fwd + c), pipeline_mode=once),
                  pl.BlockSpec((n, cw), lambda c, p: (0, f_col_bwd + c), pipeline_mode=once),
                  pl.BlockSpec((1, cw), lambda c, p: (0, c)),
                  pl.BlockSpec((cwa.shape[0], cw), lambda c, p: (0, a_col0 + c if conv_a else 0)),
                  pl.BlockSpec((cwm.shape[0], cw), lambda c, p: (0, m_col0 + c)),
                  pl.BlockSpec(f1tw.shape, lambda c, p: (0, 0, 0), pipeline_mode=once),
                  pl.BlockSpec(f2.shape, lambda c, p: (0, 0)),
                  pl.BlockSpec(g2.shape, lambda c, p: (0, 0)),
                  pl.BlockSpec(g1tw.shape, lambda c, p: (0, 0, 0), pipeline_mode=once)],
        out_specs=pl.BlockSpec((2, n, cw), lambda c, p: (p, 0, c)),
        out_shape=jax.ShapeDtypeStruct((b, n, hy_w), F32),
        scratch_shapes=[pltpu.VMEM((n1 * 2 * n2, cw), F32),
                        pltpu.VMEM((n1 * pb, cw), F32),
                        pltpu.VMEM((n2 * k1p, cw), F32),
                        pltpu.VMEM((n2 * k1p, cw), F32)],
        compiler_params=_cparams(("parallel", "arbitrary")),
    )(a, m, filt, filt, skip, cwa, cwm, f1tw, f2, g2, g1tw)


def _merge_kernel(of_ref, ob_ref, z_ref, y_ref, ga_ref, gb_ref, nw_ref, wa_ref, wh_ref, o_ref, *, dv):
    nw = nw_ref[...]
    o_all = of_ref[...] + ob_ref[...]
    gate_all = _silu(z_ref[...])
    heads = []
    for hh in range(of_ref.shape[1] // dv):
        hs = slice(hh * dv, (hh + 1) * dv)
        o = o_all[:, hs]
        o = o * lax.rsqrt(jnp.mean(o * o, axis=-1, keepdims=True) + RMS_EPS) * nw
        heads.append((o * gate_all[:, hs]).astype(BF16))
    pa = jnp.dot(jnp.concatenate(heads, axis=1), wa_ref[...], preferred_element_type=F32)
    ph = jnp.dot(y_ref[...].astype(BF16), wh_ref[...], preferred_element_type=F32)
    o_ref[...] = (jax.nn.sigmoid(ga_ref[...]) * pa + jax.nn.sigmoid(gb_ref[...]) * ph).astype(o_ref.dtype)


def _merge(o_f, o_b, z, z_blk, y_hy, gate, norm_w, w_pa, w_ph, *, dv, tm):
    m, dvw = o_f.shape
    d = w_pa.shape[1]
    tm = min(tm, m)
    kern = functools.partial(_merge_kernel, dv=dv)
    row = lambda w_, blk=0: pl.BlockSpec((tm, w_), lambda i: (i, blk))
    const = lambda a: pl.BlockSpec(a.shape, lambda i: (0, 0), pipeline_mode=pl.Buffered(1))
    return pl.pallas_call(
        kern,
        grid=(m // tm,),
        in_specs=[row(dvw), row(dvw), row(dvw, z_blk), row(y_hy.shape[1]), row(d, 0), row(d, 1),
                  pl.BlockSpec((1, dv), lambda i: (0, 0)), const(w_pa), const(w_ph)],
        out_specs=row(d),
        out_shape=jax.ShapeDtypeStruct((m, d), BF16),
        compiler_params=_cparams(("parallel",)),
    )(o_f, o_b, z, y_hy, gate, gate, norm_w, w_pa, w_ph)


def _layer_norm(x, g, b):
    mu = jnp.mean(x, axis=-1, keepdims=True)
    xc = x - mu
    var = jnp.mean(xc * xc, axis=-1, keepdims=True)
    return xc * lax.rsqrt(var + LN_EPS) * g + b


def _outproj_kernel(mg_ref, x_ref, wo_ref, g1_ref, lg_ref, lb_ref, sc_ref, sh_ref, wq_ref, sk_ref,
                    x1_ref, h2t_ref, s_ref, *, half):
    mix = jnp.dot(mg_ref[...], wo_ref[...], preferred_element_type=F32)
    x1 = _layer_norm(DEEPNORM_ALPHA * x_ref[...] + g1_ref[0] * mix, lg_ref[...], lb_ref[...])
    x1_ref[...] = x1
    h2f = x1 * (1.0 + sc_ref[0]) + sh_ref[0]
    h2 = h2f.astype(BF16)
    h2t_ref[...] = h2f.T.astype(BF16)
    qp = jnp.dot(h2, wq_ref[...], preferred_element_type=F32).astype(BF16)
    for j in range(sk_ref.shape[0]):
        js = slice(j * half, (j + 1) * half)
        s_ref[j] = lax.dot_general(sk_ref[j], qp[:, js], (((1,), (1,)), ((), ())), preferred_element_type=F32)


def _outproj(merged, x2d, w_out, g1, ln_g, ln_b, sc2, sh2, wq, subkeys, *, tm):
    m, d = x2d.shape
    r = g1.shape[0]
    tm = min(tm, m // r)
    tpm = (m // r) // tm
    nsk, nkeys, half = subkeys.shape
    assert nkeys == LANES
    kern = functools.partial(_outproj_kernel, half=half)
    row = lambda w_: pl.BlockSpec((tm, w_), lambda i: (i, 0))
    mod = pl.BlockSpec((1, 1, d), lambda i: (i // tpm, 0, 0))
    vec = pl.BlockSpec((1, d), lambda i: (0, 0))
    return pl.pallas_call(
        kern,
        grid=(m // tm,),
        in_specs=[row(d), row(d), pl.BlockSpec(w_out.shape, lambda i: (0, 0), pipeline_mode=pl.Buffered(1)),
                  mod, vec, vec, mod, mod,
                  pl.BlockSpec(wq.shape, lambda i: (0, 0), pipeline_mode=pl.Buffered(1)),
                  pl.BlockSpec(subkeys.shape, lambda i: (0, 0, 0))],
        out_specs=[row(d), pl.BlockSpec((d, tm), lambda i: (0, i)), pl.BlockSpec((nsk, nkeys, tm), lambda i: (0, 0, i))],
        out_shape=[jax.ShapeDtypeStruct((m, d), F32),
                   jax.ShapeDtypeStruct((d, m), BF16),
                   jax.ShapeDtypeStruct((nsk, nkeys, m), F32)],
        compiler_params=_cparams(("parallel",)),
    )(merged, x2d, w_out, g1, ln_g, ln_b, sc2, sh2, wq, subkeys)


def _cand_tables():
    pairs = [(a, b) for a in range(PEER_TOPK) for b in range(PEER_TOPK) if (a + 1) * (b + 1) <= PEER_TOPK]
    ia = jnp.array([p[0] for p in pairs], dtype=jnp.int32)
    ib = jnp.array([p[1] for p in pairs], dtype=jnp.int32)
    ncand = len(pairs)
    rows = -(-ncand // (2 * SUBLANES)) * (2 * SUBLANES)
    ranks = jnp.arange(LANES, dtype=jnp.int32)
    p1 = (jnp.pad(ia, (0, rows - ncand), constant_values=-1)[:, None] == ranks[None, :]).astype(F32)
    p2 = (jnp.pad(ib, (0, rows - ncand), constant_values=-1)[:, None] == ranks[None, :]).astype(F32)
    return p1, p2, ncand


def _peer_topk_kernel(s_ref, p1_ref, p2_ref, p1t_ref, c1_ref, r2_ref, e1_ref, e2_ref, *, n_heads, ncand):
    tm = s_ref.shape[2]
    neg = -jnp.inf
    hi = lax.Precision.HIGHEST
    no_rank = float(LANES)

    def pop_max(x, row_id):
        mx = jnp.max(x, axis=0, keepdims=True)
        first = jnp.min(jnp.where(x == mx, row_id, no_rank), axis=0, keepdims=True)
        hit = row_id == first
        return mx, hit, jnp.where(hit, neg, x)

    key_id = lax.broadcasted_iota(jnp.int32, (LANES, tm), 0).astype(F32)
    cand_id = lax.broadcasted_iota(jnp.int32, (p1_ref.shape[0], tm), 0).astype(F32)
    pad_rows = jnp.zeros((LANES - PEER_TOPK, tm), F32)

    def sorted_top(s):
        tops = []
        rank = jnp.full(s.shape, no_rank, F32)
        x = s
        for a in range(PEER_TOPK):
            mx, hit, x = pop_max(x, key_id)
            tops.append(mx)
            rank = jnp.where(hit, float(a), rank)
        return jnp.concatenate(tops + [pad_rows], axis=0), rank

    for h in range(n_heads):
        s1 = s_ref[2 * h]
        s2 = s_ref[2 * h + 1]
        t1, rank1 = sorted_top(s1)
        t2, rank2 = sorted_top(s2)
        cand = (jnp.dot(p1_ref[...], t1, precision=hi, preferred_element_type=F32)
                + jnp.dot(p2_ref[...], t2, precision=hi, preferred_element_type=F32))
        x = jnp.where(cand_id < ncand, cand, neg)
        sel = jnp.zeros(x.shape, F32)
        zsum = None
        cmax = None
        for r in range(PEER_TOPK):
            mx, hit, x = pop_max(x, cand_id)
            sel = jnp.where(hit, 1.0, sel)
            if r == 0:
                cmax = mx
                zsum = jnp.ones_like(mx)
            else:
                zsum = zsum + jnp.exp(mx - cmax)
        cnt = jnp.dot(p1t_ref[...].astype(BF16), sel.astype(BF16), preferred_element_type=F32)
        count1 = jnp.zeros(s1.shape, F32)
        for a in range(PEER_TOPK):
            count1 = jnp.where(rank1 == a, cnt[a:a + 1], count1)
        c1_ref[h] = count1
        r2_ref[h] = rank2.astype(r2_ref.dtype)
        e1_ref[h] = jnp.exp(s1 - t1[0:1]) / zsum
        e2_ref[h] = jnp.exp(s2 - t2[0:1]).astype(e2_ref.dtype)


def _peer_topk(scores, n_heads, *, tm):
    nsk, nkeys, m = scores.shape
    tm = min(tm, m)
    p1, p2, ncand = _cand_tables()
    kern = functools.partial(_peer_topk_kernel, n_heads=n_heads, ncand=ncand)
    full = lambda a: pl.BlockSpec(a.shape, lambda i: (0, 0))
    out = pl.BlockSpec((n_heads, nkeys, tm), lambda i: (0, 0, i))
    shp = lambda dt: jax.ShapeDtypeStruct((n_heads, nkeys, m), dt)
    p1t = p1.T
    return pl.pallas_call(
        kern,
        grid=(m // tm,),
        in_specs=[pl.BlockSpec((nsk, nkeys, tm), lambda i: (0, 0, i)), full(p1), full(p2), full(p1t)],
        out_specs=[out, out, out, out],
        out_shape=[shp(F32), shp(F32), shp(F32), shp(F32)],
        compiler_params=_cparams(("parallel",)),
    )(scores, p1, p2, p1t)


def _peer_dense_kernel(h2t_ref, u_ref, vt_ref, r2_ref, e2_ref, c1_ref, e1_ref, x1_ref, g2_ref, lg_ref, lb_ref,
                       o_ref, acc_scr, a_scr, p_scr, r2_scr, e2_scr, *, n_heads, ti):
    e = pl.program_id(1)

    @pl.when(e == 0)
    def _():
        acc_scr[...] = jnp.zeros(acc_scr.shape, F32)
        r2_scr[...] = r2_ref[...].astype(BF16)
        e2_scr[...] = e2_ref[...].astype(BF16)

    a_scr[...] = jnp.dot(u_ref[...], h2t_ref[...], preferred_element_type=F32)
    nkeys = r2_ref.shape[1]
    for ii in range(ti):
        c1_rows = [c1_ref[h, ii:ii + 1, :].astype(BF16) for h in range(n_heads)]
        e1_rows = [e1_ref[h, ii:ii + 1, :].astype(BF16) for h in range(n_heads)]
        for band in range(nkeys // PEER_BAND):
            ks = slice(band * PEER_BAND, (band + 1) * PEER_BAND)
            rs = slice(ii * nkeys + band * PEER_BAND, ii * nkeys + (band + 1) * PEER_BAND)
            gate = None
            for h in range(n_heads):
                keep = r2_scr[h, ks, :] < c1_rows[h]
                term = jnp.where(keep, e2_scr[h, ks, :], jnp.zeros((), BF16)) * e1_rows[h]
                gate = term if gate is None else gate + term
            a = a_scr[rs, :]
            act = 0.5 * a * (1.0 + lax.erf(a * (2.0 ** -0.5)))
            p_scr[rs, :] = gate * act.astype(BF16)
    acc_scr[...] += jnp.dot(vt_ref[...], p_scr[...], preferred_element_type=F32)

    @pl.when(e == pl.num_programs(1) - 1)
    def _():
        o_ref[...] = _layer_norm(DEEPNORM_ALPHA * x1_ref[...] + g2_ref[0] * acc_scr[...].T,
                                 lg_ref[...], lb_ref[...])


def _peer_dense(h2t, u_tab, vt_tab, rank2, e2, count1, e1, x1, g2, ln_g, ln_b, *, n_heads, ti, tm):
    d, m = h2t.shape
    r = g2.shape[0]
    tm = min(tm, m // r)
    tpm = (m // r) // tm
    ne = u_tab.shape[0]
    nkeys = rank2.shape[1]
    te = ti * nkeys
    kern = functools.partial(_peer_dense_kernel, n_heads=n_heads, ti=ti)
    row = lambda w_: pl.BlockSpec((tm, w_), lambda i, e: (i, 0), pipeline_mode=pl.Buffered(1))
    vec = pl.BlockSpec((1, d), lambda i, e: (0, 0))
    tbl = pl.BlockSpec((te, d), lambda i, e: (e, 0))
    second = pl.BlockSpec((n_heads, nkeys, tm), lambda i, e: (0, 0, i), pipeline_mode=pl.Buffered(1))
    first = pl.BlockSpec((n_heads, ti, tm), lambda i, e: (0, e, i))
    return pl.pallas_call(
        kern,
        grid=(m // tm, ne // te),
        in_specs=[pl.BlockSpec((d, tm), lambda i, e: (0, i), pipeline_mode=pl.Buffered(1)),
                  tbl, pl.BlockSpec((d, te), lambda i, e: (0, e)), second, second, first, first, row(d),
                  pl.BlockSpec((1, 1, d), lambda i, e: (i // tpm, 0, 0)), vec, vec],
        out_specs=pl.BlockSpec((tm, d), lambda i, e: (i, 0)),
        out_shape=jax.ShapeDtypeStruct((m, d), F32),
        scratch_shapes=[pltpu.VMEM((d, tm), F32), pltpu.VMEM((te, tm), F32), pltpu.VMEM((te, tm), BF16),
                        pltpu.VMEM((n_heads, nkeys, tm), BF16), pltpu.VMEM((n_heads, nkeys, tm), BF16)],
        compiler_params=_cparams(("parallel", "arbitrary")),
    )(h2t, u_tab, vt_tab, rank2, e2, count1, e1, x1, g2, ln_g, ln_b)


def _block(x, c, ctx, c_ctx, w_ada, b_ada, w_in, dn_conv_w, dn_a_log, dn_dt_bias, dn_norm_w, hy_conv_w, hy_w1, hy_b1, hy_w2, hy_b2, hy_w3, hy_b3, hy_freq, hy_skip, w_branch_dn, w_branch_hy, w_out, ln1_g, ln1_b, peer_wq, peer_subkeys, peer_u, peer_v, ln2_g, ln2_b):
    b, n, d = x.shape
    n_ctx = ctx.shape[1]
    assert w_ada.shape[0] == DEPTH == 1, "single-layer block: the context stream update is never read"
    hh = dn_a_log.shape[2]
    dk = dn_conv_w.shape[2] // (3 * hh)
    qk_w = hh * dk
    hy_w = hy_skip.shape[2]
    p_heads, _, nkeys, half = peer_subkeys.shape[1:]
    assert dk == LANES and nkeys == LANES and 4 * hh <= LANES and n % GRID_W == 0
    off_z = 3 * qk_w
    off_ab = 4 * qk_w
    off_hy = off_ab + 4 * hh
    off_gate = off_hy + 3 * hy_w
    assert w_in.shape[2] == off_gate + 2 * d

    cc = jnp.zeros((SUBLANES, d), F32).at[:b].set(c).at[b].set(c_ctx)
    mods = _ada(cc, w_ada[0], b_ada[0][None, :])
    sh1, sc1, g1, sh2, sc2, g2 = [mods[:b, i * d:(i + 1) * d][:, None, :] for i in range(6)]
    sh1c, sc1c = [mods[b:b + 1, i * d:(i + 1) * d][:, None, :] for i in range(2)]

    w = w_in[0]
    w_qkvz = w[:, :off_ab].astype(BF16)
    w_ab = jnp.pad(w[:, off_ab:off_hy], ((0, 0), (0, LANES - 4 * hh))).astype(BF16)
    w_hy = w[:, off_hy:off_gate].astype(BF16)
    w_gate = w[:, off_gate:].astype(BF16)

    x2d = x.reshape(b * n, d)
    c2d = ctx.reshape(b * n_ctx, d)
    proj_qkvz = _modmm(x2d, sc1, sh1, w_qkvz, tm=TM_PROJ, tn=TN_PROJ)
    proj_ab = _modmm(x2d, sc1, sh1, w_ab, tm=TM_PROJ, tn=LANES)
    proj_hy = _modmm(x2d, sc1, sh1, w_hy, tm=TM_PROJ, tn=TN_PROJ)
    proj_gate = _modmm(x2d, sc1, sh1, w_gate, tm=TM_PROJ, tn=TN_PROJ)
    projc_qkvz = _modmm(c2d, sc1c, sh1c, w_qkvz, tm=TM_ROW, tn=TN_PROJ)
    projc_ab = _modmm(c2d, sc1c, sh1c, w_ab, tm=TM_ROW, tn=LANES)

    def decay_row(a):
        row = jnp.concatenate([a, jnp.zeros_like(a)], axis=1).reshape(-1)
        return jnp.pad(row, (0, LANES - 4 * hh))[None, :]

    alog_row = decay_row(dn_a_log[0])
    dt_row = decay_row(dn_dt_bias[0])

    conv_w = dn_conv_w[0]
    prep_c = _dn_prep(projc_qkvz.reshape(b, n_ctx, -1), conv_w, projc_ab.reshape(b, n_ctx, LANES),
                      alog_row, dt_row, n_heads=hh, dk=dk, row_w=n_ctx, tt=n_ctx)
    prep_l = _dn_prep(proj_qkvz.reshape(b, n, -1), conv_w, proj_ab.reshape(b, n, LANES),
                      alog_row, dt_row, n_heads=hh, dk=dk, row_w=GRID_W, tt=min(TT_DELTA, n))
    hp = 4 if hh % 4 == 0 else (2 if hh % 2 == 0 else 1)
    s_zero = jnp.zeros((2, b, hh, dk, dk), F32)
    _, _, s_ctx = _dn_scan(*prep_c, s_zero, dk=dk, tt=n_ctx, hp=hp)
    o_f, o_b, _ = _dn_scan(*prep_l, s_ctx, dk=dk, tt=min(TT_DELTA, n), hp=hp)

    filt = _hy_filters(n, hy_w1[0], hy_b1[0], hy_w2[0], hy_b2[0], hy_w3[0], hy_b3[0], hy_freq[0], hy_w)
    tables = _dft_tables(n)
    hy3 = proj_hy.reshape(b, n, 3 * hy_w)
    cwh = hy_conv_w[0]
    cb = hy_w // LANES
    y1 = _hy_conv(hy3, 0, hy3, cb, filt, 0, 2 * cb, hy_skip[0, 0:1], cwh, cwh, tables, conv_a=True, hy_w=hy_w)
    y_hy = _hy_conv(y1, 0, hy3, 2 * cb, filt, cb, 3 * cb, hy_skip[0, 1:2], cwh, cwh, tables, conv_a=False, hy_w=hy_w)

    merged = _merge(o_f.reshape(b * n, qk_w), o_b.reshape(b * n, qk_w), proj_qkvz, off_z // qk_w,
                    y_hy.reshape(b * n, hy_w), proj_gate, dn_norm_w[0][None, :],
                    w_branch_dn[0].astype(BF16), w_branch_hy[0].astype(BF16), dv=dk, tm=TM_ROW)
    sk = peer_subkeys[0].reshape(p_heads * 2, nkeys, half).astype(BF16)
    x1, h2t, scores = _outproj(merged, x2d, w_out[0].astype(BF16), g1, ln1_g[0][None, :], ln1_b[0][None, :],
                              sc2, sh2, peer_wq[0].astype(BF16), sk, tm=TM_ROW)

    count1, rank2, e1, e2 = _peer_topk(scores, p_heads, tm=TM_ROW)
    out = _peer_dense(h2t, peer_u[0].astype(BF16), peer_v[0].T.astype(BF16), rank2, e2, count1, e1,
                      x1, g2, ln2_g[0][None, :], ln2_b[0][None, :], n_heads=p_heads, ti=TI_PEER, tm=TM_PEER)
    stages = dict(o_f=o_f, o_b=o_b, s_ctx=s_ctx, filt=filt, y1=y1, y_hy=y_hy, merged=merged, x1=x1, scores=scores,
                  count1=count1, rank2=rank2, e1=e1, e2=e2)
    return out.reshape(b, n, d), stages


def kernel(x, c, ctx, c_ctx, w_ada, b_ada, w_in, dn_conv_w, dn_a_log, dn_dt_bias, dn_norm_w, hy_conv_w, hy_w1, hy_b1, hy_w2, hy_b2, hy_w3, hy_b3, hy_freq, hy_skip, w_branch_dn, w_branch_hy, w_out, ln1_g, ln1_b, peer_wq, peer_subkeys, peer_u, peer_v, ln2_g, ln2_b):
    out, _ = _block(x, c, ctx, c_ctx, w_ada, b_ada, w_in, dn_conv_w, dn_a_log, dn_dt_bias, dn_norm_w, hy_conv_w, hy_w1, hy_b1, hy_w2, hy_b2, hy_w3, hy_b3, hy_freq, hy_skip, w_branch_dn, w_branch_hy, w_out, ln1_g, ln1_b, peer_wq, peer_subkeys, peer_u, peer_v, ln2_g, ln2_b)
    return out
```
